```python
import math
import jax, jax.numpy as jnp
from jax import lax
import numpy as np

D_MODEL = 1024
BATCH = 16
SEQ = 2048
DEPTH = 1

PLE_DIM = 256
SSM_WIDTH = 512
SSM_GROUP = 16
SSM_GROUPS = SSM_WIDTH // SSM_GROUP
SSM_STATE = 64
CONV_WIDTH = 512
CONV_GROUPS = 8
CONV_K = 3
N_EXPERT_GROUPS = 4
EXPERTS_PER_GROUP = 8
N_EXPERTS = N_EXPERT_GROUPS * EXPERTS_PER_GROUP
TOP_K_IN_GROUP = 2
EXPERT_HIDDEN = 256
PROJ_COLS = SSM_WIDTH + 3 * CONV_WIDTH + 2 * D_MODEL
NORM_EPS = 1e-6
DT_MIN = 1e-3
DT_MAX = 1e-1

kernel_name = "hybrid_s5_shortconv_hmoe_block"


def rms_norm(x, g):
    xf = x.astype(jnp.float32)
    y = xf * lax.rsqrt(jnp.mean(xf * xf, axis=-1, keepdims=True) + NORM_EPS)
    return (y * g.astype(jnp.float32)).astype(x.dtype)


def s5_mixer(u, lam_re, lam_im, log_dt, b_re, b_im, c_re, c_im, d_skip, w_glu_a, w_glu_b):
    f32 = jnp.float32
    bsz, seq, _ = u.shape
    uf = u.astype(f32).reshape(bsz, seq, SSM_GROUPS, SSM_GROUP)
    lr = lam_re.astype(f32)
    li = lam_im.astype(f32)
    dt = jnp.exp(log_dt.astype(f32))[:, None]
    mag = jnp.exp(lr * dt)
    ang = li * dt
    abar_re = mag * jnp.cos(ang)
    abar_im = mag * jnp.sin(ang)
    nr = abar_re - 1.0
    ni = abar_im
    den = lr * lr + li * li
    f_re = (nr * lr + ni * li) / den
    f_im = (ni * lr - nr * li) / den
    br = b_re.astype(f32)
    bi = b_im.astype(f32)
    bbar_re = f_re[:, :, None] * br - f_im[:, :, None] * bi
    bbar_im = f_re[:, :, None] * bi + f_im[:, :, None] * br
    x_re = jnp.einsum('blgh,gph->blgp', uf, bbar_re)
    x_im = jnp.einsum('blgh,gph->blgp', uf, bbar_im)
    a_re = jnp.broadcast_to(abar_re, (1, seq, SSM_GROUPS, SSM_STATE))
    a_im = jnp.broadcast_to(abar_im, (1, seq, SSM_GROUPS, SSM_STATE))

    def combine(e_i, e_j):
        ar_i, ai_i, sr_i, si_i = e_i
        ar_j, ai_j, sr_j, si_j = e_j
        return (ar_j * ar_i - ai_j * ai_i,
                ar_j * ai_i + ai_j * ar_i,
                ar_j * sr_i - ai_j * si_i + sr_j,
                ar_j * si_i + ai_j * sr_i + si_j)

    _, _, s_re, s_im = lax.associative_scan(combine, (a_re, a_im, x_re, x_im), axis=1)
    y = (jnp.einsum('blgp,ghp->blgh', s_re, c_re.astype(f32))
         - jnp.einsum('blgp,ghp->blgh', s_im, c_im.astype(f32)))
    y = y.reshape(bsz, seq, SSM_WIDTH) + d_skip.astype(f32) * uf.reshape(bsz, seq, SSM_WIDTH)
    y = jax.nn.gelu(y).astype(u.dtype)
    return (y @ w_glu_a) * jax.nn.sigmoid(y @ w_glu_b)


def short_conv_mixer(gate_b, gate_c, v, conv_w, conv_b, w_out):
    cv = gate_c * v
    y = lax.conv_general_dilated(
        cv, conv_w[:, None, :], window_strides=(1,), padding=[(CONV_K - 1, 0)],
        dimension_numbers=('NWC', 'WIO', 'NWC'), feature_group_count=CONV_WIDTH)
    y = y + conv_b
    return (gate_b * y) @ w_out


def hier_moe(h, w_rg, b_rg, w_re, b_re, w_gate, w_up, w_down):
    f32 = jnp.float32
    bsz, seq, d = h.shape
    t = h.reshape(-1, d)
    p_group = jax.nn.softmax((t @ w_rg + b_rg).astype(f32), axis=-1)
    g_w, g_idx = lax.top_k(p_group, 1)
    le = (t @ w_re + b_re).astype(f32).reshape(-1, N_EXPERT_GROUPS, EXPERTS_PER_GROUP)
    le_sel = jnp.take_along_axis(le, g_idx[:, :, None], axis=1)[:, 0]
    p_exp = jax.nn.softmax(le_sel, axis=-1)
    e_w, e_idx = lax.top_k(p_exp, TOP_K_IN_GROUP)
    e_w = e_w / jnp.sum(e_w, axis=-1, keepdims=True)
    weights = g_w * e_w
    ids = g_idx * EXPERTS_PER_GROUP + e_idx
    comb = jnp.sum(jax.nn.one_hot(ids, N_EXPERTS, dtype=f32) * weights[..., None], axis=1)
    out = jnp.zeros(t.shape, f32)
    for e in range(N_EXPERTS):
        hid = jax.nn.silu(t @ w_gate[e]) * (t @ w_up[e])
        out = out + comb[:, e:e + 1] * (hid @ w_down[e]).astype(f32)
    return out.reshape(bsz, seq, d).astype(h.dtype)


def setup_inputs(seed: int = 0) -> dict:
    key = jax.random.key(seed)
    ks = jax.random.split(key, 32)
    f32 = jnp.float32

    def dense(k, shape, fan_in):
        return jax.random.normal(k, shape, f32) * (fan_in ** -0.5)

    def gain(k, shape):
        return 1.0 + 0.01 * jax.random.normal(k, shape, f32)

    L = DEPTH
    lam_im0 = math.pi * jnp.arange(SSM_STATE, dtype=f32)
    return {
        "x": jax.random.normal(ks[0], (BATCH, SEQ, D_MODEL), f32),
        "p": jax.random.normal(ks[1], (DEPTH, BATCH, SEQ, PLE_DIM), f32),
        "norm_mix": gain(ks[2], (L, D_MODEL)),
        "w_in": dense(ks[3], (L, D_MODEL, PROJ_COLS), D_MODEL),
        "b_in": 0.01 * jax.random.normal(ks[4], (L, PROJ_COLS), f32),
        "ssm_lam_re": -0.5 + 0.01 * jax.random.normal(ks[5], (L, SSM_GROUPS, SSM_STATE), f32),
        "ssm_lam_im": lam_im0 + 0.01 * jax.random.normal(ks[6], (L, SSM_GROUPS, SSM_STATE), f32),
        "ssm_log_dt": jax.random.uniform(ks[7], (L, SSM_GROUPS), f32, math.log(DT_MIN), math.log(DT_MAX)),
        "ssm_b_re": dense(ks[8], (L, SSM_GROUPS, SSM_STATE, SSM_GROUP), 2 * SSM_GROUP),
        "ssm_b_im": dense(ks[9], (L, SSM_GROUPS, SSM_STATE, SSM_GROUP), 2 * SSM_GROUP),
        "ssm_c_re": dense(ks[10], (L, SSM_GROUPS, SSM_GROUP, SSM_STATE), SSM_STATE),
        "ssm_c_im": dense(ks[11], (L, SSM_GROUPS, SSM_GROUP, SSM_STATE), SSM_STATE),
        "ssm_d": jax.random.normal(ks[12], (L, SSM_WIDTH), f32),
        "w_glu_a": dense(ks[13], (L, SSM_WIDTH, D_MODEL), SSM_WIDTH),
        "w_glu_b": dense(ks[14], (L, SSM_WIDTH, D_MODEL), SSM_WIDTH),
        "conv_w": dense(ks[15], (L, CONV_K, CONV_WIDTH), CONV_K),
        "conv_b": 0.01 * jax.random.normal(ks[16], (L, CONV_WIDTH), f32),
        "w_conv_out": dense(ks[17], (L, CONV_WIDTH, D_MODEL), CONV_WIDTH),
        "w_o": dense(ks[18], (L, D_MODEL, D_MODEL), D_MODEL),
        "norm_ffn": gain(ks[19], (L, D_MODEL)),
        "w_router_group": dense(ks[20], (L, D_MODEL, N_EXPERT_GROUPS), D_MODEL),
        "b_router_group": 0.01 * jax.random.normal(ks[21], (L, N_EXPERT_GROUPS), f32),
        "w_router_expert": dense(ks[22], (L, D_MODEL, N_EXPERTS), D_MODEL),
        "b_router_expert": 0.01 * jax.random.normal(ks[23], (L, N_EXPERTS), f32),
        "w_exp_gate": dense(ks[24], (L, N_EXPERTS, D_MODEL, EXPERT_HIDDEN), D_MODEL),
        "w_exp_up": dense(ks[25], (L, N_EXPERTS, D_MODEL, EXPERT_HIDDEN), D_MODEL),
        "w_exp_down": dense(ks[26], (L, N_EXPERTS, EXPERT_HIDDEN, D_MODEL), EXPERT_HIDDEN),
        "norm_ple": gain(ks[27], (L, D_MODEL)),
        "w_ple": dense(ks[28], (L, PLE_DIM, D_MODEL), PLE_DIM),
        "w_ple_gate": dense(ks[29], (L, D_MODEL, D_MODEL), D_MODEL),
        "b_ple_gate": 0.01 * jax.random.normal(ks[30], (L, D_MODEL), f32),
        "norm_final": gain(ks[31], (D_MODEL,)),
    }


def reference(x, p, norm_mix, w_in, b_in, ssm_lam_re, ssm_lam_im, ssm_log_dt,
              ssm_b_re, ssm_b_im, ssm_c_re, ssm_c_im, ssm_d, w_glu_a, w_glu_b,
              conv_w, conv_b, w_conv_out, w_o, norm_ffn, w_router_group,
              b_router_group, w_router_expert, b_router_expert, w_exp_gate,
              w_exp_up, w_exp_down, norm_ple, w_ple, w_ple_gate, b_ple_gate,
              norm_final):
    s0 = SSM_WIDTH
    s1 = s0 + CONV_WIDTH
    s2 = s1 + CONV_WIDTH
    s3 = s2 + CONV_WIDTH
    s4 = s3 + D_MODEL
    for i in range(DEPTH):
        h = rms_norm(x, norm_mix[i])
        z = h @ w_in[i] + b_in[i]
        u_ssm = z[..., :s0]
        c_b, c_c, c_v = z[..., s0:s1], z[..., s1:s2], z[..., s2:s3]
        g_a = jax.nn.sigmoid(z[..., s3:s4])
        g_b = jax.nn.sigmoid(z[..., s4:])
        y_a = s5_mixer(u_ssm, ssm_lam_re[i], ssm_lam_im[i], ssm_log_dt[i],
                       ssm_b_re[i], ssm_b_im[i], ssm_c_re[i], ssm_c_im[i],
                       ssm_d[i], w_glu_a[i], w_glu_b[i])
        y_b = short_conv_mixer(c_b, c_c, c_v, conv_w[i], conv_b[i], w_conv_out[i])
        x = x + (g_a * y_a + g_b * y_b) @ w_o[i]
        h2 = rms_norm(x, norm_ffn[i])
        x = x + hier_moe(h2, w_router_group[i], b_router_group[i],
                         w_router_expert[i], b_router_expert[i],
                         w_exp_gate[i], w_exp_up[i], w_exp_down[i])
        h3 = rms_norm(x, norm_ple[i])
        gate = jax.nn.sigmoid(h3 @ w_ple_gate[i] + b_ple_gate[i])
        x = x + gate * (p[i] @ w_ple[i])
    return rms_norm(x, norm_final)
```

```python
import functools
import math

import jax
import jax.numpy as jnp
from jax import lax
from jax.experimental import pallas as pl
from jax.experimental.pallas import tpu as pltpu

F32 = jnp.float32
BF16 = jnp.bfloat16

D_MODEL = 1024
SSM_WIDTH = 512
SSM_GROUP = 16
SSM_GROUPS = 32
SSM_STATE = 64
CONV_WIDTH = 512
N_EXPERT_GROUPS = 4
EXPERTS_PER_GROUP = 8
N_EXPERTS = 32
EXPERT_HIDDEN = 256
NORM_EPS = 1e-6

LANES = 128
Q = 8
GROUPS_PER_LANE_TILE = LANES // SSM_GROUP
N_LANE_TILES = SSM_WIDTH // LANES
STATE_LANES = GROUPS_PER_LANE_TILE * SSM_STATE
S5_TT = 64
VMEM_LIMIT = 56 * 1024 * 1024


def _rms(x, g):
    return x * lax.rsqrt(jnp.mean(x * x, axis=-1, keepdims=True) + NORM_EPS) * g


def _const_spec(shape):
    n = len(shape)
    return pl.BlockSpec(shape, lambda *_: (0,) * n, pipeline_mode=pl.Buffered(1))


def _s5_operators(lam_re, lam_im, log_dt, b_re, b_im, c_re, c_im):
    hp = lax.Precision.HIGHEST
    G, P, H = SSM_GROUPS, SSM_STATE, SSM_GROUP
    J, G8 = N_LANE_TILES, GROUPS_PER_LANE_TILE
    lr = lam_re.astype(F32)
    li = lam_im.astype(F32)
    dt = jnp.exp(log_dt.astype(F32))[:, None]

    def apow(n):
        n = n.astype(F32)[:, None, None]
        mag = jnp.exp(lr * dt * n)
        ang = li * dt * n
        return mag * jnp.cos(ang), mag * jnp.sin(ang)

    a1_re, a1_im = apow(jnp.ones((1,), F32))
    nr = a1_re[0] - 1.0
    ni = a1_im[0]
    den = lr * lr + li * li
    f_re = (nr * lr + ni * li) / den
    f_im = (ni * lr - nr * li) / den
    br = b_re.astype(F32)
    bi = b_im.astype(F32)
    bbar_re = f_re[:, :, None] * br - f_im[:, :, None] * bi
    bbar_im = f_re[:, :, None] * bi + f_im[:, :, None] * br
    cr = c_re.astype(F32)
    ci = c_im.astype(F32)
    eye = jnp.eye(G8, dtype=F32)

    steps = jnp.arange(Q)
    al_re, al_im = apow(steps)
    gre = al_re[..., None] * bbar_re[None] - al_im[..., None] * bbar_im[None]
    gim = al_re[..., None] * bbar_im[None] + al_im[..., None] * bbar_re[None]
    kfeat = (jnp.einsum('ghp,lgpk->lgkh', cr, gre, precision=hp)
             - jnp.einsum('ghp,lgpk->lgkh', ci, gim, precision=hp))
    lag = steps[None, :] - steps[:, None]
    kt = kfeat[jnp.clip(lag, 0, Q - 1)] * (lag >= 0).astype(F32)[:, :, None, None, None]
    kt = kt.reshape(Q, Q, J, G8, H, H)
    kmat = jnp.einsum('ktjgab,gc->jkgatcb', kt, eye).reshape(J, Q * LANES, Q * LANES)

    am_re, am_im = apow(Q - 1 - steps)
    mre = am_re[..., None] * bbar_re[None] - am_im[..., None] * bbar_im[None]
    mim = am_re[..., None] * bbar_im[None] + am_im[..., None] * bbar_re[None]
    mm = jnp.stack([mre, mim], axis=0).reshape(2, Q, J, G8, SSM_STATE, H)
    mmat = jnp.einsum('rkjgpa,gc->jkgarcp', mm, eye).reshape(J, Q * LANES, 2 * STATE_LANES)

    an_re, an_im = apow(steps + 1)
    nre = cr[None] * an_re[:, :, None, :] - ci[None] * an_im[:, :, None, :]
    nim = -(cr[None] * an_im[:, :, None, :] + ci[None] * an_re[:, :, None, :])
    nn = jnp.stack([nre, nim], axis=0).reshape(2, Q, J, G8, H, SSM_STATE)
    nmat = jnp.einsum('rtjgbp,gc->jrgptcb', nn, eye).reshape(J, 2 * STATE_LANES, Q * LANES)

    aq_re, aq_im = apow(jnp.full((1,), Q))
    return (kmat.astype(BF16), mmat.astype(BF16), nmat.astype(BF16),
            aq_re.reshape(1, G * P), aq_im.reshape(1, G * P))


def _s5_body(x_ref, g_ref, wu_ref, bu_ref, m_ref, k_ref, n_ref, aqr_ref, aqi_ref, d_ref,
             o_ref, u_scr, y_scr, z_scr, ss_scr, carry_scr):
    nb = x_ref.shape[0]
    rows = nb * S5_TT
    nchunk = rows // Q

    @pl.when(pl.program_id(0) == 0)
    def _():
        carry_scr[...] = jnp.zeros_like(carry_scr)

    x = x_ref[...].reshape(rows, D_MODEL)
    h = _rms(x, g_ref[...]).astype(BF16)
    u = jnp.dot(h, wu_ref[...], preferred_element_type=F32) + bu_ref[...]
    for j in range(N_LANE_TILES):
        u_scr[j] = u[:, j * LANES:(j + 1) * LANES]

    n_st = STATE_LANES // LANES
    cpt = S5_TT // Q
    for j in range(N_LANE_TILES):
        xj = jnp.concatenate(
            [u_scr[j, pl.ds(k, nchunk, stride=Q), :] for k in range(Q)], axis=1).astype(BF16)
        z = jnp.dot(xj, m_ref[j], preferred_element_type=F32)
        for i in range(2 * n_st):
            z_scr[i] = z[:, i * LANES:(i + 1) * LANES]
        aqr = jnp.broadcast_to(aqr_ref[:, pl.ds(j * STATE_LANES, STATE_LANES)], (nb, STATE_LANES))
        aqi = jnp.broadcast_to(aqi_ref[:, pl.ds(j * STATE_LANES, STATE_LANES)], (nb, STATE_LANES))
        s_re = carry_scr[j, :, pl.ds(0, STATE_LANES)]
        s_im = carry_scr[j, :, pl.ds(STATE_LANES, STATE_LANES)]
        for c in range(cpt):
            seq_rows = pl.ds(c, nb, stride=cpt)
            for i in range(n_st):
                ss_scr[i, seq_rows, :] = s_re[:, i * LANES:(i + 1) * LANES]
                ss_scr[n_st + i, seq_rows, :] = s_im[:, i * LANES:(i + 1) * LANES]
            z_re = jnp.concatenate([z_scr[i, seq_rows, :] for i in range(n_st)], axis=1)
            z_im = jnp.concatenate([z_scr[n_st + i, seq_rows, :] for i in range(n_st)], axis=1)
            s_re, s_im = (aqr * s_re - aqi * s_im + z_re,
                          aqr * s_im + aqi * s_re + z_im)
        carry_scr[j, :, pl.ds(0, STATE_LANES)] = s_re
        carry_scr[j, :, pl.ds(STATE_LANES, STATE_LANES)] = s_im
        ss = jnp.concatenate([ss_scr[i] for i in range(2 * n_st)], axis=1).astype(BF16)
        yj = (jnp.dot(xj, k_ref[j], preferred_element_type=F32)
              + jnp.dot(ss, n_ref[j], preferred_element_type=F32))
        for k in range(Q):
            y_scr[j, pl.ds(k, nchunk, stride=Q), :] = yj[:, k * LANES:(k + 1) * LANES]

    for j in range(N_LANE_TILES):
        lanes = pl.ds(j * LANES, LANES)
        y = y_scr[j] + d_ref[:, lanes] * u_scr[j]
        o_ref[:, :, lanes] = jax.nn.gelu(y).astype(BF16).reshape(nb, S5_TT, LANES)


def _s5_call(x, g, wu, bu, mmat, kmat, nmat, aq_re, aq_im, d_skip):
    nb, seq, _ = x.shape
    rows = nb * S5_TT
    nchunk = rows // Q
    return pl.pallas_call(
        _s5_body,
        out_shape=jax.ShapeDtypeStruct((nb, seq, SSM_WIDTH), BF16),
        grid=(seq // S5_TT,),
        in_specs=[
            pl.BlockSpec((nb, S5_TT, D_MODEL), lambda i: (0, i, 0)),
            _const_spec(g.shape), _const_spec(wu.shape), _const_spec(bu.shape),
            _const_spec(mmat.shape), _const_spec(kmat.shape), _const_spec(nmat.shape),
            _const_spec(aq_re.shape), _const_spec(aq_im.shape), _const_spec(d_skip.shape),
        ],
        out_specs=pl.BlockSpec((nb, S5_TT, SSM_WIDTH), lambda i: (0, i, 0)),
        scratch_shapes=[
            pltpu.VMEM((N_LANE_TILES, rows, LANES), F32),
            pltpu.VMEM((N_LANE_TILES, rows, LANES), F32),
            pltpu.VMEM((2 * STATE_LANES // LANES, nchunk, LANES), F32),
            pltpu.VMEM((2 * STATE_LANES // LANES, nchunk, LANES), F32),
            pltpu.VMEM((N_LANE_TILES, nb, 2 * STATE_LANES), F32),
        ],
        compiler_params=pltpu.CompilerParams(
            dimension_semantics=("arbitrary",), vmem_limit_bytes=VMEM_LIMIT),
        name="s5_mixer",
    )(x, g, wu, bu, mmat, kmat, nmat, aq_re, aq_im, d_skip)


MIX_TM = 512


def _mix_body(x_ref, ys_ref, g_ref, wc_ref, bc_ref, wg_ref, bg_ref, cw_ref, cb_ref, wco_ref,
              wab_ref, wo_ref, g2_ref, wr_ref, wrh_ref, br_ref,
              x1_ref, h2_ref, comb_ref, carry_scr):
    tm = x_ref.shape[1]

    @pl.when(pl.program_id(1) == 0)
    def _():
        carry_scr[...] = jnp.zeros_like(carry_scr)

    x = x_ref[0]
    h = _rms(x, g_ref[...]).astype(BF16)

    zc = jnp.dot(h, wc_ref[...], preferred_element_type=F32) + bc_ref[...]
    c_b = zc[:, 0:CONV_WIDTH]
    cv = zc[:, CONV_WIDTH:2 * CONV_WIDTH] * zc[:, 2 * CONV_WIDTH:3 * CONV_WIDTH]
    row = lax.broadcasted_iota(jnp.int32, (tm, CONV_WIDTH), 0)
    last1 = carry_scr[7:8, :]
    last2 = carry_scr[6:7, :]
    p1 = jnp.where(row == 0, last1, pltpu.roll(cv, 1, axis=0))
    p2 = jnp.where(row == 0, last2, jnp.where(row == 1, last1, pltpu.roll(cv, 2, axis=0)))
    carry_scr[...] = cv[tm - 8:tm, :]
    conv = cw_ref[0:1, :] * p2 + cw_ref[1:2, :] * p1 + cw_ref[2:3, :] * cv + cb_ref[...]
    y_b = jnp.dot((c_b * conv).astype(BF16), wco_ref[...], preferred_element_type=F32)

    yab = jnp.dot(ys_ref[0], wab_ref[...], preferred_element_type=F32)
    y_a = yab[:, 0:D_MODEL] * jax.nn.sigmoid(yab[:, D_MODEL:2 * D_MODEL])

    zg = jnp.dot(h, wg_ref[...], preferred_element_type=F32) + bg_ref[...]
    mix = (jax.nn.sigmoid(zg[:, 0:D_MODEL]) * y_a
           + jax.nn.sigmoid(zg[:, D_MODEL:2 * D_MODEL]) * y_b)
    x1 = x + jnp.dot(mix.astype(BF16), wo_ref[...], preferred_element_type=F32)
    x1_ref[...] = x1

    h2 = _rms(x1, g2_ref[...])
    h2_hi = h2.astype(BF16)
    h2_ref[...] = h2_hi
    h2_lo = (h2 - h2_hi.astype(F32)).astype(BF16)
    lg2 = jnp.dot(h2_hi, wr_ref[...], preferred_element_type=F32)
    logits = (lg2[:, 0:LANES] + lg2[:, LANES:2 * LANES]
              + jnp.dot(h2_lo, wrh_ref[...], preferred_element_type=F32) + br_ref[...])

    lane = lax.broadcasted_iota(jnp.int32, (tm, LANES), 1)
    neg = jnp.float32(-jnp.inf)
    big = jnp.int32(1 << 20)
    is_g = (lane >= N_EXPERTS) & (lane < N_EXPERTS + N_EXPERT_GROUPS)
    gl = jnp.where(is_g, logits, neg)
    gmax = jnp.max(gl, axis=1, keepdims=True)
    g_w = 1.0 / jnp.sum(jnp.exp(gl - gmax), axis=1, keepdims=True)
    g_idx = jnp.min(jnp.where(gl == gmax, lane - N_EXPERTS, big), axis=1, keepdims=True)
    lo = g_idx * EXPERTS_PER_GROUP
    el = jnp.where((lane >= lo) & (lane < lo + EXPERTS_PER_GROUP), logits, neg)
    m1 = jnp.max(el, axis=1, keepdims=True)
    i1 = jnp.min(jnp.where(el == m1, lane, big), axis=1, keepdims=True)
    el2 = jnp.where(lane == i1, neg, el)
    m2 = jnp.max(el2, axis=1, keepdims=True)
    i2 = jnp.min(jnp.where(el2 == m2, lane, big), axis=1, keepdims=True)
    r = jnp.exp(m2 - m1)
    w1 = g_w / (1.0 + r)
    w2 = g_w * r / (1.0 + r)
    comb_ref[...] = jnp.where(lane == i1, w1, 0.0) + jnp.where(lane == i2, w2, 0.0)


def _mix_call(x, ys, g, wc, bc, wg, bg, cw, cb, wco, wab, wo, g2, wr, wrh, br):
    nb, seq, _ = x.shape
    T = nb * seq
    nl = seq // MIX_TM
    consts = (g, wc, bc, wg, bg, cw, cb, wco, wab, wo, g2, wr, wrh, br)
    tok = lambda b, l: (b * nl + l, 0)
    return pl.pallas_call(
        _mix_body,
        out_shape=(jax.ShapeDtypeStruct((T, D_MODEL), F32),
                   jax.ShapeDtypeStruct((T, D_MODEL), BF16),
                   jax.ShapeDtypeStruct((T, LANES), F32)),
        grid=(nb, nl),
        in_specs=[pl.BlockSpec((1, MIX_TM, D_MODEL), lambda b, l: (b, l, 0)),
                  pl.BlockSpec((1, MIX_TM, SSM_WIDTH), lambda b, l: (b, l, 0))]
                 + [_const_spec(c.shape) for c in consts],
        out_specs=(pl.BlockSpec((MIX_TM, D_MODEL), tok),
                   pl.BlockSpec((MIX_TM, D_MODEL), tok),
                   pl.BlockSpec((MIX_TM, LANES), tok)),
        scratch_shapes=[pltpu.VMEM((8, CONV_WIDTH), F32)],
        compiler_params=pltpu.CompilerParams(
            dimension_semantics=("arbitrary", "arbitrary"), vmem_limit_bytes=VMEM_LIMIT),
        name="conv_glu_router",
    )(x, ys, *consts)


MOE_TM = 1024


def _moe_body(h2_ref, comb_ref, x1_ref, wg_ref, wu_ref, wd_ref, o_ref):
    e = pl.program_id(1)

    @pl.when(e == 0)
    def _():
        o_ref[...] = x1_ref[...]

    h2 = h2_ref[...]
    lane = lax.broadcasted_iota(jnp.int32, comb_ref.shape, 1)
    c = jnp.sum(jnp.where(lane == e, comb_ref[...], 0.0), axis=1, keepdims=True)
    gate = jnp.dot(h2, wg_ref[0], preferred_element_type=F32)
    up = jnp.dot(h2, wu_ref[0], preferred_element_type=F32)
    hid = (jax.nn.silu(gate) * up).astype(BF16)
    o_ref[...] += c * jnp.dot(hid, wd_ref[0], preferred_element_type=F32)


def _moe_call(h2, comb, x1, wg, wu, wd):
    T = h2.shape[0]
    return pl.pallas_call(
        _moe_body,
        out_shape=jax.ShapeDtypeStruct((T, D_MODEL), F32),
        grid=(T // MOE_TM, N_EXPERTS),
        in_specs=[pl.BlockSpec((MOE_TM, D_MODEL), lambda i, e: (i, 0)),
                  pl.BlockSpec((MOE_TM, LANES), lambda i, e: (i, 0)),
                  pl.BlockSpec((MOE_TM, D_MODEL), lambda i, e: (i, 0)),
                  pl.BlockSpec((1, D_MODEL, EXPERT_HIDDEN), lambda i, e: (e, 0, 0)),
                  pl.BlockSpec((1, D_MODEL, EXPERT_HIDDEN), lambda i, e: (e, 0, 0)),
                  pl.BlockSpec((1, EXPERT_HIDDEN, D_MODEL), lambda i, e: (e, 0, 0))],
        out_specs=pl.BlockSpec((MOE_TM, D_MODEL), lambda i, e: (i, 0)),
        compiler_params=pltpu.CompilerParams(
            dimension_semantics=("arbitrary", "arbitrary"), vmem_limit_bytes=VMEM_LIMIT),
        name="moe_dense",
    )(h2, comb, x1, wg, wu, wd)


PLE_TM = 512


def _ple_body(x_ref, p_ref, g3_ref, wpg_ref, bpg_ref, wple_ref, gf_ref, o_ref):
    x2 = x_ref[...]
    h3 = _rms(x2, g3_ref[...]).astype(BF16)
    gate = jax.nn.sigmoid(jnp.dot(h3, wpg_ref[...], preferred_element_type=F32) + bpg_ref[...])
    pe = jnp.dot(p_ref[...].astype(BF16), wple_ref[...], preferred_element_type=F32)
    x3 = x2 + gate * pe
    o_ref[...] = _rms(x3, gf_ref[...])


def _ple_call(x2, p, g3, wpg, bpg, wple, gf):
    T = x2.shape[0]
    consts = (g3, wpg, bpg, wple, gf)
    return pl.pallas_call(
        _ple_body,
        out_shape=jax.ShapeDtypeStruct((T, D_MODEL), F32),
        grid=(T // PLE_TM,),
        in_specs=[pl.BlockSpec((PLE_TM, D_MODEL), lambda i: (i, 0)),
                  pl.BlockSpec((PLE_TM, p.shape[1]), lambda i: (i, 0))]
                 + [_const_spec(c.shape) for c in consts],
        out_specs=pl.BlockSpec((PLE_TM, D_MODEL), lambda i: (i, 0)),
        compiler_params=pltpu.CompilerParams(
            dimension_semantics=("arbitrary",), vmem_limit_bytes=VMEM_LIMIT),
        name="ple_final",
    )(x2, p, *consts)


def _layer(x, p, norm_mix, w_in, b_in, lam_re, lam_im, log_dt, b_re, b_im, c_re, c_im, d_skip,
           w_glu_a, w_glu_b, conv_w, conv_b, w_conv_out, w_o, norm_ffn, w_rg, b_rg, w_re, b_re_r,
           w_eg, w_eu, w_ed, norm_ple, w_ple, w_pg, b_pg, norm_out):
    nb, seq, d = x.shape
    T = nb * seq
    s0 = SSM_WIDTH
    s3 = s0 + 3 * CONV_WIDTH
    row = lambda v: v.reshape(1, -1).astype(F32)

    kmat, mmat, nmat, aq_re, aq_im = _s5_operators(lam_re, lam_im, log_dt, b_re, b_im, c_re, c_im)
    ys = _s5_call(x, row(norm_mix), w_in[:, :s0].astype(BF16), row(b_in[:s0]),
                  mmat, kmat, nmat, aq_re, aq_im, row(d_skip))

    w_r = jnp.zeros((d, LANES), F32).at[:, :N_EXPERTS].set(w_re)
    w_r = w_r.at[:, N_EXPERTS:N_EXPERTS + N_EXPERT_GROUPS].set(w_rg)
    b_r = jnp.zeros((1, LANES), F32).at[0, :N_EXPERTS].set(b_re_r)
    b_r = b_r.at[0, N_EXPERTS:N_EXPERTS + N_EXPERT_GROUPS].set(b_rg)
    w_r_hi = w_r.astype(BF16)
    w_r_lo = (w_r - w_r_hi.astype(F32)).astype(BF16)

    x1, h2, comb = _mix_call(
        x, ys, row(norm_mix),
        w_in[:, s0:s3].astype(BF16), row(b_in[s0:s3]),
        w_in[:, s3:].astype(BF16), row(b_in[s3:]),
        conv_w.astype(F32), row(conv_b), w_conv_out.astype(BF16),
        jnp.concatenate([w_glu_a, w_glu_b], axis=1).astype(BF16), w_o.astype(BF16),
        row(norm_ffn), jnp.concatenate([w_r_hi, w_r_lo], axis=1), w_r_hi, b_r)

    x2 = _moe_call(h2, comb, x1, w_eg.astype(BF16), w_eu.astype(BF16), w_ed.astype(BF16))

    out = _ple_call(x2, p.reshape(T, -1), row(norm_ple), w_pg.astype(BF16), row(b_pg),
                    w_ple.astype(BF16), row(norm_out))
    return out.reshape(nb, seq, d)


def kernel(x, p, norm_mix, w_in, b_in, ssm_lam_re, ssm_lam_im, ssm_log_dt, ssm_b_re, ssm_b_im, ssm_c_re, ssm_c_im, ssm_d, w_glu_a, w_glu_b, conv_w, conv_b, w_conv_out, w_o, norm_ffn, w_router_group, b_router_group, w_router_expert, b_router_expert, w_exp_gate, w_exp_up, w_exp_down, norm_ple, w_ple, w_ple_gate, b_ple_gate, norm_final):
    assert p.shape[0] == 1, "the final RMSNorm is fused into the (single) layer's last kernel"
    i = 0
    return _layer(x, p[i], norm_mix[i], w_in[i], b_in[i], ssm_lam_re[i], ssm_lam_im[i],
                  ssm_log_dt[i], ssm_b_re[i], ssm_b_im[i], ssm_c_re[i], ssm_c_im[i], ssm_d[i],
                  w_glu_a[i], w_glu_b[i], conv_w[i], conv_b[i], w_conv_out[i], w_o[i],
                  norm_ffn[i], w_router_group[i], b_router_group[i], w_router_expert[i],
                  b_router_expert[i], w_exp_gate[i], w_exp_up[i], w_exp_down[i], norm_ple[i],
                  w_ple[i], w_ple_gate[i], b_ple_gate[i], norm_final)
```

```python
import functools
import math

import jax
import jax.numpy as jnp
from jax import lax
from jax.experimental import pallas as pl
from jax.experimental.pallas import tpu as pltpu
from jax.experimental.pallas import tpu_sc as plsc

F32 = jnp.float32
BF16 = jnp.bfloat16

D_MODEL = 1024
SSM_WIDTH = 512
SSM_GROUP = 16
SSM_GROUPS = 32
SSM_STATE = 64
CONV_WIDTH = 512
N_EXPERT_GROUPS = 4
EXPERTS_PER_GROUP = 8
N_EXPERTS = 32
EXPERT_HIDDEN = 256
NORM_EPS = 1e-6

LANES = 128
Q = 8
GROUPS_PER_LANE_TILE = LANES // SSM_GROUP
N_LANE_TILES = SSM_WIDTH // LANES
STATE_LANES = GROUPS_PER_LANE_TILE * SSM_STATE
S5_TT = 64
VMEM_LIMIT = 56 * 1024 * 1024


def _rms(x, g):
    return x * lax.rsqrt(jnp.mean(x * x, axis=-1, keepdims=True) + NORM_EPS) * g


def _pack_bf16_pairs(a):
    w = a.shape[1] // 2
    lo = lax.shift_right_logical(lax.bitcast_convert_type(a[:, :w], jnp.int32), 16)
    hi = lax.bitcast_convert_type(a[:, w:], jnp.int32) & jnp.int32(-65536)
    return lo | hi


def _unpack_bf16_pairs(word):
    lo = lax.bitcast_convert_type(lax.shift_left(word, 16), F32)
    hi = lax.bitcast_convert_type(word & jnp.int32(-65536), F32)
    return lo, hi


def _const_spec(shape):
    n = len(shape)
    return pl.BlockSpec(shape, lambda *_: (0,) * n, pipeline_mode=pl.Buffered(1))


def _s5_operators(lam_re, lam_im, log_dt, b_re, b_im, c_re, c_im):
    hp = lax.Precision.HIGHEST
    G, P, H = SSM_GROUPS, SSM_STATE, SSM_GROUP
    J, G8 = N_LANE_TILES, GROUPS_PER_LANE_TILE
    lr = lam_re.astype(F32)
    li = lam_im.astype(F32)
    dt = jnp.exp(log_dt.astype(F32))[:, None]

    def apow(n):
        n = n.astype(F32)[:, None, None]
        mag = jnp.exp(lr * dt * n)
        ang = li * dt * n
        return mag * jnp.cos(ang), mag * jnp.sin(ang)

    a1_re, a1_im = apow(jnp.ones((1,), F32))
    nr = a1_re[0] - 1.0
    ni = a1_im[0]
    den = lr * lr + li * li
    f_re = (nr * lr + ni * li) / den
    f_im = (ni * lr - nr * li) / den
    br = b_re.astype(F32)
    bi = b_im.astype(F32)
    bbar_re = f_re[:, :, None] * br - f_im[:, :, None] * bi
    bbar_im = f_re[:, :, None] * bi + f_im[:, :, None] * br
    cr = c_re.astype(F32)
    ci = c_im.astype(F32)
    eye = jnp.eye(G8, dtype=F32)

    steps = jnp.arange(Q)
    al_re, al_im = apow(steps)
    gre = al_re[..., None] * bbar_re[None] - al_im[..., None] * bbar_im[None]
    gim = al_re[..., None] * bbar_im[None] + al_im[..., None] * bbar_re[None]
    kfeat = (jnp.einsum('ghp,lgpk->lgkh', cr, gre, precision=hp)
             - jnp.einsum('ghp,lgpk->lgkh', ci, gim, precision=hp))
    lag = steps[None, :] - steps[:, None]
    kt = kfeat[jnp.clip(lag, 0, Q - 1)] * (lag >= 0).astype(F32)[:, :, None, None, None]
    kt = kt.reshape(Q, Q, J, G8, H, H)
    kmat = jnp.einsum('ktjgab,gc->jkgatcb', kt, eye).reshape(J, Q * LANES, Q * LANES)

    am_re, am_im = apow(Q - 1 - steps)
    mre = am_re[..., None] * bbar_re[None] - am_im[..., None] * bbar_im[None]
    mim = am_re[..., None] * bbar_im[None] + am_im[..., None] * bbar_re[None]
    mm = jnp.stack([mre, mim], axis=0).reshape(2, Q, J, G8, SSM_STATE, H)
    mmat = jnp.einsum('rkjgpa,gc->jkgarcp', mm, eye).reshape(J, Q * LANES, 2 * STATE_LANES)

    an_re, an_im = apow(steps + 1)
    nre = cr[None] * an_re[:, :, None, :] - ci[None] * an_im[:, :, None, :]
    nim = -(cr[None] * an_im[:, :, None, :] + ci[None] * an_re[:, :, None, :])
    nn = jnp.stack([nre, nim], axis=0).reshape(2, Q, J, G8, H, SSM_STATE)
    nmat = jnp.einsum('rtjgbp,gc->jrgptcb', nn, eye).reshape(J, 2 * STATE_LANES, Q * LANES)

    aq_re, aq_im = apow(jnp.full((1,), Q))
    return (kmat.astype(BF16), mmat.astype(BF16), nmat.astype(BF16),
            aq_re.reshape(1, G * P), aq_im.reshape(1, G * P))


def _s5_body(x_ref, g_ref, wu_ref, bu_ref, m_ref, k_ref, n_ref, aqr_ref, aqi_ref, d_ref,
             o_ref, u_scr, y_scr, z_scr, ss_scr, carry_scr):
    nb = x_ref.shape[0]
    rows = nb * S5_TT
    nchunk = rows // Q

    @pl.when(pl.program_id(0) == 0)
    def _():
        carry_scr[...] = jnp.zeros_like(carry_scr)

    x = x_ref[...].reshape(rows, D_MODEL)
    h = _rms(x, g_ref[...]).astype(BF16)
    u = jnp.dot(h, wu_ref[...], preferred_element_type=F32) + bu_ref[...]
    for j in range(N_LANE_TILES):
        u_scr[j] = u[:, j * LANES:(j + 1) * LANES]

    n_st = STATE_LANES // LANES
    cpt = S5_TT // Q
    for j in range(N_LANE_TILES):
        xj = jnp.concatenate(
            [u_scr[j, pl.ds(k, nchunk, stride=Q), :] for k in range(Q)], axis=1).astype(BF16)
        z = jnp.dot(xj, m_ref[j], preferred_element_type=F32)
        for i in range(2 * n_st):
            z_scr[i] = z[:, i * LANES:(i + 1) * LANES]
        aqr = jnp.broadcast_to(aqr_ref[:, pl.ds(j * STATE_LANES, STATE_LANES)], (nb, STATE_LANES))
        aqi = jnp.broadcast_to(aqi_ref[:, pl.ds(j * STATE_LANES, STATE_LANES)], (nb, STATE_LANES))
        s_re = carry_scr[j, :, pl.ds(0, STATE_LANES)]
        s_im = carry_scr[j, :, pl.ds(STATE_LANES, STATE_LANES)]
        for c in range(cpt):
            seq_rows = pl.ds(c, nb, stride=cpt)
            for i in range(n_st):
                ss_scr[i, seq_rows, :] = s_re[:, i * LANES:(i + 1) * LANES]
                ss_scr[n_st + i, seq_rows, :] = s_im[:, i * LANES:(i + 1) * LANES]
            z_re = jnp.concatenate([z_scr[i, seq_rows, :] for i in range(n_st)], axis=1)
            z_im = jnp.concatenate([z_scr[n_st + i, seq_rows, :] for i in range(n_st)], axis=1)
            s_re, s_im = (aqr * s_re - aqi * s_im + z_re,
                          aqr * s_im + aqi * s_re + z_im)
        carry_scr[j, :, pl.ds(0, STATE_LANES)] = s_re
        carry_scr[j, :, pl.ds(STATE_LANES, STATE_LANES)] = s_im
        ss = jnp.concatenate([ss_scr[i] for i in range(2 * n_st)], axis=1).astype(BF16)
        yj = (jnp.dot(xj, k_ref[j], preferred_element_type=F32)
              + jnp.dot(ss, n_ref[j], preferred_element_type=F32))
        for k in range(Q):
            y_scr[j, pl.ds(k, nchunk, stride=Q), :] = yj[:, k * LANES:(k + 1) * LANES]

    for j in range(N_LANE_TILES):
        lanes = pl.ds(j * LANES, LANES)
        y = y_scr[j] + d_ref[:, lanes] * u_scr[j]
        o_ref[:, :, lanes] = jax.nn.gelu(y).astype(BF16).reshape(nb, S5_TT, LANES)


def _s5_call(x, g, wu, bu, mmat, kmat, nmat, aq_re, aq_im, d_skip):
    nb, seq, _ = x.shape
    rows = nb * S5_TT
    nchunk = rows // Q
    return pl.pallas_call(
        _s5_body,
        out_shape=jax.ShapeDtypeStruct((nb, seq, SSM_WIDTH), BF16),
        grid=(seq // S5_TT,),
        in_specs=[
            pl.BlockSpec((nb, S5_TT, D_MODEL), lambda i: (0, i, 0)),
            _const_spec(g.shape), _const_spec(wu.shape), _const_spec(bu.shape),
            _const_spec(mmat.shape), _const_spec(kmat.shape), _const_spec(nmat.shape),
            _const_spec(aq_re.shape), _const_spec(aq_im.shape), _const_spec(d_skip.shape),
        ],
        out_specs=pl.BlockSpec((nb, S5_TT, SSM_WIDTH), lambda i: (0, i, 0)),
        scratch_shapes=[
            pltpu.VMEM((N_LANE_TILES, rows, LANES), F32),
            pltpu.VMEM((N_LANE_TILES, rows, LANES), F32),
            pltpu.VMEM((2 * STATE_LANES // LANES, nchunk, LANES), F32),
            pltpu.VMEM((2 * STATE_LANES // LANES, nchunk, LANES), F32),
            pltpu.VMEM((N_LANE_TILES, nb, 2 * STATE_LANES), F32),
        ],
        compiler_params=pltpu.CompilerParams(
            dimension_semantics=("arbitrary",), vmem_limit_bytes=VMEM_LIMIT),
        name="s5_mixer",
    )(x, g, wu, bu, mmat, kmat, nmat, aq_re, aq_im, d_skip)


MIX_TM = 512


def _mix_body(x_ref, ys_ref, g_ref, wc_ref, bc_ref, wg_ref, bg_ref, cw_ref, cb_ref, wco_ref,
              wab_ref, wo_ref, g2_ref, wr_ref, wrh_ref, br_ref,
              x1_ref, h2p_ref, route_ref, cnt_ref, carry_scr, cnt_scr):
    tm = x_ref.shape[1]

    @pl.when(pl.program_id(1) == 0)
    def _():
        carry_scr[...] = jnp.zeros_like(carry_scr)

    @pl.when((pl.program_id(0) == 0) & (pl.program_id(1) == 0))
    def _():
        cnt_scr[...] = jnp.zeros_like(cnt_scr)

    x = x_ref[0]
    h = _rms(x, g_ref[...]).astype(BF16)

    zc = jnp.dot(h, wc_ref[...], preferred_element_type=F32) + bc_ref[...]
    c_b = zc[:, 0:CONV_WIDTH]
    cv = zc[:, CONV_WIDTH:2 * CONV_WIDTH] * zc[:, 2 * CONV_WIDTH:3 * CONV_WIDTH]
    row = lax.broadcasted_iota(jnp.int32, (tm, CONV_WIDTH), 0)
    last1 = carry_scr[7:8, :]
    last2 = carry_scr[6:7, :]
    p1 = jnp.where(row == 0, last1, pltpu.roll(cv, 1, axis=0))
    p2 = jnp.where(row == 0, last2, jnp.where(row == 1, last1, pltpu.roll(cv, 2, axis=0)))
    carry_scr[...] = cv[tm - 8:tm, :]
    conv = cw_ref[0:1, :] * p2 + cw_ref[1:2, :] * p1 + cw_ref[2:3, :] * cv + cb_ref[...]
    y_b = jnp.dot((c_b * conv).astype(BF16), wco_ref[...], preferred_element_type=F32)

    yab = jnp.dot(ys_ref[0], wab_ref[...], preferred_element_type=F32)
    y_a = yab[:, 0:D_MODEL] * jax.nn.sigmoid(yab[:, D_MODEL:2 * D_MODEL])

    zg = jnp.dot(h, wg_ref[...], preferred_element_type=F32) + bg_ref[...]
    mix = (jax.nn.sigmoid(zg[:, 0:D_MODEL]) * y_a
           + jax.nn.sigmoid(zg[:, D_MODEL:2 * D_MODEL]) * y_b)
    x1 = x + jnp.dot(mix.astype(BF16), wo_ref[...], preferred_element_type=F32)
    x1_ref[...] = x1

    h2 = _rms(x1, g2_ref[...])
    h2_hi = h2.astype(BF16)
    h2p_ref[...] = _pack_bf16_pairs(h2_hi.astype(F32))
    h2_lo = (h2 - h2_hi.astype(F32)).astype(BF16)
    lg2 = jnp.dot(h2_hi, wr_ref[...], preferred_element_type=F32)
    logits = (lg2[:, 0:LANES] + lg2[:, LANES:2 * LANES]
              + jnp.dot(h2_lo, wrh_ref[...], preferred_element_type=F32) + br_ref[...])

    lane = lax.broadcasted_iota(jnp.int32, (tm, LANES), 1)
    neg = jnp.float32(-jnp.inf)
    big = jnp.int32(1 << 20)
    is_g = (lane >= N_EXPERTS) & (lane < N_EXPERTS + N_EXPERT_GROUPS)
    gl = jnp.where(is_g, logits, neg)
    gmax = jnp.max(gl, axis=1, keepdims=True)
    g_w = 1.0 / jnp.sum(jnp.exp(gl - gmax), axis=1, keepdims=True)
    g_idx = jnp.min(jnp.where(gl == gmax, lane - N_EXPERTS, big), axis=1, keepdims=True)
    lo = g_idx * EXPERTS_PER_GROUP
    el = jnp.where((lane >= lo) & (lane < lo + EXPERTS_PER_GROUP), logits, neg)
    m1 = jnp.max(el, axis=1, keepdims=True)
    i1 = jnp.min(jnp.where(el == m1, lane, big), axis=1, keepdims=True)
    el2 = jnp.where(lane == i1, neg, el)
    m2 = jnp.max(el2, axis=1, keepdims=True)
    i2 = jnp.min(jnp.where(el2 == m2, lane, big), axis=1, keepdims=True)
    r = jnp.exp(m2 - m1)
    w1 = g_w / (1.0 + r)
    w2 = g_w * r / (1.0 + r)

    picks = ((lane == i1) | (lane == i2)).astype(BF16)
    r_i = lax.broadcasted_iota(jnp.int32, (tm, tm), 0)
    c_i = lax.broadcasted_iota(jnp.int32, (tm, tm), 1)
    before = (c_i < r_i).astype(BF16)
    excl = jnp.dot(before, picks, preferred_element_type=F32) + cnt_scr[...]
    rank1 = jnp.sum(jnp.where(lane == i1, excl, 0.0), axis=1, keepdims=True)
    rank2 = jnp.sum(jnp.where(lane == i2, excl, 0.0), axis=1, keepdims=True)
    cnt = cnt_scr[...] + jnp.sum(picks.astype(F32), axis=0, keepdims=True)
    cnt_scr[...] = cnt
    cnt_ref[...] = cnt
    route = jnp.where(lane == 0, i1.astype(F32), 0.0)
    route = jnp.where(lane == 1, i2.astype(F32), route)
    route = jnp.where(lane == 2, w1, route)
    route = jnp.where(lane == 3, w2, route)
    route = jnp.where(lane == 4, rank1, route)
    route_ref[...] = jnp.where(lane == 5, rank2, route)


def _mix_call(x, ys, g, wc, bc, wg, bg, cw, cb, wco, wab, wo, g2, wr, wrh, br):
    nb, seq, _ = x.shape
    T = nb * seq
    nl = seq // MIX_TM
    consts = (g, wc, bc, wg, bg, cw, cb, wco, wab, wo, g2, wr, wrh, br)
    tok = lambda b, l: (b * nl + l, 0)
    return pl.pallas_call(
        _mix_body,
        out_shape=(jax.ShapeDtypeStruct((T, D_MODEL), F32),
                   jax.ShapeDtypeStruct((T, D_MODEL // 2), jnp.int32),
                   jax.ShapeDtypeStruct((T, LANES), F32),
                   jax.ShapeDtypeStruct((1, LANES), F32)),
        grid=(nb, nl),
        in_specs=[pl.BlockSpec((1, MIX_TM, D_MODEL), lambda b, l: (b, l, 0)),
                  pl.BlockSpec((1, MIX_TM, SSM_WIDTH), lambda b, l: (b, l, 0))]
                 + [_const_spec(c.shape) for c in consts],
        out_specs=(pl.BlockSpec((MIX_TM, D_MODEL), tok),
                   pl.BlockSpec((MIX_TM, D_MODEL // 2), tok),
                   pl.BlockSpec((MIX_TM, LANES), tok),
                   pl.BlockSpec((1, LANES), lambda b, l: (0, 0))),
        scratch_shapes=[pltpu.VMEM((8, CONV_WIDTH), F32), pltpu.VMEM((1, LANES), F32)],
        compiler_params=pltpu.CompilerParams(
            dimension_semantics=("arbitrary", "arbitrary"), vmem_limit_bytes=VMEM_LIMIT),
        name="conv_glu_router",
    )(x, ys, *consts)


SC_CORES = 2
SC_SUBCORES = 16
SC_WORKERS = SC_CORES * SC_SUBCORES
SC_ROWS = 64


def _sc_mesh():
    return plsc.VectorSubcoreMesh(core_axis_name="c", subcore_axis_name="s")


def _sc_worker_id():
    return lax.axis_index("s") * SC_CORES + lax.axis_index("c")


def _sc_dispatch(rows, pos_a, pos_b, n_out):
    T, W = rows.shape
    per_w = T // SC_WORKERS
    nch = per_w // SC_ROWS
    assert per_w * SC_WORKERS == T and nch * SC_ROWS == per_w and nch % 2 == 0
    idx_a = pos_a.reshape(SC_WORKERS, nch, SC_ROWS)
    idx_b = pos_b.reshape(SC_WORKERS, nch, SC_ROWS)

    @functools.partial(
        pl.kernel, mesh=_sc_mesh(),
        out_type=jax.ShapeDtypeStruct((n_out, W), rows.dtype),
        scratch_types=[
            pltpu.VMEM((nch, SC_ROWS), jnp.int32),
            pltpu.VMEM((nch, SC_ROWS), jnp.int32),
            pltpu.VMEM((2, SC_ROWS, W), rows.dtype),
            pltpu.SemaphoreType.DMA((2,)),
            pltpu.SemaphoreType.DMA((2,)),
            pltpu.SemaphoreType.DMA((2,)),
        ],
        name="moe_dispatch",
    )
    def k(rows_hbm, ia_hbm, ib_hbm, out_hbm, ia_v, ib_v, buf, gsem, asem, bsem):
        wid = _sc_worker_id()
        base = wid * per_w
        pltpu.sync_copy(ia_hbm.at[wid], ia_v)
        pltpu.sync_copy(ib_hbm.at[wid], ib_v)

        def get(j, b):
            return pltpu.make_async_copy(
                rows_hbm.at[pl.ds(base + j * SC_ROWS, SC_ROWS)], buf.at[b], gsem.at[b])

        def put_a(j, b):
            return pltpu.make_async_copy(buf.at[b], out_hbm.at[ia_v.at[j]], asem.at[b])

        def put_b(j, b):
            return pltpu.make_async_copy(buf.at[b], out_hbm.at[ib_v.at[j]], bsem.at[b])

        get(0, 0).start()

        @pl.loop(0, nch, step=2)
        def _(j0):
            for b in range(2):
                j = j0 + b
                nb = 1 - b
                get(j, b).wait()

                @pl.when(j + 1 < nch)
                def _():
                    @pl.when(j >= 1)
                    def _():
                        put_a(j - 1, nb).wait()
                        put_b(j - 1, nb).wait()
                    get(j + 1, nb).start()

                put_a(j, b).start()
                put_b(j, b).start()

        for b in range(2):
            put_a(nch - 2 + b, b).wait()
            put_b(nch - 2 + b, b).wait()

    return k(rows, idx_a, idx_b)


def _sc_gather(table, idx):
    _, W = table.shape
    B = idx.shape[0]
    per_w = B // SC_WORKERS
    nch = per_w // SC_ROWS
    assert per_w * SC_WORKERS == B and nch * SC_ROWS == per_w and nch % 2 == 0
    idx3 = idx.reshape(SC_WORKERS, nch, SC_ROWS)

    @functools.partial(
        pl.kernel, mesh=_sc_mesh(),
        out_type=jax.ShapeDtypeStruct((B, W), table.dtype),
        scratch_types=[
            pltpu.VMEM((nch, SC_ROWS), jnp.int32),
            pltpu.VMEM((2, SC_ROWS, W), table.dtype),
            pltpu.SemaphoreType.DMA((2,)),
            pltpu.SemaphoreType.DMA((2,)),
        ],
        name="moe_combine_gather",
    )
    def k(table_hbm, idx_hbm, out_hbm, idx_v, buf, gsem, osem):
        wid = _sc_worker_id()
        base = wid * per_w
        pltpu.sync_copy(idx_hbm.at[wid], idx_v)

        def get(j, b):
            return pltpu.make_async_copy(table_hbm.at[idx_v.at[j]], buf.at[b], gsem.at[b])

        def put(j, b):
            return pltpu.make_async_copy(
                buf.at[b], out_hbm.at[pl.ds(base + j * SC_ROWS, SC_ROWS)], osem.at[b])

        get(0, 0).start()

        @pl.loop(0, nch, step=2)
        def _(j0):
            for b in range(2):
                j = j0 + b
                nb = 1 - b
                get(j, b).wait()

                @pl.when(j + 1 < nch)
                def _():
                    @pl.when(j >= 1)
                    def _():
                        put(j - 1, nb).wait()
                    get(j + 1, nb).start()

                put(j, b).start()

        for b in range(2):
            put(nch - 2 + b, b).wait()

    return k(table, idx3)


MOE_RT = 512


def _moe_body(te_ref, nt_ref, x_ref, wg_ref, wu_ref, wd_ref, o_ref):
    @pl.when(pl.program_id(0) < nt_ref[0])
    def _():
        half = D_MODEL // 2
        lo, hi = _unpack_bf16_pairs(x_ref[...])
        lo = lo.astype(BF16)
        hi = hi.astype(BF16)
        gate = (jnp.dot(lo, wg_ref[0, 0:half, :], preferred_element_type=F32)
                + jnp.dot(hi, wg_ref[0, half:D_MODEL, :], preferred_element_type=F32))
        up = (jnp.dot(lo, wu_ref[0, 0:half, :], preferred_element_type=F32)
              + jnp.dot(hi, wu_ref[0, half:D_MODEL, :], preferred_element_type=F32))
        hid = (jax.nn.silu(gate) * up).astype(BF16)
        y = jnp.dot(hid, wd_ref[0], preferred_element_type=F32)
        o_ref[...] = _pack_bf16_pairs(y.astype(BF16).astype(F32))


def _moe_call(tile_expert, n_tiles, xs, wg, wu, wd):
    R = xs.shape[0]
    half = D_MODEL // 2
    row_map = lambda i, te, nt: (jnp.minimum(i, nt[0] - 1), 0)
    w_map = lambda i, te, nt: (te[i], 0, 0)
    return pl.pallas_call(
        _moe_body,
        out_shape=jax.ShapeDtypeStruct((R, half), jnp.int32),
        grid_spec=pltpu.PrefetchScalarGridSpec(
            num_scalar_prefetch=2,
            grid=(R // MOE_RT,),
            in_specs=[pl.BlockSpec((MOE_RT, half), row_map),
                      pl.BlockSpec((1, D_MODEL, EXPERT_HIDDEN), w_map),
                      pl.BlockSpec((1, D_MODEL, EXPERT_HIDDEN), w_map),
                      pl.BlockSpec((1, EXPERT_HIDDEN, D_MODEL), w_map)],
            out_specs=pl.BlockSpec((MOE_RT, half), row_map)),
        compiler_params=pltpu.CompilerParams(
            dimension_semantics=("arbitrary",), vmem_limit_bytes=VMEM_LIMIT),
        name="moe_experts",
    )(tile_expert, n_tiles, xs, wg, wu, wd)


def _moe_plan(route, counts, n_rows):
    cnt = counts[0, :N_EXPERTS].astype(jnp.int32)
    tiles = (cnt + MOE_RT - 1) // MOE_RT
    tile_end = jnp.cumsum(tiles)
    n_tiles = tile_end[-1:]
    row_start = (tile_end - tiles) * MOE_RT
    ids = route[:, 0:2].astype(jnp.int32)
    ranks = route[:, 4:6].astype(jnp.int32)
    pos = jnp.take(row_start, ids) + ranks
    tile_id = jnp.minimum(jnp.arange(n_rows // MOE_RT, dtype=jnp.int32), n_tiles - 1)
    tile_expert = jnp.sum((tile_id[:, None] >= tile_end[None, :]).astype(jnp.int32), axis=1)
    return pos[:, 0], pos[:, 1], tile_expert, n_tiles


PLE_TM = 512


def _ple_body(x_ref, ya_ref, yb_ref, route_ref, p_ref, g3_ref, wpg_ref, bpg_ref, wple_ref, gf_ref,
              o_ref):
    ya = jnp.concatenate(_unpack_bf16_pairs(ya_ref[...]), axis=1)
    yb = jnp.concatenate(_unpack_bf16_pairs(yb_ref[...]), axis=1)
    route = route_ref[...]
    x2 = x_ref[...] + route[:, 2:3] * ya + route[:, 3:4] * yb
    h3 = _rms(x2, g3_ref[...]).astype(BF16)
    gate = jax.nn.sigmoid(jnp.dot(h3, wpg_ref[...], preferred_element_type=F32) + bpg_ref[...])
    pe = jnp.dot(p_ref[...].astype(BF16), wple_ref[...], preferred_element_type=F32)
    x3 = x2 + gate * pe
    o_ref[...] = _rms(x3, gf_ref[...])


def _ple_call(x1, y_picks, route, p, g3, wpg, bpg, wple, gf):
    T = x1.shape[0]
    nt = T // PLE_TM
    consts = (g3, wpg, bpg, wple, gf)
    return pl.pallas_call(
        _ple_body,
        out_shape=jax.ShapeDtypeStruct((T, D_MODEL), F32),
        grid=(nt,),
        in_specs=[pl.BlockSpec((PLE_TM, D_MODEL), lambda i: (i, 0)),
                  pl.BlockSpec((PLE_TM, D_MODEL // 2), lambda i: (i, 0)),
                  pl.BlockSpec((PLE_TM, D_MODEL // 2), lambda i: (i + nt, 0)),
                  pl.BlockSpec((PLE_TM, LANES), lambda i: (i, 0)),
                  pl.BlockSpec((PLE_TM, p.shape[1]), lambda i: (i, 0))]
                 + [_const_spec(c.shape) for c in consts],
        out_specs=pl.BlockSpec((PLE_TM, D_MODEL), lambda i: (i, 0)),
        compiler_params=pltpu.CompilerParams(
            dimension_semantics=("arbitrary",), vmem_limit_bytes=VMEM_LIMIT),
        name="ple_final",
    )(x1, y_picks, y_picks, route, p, *consts)


def _layer(x, p, norm_mix, w_in, b_in, lam_re, lam_im, log_dt, b_re, b_im, c_re, c_im, d_skip,
           w_glu_a, w_glu_b, conv_w, conv_b, w_conv_out, w_o, norm_ffn, w_rg, b_rg, w_re, b_re_r,
           w_eg, w_eu, w_ed, norm_ple, w_ple, w_pg, b_pg, norm_out):
    nb, seq, d = x.shape
    T = nb * seq
    s0 = SSM_WIDTH
    s3 = s0 + 3 * CONV_WIDTH
    row = lambda v: v.reshape(1, -1).astype(F32)

    kmat, mmat, nmat, aq_re, aq_im = _s5_operators(lam_re, lam_im, log_dt, b_re, b_im, c_re, c_im)
    ys = _s5_call(x, row(norm_mix), w_in[:, :s0].astype(BF16), row(b_in[:s0]),
                  mmat, kmat, nmat, aq_re, aq_im, row(d_skip))

    w_r = jnp.zeros((d, LANES), F32).at[:, :N_EXPERTS].set(w_re)
    w_r = w_r.at[:, N_EXPERTS:N_EXPERTS + N_EXPERT_GROUPS].set(w_rg)
    b_r = jnp.zeros((1, LANES), F32).at[0, :N_EXPERTS].set(b_re_r)
    b_r = b_r.at[0, N_EXPERTS:N_EXPERTS + N_EXPERT_GROUPS].set(b_rg)
    w_r_hi = w_r.astype(BF16)
    w_r_lo = (w_r - w_r_hi.astype(F32)).astype(BF16)

    x1, h2p, route, counts = _mix_call(
        x, ys, row(norm_mix),
        w_in[:, s0:s3].astype(BF16), row(b_in[s0:s3]),
        w_in[:, s3:].astype(BF16), row(b_in[s3:]),
        conv_w.astype(F32), row(conv_b), w_conv_out.astype(BF16),
        jnp.concatenate([w_glu_a, w_glu_b], axis=1).astype(BF16), w_o.astype(BF16),
        row(norm_ffn), jnp.concatenate([w_r_hi, w_r_lo], axis=1), w_r_hi, b_r)

    n_rows = 2 * T + N_EXPERTS * MOE_RT
    pos_a, pos_b, tile_expert, n_tiles = _moe_plan(route, counts, n_rows)
    xs = _sc_dispatch(h2p, pos_a, pos_b, n_rows)
    ysort = _moe_call(tile_expert, n_tiles, xs,
                      w_eg.astype(BF16), w_eu.astype(BF16), w_ed.astype(BF16))
    y_picks = _sc_gather(ysort, jnp.concatenate([pos_a, pos_b]))

    out = _ple_call(x1, y_picks, route, p.reshape(T, -1), row(norm_ple), w_pg.astype(BF16),
                    row(b_pg), w_ple.astype(BF16), row(norm_out))
    return out.reshape(nb, seq, d)


def kernel(x, p, norm_mix, w_in, b_in, ssm_lam_re, ssm_lam_im, ssm_log_dt, ssm_b_re, ssm_b_im, ssm_c_re, ssm_c_im, ssm_d, w_glu_a, w_glu_b, conv_w, conv_b, w_conv_out, w_o, norm_ffn, w_router_group, b_router_group, w_router_expert, b_router_expert, w_exp_gate, w_exp_up, w_exp_down, norm_ple, w_ple, w_ple_gate, b_ple_gate, norm_final):
    assert p.shape[0] == 1, "the final RMSNorm is fused into the (single) layer's last kernel"
    i = 0
    return _layer(x, p[i], norm_mix[i], w_in[i], b_in[i], ssm_lam_re[i], ssm_lam_im[i],
                  ssm_log_dt[i], ssm_b_re[i], ssm_b_im[i], ssm_c_re[i], ssm_c_im[i], ssm_d[i],
                  w_glu_a[i], w_glu_b[i], conv_w[i], conv_b[i], w_conv_out[i], w_o[i],
                  norm_ffn[i], w_router_group[i], b_router_group[i], w_router_expert[i],
                  b_router_expert[i], w_exp_gate[i], w_exp_up[i], w_exp_down[i], norm_ple[i],
                  w_ple[i], w_ple_gate[i], b_ple_gate[i], norm_final)
```

```python
import functools
import math

import jax
import jax.numpy as jnp
from jax import lax
from jax.experimental import pallas as pl
from jax.experimental.pallas import tpu as pltpu
from jax.experimental.pallas import tpu_sc as plsc

F32 = jnp.float32
BF16 = jnp.bfloat16

D_MODEL = 1024
SSM_WIDTH = 512
SSM_GROUP = 16
SSM_GROUPS = 32
SSM_STATE = 64
CONV_WIDTH = 512
N_EXPERT_GROUPS = 4
EXPERTS_PER_GROUP = 8
N_EXPERTS = 32
EXPERT_HIDDEN = 256
NORM_EPS = 1e-6

LANES = 128
Q = 8
GROUPS_PER_LANE_TILE = LANES // SSM_GROUP
N_LANE_TILES = SSM_WIDTH // LANES
STATE_LANES = GROUPS_PER_LANE_TILE * SSM_STATE
S5_TT = 64
VMEM_LIMIT = 56 * 1024 * 1024


def _rms(x, g):
    return x * lax.rsqrt(jnp.mean(x * x, axis=-1, keepdims=True) + NORM_EPS) * g


def _pack_bf16_pairs(a):
    w = a.shape[1] // 2
    lo = lax.shift_right_logical(lax.bitcast_convert_type(a[:, :w], jnp.int32), 16)
    hi = lax.bitcast_convert_type(a[:, w:], jnp.int32) & jnp.int32(-65536)
    return lo | hi


def _unpack_bf16_pairs(word):
    lo = lax.bitcast_convert_type(lax.shift_left(word, 16), F32)
    hi = lax.bitcast_convert_type(word & jnp.int32(-65536), F32)
    return lo, hi


def _const_spec(shape):
    n = len(shape)
    return pl.BlockSpec(shape, lambda *_: (0,) * n, pipeline_mode=pl.Buffered(1))


def _s5_operators(lam_re, lam_im, log_dt, b_re, b_im, c_re, c_im):
    G, P, H = SSM_GROUPS, SSM_STATE, SSM_GROUP
    J = N_LANE_TILES
    lr = lam_re.astype(F32)
    li = lam_im.astype(F32)
    dt = jnp.exp(log_dt.astype(F32))[:, None]

    def apow(n):
        n = n.astype(F32)[:, None, None]
        mag = jnp.exp(lr * dt * n)
        ang = li * dt * n
        return mag * jnp.cos(ang), mag * jnp.sin(ang)

    a1_re, a1_im = apow(jnp.ones((1,), F32))
    nr = a1_re[0] - 1.0
    ni = a1_im[0]
    den = lr * lr + li * li
    f_re = (nr * lr + ni * li) / den
    f_im = (ni * lr - nr * li) / den
    br = b_re.astype(F32)
    bi = b_im.astype(F32)
    bbar_re = f_re[:, :, None] * br - f_im[:, :, None] * bi
    bbar_im = f_re[:, :, None] * bi + f_im[:, :, None] * br
    to_rows = lambda v, perm: jnp.transpose(v, perm).reshape(H, G * P)
    bt_re = to_rows(bbar_re, (2, 0, 1))
    bt_im = to_rows(bbar_im, (2, 0, 1))
    ct_re = to_rows(c_re.astype(F32), (1, 0, 2))
    ct_im = to_rows(c_im.astype(F32), (1, 0, 2))
    ap_re, ap_im = apow(jnp.arange(Q + 1))
    ap_re = ap_re.reshape(Q + 1, G * P)
    ap_im = ap_im.reshape(Q + 1, G * P)

    blk = lambda r: pl.BlockSpec((r, STATE_LANES), lambda j: (0, j))
    mat = pl.BlockSpec((1, Q * LANES, Q * LANES), lambda j: (j, 0, 0))
    shape = jax.ShapeDtypeStruct((J, Q * LANES, Q * LANES), BF16)
    kmat, mmat, nmat = pl.pallas_call(
        _s5_ops_body,
        out_shape=(shape, shape, shape),
        grid=(J,),
        in_specs=[blk(Q + 1), blk(Q + 1), blk(H), blk(H), blk(H), blk(H)],
        out_specs=(mat, mat, mat),
        compiler_params=pltpu.CompilerParams(
            dimension_semantics=("arbitrary",), vmem_limit_bytes=VMEM_LIMIT),
        name="s5_operators",
    )(ap_re, ap_im, bt_re, bt_im, ct_re, ct_im)
    return kmat, mmat, nmat, ap_re[Q:Q + 1], ap_im[Q:Q + 1]


def _s5_ops_body(apr_ref, api_ref, btr_ref, bti_ref, ctr_ref, cti_ref, k_ref, m_ref, n_ref):
    ri = lax.broadcasted_iota(jnp.int32, (LANES, STATE_LANES), 0)
    li = lax.broadcasted_iota(jnp.int32, (LANES, STATE_LANES), 1)
    same_group = (ri // SSM_GROUP) == (li // SSM_STATE)

    def expand(ref):
        tiled = jnp.concatenate([ref[...]] * GROUPS_PER_LANE_TILE, axis=0)
        return jnp.where(same_group, tiled, 0.0)

    b_re, b_im, c_re, c_im = expand(btr_ref), expand(bti_ref), expand(ctr_ref), expand(cti_ref)

    def cmul(n, x_re, x_im):
        a_re = apr_ref[n:n + 1, :]
        a_im = api_ref[n:n + 1, :]
        return a_re * x_re - a_im * x_im, a_re * x_im + a_im * x_re

    m_blocks = []
    for k in range(Q):
        g_re, g_im = cmul(Q - 1 - k, b_re, b_im)
        m_blocks.append(jnp.concatenate([g_re, g_im], axis=1))
    m = jnp.concatenate(m_blocks, axis=0)

    nt_blocks = []
    for t in range(Q):
        g_re, g_im = cmul(t + 1, c_re, c_im)
        nt_blocks.append(jnp.concatenate([g_re, -g_im], axis=1))
    nt = jnp.concatenate(nt_blocks, axis=0)

    n0t = jnp.concatenate([c_re, -c_im], axis=1)
    p = lax.dot_general(m, n0t, (((1,), (1,)), ((), ())),
                        precision=lax.Precision.HIGHEST, preferred_element_type=F32)
    zeros = jnp.zeros((LANES, LANES), F32)
    cols = []
    for t in range(Q):
        cols.append(jnp.concatenate(
            [p[(Q - 1 - (t - k)) * LANES:(Q - (t - k)) * LANES, :] if t >= k else zeros
             for k in range(Q)], axis=0))
    k_ref[0] = jnp.concatenate(cols, axis=1).astype(BF16)
    m_ref[0] = m.astype(BF16)
    n_ref[0] = nt.T.astype(BF16)


def _s5_body(x_ref, g_ref, wu_ref, bu_ref, m_ref, k_ref, n_ref, aqr_ref, aqi_ref, d_ref,
             o_ref, u_scr, y_scr, z_scr, ss_scr, carry_scr):
    nb = x_ref.shape[0]
    rows = nb * S5_TT
    nchunk = rows // Q

    @pl.when(pl.program_id(0) == 0)
    def _():
        carry_scr[...] = jnp.zeros_like(carry_scr)

    x = x_ref[...].reshape(rows, D_MODEL)
    h = _rms(x, g_ref[...]).astype(BF16)
    u = jnp.dot(h, wu_ref[...], preferred_element_type=F32) + bu_ref[...]
    for j in range(N_LANE_TILES):
        u_scr[j] = u[:, j * LANES:(j + 1) * LANES]

    n_st = STATE_LANES // LANES
    cpt = S5_TT // Q
    for j in range(N_LANE_TILES):
        xj = jnp.concatenate(
            [u_scr[j, pl.ds(k, nchunk, stride=Q), :] for k in range(Q)], axis=1).astype(BF16)
        z = jnp.dot(xj, m_ref[j], preferred_element_type=F32)
        for i in range(2 * n_st):
            z_scr[i] = z[:, i * LANES:(i + 1) * LANES]
        aqr = jnp.broadcast_to(aqr_ref[:, pl.ds(j * STATE_LANES, STATE_LANES)], (nb, STATE_LANES))
        aqi = jnp.broadcast_to(aqi_ref[:, pl.ds(j * STATE_LANES, STATE_LANES)], (nb, STATE_LANES))
        s_re = carry_scr[j, :, pl.ds(0, STATE_LANES)]
        s_im = carry_scr[j, :, pl.ds(STATE_LANES, STATE_LANES)]
        for c in range(cpt):
            seq_rows = pl.ds(c, nb, stride=cpt)
            for i in range(n_st):
                ss_scr[i, seq_rows, :] = s_re[:, i * LANES:(i + 1) * LANES]
                ss_scr[n_st + i, seq_rows, :] = s_im[:, i * LANES:(i + 1) * LANES]
            z_re = jnp.concatenate([z_scr[i, seq_rows, :] for i in range(n_st)], axis=1)
            z_im = jnp.concatenate([z_scr[n_st + i, seq_rows, :] for i in range(n_st)], axis=1)
            s_re, s_im = (aqr * s_re - aqi * s_im + z_re,
                          aqr * s_im + aqi * s_re + z_im)
        carry_scr[j, :, pl.ds(0, STATE_LANES)] = s_re
        carry_scr[j, :, pl.ds(STATE_LANES, STATE_LANES)] = s_im
        ss = jnp.concatenate([ss_scr[i] for i in range(2 * n_st)], axis=1).astype(BF16)
        yj = (jnp.dot(xj, k_ref[j], preferred_element_type=F32)
              + jnp.dot(ss, n_ref[j], preferred_element_type=F32))
        for k in range(Q):
            y_scr[j, pl.ds(k, nchunk, stride=Q), :] = yj[:, k * LANES:(k + 1) * LANES]

    for j in range(N_LANE_TILES):
        lanes = pl.ds(j * LANES, LANES)
        y = y_scr[j] + d_ref[:, lanes] * u_scr[j]
        o_ref[:, :, lanes] = jax.nn.gelu(y).astype(BF16).reshape(nb, S5_TT, LANES)


def _s5_call(x, g, wu, bu, mmat, kmat, nmat, aq_re, aq_im, d_skip):
    nb, seq, _ = x.shape
    rows = nb * S5_TT
    nchunk = rows // Q
    return pl.pallas_call(
        _s5_body,
        out_shape=jax.ShapeDtypeStruct((nb, seq, SSM_WIDTH), BF16),
        grid=(seq // S5_TT,),
        in_specs=[
            pl.BlockSpec((nb, S5_TT, D_MODEL), lambda i: (0, i, 0)),
            _const_spec(g.shape), _const_spec(wu.shape), _const_spec(bu.shape),
            _const_spec(mmat.shape), _const_spec(kmat.shape), _const_spec(nmat.shape),
            _const_spec(aq_re.shape), _const_spec(aq_im.shape), _const_spec(d_skip.shape),
        ],
        out_specs=pl.BlockSpec((nb, S5_TT, SSM_WIDTH), lambda i: (0, i, 0)),
        scratch_shapes=[
            pltpu.VMEM((N_LANE_TILES, rows, LANES), F32),
            pltpu.VMEM((N_LANE_TILES, rows, LANES), F32),
            pltpu.VMEM((2 * STATE_LANES // LANES, nchunk, LANES), F32),
            pltpu.VMEM((2 * STATE_LANES // LANES, nchunk, LANES), F32),
            pltpu.VMEM((N_LANE_TILES, nb, 2 * STATE_LANES), F32),
        ],
        compiler_params=pltpu.CompilerParams(
            dimension_semantics=("arbitrary",), vmem_limit_bytes=VMEM_LIMIT),
        name="s5_mixer",
    )(x, g, wu, bu, mmat, kmat, nmat, aq_re, aq_im, d_skip)


MIX_TM = 512


def _mix_body(x_ref, ys_ref, g_ref, wc_ref, bc_ref, wg_ref, bg_ref, cw_ref, cb_ref, wco_ref,
              wab_ref, wo_ref, g2_ref, wr_ref, wrh_ref, br_ref,
              x1_ref, h2p_ref, route_ref, route_t_ref, cnt_ref, carry_scr, cnt_scr):
    tm = x_ref.shape[1]

    @pl.when(pl.program_id(1) == 0)
    def _():
        carry_scr[...] = jnp.zeros_like(carry_scr)

    @pl.when((pl.program_id(0) == 0) & (pl.program_id(1) == 0))
    def _():
        cnt_scr[...] = jnp.zeros_like(cnt_scr)

    x = x_ref[0]
    h = _rms(x, g_ref[...]).astype(BF16)

    zc = jnp.dot(h, wc_ref[...], preferred_element_type=F32) + bc_ref[...]
    c_b = zc[:, 0:CONV_WIDTH]
    cv = zc[:, CONV_WIDTH:2 * CONV_WIDTH] * zc[:, 2 * CONV_WIDTH:3 * CONV_WIDTH]
    row = lax.broadcasted_iota(jnp.int32, (tm, CONV_WIDTH), 0)
    last1 = carry_scr[7:8, :]
    last2 = carry_scr[6:7, :]
    p1 = jnp.where(row == 0, last1, pltpu.roll(cv, 1, axis=0))
    p2 = jnp.where(row == 0, last2, jnp.where(row == 1, last1, pltpu.roll(cv, 2, axis=0)))
    carry_scr[...] = cv[tm - 8:tm, :]
    conv = cw_ref[0:1, :] * p2 + cw_ref[1:2, :] * p1 + cw_ref[2:3, :] * cv + cb_ref[...]
    y_b = jnp.dot((c_b * conv).astype(BF16), wco_ref[...], preferred_element_type=F32)

    yab = jnp.dot(ys_ref[0], wab_ref[...], preferred_element_type=F32)
    y_a = yab[:, 0:D_MODEL] * jax.nn.sigmoid(yab[:, D_MODEL:2 * D_MODEL])

    zg = jnp.dot(h, wg_ref[...], preferred_element_type=F32) + bg_ref[...]
    mix = (jax.nn.sigmoid(zg[:, 0:D_MODEL]) * y_a
           + jax.nn.sigmoid(zg[:, D_MODEL:2 * D_MODEL]) * y_b)
    x1 = x + jnp.dot(mix.astype(BF16), wo_ref[...], preferred_element_type=F32)
    x1_ref[...] = x1

    h2 = _rms(x1, g2_ref[...])
    h2_hi = h2.astype(BF16)
    h2p_ref[...] = _pack_bf16_pairs(h2_hi.astype(F32))
    h2_lo = (h2 - h2_hi.astype(F32)).astype(BF16)
    lg2 = jnp.dot(h2_hi, wr_ref[...], preferred_element_type=F32)
    logits = (lg2[:, 0:LANES] + lg2[:, LANES:2 * LANES]
              + jnp.dot(h2_lo, wrh_ref[...], preferred_element_type=F32) + br_ref[...])

    lane = lax.broadcasted_iota(jnp.int32, (tm, LANES), 1)
    neg = jnp.float32(-jnp.inf)
    big = jnp.int32(1 << 20)
    is_g = (lane >= N_EXPERTS) & (lane < N_EXPERTS + N_EXPERT_GROUPS)
    gl = jnp.where(is_g, logits, neg)
    gmax = jnp.max(gl, axis=1, keepdims=True)
    g_w = 1.0 / jnp.sum(jnp.exp(gl - gmax), axis=1, keepdims=True)
    g_idx = jnp.min(jnp.where(gl == gmax, lane - N_EXPERTS, big), axis=1, keepdims=True)
    lo = g_idx * EXPERTS_PER_GROUP
    el = jnp.where((lane >= lo) & (lane < lo + EXPERTS_PER_GROUP), logits, neg)
    m1 = jnp.max(el, axis=1, keepdims=True)
    i1 = jnp.min(jnp.where(el == m1, lane, big), axis=1, keepdims=True)
    el2 = jnp.where(lane == i1, neg, el)
    m2 = jnp.max(el2, axis=1, keepdims=True)
    i2 = jnp.min(jnp.where(el2 == m2, lane, big), axis=1, keepdims=True)
    r = jnp.exp(m2 - m1)
    w1 = g_w / (1.0 + r)
    w2 = g_w * r / (1.0 + r)

    picks = ((lane == i1) | (lane == i2)).astype(BF16)
    r_i = lax.broadcasted_iota(jnp.int32, (tm, tm), 0)
    c_i = lax.broadcasted_iota(jnp.int32, (tm, tm), 1)
    before = (c_i < r_i).astype(BF16)
    excl = jnp.dot(before, picks, preferred_element_type=F32) + cnt_scr[...]
    rank1 = jnp.sum(jnp.where(lane == i1, excl, 0.0), axis=1, keepdims=True)
    rank2 = jnp.sum(jnp.where(lane == i2, excl, 0.0), axis=1, keepdims=True)
    cnt = cnt_scr[...] + jnp.sum(picks.astype(F32), axis=0, keepdims=True)
    cnt_scr[...] = cnt
    cnt_ref[...] = cnt
    route = jnp.where(lane == 0, i1.astype(F32), 0.0)
    route = jnp.where(lane == 1, i2.astype(F32), route)
    route = jnp.where(lane == 2, w1, route)
    route = jnp.where(lane == 3, w2, route)
    route = jnp.where(lane == 4, rank1, route)
    route = jnp.where(lane == 5, rank2, route)
    route_ref[...] = route
    route_t_ref[...] = route.T[0:8, :]


def _mix_call(x, ys, g, wc, bc, wg, bg, cw, cb, wco, wab, wo, g2, wr, wrh, br):
    nb, seq, _ = x.shape
    T = nb * seq
    nl = seq // MIX_TM
    consts = (g, wc, bc, wg, bg, cw, cb, wco, wab, wo, g2, wr, wrh, br)
    tok = lambda b, l: (b * nl + l, 0)
    return pl.pallas_call(
        _mix_body,
        out_shape=(jax.ShapeDtypeStruct((T, D_MODEL), F32),
                   jax.ShapeDtypeStruct((T, D_MODEL // 2), jnp.int32),
                   jax.ShapeDtypeStruct((T, LANES), F32),
                   jax.ShapeDtypeStruct((8, T), F32),
                   jax.ShapeDtypeStruct((1, LANES), F32)),
        grid=(nb, nl),
        in_specs=[pl.BlockSpec((1, MIX_TM, D_MODEL), lambda b, l: (b, l, 0)),
                  pl.BlockSpec((1, MIX_TM, SSM_WIDTH), lambda b, l: (b, l, 0))]
                 + [_const_spec(c.shape) for c in consts],
        out_specs=(pl.BlockSpec((MIX_TM, D_MODEL), tok),
                   pl.BlockSpec((MIX_TM, D_MODEL // 2), tok),
                   pl.BlockSpec((MIX_TM, LANES), tok),
                   pl.BlockSpec((8, MIX_TM), lambda b, l: (0, b * nl + l)),
                   pl.BlockSpec((1, LANES), lambda b, l: (0, 0))),
        scratch_shapes=[pltpu.VMEM((8, CONV_WIDTH), F32), pltpu.VMEM((1, LANES), F32)],
        compiler_params=pltpu.CompilerParams(
            dimension_semantics=("arbitrary", "arbitrary"), vmem_limit_bytes=VMEM_LIMIT),
        name="conv_glu_router",
    )(x, ys, *consts)


SC_CORES = 2
SC_SUBCORES = 16
SC_WORKERS = SC_CORES * SC_SUBCORES
SC_ROWS = 64


def _sc_mesh():
    return plsc.VectorSubcoreMesh(core_axis_name="c", subcore_axis_name="s")


def _sc_worker_id():
    return lax.axis_index("s") * SC_CORES + lax.axis_index("c")


def _sc_dispatch(rows, pos_a, pos_b, n_out):
    T, W = rows.shape
    per_w = T // SC_WORKERS
    nch = per_w // SC_ROWS
    assert per_w * SC_WORKERS == T and nch * SC_ROWS == per_w and nch % 2 == 0
    idx_a = pos_a.reshape(SC_WORKERS, nch, SC_ROWS)
    idx_b = pos_b.reshape(SC_WORKERS, nch, SC_ROWS)

    @functools.partial(
        pl.kernel, mesh=_sc_mesh(),
        out_type=jax.ShapeDtypeStruct((n_out, W), rows.dtype),
        scratch_types=[
            pltpu.VMEM((nch, SC_ROWS), jnp.int32),
            pltpu.VMEM((nch, SC_ROWS), jnp.int32),
            pltpu.VMEM((2, SC_ROWS, W), rows.dtype),
            pltpu.SemaphoreType.DMA((2,)),
            pltpu.SemaphoreType.DMA((2,)),
            pltpu.SemaphoreType.DMA((2,)),
        ],
        name="moe_dispatch",
    )
    def k(rows_hbm, ia_hbm, ib_hbm, out_hbm, ia_v, ib_v, buf, gsem, asem, bsem):
        wid = _sc_worker_id()
        base = wid * per_w
        pltpu.sync_copy(ia_hbm.at[wid], ia_v)
        pltpu.sync_copy(ib_hbm.at[wid], ib_v)

        def get(j, b):
            return pltpu.make_async_copy(
                rows_hbm.at[pl.ds(base + j * SC_ROWS, SC_ROWS)], buf.at[b], gsem.at[b])

        def put_a(j, b):
            return pltpu.make_async_copy(buf.at[b], out_hbm.at[ia_v.at[j]], asem.at[b])

        def put_b(j, b):
            return pltpu.make_async_copy(buf.at[b], out_hbm.at[ib_v.at[j]], bsem.at[b])

        get(0, 0).start()

        @pl.loop(0, nch, step=2)
        def _(j0):
            for b in range(2):
                j = j0 + b
                nb = 1 - b
                get(j, b).wait()

                @pl.when(j + 1 < nch)
                def _():
                    @pl.when(j >= 1)
                    def _():
                        put_a(j - 1, nb).wait()
                        put_b(j - 1, nb).wait()
                    get(j + 1, nb).start()

                put_a(j, b).start()
                put_b(j, b).start()

        for b in range(2):
            put_a(nch - 2 + b, b).wait()
            put_b(nch - 2 + b, b).wait()

    return k(rows, idx_a, idx_b)


def _sc_gather(table, idx):
    _, W = table.shape
    B = idx.shape[0]
    per_w = B // SC_WORKERS
    nch = per_w // SC_ROWS
    assert per_w * SC_WORKERS == B and nch * SC_ROWS == per_w and nch % 2 == 0
    idx3 = idx.reshape(SC_WORKERS, nch, SC_ROWS)

    @functools.partial(
        pl.kernel, mesh=_sc_mesh(),
        out_type=jax.ShapeDtypeStruct((B, W), table.dtype),
        scratch_types=[
            pltpu.VMEM((nch, SC_ROWS), jnp.int32),
            pltpu.VMEM((2, SC_ROWS, W), table.dtype),
            pltpu.SemaphoreType.DMA((2,)),
            pltpu.SemaphoreType.DMA((2,)),
        ],
        name="moe_combine_gather",
    )
    def k(table_hbm, idx_hbm, out_hbm, idx_v, buf, gsem, osem):
        wid = _sc_worker_id()
        base = wid * per_w
        pltpu.sync_copy(idx_hbm.at[wid], idx_v)

        def get(j, b):
            return pltpu.make_async_copy(table_hbm.at[idx_v.at[j]], buf.at[b], gsem.at[b])

        def put(j, b):
            return pltpu.make_async_copy(
                buf.at[b], out_hbm.at[pl.ds(base + j * SC_ROWS, SC_ROWS)], osem.at[b])

        get(0, 0).start()

        @pl.loop(0, nch, step=2)
        def _(j0):
            for b in range(2):
                j = j0 + b
                nb = 1 - b
                get(j, b).wait()

                @pl.when(j + 1 < nch)
                def _():
                    @pl.when(j >= 1)
                    def _():
                        put(j - 1, nb).wait()
                    get(j + 1, nb).start()

                put(j, b).start()

        for b in range(2):
            put(nch - 2 + b, b).wait()

    return k(table, idx3)


MOE_RT = 512


def _moe_body(te_ref, nt_ref, x_ref, wg_ref, wu_ref, wd_ref, o_ref, wg_scr, wu_scr, wd_scr):
    i = pl.program_id(0)
    prev = jnp.maximum(i - 1, 0)

    @pl.when((i == 0) | (te_ref[i] != te_ref[prev]))
    def _():
        wg_scr[...] = wg_ref[0].astype(BF16)
        wu_scr[...] = wu_ref[0].astype(BF16)
        wd_scr[...] = wd_ref[0].astype(BF16)

    @pl.when(i < nt_ref[0])
    def _():
        half = D_MODEL // 2
        lo, hi = _unpack_bf16_pairs(x_ref[...])
        lo = lo.astype(BF16)
        hi = hi.astype(BF16)
        gate = (jnp.dot(lo, wg_scr[0:half, :], preferred_element_type=F32)
                + jnp.dot(hi, wg_scr[half:D_MODEL, :], preferred_element_type=F32))
        up = (jnp.dot(lo, wu_scr[0:half, :], preferred_element_type=F32)
              + jnp.dot(hi, wu_scr[half:D_MODEL, :], preferred_element_type=F32))
        hid = (jax.nn.silu(gate) * up).astype(BF16)
        y = jnp.dot(hid, wd_scr[...], preferred_element_type=F32)
        o_ref[...] = _pack_bf16_pairs(y.astype(BF16).astype(F32))


def _moe_call(tile_expert, n_tiles, xs, wg, wu, wd):
    R = xs.shape[0]
    half = D_MODEL // 2
    row_map = lambda i, te, nt: (jnp.minimum(i, nt[0] - 1), 0)
    w_map = lambda i, te, nt: (te[i], 0, 0)
    return pl.pallas_call(
        _moe_body,
        out_shape=jax.ShapeDtypeStruct((R, half), jnp.int32),
        grid_spec=pltpu.PrefetchScalarGridSpec(
            num_scalar_prefetch=2,
            grid=(R // MOE_RT,),
            in_specs=[pl.BlockSpec((MOE_RT, half), row_map),
                      pl.BlockSpec((1, D_MODEL, EXPERT_HIDDEN), w_map),
                      pl.BlockSpec((1, D_MODEL, EXPERT_HIDDEN), w_map),
                      pl.BlockSpec((1, EXPERT_HIDDEN, D_MODEL), w_map)],
            out_specs=pl.BlockSpec((MOE_RT, half), row_map),
            scratch_shapes=[pltpu.VMEM((D_MODEL, EXPERT_HIDDEN), BF16),
                            pltpu.VMEM((D_MODEL, EXPERT_HIDDEN), BF16),
                            pltpu.VMEM((EXPERT_HIDDEN, D_MODEL), BF16)]),
        compiler_params=pltpu.CompilerParams(
            dimension_semantics=("arbitrary",), vmem_limit_bytes=VMEM_LIMIT),
        name="moe_experts",
    )(tile_expert, n_tiles, xs, wg, wu, wd)


def _moe_plan(route_t, counts, n_rows):
    cnt = counts[0, :N_EXPERTS].astype(jnp.int32)
    tiles = (cnt + MOE_RT - 1) // MOE_RT
    tile_end = jnp.cumsum(tiles)
    n_tiles = tile_end[-1:]
    row_start = (tile_end - tiles) * MOE_RT
    ids = route_t[0:2].astype(jnp.int32)
    ranks = route_t[4:6].astype(jnp.int32)
    pos = jnp.take(row_start, ids) + ranks
    tile_id = jnp.minimum(jnp.arange(n_rows // MOE_RT, dtype=jnp.int32), n_tiles - 1)
    tile_expert = jnp.sum((tile_id[:, None] >= tile_end[None, :]).astype(jnp.int32), axis=1)
    return pos, tile_expert, n_tiles


PLE_TM = 512


def _ple_body(x_ref, ya_ref, yb_ref, route_ref, p_ref, g3_ref, wpg_ref, bpg_ref, wple_ref, gf_ref,
              o_ref):
    ya = jnp.concatenate(_unpack_bf16_pairs(ya_ref[...]), axis=1)
    yb = jnp.concatenate(_unpack_bf16_pairs(yb_ref[...]), axis=1)
    route = route_ref[...]
    x2 = x_ref[...] + route[:, 2:3] * ya + route[:, 3:4] * yb
    h3 = _rms(x2, g3_ref[...]).astype(BF16)
    gate = jax.nn.sigmoid(jnp.dot(h3, wpg_ref[...], preferred_element_type=F32) + bpg_ref[...])
    pe = jnp.dot(p_ref[...].astype(BF16), wple_ref[...], preferred_element_type=F32)
    x3 = x2 + gate * pe
    o_ref[...] = _rms(x3, gf_ref[...])


def _ple_call(x1, y_picks, route, p, g3, wpg, bpg, wple, gf):
    T = x1.shape[0]
    nt = T // PLE_TM
    consts = (g3, wpg, bpg, wple, gf)
    return pl.pallas_call(
        _ple_body,
        out_shape=jax.ShapeDtypeStruct((T, D_MODEL), F32),
        grid=(nt,),
        in_specs=[pl.BlockSpec((PLE_TM, D_MODEL), lambda i: (i, 0)),
                  pl.BlockSpec((PLE_TM, D_MODEL // 2), lambda i: (i, 0)),
                  pl.BlockSpec((PLE_TM, D_MODEL // 2), lambda i: (i + nt, 0)),
                  pl.BlockSpec((PLE_TM, LANES), lambda i: (i, 0)),
                  pl.BlockSpec((PLE_TM, p.shape[1]), lambda i: (i, 0))]
                 + [_const_spec(c.shape) for c in consts],
        out_specs=pl.BlockSpec((PLE_TM, D_MODEL), lambda i: (i, 0)),
        compiler_params=pltpu.CompilerParams(
            dimension_semantics=("arbitrary",), vmem_limit_bytes=VMEM_LIMIT),
        name="ple_final",
    )(x1, y_picks, y_picks, route, p, *consts)


def _layer(x, p, norm_mix, w_in, b_in, lam_re, lam_im, log_dt, b_re, b_im, c_re, c_im, d_skip,
           w_glu_a, w_glu_b, conv_w, conv_b, w_conv_out, w_o, norm_ffn, w_rg, b_rg, w_re, b_re_r,
           w_eg, w_eu, w_ed, norm_ple, w_ple, w_pg, b_pg, norm_out):
    nb, seq, d = x.shape
    T = nb * seq
    s0 = SSM_WIDTH
    s3 = s0 + 3 * CONV_WIDTH
    row = lambda v: v.reshape(1, -1).astype(F32)

    kmat, mmat, nmat, aq_re, aq_im = _s5_operators(lam_re, lam_im, log_dt, b_re, b_im, c_re, c_im)
    ys = _s5_call(x, row(norm_mix), w_in[:, :s0].astype(BF16), row(b_in[:s0]),
                  mmat, kmat, nmat, aq_re, aq_im, row(d_skip))

    w_r = jnp.zeros((d, LANES), F32).at[:, :N_EXPERTS].set(w_re)
    w_r = w_r.at[:, N_EXPERTS:N_EXPERTS + N_EXPERT_GROUPS].set(w_rg)
    b_r = jnp.zeros((1, LANES), F32).at[0, :N_EXPERTS].set(b_re_r)
    b_r = b_r.at[0, N_EXPERTS:N_EXPERTS + N_EXPERT_GROUPS].set(b_rg)
    w_r_hi = w_r.astype(BF16)
    w_r_lo = (w_r - w_r_hi.astype(F32)).astype(BF16)

    x1, h2p, route, route_t, counts = _mix_call(
        x, ys, row(norm_mix),
        w_in[:, s0:s3].astype(BF16), row(b_in[s0:s3]),
        w_in[:, s3:].astype(BF16), row(b_in[s3:]),
        conv_w.astype(F32), row(conv_b), w_conv_out.astype(BF16),
        jnp.concatenate([w_glu_a, w_glu_b], axis=1).astype(BF16), w_o.astype(BF16),
        row(norm_ffn), jnp.concatenate([w_r_hi, w_r_lo], axis=1), w_r_hi, b_r)

    n_rows = 2 * T + N_EXPERTS * MOE_RT
    pos, tile_expert, n_tiles = _moe_plan(route_t, counts, n_rows)
    xs = _sc_dispatch(h2p, pos[0], pos[1], n_rows)
    ysort = _moe_call(tile_expert, n_tiles, xs, w_eg, w_eu, w_ed)
    y_picks = _sc_gather(ysort, pos.reshape(-1))

    out = _ple_call(x1, y_picks, route, p.reshape(T, -1), row(norm_ple), w_pg.astype(BF16),
                    row(b_pg), w_ple.astype(BF16), row(norm_out))
    return out.reshape(nb, seq, d)


def kernel(x, p, norm_mix, w_in, b_in, ssm_lam_re, ssm_lam_im, ssm_log_dt, ssm_b_re, ssm_b_im, ssm_c_re, ssm_c_im, ssm_d, w_glu_a, w_glu_b, conv_w, conv_b, w_conv_out, w_o, norm_ffn, w_router_group, b_router_group, w_router_expert, b_router_expert, w_exp_gate, w_exp_up, w_exp_down, norm_ple, w_ple, w_ple_gate, b_ple_gate, norm_final):
    assert p.shape[0] == 1, "the final RMSNorm is fused into the (single) layer's last kernel"
    i = 0
    return _layer(x, p[i], norm_mix[i], w_in[i], b_in[i], ssm_lam_re[i], ssm_lam_im[i],
                  ssm_log_dt[i], ssm_b_re[i], ssm_b_im[i], ssm_c_re[i], ssm_c_im[i], ssm_d[i],
                  w_glu_a[i], w_glu_b[i], conv_w[i], conv_b[i], w_conv_out[i], w_o[i],
                  norm_ffn[i], w_router_group[i], b_router_group[i], w_router_expert[i],
                  b_router_expert[i], w_exp_gate[i], w_exp_up[i], w_exp_down[i], norm_ple[i],
                  w_ple[i], w_ple_gate[i], b_ple_gate[i], norm_final)
```

```python
import functools
import math

import jax
import jax.numpy as jnp
from jax import lax
from jax.experimental import pallas as pl
from jax.experimental.pallas import tpu as pltpu
from jax.experimental.pallas import tpu_sc as plsc

F32 = jnp.float32
BF16 = jnp.bfloat16

D_MODEL = 1024
SSM_WIDTH = 512
SSM_GROUP = 16
SSM_GROUPS = 32
SSM_STATE = 64
CONV_WIDTH = 512
N_EXPERT_GROUPS = 4
EXPERTS_PER_GROUP = 8
N_EXPERTS = 32
EXPERT_HIDDEN = 256
NORM_EPS = 1e-6

LANES = 128
Q = 8
GROUPS_PER_LANE_TILE = LANES // SSM_GROUP
N_LANE_TILES = SSM_WIDTH // LANES
STATE_LANES = GROUPS_PER_LANE_TILE * SSM_STATE
S5_TT = 64
VMEM_LIMIT = 56 * 1024 * 1024


def _rms(x, g):
    return x * lax.rsqrt(jnp.mean(x * x, axis=-1, keepdims=True) + NORM_EPS) * g


def _pack_bf16_pairs(a):
    w = a.shape[1] // 2
    lo = lax.shift_right_logical(lax.bitcast_convert_type(a[:, :w], jnp.int32), 16)
    hi = lax.bitcast_convert_type(a[:, w:], jnp.int32) & jnp.int32(-65536)
    return lo | hi


def _unpack_bf16_pairs(word):
    lo = lax.bitcast_convert_type(lax.shift_left(word, 16), F32)
    hi = lax.bitcast_convert_type(word & jnp.int32(-65536), F32)
    return lo, hi


def _const_spec(shape):
    n = len(shape)
    return pl.BlockSpec(shape, lambda *_: (0,) * n, pipeline_mode=pl.Buffered(1))


def _s5_operators(lam_re, lam_im, log_dt, b_re, b_im, c_re, c_im):
    G, P, H = SSM_GROUPS, SSM_STATE, SSM_GROUP
    J = N_LANE_TILES
    lr = lam_re.astype(F32)
    li = lam_im.astype(F32)
    dt = jnp.exp(log_dt.astype(F32))[:, None]

    def apow(n):
        n = n.astype(F32)[:, None, None]
        mag = jnp.exp(lr * dt * n)
        ang = li * dt * n
        return mag * jnp.cos(ang), mag * jnp.sin(ang)

    a1_re, a1_im = apow(jnp.ones((1,), F32))
    nr = a1_re[0] - 1.0
    ni = a1_im[0]
    den = lr * lr + li * li
    f_re = (nr * lr + ni * li) / den
    f_im = (ni * lr - nr * li) / den
    br = b_re.astype(F32)
    bi = b_im.astype(F32)
    bbar_re = f_re[:, :, None] * br - f_im[:, :, None] * bi
    bbar_im = f_re[:, :, None] * bi + f_im[:, :, None] * br
    to_rows = lambda v, perm: jnp.transpose(v, perm).reshape(H, G * P)
    bt_re = to_rows(bbar_re, (2, 0, 1))
    bt_im = to_rows(bbar_im, (2, 0, 1))
    ct_re = to_rows(c_re.astype(F32), (1, 0, 2))
    ct_im = to_rows(c_im.astype(F32), (1, 0, 2))
    ap_re, ap_im = apow(jnp.arange(Q + 1))
    ap_re = ap_re.reshape(Q + 1, G * P)
    ap_im = ap_im.reshape(Q + 1, G * P)

    blk = lambda r: pl.BlockSpec((r, STATE_LANES), lambda j: (0, j))
    mat = pl.BlockSpec((1, Q * LANES, Q * LANES), lambda j: (j, 0, 0))
    shape = jax.ShapeDtypeStruct((J, Q * LANES, Q * LANES), BF16)
    kmat, mmat, nmat = pl.pallas_call(
        _s5_ops_body,
        out_shape=(shape, shape, shape),
        grid=(J,),
        in_specs=[blk(Q + 1), blk(Q + 1), blk(H), blk(H), blk(H), blk(H)],
        out_specs=(mat, mat, mat),
        compiler_params=pltpu.CompilerParams(
            dimension_semantics=("arbitrary",), vmem_limit_bytes=VMEM_LIMIT),
        name="s5_operators",
    )(ap_re, ap_im, bt_re, bt_im, ct_re, ct_im)
    return kmat, mmat, nmat, ap_re[Q:Q + 1], ap_im[Q:Q + 1]


def _s5_ops_body(apr_ref, api_ref, btr_ref, bti_ref, ctr_ref, cti_ref, k_ref, m_ref, n_ref):
    ri = lax.broadcasted_iota(jnp.int32, (LANES, STATE_LANES), 0)
    li = lax.broadcasted_iota(jnp.int32, (LANES, STATE_LANES), 1)
    same_group = (ri // SSM_GROUP) == (li // SSM_STATE)

    def expand(ref):
        tiled = jnp.concatenate([ref[...]] * GROUPS_PER_LANE_TILE, axis=0)
        return jnp.where(same_group, tiled, 0.0)

    b_re, b_im, c_re, c_im = expand(btr_ref), expand(bti_ref), expand(ctr_ref), expand(cti_ref)

    def cmul(n, x_re, x_im):
        a_re = apr_ref[n:n + 1, :]
        a_im = api_ref[n:n + 1, :]
        return a_re * x_re - a_im * x_im, a_re * x_im + a_im * x_re

    m_blocks = []
    for k in range(Q):
        g_re, g_im = cmul(Q - 1 - k, b_re, b_im)
        m_blocks.append(jnp.concatenate([g_re, g_im], axis=1))
    m = jnp.concatenate(m_blocks, axis=0)

    nt_blocks = []
    for t in range(Q):
        g_re, g_im = cmul(t + 1, c_re, c_im)
        nt_blocks.append(jnp.concatenate([g_re, -g_im], axis=1))
    nt = jnp.concatenate(nt_blocks, axis=0)

    n0t = jnp.concatenate([c_re, -c_im], axis=1)
    p = lax.dot_general(m, n0t, (((1,), (1,)), ((), ())),
                        precision=lax.Precision.HIGHEST, preferred_element_type=F32)
    zeros = jnp.zeros((LANES, LANES), F32)
    cols = []
    for t in range(Q):
        cols.append(jnp.concatenate(
            [p[(Q - 1 - (t - k)) * LANES:(Q - (t - k)) * LANES, :] if t >= k else zeros
             for k in range(Q)], axis=0))
    k_ref[0] = jnp.concatenate(cols, axis=1).astype(BF16)
    m_ref[0] = m.astype(BF16)
    n_ref[0] = nt.T.astype(BF16)


def _s5_body(x_ref, g_ref, wu_ref, bu_ref, m_ref, k_ref, n_ref, aqr_ref, aqi_ref, d_ref,
             o_ref, u_scr, y_scr, z_scr, ss_scr, carry_scr):
    nb = x_ref.shape[0]
    rows = nb * S5_TT
    nchunk = rows // Q

    @pl.when(pl.program_id(0) == 0)
    def _():
        carry_scr[...] = jnp.zeros_like(carry_scr)

    x = x_ref[...].reshape(rows, D_MODEL)
    h = _rms(x, g_ref[...]).astype(BF16)
    u = jnp.dot(h, wu_ref[...], preferred_element_type=F32) + bu_ref[...]
    for j in range(N_LANE_TILES):
        u_scr[j] = u[:, j * LANES:(j + 1) * LANES]

    n_st = STATE_LANES // LANES
    cpt = S5_TT // Q
    for j in range(N_LANE_TILES):
        xj = jnp.concatenate(
            [u_scr[j, pl.ds(k, nchunk, stride=Q), :] for k in range(Q)], axis=1).astype(BF16)
        z = jnp.dot(xj, m_ref[j], preferred_element_type=F32)
        for i in range(2 * n_st):
            z_scr[i] = z[:, i * LANES:(i + 1) * LANES]
        aqr = jnp.broadcast_to(aqr_ref[:, pl.ds(j * STATE_LANES, STATE_LANES)], (nb, STATE_LANES))
        aqi = jnp.broadcast_to(aqi_ref[:, pl.ds(j * STATE_LANES, STATE_LANES)], (nb, STATE_LANES))
        s_re = carry_scr[j, :, pl.ds(0, STATE_LANES)]
        s_im = carry_scr[j, :, pl.ds(STATE_LANES, STATE_LANES)]
        for c in range(cpt):
            seq_rows = pl.ds(c, nb, stride=cpt)
            for i in range(n_st):
                ss_scr[i, seq_rows, :] = s_re[:, i * LANES:(i + 1) * LANES]
                ss_scr[n_st + i, seq_rows, :] = s_im[:, i * LANES:(i + 1) * LANES]
            z_re = jnp.concatenate([z_scr[i, seq_rows, :] for i in range(n_st)], axis=1)
            z_im = jnp.concatenate([z_scr[n_st + i, seq_rows, :] for i in range(n_st)], axis=1)
            s_re, s_im = (aqr * s_re - aqi * s_im + z_re,
                          aqr * s_im + aqi * s_re + z_im)
        carry_scr[j, :, pl.ds(0, STATE_LANES)] = s_re
        carry_scr[j, :, pl.ds(STATE_LANES, STATE_LANES)] = s_im
        ss = jnp.concatenate([ss_scr[i] for i in range(2 * n_st)], axis=1).astype(BF16)
        yj = (jnp.dot(xj, k_ref[j], preferred_element_type=F32)
              + jnp.dot(ss, n_ref[j], preferred_element_type=F32))
        for k in range(Q):
            y_scr[j, pl.ds(k, nchunk, stride=Q), :] = yj[:, k * LANES:(k + 1) * LANES]

    for j in range(N_LANE_TILES):
        lanes = pl.ds(j * LANES, LANES)
        y = y_scr[j] + d_ref[:, lanes] * u_scr[j]
        o_ref[:, :, lanes] = jax.nn.gelu(y).astype(BF16).reshape(nb, S5_TT, LANES)


def _s5_call(x, g, wu, bu, mmat, kmat, nmat, aq_re, aq_im, d_skip):
    nb, seq, _ = x.shape
    rows = nb * S5_TT
    nchunk = rows // Q
    return pl.pallas_call(
        _s5_body,
        out_shape=jax.ShapeDtypeStruct((nb, seq, SSM_WIDTH), BF16),
        grid=(seq // S5_TT,),
        in_specs=[
            pl.BlockSpec((nb, S5_TT, D_MODEL), lambda i: (0, i, 0)),
            _const_spec(g.shape), _const_spec(wu.shape), _const_spec(bu.shape),
            _const_spec(mmat.shape), _const_spec(kmat.shape), _const_spec(nmat.shape),
            _const_spec(aq_re.shape), _const_spec(aq_im.shape), _const_spec(d_skip.shape),
        ],
        out_specs=pl.BlockSpec((nb, S5_TT, SSM_WIDTH), lambda i: (0, i, 0)),
        scratch_shapes=[
            pltpu.VMEM((N_LANE_TILES, rows, LANES), F32),
            pltpu.VMEM((N_LANE_TILES, rows, LANES), F32),
            pltpu.VMEM((2 * STATE_LANES // LANES, nchunk, LANES), F32),
            pltpu.VMEM((2 * STATE_LANES // LANES, nchunk, LANES), F32),
            pltpu.VMEM((N_LANE_TILES, nb, 2 * STATE_LANES), F32),
        ],
        compiler_params=pltpu.CompilerParams(
            dimension_semantics=("arbitrary",), vmem_limit_bytes=VMEM_LIMIT),
        name="s5_mixer",
    )(x, g, wu, bu, mmat, kmat, nmat, aq_re, aq_im, d_skip)


MIX_TM = 512


def _mix_body(x_ref, ys_ref, g_ref, wc_ref, bc_ref, wg_ref, bg_ref, cw_ref, cb_ref, wco_ref,
              wab_ref, wo_ref, g2_ref, wr_ref, wrh_ref, br_ref,
              x1_ref, h2p_ref, route_ref, route_t_ref, cnt_ref, carry_scr, cnt_scr):
    tm = x_ref.shape[1]

    @pl.when(pl.program_id(1) == 0)
    def _():
        carry_scr[...] = jnp.zeros_like(carry_scr)

    @pl.when((pl.program_id(0) == 0) & (pl.program_id(1) == 0))
    def _():
        cnt_scr[...] = jnp.zeros_like(cnt_scr)

    x = x_ref[0]
    h = _rms(x, g_ref[...]).astype(BF16)

    zc = jnp.dot(h, wc_ref[...], preferred_element_type=F32) + bc_ref[...]
    c_b = zc[:, 0:CONV_WIDTH]
    cv = zc[:, CONV_WIDTH:2 * CONV_WIDTH] * zc[:, 2 * CONV_WIDTH:3 * CONV_WIDTH]
    row = lax.broadcasted_iota(jnp.int32, (tm, CONV_WIDTH), 0)
    last1 = carry_scr[7:8, :]
    last2 = carry_scr[6:7, :]
    p1 = jnp.where(row == 0, last1, pltpu.roll(cv, 1, axis=0))
    p2 = jnp.where(row == 0, last2, jnp.where(row == 1, last1, pltpu.roll(cv, 2, axis=0)))
    carry_scr[...] = cv[tm - 8:tm, :]
    conv = cw_ref[0:1, :] * p2 + cw_ref[1:2, :] * p1 + cw_ref[2:3, :] * cv + cb_ref[...]
    y_b = jnp.dot((c_b * conv).astype(BF16), wco_ref[...], preferred_element_type=F32)

    yab = jnp.dot(ys_ref[0], wab_ref[...], preferred_element_type=F32)
    y_a = yab[:, 0:D_MODEL] * jax.nn.sigmoid(yab[:, D_MODEL:2 * D_MODEL])

    zg = jnp.dot(h, wg_ref[...], preferred_element_type=F32) + bg_ref[...]
    mix = (jax.nn.sigmoid(zg[:, 0:D_MODEL]) * y_a
           + jax.nn.sigmoid(zg[:, D_MODEL:2 * D_MODEL]) * y_b)
    x1 = x + jnp.dot(mix.astype(BF16), wo_ref[...], preferred_element_type=F32)
    x1_ref[...] = x1

    h2 = _rms(x1, g2_ref[...])
    h2_hi = h2.astype(BF16)
    h2p_ref[...] = _pack_bf16_pairs(h2_hi.astype(F32))
    h2_lo = (h2 - h2_hi.astype(F32)).astype(BF16)
    lg2 = jnp.dot(h2_hi, wr_ref[...], preferred_element_type=F32)
    logits = (lg2[:, 0:LANES] + lg2[:, LANES:2 * LANES]
              + jnp.dot(h2_lo, wrh_ref[...], preferred_element_type=F32) + br_ref[...])

    lane = lax.broadcasted_iota(jnp.int32, (tm, LANES), 1)
    neg = jnp.float32(-jnp.inf)
    big = jnp.int32(1 << 20)
    is_g = (lane >= N_EXPERTS) & (lane < N_EXPERTS + N_EXPERT_GROUPS)
    gl = jnp.where(is_g, logits, neg)
    gmax = jnp.max(gl, axis=1, keepdims=True)
    g_w = 1.0 / jnp.sum(jnp.exp(gl - gmax), axis=1, keepdims=True)
    g_idx = jnp.min(jnp.where(gl == gmax, lane - N_EXPERTS, big), axis=1, keepdims=True)
    lo = g_idx * EXPERTS_PER_GROUP
    el = jnp.where((lane >= lo) & (lane < lo + EXPERTS_PER_GROUP), logits, neg)
    m1 = jnp.max(el, axis=1, keepdims=True)
    i1 = jnp.min(jnp.where(el == m1, lane, big), axis=1, keepdims=True)
    el2 = jnp.where(lane == i1, neg, el)
    m2 = jnp.max(el2, axis=1, keepdims=True)
    i2 = jnp.min(jnp.where(el2 == m2, lane, big), axis=1, keepdims=True)
    r = jnp.exp(m2 - m1)
    w1 = g_w / (1.0 + r)
    w2 = g_w * r / (1.0 + r)

    picks = ((lane == i1) | (lane == i2)).astype(BF16)
    r_i = lax.broadcasted_iota(jnp.int32, (tm, tm), 0)
    c_i = lax.broadcasted_iota(jnp.int32, (tm, tm), 1)
    before = (c_i < r_i).astype(BF16)
    excl = jnp.dot(before, picks, preferred_element_type=F32) + cnt_scr[...]
    rank1 = jnp.sum(jnp.where(lane == i1, excl, 0.0), axis=1, keepdims=True)
    rank2 = jnp.sum(jnp.where(lane == i2, excl, 0.0), axis=1, keepdims=True)
    cnt = cnt_scr[...] + jnp.sum(picks.astype(F32), axis=0, keepdims=True)
    cnt_scr[...] = cnt
    cnt_ref[...] = cnt
    route = jnp.where(lane == 0, i1.astype(F32), 0.0)
    route = jnp.where(lane == 1, i2.astype(F32), route)
    route = jnp.where(lane == 2, w1, route)
    route = jnp.where(lane == 3, w2, route)
    route = jnp.where(lane == 4, rank1, route)
    route = jnp.where(lane == 5, rank2, route)
    route_ref[...] = route
    route_t_ref[...] = route.T[0:8, :]


def _mix_call(x, ys, g, wc, bc, wg, bg, cw, cb, wco, wab, wo, g2, wr, wrh, br):
    nb, seq, _ = x.shape
    T = nb * seq
    nl = seq // MIX_TM
    consts = (g, wc, bc, wg, bg, cw, cb, wco, wab, wo, g2, wr, wrh, br)
    tok = lambda b, l: (b * nl + l, 0)
    return pl.pallas_call(
        _mix_body,
        out_shape=(jax.ShapeDtypeStruct((T, D_MODEL), F32),
                   jax.ShapeDtypeStruct((T, D_MODEL // 2), jnp.int32),
                   jax.ShapeDtypeStruct((T, LANES), F32),
                   jax.ShapeDtypeStruct((8, T), F32),
                   jax.ShapeDtypeStruct((1, LANES), F32)),
        grid=(nb, nl),
        in_specs=[pl.BlockSpec((1, MIX_TM, D_MODEL), lambda b, l: (b, l, 0)),
                  pl.BlockSpec((1, MIX_TM, SSM_WIDTH), lambda b, l: (b, l, 0))]
                 + [_const_spec(c.shape) for c in consts],
        out_specs=(pl.BlockSpec((MIX_TM, D_MODEL), tok),
                   pl.BlockSpec((MIX_TM, D_MODEL // 2), tok),
                   pl.BlockSpec((MIX_TM, LANES), tok),
                   pl.BlockSpec((8, MIX_TM), lambda b, l: (0, b * nl + l)),
                   pl.BlockSpec((1, LANES), lambda b, l: (0, 0))),
        scratch_shapes=[pltpu.VMEM((8, CONV_WIDTH), F32), pltpu.VMEM((1, LANES), F32)],
        compiler_params=pltpu.CompilerParams(
            dimension_semantics=("arbitrary", "arbitrary"), vmem_limit_bytes=VMEM_LIMIT),
        name="conv_glu_router",
    )(x, ys, *consts)


SC_CORES = 2
SC_SUBCORES = 16
SC_WORKERS = SC_CORES * SC_SUBCORES
SC_ROWS = 64


def _sc_mesh():
    return plsc.VectorSubcoreMesh(core_axis_name="c", subcore_axis_name="s")


def _sc_worker_id():
    return lax.axis_index("s") * SC_CORES + lax.axis_index("c")


def _sc_dispatch(rows, pos_a, pos_b, n_out):
    T, W = rows.shape
    per_w = T // SC_WORKERS
    nch = per_w // SC_ROWS
    assert per_w * SC_WORKERS == T and nch * SC_ROWS == per_w and nch % 2 == 0
    idx_a = pos_a.reshape(SC_WORKERS, nch, SC_ROWS)
    idx_b = pos_b.reshape(SC_WORKERS, nch, SC_ROWS)

    @functools.partial(
        pl.kernel, mesh=_sc_mesh(),
        out_type=jax.ShapeDtypeStruct((n_out, W), rows.dtype),
        scratch_types=[
            pltpu.VMEM((nch, SC_ROWS), jnp.int32),
            pltpu.VMEM((nch, SC_ROWS), jnp.int32),
            pltpu.VMEM((2, SC_ROWS, W), rows.dtype),
            pltpu.SemaphoreType.DMA((2,)),
            pltpu.SemaphoreType.DMA((2,)),
            pltpu.SemaphoreType.DMA((2,)),
        ],
        name="moe_dispatch",
    )
    def k(rows_hbm, ia_hbm, ib_hbm, out_hbm, ia_v, ib_v, buf, gsem, asem, bsem):
        wid = _sc_worker_id()
        base = wid * per_w
        pltpu.sync_copy(ia_hbm.at[wid], ia_v)
        pltpu.sync_copy(ib_hbm.at[wid], ib_v)

        def get(j, b):
            return pltpu.make_async_copy(
                rows_hbm.at[pl.ds(base + j * SC_ROWS, SC_ROWS)], buf.at[b], gsem.at[b])

        def put_a(j, b):
            return pltpu.make_async_copy(buf.at[b], out_hbm.at[ia_v.at[j]], asem.at[b])

        def put_b(j, b):
            return pltpu.make_async_copy(buf.at[b], out_hbm.at[ib_v.at[j]], bsem.at[b])

        get(0, 0).start()

        @pl.loop(0, nch, step=2)
        def _(j0):
            for b in range(2):
                j = j0 + b
                nb = 1 - b
                get(j, b).wait()

                @pl.when(j + 1 < nch)
                def _():
                    @pl.when(j >= 1)
                    def _():
                        put_a(j - 1, nb).wait()
                        put_b(j - 1, nb).wait()
                    get(j + 1, nb).start()

                put_a(j, b).start()
                put_b(j, b).start()

        for b in range(2):
            put_a(nch - 2 + b, b).wait()
            put_b(nch - 2 + b, b).wait()

    return k(rows, idx_a, idx_b)


def _sc_gather(table, idx):
    _, W = table.shape
    B = idx.shape[0]
    per_w = B // SC_WORKERS
    nch = per_w // SC_ROWS
    assert per_w * SC_WORKERS == B and nch * SC_ROWS == per_w and nch % 2 == 0
    idx3 = idx.reshape(SC_WORKERS, nch, SC_ROWS)

    @functools.partial(
        pl.kernel, mesh=_sc_mesh(),
        out_type=jax.ShapeDtypeStruct((B, W), table.dtype),
        scratch_types=[
            pltpu.VMEM((nch, SC_ROWS), jnp.int32),
            pltpu.VMEM((2, SC_ROWS, W), table.dtype),
            pltpu.SemaphoreType.DMA((2,)),
            pltpu.SemaphoreType.DMA((2,)),
        ],
        name="moe_combine_gather",
    )
    def k(table_hbm, idx_hbm, out_hbm, idx_v, buf, gsem, osem):
        wid = _sc_worker_id()
        base = wid * per_w
        pltpu.sync_copy(idx_hbm.at[wid], idx_v)

        def get(j, b):
            return pltpu.make_async_copy(table_hbm.at[idx_v.at[j]], buf.at[b], gsem.at[b])

        def put(j, b):
            return pltpu.make_async_copy(
                buf.at[b], out_hbm.at[pl.ds(base + j * SC_ROWS, SC_ROWS)], osem.at[b])

        get(0, 0).start()

        @pl.loop(0, nch, step=2)
        def _(j0):
            for b in range(2):
                j = j0 + b
                nb = 1 - b
                get(j, b).wait()

                @pl.when(j + 1 < nch)
                def _():
                    @pl.when(j >= 1)
                    def _():
                        put(j - 1, nb).wait()
                    get(j + 1, nb).start()

                put(j, b).start()

        for b in range(2):
            put(nch - 2 + b, b).wait()

    return k(table, idx3)


MOE_RT = 512


def _moe_body(te_ref, nt_ref, x_ref, wg_ref, wu_ref, wd_ref, o_ref, wg_scr, wu_scr, wd_scr):
    i = pl.program_id(0)
    prev = jnp.maximum(i - 1, 0)

    @pl.when((i == 0) | (te_ref[i] != te_ref[prev]))
    def _():
        wg_scr[...] = wg_ref[0].astype(BF16)
        wu_scr[...] = wu_ref[0].astype(BF16)
        wd_scr[...] = wd_ref[0].astype(BF16)

    @pl.when(i < nt_ref[0])
    def _():
        half = D_MODEL // 2
        lo, hi = _unpack_bf16_pairs(x_ref[...])
        lo = lo.astype(BF16)
        hi = hi.astype(BF16)
        gate = (jnp.dot(lo, wg_scr[0:half, :], preferred_element_type=F32)
                + jnp.dot(hi, wg_scr[half:D_MODEL, :], preferred_element_type=F32))
        up = (jnp.dot(lo, wu_scr[0:half, :], preferred_element_type=F32)
              + jnp.dot(hi, wu_scr[half:D_MODEL, :], preferred_element_type=F32))
        hid = (jax.nn.silu(gate) * up).astype(BF16)
        y = jnp.dot(hid, wd_scr[...], preferred_element_type=F32)
        o_ref[...] = _pack_bf16_pairs(y.astype(BF16).astype(F32))


def _moe_call(tile_expert, n_tiles, xs, wg, wu, wd):
    R = xs.shape[0]
    half = D_MODEL // 2
    row_map = lambda i, te, nt: (jnp.minimum(i, nt[0] - 1), 0)
    w_map = lambda i, te, nt: (te[i], 0, 0)
    return pl.pallas_call(
        _moe_body,
        out_shape=jax.ShapeDtypeStruct((R, half), jnp.int32),
        grid_spec=pltpu.PrefetchScalarGridSpec(
            num_scalar_prefetch=2,
            grid=(R // MOE_RT,),
            in_specs=[pl.BlockSpec((MOE_RT, half), row_map),
                      pl.BlockSpec((1, D_MODEL, EXPERT_HIDDEN), w_map),
                      pl.BlockSpec((1, D_MODEL, EXPERT_HIDDEN), w_map),
                      pl.BlockSpec((1, EXPERT_HIDDEN, D_MODEL), w_map)],
            out_specs=pl.BlockSpec((MOE_RT, half), row_map),
            scratch_shapes=[pltpu.VMEM((D_MODEL, EXPERT_HIDDEN), BF16),
                            pltpu.VMEM((D_MODEL, EXPERT_HIDDEN), BF16),
                            pltpu.VMEM((EXPERT_HIDDEN, D_MODEL), BF16)]),
        compiler_params=pltpu.CompilerParams(
            dimension_semantics=("arbitrary",), vmem_limit_bytes=VMEM_LIMIT),
        name="moe_experts",
    )(tile_expert, n_tiles, xs, wg, wu, wd)


def _moe_plan(route_t, counts, n_rows):
    cnt = counts[0, :N_EXPERTS].astype(jnp.int32)
    tiles = (cnt + MOE_RT - 1) // MOE_RT
    tile_end = jnp.cumsum(tiles)
    n_tiles = tile_end[-1:]
    row_start = (tile_end - tiles) * MOE_RT
    ids = route_t[0:2].astype(jnp.int32)
    ranks = route_t[4:6].astype(jnp.int32)
    experts = jnp.arange(N_EXPERTS, dtype=jnp.int32)[:, None, None]
    pos = ranks + jnp.sum(jnp.where(ids[None] == experts, row_start[:, None, None], 0), axis=0)
    tile_id = jnp.minimum(jnp.arange(n_rows // MOE_RT, dtype=jnp.int32), n_tiles - 1)
    tile_expert = jnp.sum((tile_id[:, None] >= tile_end[None, :]).astype(jnp.int32), axis=1)
    return pos, tile_expert, n_tiles


PLE_TM = 512


def _ple_body(x_ref, ya_ref, yb_ref, route_ref, p_ref, g3_ref, wpg_ref, bpg_ref, wple_ref, gf_ref,
              o_ref):
    ya = jnp.concatenate(_unpack_bf16_pairs(ya_ref[...]), axis=1)
    yb = jnp.concatenate(_unpack_bf16_pairs(yb_ref[...]), axis=1)
    route = route_ref[...]
    x2 = x_ref[...] + route[:, 2:3] * ya + route[:, 3:4] * yb
    h3 = _rms(x2, g3_ref[...]).astype(BF16)
    gate = jax.nn.sigmoid(jnp.dot(h3, wpg_ref[...], preferred_element_type=F32) + bpg_ref[...])
    pe = jnp.dot(p_ref[...].astype(BF16), wple_ref[...], preferred_element_type=F32)
    x3 = x2 + gate * pe
    o_ref[...] = _rms(x3, gf_ref[...])


def _ple_call(x1, y_picks, route, p, g3, wpg, bpg, wple, gf):
    T = x1.shape[0]
    nt = T // PLE_TM
    consts = (g3, wpg, bpg, wple, gf)
    return pl.pallas_call(
        _ple_body,
        out_shape=jax.ShapeDtypeStruct((T, D_MODEL), F32),
        grid=(nt,),
        in_specs=[pl.BlockSpec((PLE_TM, D_MODEL), lambda i: (i, 0)),
                  pl.BlockSpec((PLE_TM, D_MODEL // 2), lambda i: (i, 0)),
                  pl.BlockSpec((PLE_TM, D_MODEL // 2), lambda i: (i + nt, 0)),
                  pl.BlockSpec((PLE_TM, LANES), lambda i: (i, 0)),
                  pl.BlockSpec((PLE_TM, p.shape[1]), lambda i: (i, 0))]
                 + [_const_spec(c.shape) for c in consts],
        out_specs=pl.BlockSpec((PLE_TM, D_MODEL), lambda i: (i, 0)),
        compiler_params=pltpu.CompilerParams(
            dimension_semantics=("arbitrary",), vmem_limit_bytes=VMEM_LIMIT),
        name="ple_final",
    )(x1, y_picks, y_picks, route, p, *consts)


def _layer(x, p, norm_mix, w_in, b_in, lam_re, lam_im, log_dt, b_re, b_im, c_re, c_im, d_skip,
           w_glu_a, w_glu_b, conv_w, conv_b, w_conv_out, w_o, norm_ffn, w_rg, b_rg, w_re, b_re_r,
           w_eg, w_eu, w_ed, norm_ple, w_ple, w_pg, b_pg, norm_out):
    nb, seq, d = x.shape
    T = nb * seq
    s0 = SSM_WIDTH
    s3 = s0 + 3 * CONV_WIDTH
    row = lambda v: v.reshape(1, -1).astype(F32)

    kmat, mmat, nmat, aq_re, aq_im = _s5_operators(lam_re, lam_im, log_dt, b_re, b_im, c_re, c_im)
    ys = _s5_call(x, row(norm_mix), w_in[:, :s0].astype(BF16), row(b_in[:s0]),
                  mmat, kmat, nmat, aq_re, aq_im, row(d_skip))

    w_r = jnp.zeros((d, LANES), F32).at[:, :N_EXPERTS].set(w_re)
    w_r = w_r.at[:, N_EXPERTS:N_EXPERTS + N_EXPERT_GROUPS].set(w_rg)
    b_r = jnp.zeros((1, LANES), F32).at[0, :N_EXPERTS].set(b_re_r)
    b_r = b_r.at[0, N_EXPERTS:N_EXPERTS + N_EXPERT_GROUPS].set(b_rg)
    w_r_hi = w_r.astype(BF16)
    w_r_lo = (w_r - w_r_hi.astype(F32)).astype(BF16)

    x1, h2p, route, route_t, counts = _mix_call(
        x, ys, row(norm_mix),
        w_in[:, s0:s3].astype(BF16), row(b_in[s0:s3]),
        w_in[:, s3:].astype(BF16), row(b_in[s3:]),
        conv_w.astype(F32), row(conv_b), w_conv_out.astype(BF16),
        jnp.concatenate([w_glu_a, w_glu_b], axis=1).astype(BF16), w_o.astype(BF16),
        row(norm_ffn), jnp.concatenate([w_r_hi, w_r_lo], axis=1), w_r_hi, b_r)

    n_rows = 2 * T + N_EXPERTS * MOE_RT
    pos, tile_expert, n_tiles = _moe_plan(route_t, counts, n_rows)
    xs = _sc_dispatch(h2p, pos[0], pos[1], n_rows)
    ysort = _moe_call(tile_expert, n_tiles, xs, w_eg, w_eu, w_ed)
    y_picks = _sc_gather(ysort, pos.reshape(-1))

    out = _ple_call(x1, y_picks, route, p.reshape(T, -1), row(norm_ple), w_pg.astype(BF16),
                    row(b_pg), w_ple.astype(BF16), row(norm_out))
    return out.reshape(nb, seq, d)


def kernel(x, p, norm_mix, w_in, b_in, ssm_lam_re, ssm_lam_im, ssm_log_dt, ssm_b_re, ssm_b_im, ssm_c_re, ssm_c_im, ssm_d, w_glu_a, w_glu_b, conv_w, conv_b, w_conv_out, w_o, norm_ffn, w_router_group, b_router_group, w_router_expert, b_router_expert, w_exp_gate, w_exp_up, w_exp_down, norm_ple, w_ple, w_ple_gate, b_ple_gate, norm_final):
    assert p.shape[0] == 1, "the final RMSNorm is fused into the (single) layer's last kernel"
    i = 0
    return _layer(x, p[i], norm_mix[i], w_in[i], b_in[i], ssm_lam_re[i], ssm_lam_im[i],
                  ssm_log_dt[i], ssm_b_re[i], ssm_b_im[i], ssm_c_re[i], ssm_c_im[i], ssm_d[i],
                  w_glu_a[i], w_glu_b[i], conv_w[i], conv_b[i], w_conv_out[i], w_o[i],
                  norm_ffn[i], w_router_group[i], b_router_group[i], w_router_expert[i],
                  b_router_expert[i], w_exp_gate[i], w_exp_up[i], w_exp_down[i], norm_ple[i],
                  w_ple[i], w_ple_gate[i], b_ple_gate[i], norm_final)
```

```python
import functools
import math

import jax
import jax.numpy as jnp
from jax import lax
from jax.experimental import pallas as pl
from jax.experimental.pallas import tpu as pltpu
from jax.experimental.pallas import tpu_sc as plsc

F32 = jnp.float32
BF16 = jnp.bfloat16

D_MODEL = 1024
SSM_WIDTH = 512
SSM_GROUP = 16
SSM_GROUPS = 32
SSM_STATE = 64
CONV_WIDTH = 512
N_EXPERT_GROUPS = 4
EXPERTS_PER_GROUP = 8
N_EXPERTS = 32
EXPERT_HIDDEN = 256
NORM_EPS = 1e-6

LANES = 128
Q = 8
GROUPS_PER_LANE_TILE = LANES // SSM_GROUP
N_LANE_TILES = SSM_WIDTH // LANES
STATE_LANES = GROUPS_PER_LANE_TILE * SSM_STATE
S5_ROWS = 1024
N_PARTS = 2
VMEM_LIMIT = 56 * 1024 * 1024


def _rms(x, g):
    return x * lax.rsqrt(jnp.mean(x * x, axis=-1, keepdims=True) + NORM_EPS) * g


def _pack_bf16_pairs(a):
    w = a.shape[1] // 2
    lo = lax.shift_right_logical(lax.bitcast_convert_type(a[:, :w], jnp.int32), 16)
    hi = lax.bitcast_convert_type(a[:, w:], jnp.int32) & jnp.int32(-65536)
    return lo | hi


def _unpack_bf16_pairs(word):
    lo = lax.bitcast_convert_type(lax.shift_left(word, 16), F32)
    hi = lax.bitcast_convert_type(word & jnp.int32(-65536), F32)
    return lo, hi


def _const_spec(shape):
    n = len(shape)
    return pl.BlockSpec(shape, lambda *_: (0,) * n, pipeline_mode=pl.Buffered(1))


def _s5_operators(lam_re, lam_im, log_dt, b_re, b_im, c_re, c_im):
    G, P, H = SSM_GROUPS, SSM_STATE, SSM_GROUP
    J = N_LANE_TILES
    lr = lam_re.astype(F32)
    li = lam_im.astype(F32)
    dt = jnp.exp(log_dt.astype(F32))[:, None]

    def apow(n):
        n = n.astype(F32)[:, None, None]
        mag = jnp.exp(lr * dt * n)
        ang = li * dt * n
        return mag * jnp.cos(ang), mag * jnp.sin(ang)

    a1_re, a1_im = apow(jnp.ones((1,), F32))
    nr = a1_re[0] - 1.0
    ni = a1_im[0]
    den = lr * lr + li * li
    f_re = (nr * lr + ni * li) / den
    f_im = (ni * lr - nr * li) / den
    br = b_re.astype(F32)
    bi = b_im.astype(F32)
    bbar_re = f_re[:, :, None] * br - f_im[:, :, None] * bi
    bbar_im = f_re[:, :, None] * bi + f_im[:, :, None] * br
    to_rows = lambda v, perm: jnp.transpose(v, perm).reshape(H, G * P)
    bt_re = to_rows(bbar_re, (2, 0, 1))
    bt_im = to_rows(bbar_im, (2, 0, 1))
    ct_re = to_rows(c_re.astype(F32), (1, 0, 2))
    ct_im = to_rows(c_im.astype(F32), (1, 0, 2))
    ap_re, ap_im = apow(jnp.arange(Q + 1))
    ap_re = ap_re.reshape(Q + 1, G * P)
    ap_im = ap_im.reshape(Q + 1, G * P)

    blk = lambda r: pl.BlockSpec((r, STATE_LANES), lambda j: (0, j))
    mat = pl.BlockSpec((1, Q * LANES, Q * LANES), lambda j: (j, 0, 0))
    shape = jax.ShapeDtypeStruct((J, Q * LANES, Q * LANES), BF16)
    kmat, mmat, nmat = pl.pallas_call(
        _s5_ops_body,
        out_shape=(shape, shape, shape),
        grid=(J,),
        in_specs=[blk(Q + 1), blk(Q + 1), blk(H), blk(H), blk(H), blk(H)],
        out_specs=(mat, mat, mat),
        compiler_params=pltpu.CompilerParams(
            dimension_semantics=("arbitrary",), vmem_limit_bytes=VMEM_LIMIT),
        name="s5_operators",
    )(ap_re, ap_im, bt_re, bt_im, ct_re, ct_im)
    return kmat, mmat, nmat, ap_re[Q:Q + 1], ap_im[Q:Q + 1]


def _s5_ops_body(apr_ref, api_ref, btr_ref, bti_ref, ctr_ref, cti_ref, k_ref, m_ref, n_ref):
    ri = lax.broadcasted_iota(jnp.int32, (LANES, STATE_LANES), 0)
    li = lax.broadcasted_iota(jnp.int32, (LANES, STATE_LANES), 1)
    same_group = (ri // SSM_GROUP) == (li // SSM_STATE)

    def expand(ref):
        tiled = jnp.concatenate([ref[...]] * GROUPS_PER_LANE_TILE, axis=0)
        return jnp.where(same_group, tiled, 0.0)

    b_re, b_im, c_re, c_im = expand(btr_ref), expand(bti_ref), expand(ctr_ref), expand(cti_ref)

    def cmul(n, x_re, x_im):
        a_re = apr_ref[n:n + 1, :]
        a_im = api_ref[n:n + 1, :]
        return a_re * x_re - a_im * x_im, a_re * x_im + a_im * x_re

    m_blocks = []
    for k in range(Q):
        g_re, g_im = cmul(Q - 1 - k, b_re, b_im)
        m_blocks.append(jnp.concatenate([g_re, g_im], axis=1))
    m = jnp.concatenate(m_blocks, axis=0)

    nt_blocks = []
    for t in range(Q):
        g_re, g_im = cmul(t + 1, c_re, c_im)
        nt_blocks.append(jnp.concatenate([g_re, -g_im], axis=1))
    nt = jnp.concatenate(nt_blocks, axis=0)

    n0t = jnp.concatenate([c_re, -c_im], axis=1)
    p = lax.dot_general(m, n0t, (((1,), (1,)), ((), ())),
                        precision=lax.Precision.HIGHEST, preferred_element_type=F32)
    zeros = jnp.zeros((LANES, LANES), F32)
    cols = []
    for t in range(Q):
        cols.append(jnp.concatenate(
            [p[(Q - 1 - (t - k)) * LANES:(Q - (t - k)) * LANES, :] if t >= k else zeros
             for k in range(Q)], axis=0))
    k_ref[0] = jnp.concatenate(cols, axis=1).astype(BF16)
    m_ref[0] = m.astype(BF16)
    n_ref[0] = nt.T.astype(BF16)


def _s5_body(x_ref, g_ref, wu_ref, bu_ref, m_ref, k_ref, n_ref, aqr_ref, aqi_ref, d_ref,
             o_ref, u_scr, y_scr, z_scr, ss_scr, carry_scr):
    nb, tt = x_ref.shape[0], x_ref.shape[1]
    rows = nb * tt
    nchunk = rows // Q

    @pl.when(pl.program_id(0) == 0)
    def _():
        carry_scr[...] = jnp.zeros_like(carry_scr)

    x = x_ref[...].reshape(rows, D_MODEL)
    h = _rms(x, g_ref[...]).astype(BF16)
    u = jnp.dot(h, wu_ref[...], preferred_element_type=F32) + bu_ref[...]
    for j in range(N_LANE_TILES):
        u_scr[j] = u[:, j * LANES:(j + 1) * LANES]

    n_st = STATE_LANES // LANES
    cpt = tt // Q
    for j in range(N_LANE_TILES):
        xj = jnp.concatenate(
            [u_scr[j, pl.ds(k, nchunk, stride=Q), :] for k in range(Q)], axis=1).astype(BF16)
        z = jnp.dot(xj, m_ref[j], preferred_element_type=F32)
        for i in range(2 * n_st):
            z_scr[i] = z[:, i * LANES:(i + 1) * LANES]
        aqr = jnp.broadcast_to(aqr_ref[:, pl.ds(j * STATE_LANES, STATE_LANES)], (nb, STATE_LANES))
        aqi = jnp.broadcast_to(aqi_ref[:, pl.ds(j * STATE_LANES, STATE_LANES)], (nb, STATE_LANES))
        s_re = carry_scr[j, :, pl.ds(0, STATE_LANES)]
        s_im = carry_scr[j, :, pl.ds(STATE_LANES, STATE_LANES)]
        for c in range(cpt):
            seq_rows = pl.ds(c, nb, stride=cpt)
            for i in range(n_st):
                ss_scr[i, seq_rows, :] = s_re[:, i * LANES:(i + 1) * LANES]
                ss_scr[n_st + i, seq_rows, :] = s_im[:, i * LANES:(i + 1) * LANES]
            z_re = jnp.concatenate([z_scr[i, seq_rows, :] for i in range(n_st)], axis=1)
            z_im = jnp.concatenate([z_scr[n_st + i, seq_rows, :] for i in range(n_st)], axis=1)
            s_re, s_im = (aqr * s_re - aqi * s_im + z_re,
                          aqr * s_im + aqi * s_re + z_im)
        carry_scr[j, :, pl.ds(0, STATE_LANES)] = s_re
        carry_scr[j, :, pl.ds(STATE_LANES, STATE_LANES)] = s_im
        ss = jnp.concatenate([ss_scr[i] for i in range(2 * n_st)], axis=1).astype(BF16)
        yj = (jnp.dot(xj, k_ref[j], preferred_element_type=F32)
              + jnp.dot(ss, n_ref[j], preferred_element_type=F32))
        for k in range(Q):
            y_scr[j, pl.ds(k, nchunk, stride=Q), :] = yj[:, k * LANES:(k + 1) * LANES]

    for j in range(N_LANE_TILES):
        lanes = pl.ds(j * LANES, LANES)
        y = y_scr[j] + d_ref[:, lanes] * u_scr[j]
        o_ref[:, :, lanes] = jax.nn.gelu(y).astype(BF16).reshape(nb, tt, LANES)


def _s5_call(x, part, g, wu, bu, mmat, kmat, nmat, aq_re, aq_im, d_skip):
    nb, seq = x.shape[0] // N_PARTS, x.shape[1]
    tt = S5_ROWS // nb
    rows = S5_ROWS
    nchunk = rows // Q
    return pl.pallas_call(
        _s5_body,
        out_shape=jax.ShapeDtypeStruct((nb, seq, SSM_WIDTH), BF16),
        grid=(seq // tt,),
        in_specs=[
            pl.BlockSpec((nb, tt, D_MODEL), lambda i: (part, i, 0)),
            _const_spec(g.shape), _const_spec(wu.shape), _const_spec(bu.shape),
            _const_spec(mmat.shape), _const_spec(kmat.shape), _const_spec(nmat.shape),
            _const_spec(aq_re.shape), _const_spec(aq_im.shape), _const_spec(d_skip.shape),
        ],
        out_specs=pl.BlockSpec((nb, tt, SSM_WIDTH), lambda i: (0, i, 0)),
        scratch_shapes=[
            pltpu.VMEM((N_LANE_TILES, rows, LANES), F32),
            pltpu.VMEM((N_LANE_TILES, rows, LANES), F32),
            pltpu.VMEM((2 * STATE_LANES // LANES, nchunk, LANES), F32),
            pltpu.VMEM((2 * STATE_LANES // LANES, nchunk, LANES), F32),
            pltpu.VMEM((N_LANE_TILES, nb, 2 * STATE_LANES), F32),
        ],
        compiler_params=pltpu.CompilerParams(
            dimension_semantics=("arbitrary",), vmem_limit_bytes=VMEM_LIMIT),
        name="s5_mixer",
    )(x, g, wu, bu, mmat, kmat, nmat, aq_re, aq_im, d_skip)


MIX_TM = 1024
MIX_SUB = 512


def _mix_body(x_ref, ys_ref, g_ref, wc_ref, bc_ref, wg_ref, bg_ref, cw_ref, cb_ref, wco_ref,
              wab_ref, wo_ref, g2_ref, wr_ref, wrh_ref, br_ref,
              x1_ref, h2p_ref, route_ref, route_t_ref, cnt_ref, carry_scr, cnt_scr):
    @pl.when(pl.program_id(1) == 0)
    def _():
        carry_scr[...] = jnp.zeros_like(carry_scr)

    @pl.when((pl.program_id(0) == 0) & (pl.program_id(1) == 0))
    def _():
        cnt_scr[...] = jnp.zeros_like(cnt_scr)

    tail = carry_scr[...]
    cnt = cnt_scr[...]
    for s in range(x_ref.shape[1] // MIX_SUB):
        rows = pl.ds(s * MIX_SUB, MIX_SUB)
        x1, h2p, route, tail, cnt = _mix_tile(
            x_ref[0, rows, :], ys_ref[0, rows, :], tail, cnt,
            g_ref, wc_ref, bc_ref, wg_ref, bg_ref, cw_ref, cb_ref, wco_ref,
            wab_ref, wo_ref, g2_ref, wr_ref, wrh_ref, br_ref)
        x1_ref[rows, :] = x1
        h2p_ref[rows, :] = h2p
        route_ref[rows, :] = route
        route_t_ref[:, rows] = route.T[0:8, :]
    carry_scr[...] = tail
    cnt_scr[...] = cnt
    cnt_ref[...] = cnt


def _mix_tile(x, ys, tail, cnt, g_ref, wc_ref, bc_ref, wg_ref, bg_ref, cw_ref, cb_ref, wco_ref,
              wab_ref, wo_ref, g2_ref, wr_ref, wrh_ref, br_ref):
    tm = x.shape[0]
    h = _rms(x, g_ref[...]).astype(BF16)

    zc = jnp.dot(h, wc_ref[...], preferred_element_type=F32) + bc_ref[...]
    c_b = zc[:, 0:CONV_WIDTH]
    cv = zc[:, CONV_WIDTH:2 * CONV_WIDTH] * zc[:, 2 * CONV_WIDTH:3 * CONV_WIDTH]
    row = lax.broadcasted_iota(jnp.int32, (tm, CONV_WIDTH), 0)
    last1 = tail[7:8, :]
    last2 = tail[6:7, :]
    p1 = jnp.where(row == 0, last1, pltpu.roll(cv, 1, axis=0))
    p2 = jnp.where(row == 0, last2, jnp.where(row == 1, last1, pltpu.roll(cv, 2, axis=0)))
    new_tail = cv[tm - 8:tm, :]
    conv = cw_ref[0:1, :] * p2 + cw_ref[1:2, :] * p1 + cw_ref[2:3, :] * cv + cb_ref[...]
    y_b = jnp.dot((c_b * conv).astype(BF16), wco_ref[...], preferred_element_type=F32)

    yab = jnp.dot(ys, wab_ref[...], preferred_element_type=F32)
    y_a = yab[:, 0:D_MODEL] * jax.nn.sigmoid(yab[:, D_MODEL:2 * D_MODEL])

    zg = jnp.dot(h, wg_ref[...], preferred_element_type=F32) + bg_ref[...]
    mix = (jax.nn.sigmoid(zg[:, 0:D_MODEL]) * y_a
           + jax.nn.sigmoid(zg[:, D_MODEL:2 * D_MODEL]) * y_b)
    x1 = x + jnp.dot(mix.astype(BF16), wo_ref[...], preferred_element_type=F32)

    h2 = _rms(x1, g2_ref[...])
    h2_hi = h2.astype(BF16)
    h2p = _pack_bf16_pairs(h2_hi.astype(F32))
    h2_lo = (h2 - h2_hi.astype(F32)).astype(BF16)
    lg2 = jnp.dot(h2_hi, wr_ref[...], preferred_element_type=F32)
    logits = (lg2[:, 0:LANES] + lg2[:, LANES:2 * LANES]
              + jnp.dot(h2_lo, wrh_ref[...], preferred_element_type=F32) + br_ref[...])

    lane = lax.broadcasted_iota(jnp.int32, (tm, LANES), 1)
    neg = jnp.float32(-jnp.inf)
    big = jnp.int32(1 << 20)
    is_g = (lane >= N_EXPERTS) & (lane < N_EXPERTS + N_EXPERT_GROUPS)
    gl = jnp.where(is_g, logits, neg)
    gmax = jnp.max(gl, axis=1, keepdims=True)
    g_w = 1.0 / jnp.sum(jnp.exp(gl - gmax), axis=1, keepdims=True)
    g_idx = jnp.min(jnp.where(gl == gmax, lane - N_EXPERTS, big), axis=1, keepdims=True)
    lo = g_idx * EXPERTS_PER_GROUP
    el = jnp.where((lane >= lo) & (lane < lo + EXPERTS_PER_GROUP), logits, neg)
    m1 = jnp.max(el, axis=1, keepdims=True)
    i1 = jnp.min(jnp.where(el == m1, lane, big), axis=1, keepdims=True)
    el2 = jnp.where(lane == i1, neg, el)
    m2 = jnp.max(el2, axis=1, keepdims=True)
    i2 = jnp.min(jnp.where(el2 == m2, lane, big), axis=1, keepdims=True)
    r = jnp.exp(m2 - m1)
    w1 = g_w / (1.0 + r)
    w2 = g_w * r / (1.0 + r)

    picks = ((lane == i1) | (lane == i2)).astype(BF16)
    r_i = lax.broadcasted_iota(jnp.int32, (tm, tm), 0)
    c_i = lax.broadcasted_iota(jnp.int32, (tm, tm), 1)
    before = (c_i < r_i).astype(BF16)
    excl = jnp.dot(before, picks, preferred_element_type=F32) + cnt
    rank1 = jnp.sum(jnp.where(lane == i1, excl, 0.0), axis=1, keepdims=True)
    rank2 = jnp.sum(jnp.where(lane == i2, excl, 0.0), axis=1, keepdims=True)
    new_cnt = cnt + jnp.sum(picks.astype(F32), axis=0, keepdims=True)
    route = jnp.where(lane == 0, i1.astype(F32), 0.0)
    route = jnp.where(lane == 1, i2.astype(F32), route)
    route = jnp.where(lane == 2, w1, route)
    route = jnp.where(lane == 3, w2, route)
    route = jnp.where(lane == 4, rank1, route)
    route = jnp.where(lane == 5, rank2, route)
    return x1, h2p, route, new_tail, new_cnt


def _mix_call(x, part, ys, g, wc, bc, wg, bg, cw, cb, wco, wab, wo, g2, wr, wrh, br):
    nb, seq, _ = ys.shape
    T = nb * seq
    nl = seq // MIX_TM
    consts = (g, wc, bc, wg, bg, cw, cb, wco, wab, wo, g2, wr, wrh, br)
    tok = lambda b, l: (b * nl + l, 0)
    return pl.pallas_call(
        _mix_body,
        out_shape=(jax.ShapeDtypeStruct((T, D_MODEL), F32),
                   jax.ShapeDtypeStruct((T, D_MODEL // 2), jnp.int32),
                   jax.ShapeDtypeStruct((T, LANES), F32),
                   jax.ShapeDtypeStruct((8, T), F32),
                   jax.ShapeDtypeStruct((1, LANES), F32)),
        grid=(nb, nl),
        in_specs=[pl.BlockSpec((1, MIX_TM, D_MODEL), lambda b, l: (part * nb + b, l, 0)),
                  pl.BlockSpec((1, MIX_TM, SSM_WIDTH), lambda b, l: (b, l, 0))]
                 + [_const_spec(c.shape) for c in consts],
        out_specs=(pl.BlockSpec((MIX_TM, D_MODEL), tok),
                   pl.BlockSpec((MIX_TM, D_MODEL // 2), tok),
                   pl.BlockSpec((MIX_TM, LANES), tok),
                   pl.BlockSpec((8, MIX_TM), lambda b, l: (0, b * nl + l)),
                   pl.BlockSpec((1, LANES), lambda b, l: (0, 0))),
        scratch_shapes=[pltpu.VMEM((8, CONV_WIDTH), F32), pltpu.VMEM((1, LANES), F32)],
        compiler_params=pltpu.CompilerParams(
            dimension_semantics=("arbitrary", "arbitrary"), vmem_limit_bytes=VMEM_LIMIT),
        name="conv_glu_router",
    )(x, ys, *consts)


SC_CORES = 2
SC_SUBCORES = 16
SC_WORKERS = SC_CORES * SC_SUBCORES
SC_ROWS = 64


def _sc_mesh():
    return plsc.VectorSubcoreMesh(core_axis_name="c", subcore_axis_name="s")


def _sc_worker_id():
    return lax.axis_index("s") * SC_CORES + lax.axis_index("c")


def _sc_dispatch(rows, pos_a, pos_b, n_out):
    T, W = rows.shape
    per_w = T // SC_WORKERS
    nch = per_w // SC_ROWS
    assert per_w * SC_WORKERS == T and nch * SC_ROWS == per_w and nch % 2 == 0
    idx_a = pos_a.reshape(SC_WORKERS, nch, SC_ROWS)
    idx_b = pos_b.reshape(SC_WORKERS, nch, SC_ROWS)

    @functools.partial(
        pl.kernel, mesh=_sc_mesh(),
        out_type=jax.ShapeDtypeStruct((n_out, W), rows.dtype),
        scratch_types=[
            pltpu.VMEM((nch, SC_ROWS), jnp.int32),
            pltpu.VMEM((nch, SC_ROWS), jnp.int32),
            pltpu.VMEM((2, SC_ROWS, W), rows.dtype),
            pltpu.SemaphoreType.DMA((2,)),
            pltpu.SemaphoreType.DMA((2,)),
            pltpu.SemaphoreType.DMA((2,)),
        ],
        name="moe_dispatch",
    )
    def k(rows_hbm, ia_hbm, ib_hbm, out_hbm, ia_v, ib_v, buf, gsem, asem, bsem):
        wid = _sc_worker_id()
        base = wid * per_w
        pltpu.sync_copy(ia_hbm.at[wid], ia_v)
        pltpu.sync_copy(ib_hbm.at[wid], ib_v)

        def get(j, b):
            return pltpu.make_async_copy(
                rows_hbm.at[pl.ds(base + j * SC_ROWS, SC_ROWS)], buf.at[b], gsem.at[b])

        def put_a(j, b):
            return pltpu.make_async_copy(buf.at[b], out_hbm.at[ia_v.at[j]], asem.at[b])

        def put_b(j, b):
            return pltpu.make_async_copy(buf.at[b], out_hbm.at[ib_v.at[j]], bsem.at[b])

        get(0, 0).start()

        @pl.loop(0, nch, step=2)
        def _(j0):
            for b in range(2):
                j = j0 + b
                nb = 1 - b
                get(j, b).wait()

                @pl.when(j + 1 < nch)
                def _():
                    @pl.when(j >= 1)
                    def _():
                        put_a(j - 1, nb).wait()
                        put_b(j - 1, nb).wait()
                    get(j + 1, nb).start()

                put_a(j, b).start()
                put_b(j, b).start()

        for b in range(2):
            put_a(nch - 2 + b, b).wait()
            put_b(nch - 2 + b, b).wait()

    return k(rows, idx_a, idx_b)


def _sc_gather(table, idx):
    _, W = table.shape
    B = idx.shape[0]
    per_w = B // SC_WORKERS
    nch = per_w // SC_ROWS
    assert per_w * SC_WORKERS == B and nch * SC_ROWS == per_w and nch % 2 == 0
    idx3 = idx.reshape(SC_WORKERS, nch, SC_ROWS)

    @functools.partial(
        pl.kernel, mesh=_sc_mesh(),
        out_type=jax.ShapeDtypeStruct((B, W), table.dtype),
        scratch_types=[
            pltpu.VMEM((nch, SC_ROWS), jnp.int32),
            pltpu.VMEM((2, SC_ROWS, W), table.dtype),
            pltpu.SemaphoreType.DMA((2,)),
            pltpu.SemaphoreType.DMA((2,)),
        ],
        name="moe_combine_gather",
    )
    def k(table_hbm, idx_hbm, out_hbm, idx_v, buf, gsem, osem):
        wid = _sc_worker_id()
        base = wid * per_w
        pltpu.sync_copy(idx_hbm.at[wid], idx_v)

        def get(j, b):
            return pltpu.make_async_copy(table_hbm.at[idx_v.at[j]], buf.at[b], gsem.at[b])

        def put(j, b):
            return pltpu.make_async_copy(
                buf.at[b], out_hbm.at[pl.ds(base + j * SC_ROWS, SC_ROWS)], osem.at[b])

        get(0, 0).start()

        @pl.loop(0, nch, step=2)
        def _(j0):
            for b in range(2):
                j = j0 + b
                nb = 1 - b
                get(j, b).wait()

                @pl.when(j + 1 < nch)
                def _():
                    @pl.when(j >= 1)
                    def _():
                        put(j - 1, nb).wait()
                    get(j + 1, nb).start()

                put(j, b).start()

        for b in range(2):
            put(nch - 2 + b, b).wait()

    return k(table, idx3)


MOE_RT = 512


def _moe_body(te_ref, nt_ref, x_ref, wg_ref, wu_ref, wd_ref, o_ref, wg_scr, wu_scr, wd_scr):
    i = pl.program_id(0)
    prev = jnp.maximum(i - 1, 0)

    @pl.when((i == 0) | (te_ref[i] != te_ref[prev]))
    def _():
        wg_scr[...] = wg_ref[0].astype(BF16)
        wu_scr[...] = wu_ref[0].astype(BF16)
        wd_scr[...] = wd_ref[0].astype(BF16)

    @pl.when(i < nt_ref[0])
    def _():
        half = D_MODEL // 2
        lo, hi = _unpack_bf16_pairs(x_ref[...])
        lo = lo.astype(BF16)
        hi = hi.astype(BF16)
        gate = (jnp.dot(lo, wg_scr[0:half, :], preferred_element_type=F32)
                + jnp.dot(hi, wg_scr[half:D_MODEL, :], preferred_element_type=F32))
        up = (jnp.dot(lo, wu_scr[0:half, :], preferred_element_type=F32)
              + jnp.dot(hi, wu_scr[half:D_MODEL, :], preferred_element_type=F32))
        hid = (jax.nn.silu(gate) * up).astype(BF16)
        y = jnp.dot(hid, wd_scr[...], preferred_element_type=F32)
        o_ref[...] = _pack_bf16_pairs(y.astype(BF16).astype(F32))


def _moe_call(tile_expert, n_tiles, xs, wg, wu, wd):
    R = xs.shape[0]
    half = D_MODEL // 2
    row_map = lambda i, te, nt: (jnp.minimum(i, nt[0] - 1), 0)
    w_map = lambda i, te, nt: (te[i], 0, 0)
    return pl.pallas_call(
        _moe_body,
        out_shape=jax.ShapeDtypeStruct((R, half), jnp.int32),
        grid_spec=pltpu.PrefetchScalarGridSpec(
            num_scalar_prefetch=2,
            grid=(R // MOE_RT,),
            in_specs=[pl.BlockSpec((MOE_RT, half), row_map),
                      pl.BlockSpec((1, D_MODEL, EXPERT_HIDDEN), w_map),
                      pl.BlockSpec((1, D_MODEL, EXPERT_HIDDEN), w_map),
                      pl.BlockSpec((1, EXPERT_HIDDEN, D_MODEL), w_map)],
            out_specs=pl.BlockSpec((MOE_RT, half), row_map),
            scratch_shapes=[pltpu.VMEM((D_MODEL, EXPERT_HIDDEN), BF16),
                            pltpu.VMEM((D_MODEL, EXPERT_HIDDEN), BF16),
                            pltpu.VMEM((EXPERT_HIDDEN, D_MODEL), BF16)]),
        compiler_params=pltpu.CompilerParams(
            dimension_semantics=("arbitrary",), vmem_limit_bytes=VMEM_LIMIT),
        name="moe_experts",
    )(tile_expert, n_tiles, xs, wg, wu, wd)


def _moe_plan(route_t, counts, n_rows):
    cnt = counts[0, :N_EXPERTS].astype(jnp.int32)
    tiles = (cnt + MOE_RT - 1) // MOE_RT
    tile_end = jnp.cumsum(tiles)
    n_tiles = tile_end[-1:]
    row_start = (tile_end - tiles) * MOE_RT
    ids = route_t[0:2].astype(jnp.int32)
    ranks = route_t[4:6].astype(jnp.int32)
    experts = jnp.arange(N_EXPERTS, dtype=jnp.int32)[:, None, None]
    pos = ranks + jnp.sum(jnp.where(ids[None] == experts, row_start[:, None, None], 0), axis=0)
    tile_id = jnp.minimum(jnp.arange(n_rows // MOE_RT, dtype=jnp.int32), n_tiles - 1)
    tile_expert = jnp.sum((tile_id[:, None] >= tile_end[None, :]).astype(jnp.int32), axis=1)
    return pos, tile_expert, n_tiles


PLE_TM = 512


def _ple_body(x_ref, ya_ref, yb_ref, route_ref, p_ref, g3_ref, wpg_ref, bpg_ref, wple_ref, gf_ref,
              *rest):
    o_ref = rest[-1]
    ya = jnp.concatenate(_unpack_bf16_pairs(ya_ref[...]), axis=1)
    yb = jnp.concatenate(_unpack_bf16_pairs(yb_ref[...]), axis=1)
    route = route_ref[...]
    x2 = x_ref[...] + route[:, 2:3] * ya + route[:, 3:4] * yb
    h3 = _rms(x2, g3_ref[...]).astype(BF16)
    gate = jax.nn.sigmoid(jnp.dot(h3, wpg_ref[...], preferred_element_type=F32) + bpg_ref[...])
    pe = jnp.dot(p_ref[...].astype(BF16), wple_ref[...], preferred_element_type=F32)
    x3 = x2 + gate * pe
    o_ref[...] = _rms(x3, gf_ref[...])


def _ple_call(x1, y_picks, route, p, part, out_so_far, g3, wpg, bpg, wple, gf):
    tg = x1.shape[0]
    nt = tg // PLE_TM
    consts = (g3, wpg, bpg, wple, gf)
    full = lambda i: (part * nt + i, 0)
    operands = [x1, y_picks, y_picks, route, p, *consts]
    in_specs = ([pl.BlockSpec((PLE_TM, D_MODEL), lambda i: (i, 0)),
                 pl.BlockSpec((PLE_TM, D_MODEL // 2), lambda i: (i, 0)),
                 pl.BlockSpec((PLE_TM, D_MODEL // 2), lambda i: (i + nt, 0)),
                 pl.BlockSpec((PLE_TM, LANES), lambda i: (i, 0)),
                 pl.BlockSpec((PLE_TM, p.shape[1]), full)]
                + [_const_spec(c.shape) for c in consts])
    aliases = {}
    if out_so_far is not None:
        aliases = {len(operands): 0}
        operands.append(out_so_far)
        in_specs.append(pl.BlockSpec(memory_space=pl.ANY))
    return pl.pallas_call(
        _ple_body,
        out_shape=jax.ShapeDtypeStruct((p.shape[0], D_MODEL), F32),
        grid=(nt,),
        in_specs=in_specs,
        out_specs=pl.BlockSpec((PLE_TM, D_MODEL), full),
        input_output_aliases=aliases,
        compiler_params=pltpu.CompilerParams(
            dimension_semantics=("arbitrary",), vmem_limit_bytes=VMEM_LIMIT),
        name="ple_final",
    )(*operands)


def _layer(x, p, norm_mix, w_in, b_in, lam_re, lam_im, log_dt, b_re, b_im, c_re, c_im, d_skip,
           w_glu_a, w_glu_b, conv_w, conv_b, w_conv_out, w_o, norm_ffn, w_rg, b_rg, w_re, b_re_r,
           w_eg, w_eu, w_ed, norm_ple, w_ple, w_pg, b_pg, norm_out):
    nb, seq, d = x.shape
    T = nb * seq
    s0 = SSM_WIDTH
    s3 = s0 + 3 * CONV_WIDTH
    row = lambda v: v.reshape(1, -1).astype(F32)

    kmat, mmat, nmat, aq_re, aq_im = _s5_operators(lam_re, lam_im, log_dt, b_re, b_im, c_re, c_im)
    s5_consts = (row(norm_mix), w_in[:, :s0].astype(BF16), row(b_in[:s0]),
                 mmat, kmat, nmat, aq_re, aq_im, row(d_skip))

    w_r = jnp.zeros((d, LANES), F32).at[:, :N_EXPERTS].set(w_re)
    w_r = w_r.at[:, N_EXPERTS:N_EXPERTS + N_EXPERT_GROUPS].set(w_rg)
    b_r = jnp.zeros((1, LANES), F32).at[0, :N_EXPERTS].set(b_re_r)
    b_r = b_r.at[0, N_EXPERTS:N_EXPERTS + N_EXPERT_GROUPS].set(b_rg)
    w_r_hi = w_r.astype(BF16)
    w_r_lo = (w_r - w_r_hi.astype(F32)).astype(BF16)

    mix_consts = (row(norm_mix),
                  w_in[:, s0:s3].astype(BF16), row(b_in[s0:s3]),
                  w_in[:, s3:].astype(BF16), row(b_in[s3:]),
                  conv_w.astype(F32), row(conv_b), w_conv_out.astype(BF16),
                  jnp.concatenate([w_glu_a, w_glu_b], axis=1).astype(BF16), w_o.astype(BF16),
                  row(norm_ffn), jnp.concatenate([w_r_hi, w_r_lo], axis=1), w_r_hi, b_r)
    ple_consts = (row(norm_ple), w_pg.astype(BF16), row(b_pg), w_ple.astype(BF16), row(norm_out))
    p2d = p.reshape(T, -1)
    tg = T // N_PARTS
    n_rows = 2 * tg + N_EXPERTS * MOE_RT

    out = None
    for part in range(N_PARTS):
        ys = _s5_call(x, part, *s5_consts)
        x1, h2p, route, route_t, counts = _mix_call(x, part, ys, *mix_consts)
        pos, tile_expert, n_tiles = _moe_plan(route_t, counts, n_rows)
        xs = _sc_dispatch(h2p, pos[0], pos[1], n_rows)
        ysort = _moe_call(tile_expert, n_tiles, xs, w_eg, w_eu, w_ed)
        y_picks = _sc_gather(ysort, pos.reshape(-1))
        out = _ple_call(x1, y_picks, route, p2d, part, out, *ple_consts)
    return out.reshape(nb, seq, d)


def kernel(x, p, norm_mix, w_in, b_in, ssm_lam_re, ssm_lam_im, ssm_log_dt, ssm_b_re, ssm_b_im, ssm_c_re, ssm_c_im, ssm_d, w_glu_a, w_glu_b, conv_w, conv_b, w_conv_out, w_o, norm_ffn, w_router_group, b_router_group, w_router_expert, b_router_expert, w_exp_gate, w_exp_up, w_exp_down, norm_ple, w_ple, w_ple_gate, b_ple_gate, norm_final):
    assert p.shape[0] == 1, "the final RMSNorm is fused into the (single) layer's last kernel"
    i = 0
    return _layer(x, p[i], norm_mix[i], w_in[i], b_in[i], ssm_lam_re[i], ssm_lam_im[i],
                  ssm_log_dt[i], ssm_b_re[i], ssm_b_im[i], ssm_c_re[i], ssm_c_im[i], ssm_d[i],
                  w_glu_a[i], w_glu_b[i], conv_w[i], conv_b[i], w_conv_out[i], w_o[i],
                  norm_ffn[i], w_router_group[i], b_router_group[i], w_router_expert[i],
                  b_router_expert[i], w_exp_gate[i], w_exp_up[i], w_exp_down[i], norm_ple[i],
                  w_ple[i], w_ple_gate[i], b_ple_gate[i], norm_final)
```

```python
import functools
import math

import jax
import jax.numpy as jnp
from jax import lax
from jax.experimental import pallas as pl
from jax.experimental.pallas import tpu as pltpu
from jax.experimental.pallas import tpu_sc as plsc

F32 = jnp.float32
BF16 = jnp.bfloat16

D_MODEL = 1024
SSM_WIDTH = 512
SSM_GROUP = 16
SSM_GROUPS = 32
SSM_STATE = 64
CONV_WIDTH = 512
N_EXPERT_GROUPS = 4
EXPERTS_PER_GROUP = 8
N_EXPERTS = 32
EXPERT_HIDDEN = 256
NORM_EPS = 1e-6

LANES = 128
Q = 8
GROUPS_PER_LANE_TILE = LANES // SSM_GROUP
N_LANE_TILES = SSM_WIDTH // LANES
STATE_LANES = GROUPS_PER_LANE_TILE * SSM_STATE
S5_ROWS = 1024
N_PARTS = 1
VMEM_LIMIT = 56 * 1024 * 1024


def _rms(x, g):
    return x * lax.rsqrt(jnp.mean(x * x, axis=-1, keepdims=True) + NORM_EPS) * g


def _pack_bf16_pairs(a):
    w = a.shape[1] // 2
    lo = lax.shift_right_logical(lax.bitcast_convert_type(a[:, :w], jnp.int32), 16)
    hi = lax.bitcast_convert_type(a[:, w:], jnp.int32) & jnp.int32(-65536)
    return lo | hi


def _unpack_bf16_pairs(word):
    lo = lax.bitcast_convert_type(lax.shift_left(word, 16), F32)
    hi = lax.bitcast_convert_type(word & jnp.int32(-65536), F32)
    return lo, hi


def _const_spec(shape):
    n = len(shape)
    return pl.BlockSpec(shape, lambda *_: (0,) * n, pipeline_mode=pl.Buffered(1))


def _s5_operators(lam_re, lam_im, log_dt, b_re, b_im, c_re, c_im):
    G, P, H = SSM_GROUPS, SSM_STATE, SSM_GROUP
    J = N_LANE_TILES
    lr = lam_re.astype(F32)
    li = lam_im.astype(F32)
    dt = jnp.exp(log_dt.astype(F32))[:, None]

    def apow(n):
        n = n.astype(F32)[:, None, None]
        mag = jnp.exp(lr * dt * n)
        ang = li * dt * n
        return mag * jnp.cos(ang), mag * jnp.sin(ang)

    a1_re, a1_im = apow(jnp.ones((1,), F32))
    nr = a1_re[0] - 1.0
    ni = a1_im[0]
    den = lr * lr + li * li
    f_re = (nr * lr + ni * li) / den
    f_im = (ni * lr - nr * li) / den
    br = b_re.astype(F32)
    bi = b_im.astype(F32)
    bbar_re = f_re[:, :, None] * br - f_im[:, :, None] * bi
    bbar_im = f_re[:, :, None] * bi + f_im[:, :, None] * br
    to_rows = lambda v, perm: jnp.transpose(v, perm).reshape(H, G * P)
    bt_re = to_rows(bbar_re, (2, 0, 1))
    bt_im = to_rows(bbar_im, (2, 0, 1))
    ct_re = to_rows(c_re.astype(F32), (1, 0, 2))
    ct_im = to_rows(c_im.astype(F32), (1, 0, 2))
    ap_re, ap_im = apow(jnp.arange(Q + 1))
    ap_re = ap_re.reshape(Q + 1, G * P)
    ap_im = ap_im.reshape(Q + 1, G * P)

    blk = lambda r: pl.BlockSpec((r, STATE_LANES), lambda j: (0, j))
    mat = pl.BlockSpec((1, Q * LANES, Q * LANES), lambda j: (j, 0, 0))
    shape = jax.ShapeDtypeStruct((J, Q * LANES, Q * LANES), BF16)
    kmat, mmat, nmat = pl.pallas_call(
        _s5_ops_body,
        out_shape=(shape, shape, shape),
        grid=(J,),
        in_specs=[blk(Q + 1), blk(Q + 1), blk(H), blk(H), blk(H), blk(H)],
        out_specs=(mat, mat, mat),
        compiler_params=pltpu.CompilerParams(
            dimension_semantics=("arbitrary",), vmem_limit_bytes=VMEM_LIMIT),
        name="s5_operators",
    )(ap_re, ap_im, bt_re, bt_im, ct_re, ct_im)
    return kmat, mmat, nmat, ap_re[Q:Q + 1], ap_im[Q:Q + 1]


def _s5_ops_body(apr_ref, api_ref, btr_ref, bti_ref, ctr_ref, cti_ref, k_ref, m_ref, n_ref):
    ri = lax.broadcasted_iota(jnp.int32, (LANES, STATE_LANES), 0)
    li = lax.broadcasted_iota(jnp.int32, (LANES, STATE_LANES), 1)
    same_group = (ri // SSM_GROUP) == (li // SSM_STATE)

    def expand(ref):
        tiled = jnp.concatenate([ref[...]] * GROUPS_PER_LANE_TILE, axis=0)
        return jnp.where(same_group, tiled, 0.0)

    b_re, b_im, c_re, c_im = expand(btr_ref), expand(bti_ref), expand(ctr_ref), expand(cti_ref)

    def cmul(n, x_re, x_im):
        a_re = apr_ref[n:n + 1, :]
        a_im = api_ref[n:n + 1, :]
        return a_re * x_re - a_im * x_im, a_re * x_im + a_im * x_re

    m_blocks = []
    for k in range(Q):
        g_re, g_im = cmul(Q - 1 - k, b_re, b_im)
        m_blocks.append(jnp.concatenate([g_re, g_im], axis=1))
    m = jnp.concatenate(m_blocks, axis=0)

    nt_blocks = []
    for t in range(Q):
        g_re, g_im = cmul(t + 1, c_re, c_im)
        nt_blocks.append(jnp.concatenate([g_re, -g_im], axis=1))
    nt = jnp.concatenate(nt_blocks, axis=0)

    n0t = jnp.concatenate([c_re, -c_im], axis=1)
    p = lax.dot_general(m, n0t, (((1,), (1,)), ((), ())),
                        precision=lax.Precision.HIGHEST, preferred_element_type=F32)
    zeros = jnp.zeros((LANES, LANES), F32)
    cols = []
    for t in range(Q):
        cols.append(jnp.concatenate(
            [p[(Q - 1 - (t - k)) * LANES:(Q - (t - k)) * LANES, :] if t >= k else zeros
             for k in range(Q)], axis=0))
    k_ref[0] = jnp.concatenate(cols, axis=1).astype(BF16)
    m_ref[0] = m.astype(BF16)
    n_ref[0] = nt.T.astype(BF16)


def _s5_body(x_ref, g_ref, wu_ref, bu_ref, m_ref, k_ref, n_ref, aqr_ref, aqi_ref, d_ref,
             o_ref, u_scr, y_scr, z_scr, ss_scr, carry_scr):
    nb, tt = x_ref.shape[0], x_ref.shape[1]
    rows = nb * tt
    nchunk = rows // Q

    @pl.when(pl.program_id(0) == 0)
    def _():
        carry_scr[...] = jnp.zeros_like(carry_scr)

    x = x_ref[...].reshape(rows, D_MODEL)
    h = _rms(x, g_ref[...]).astype(BF16)
    u = jnp.dot(h, wu_ref[...], preferred_element_type=F32) + bu_ref[...]
    for j in range(N_LANE_TILES):
        u_scr[j] = u[:, j * LANES:(j + 1) * LANES]

    n_st = STATE_LANES // LANES
    cpt = tt // Q
    for j in range(N_LANE_TILES):
        xj = jnp.concatenate(
            [u_scr[j, pl.ds(k, nchunk, stride=Q), :] for k in range(Q)], axis=1).astype(BF16)
        z = jnp.dot(xj, m_ref[j], preferred_element_type=F32)
        for i in range(2 * n_st):
            z_scr[i] = z[:, i * LANES:(i + 1) * LANES]
        aqr = jnp.broadcast_to(aqr_ref[:, pl.ds(j * STATE_LANES, STATE_LANES)], (nb, STATE_LANES))
        aqi = jnp.broadcast_to(aqi_ref[:, pl.ds(j * STATE_LANES, STATE_LANES)], (nb, STATE_LANES))
        s_re = carry_scr[j, :, pl.ds(0, STATE_LANES)]
        s_im = carry_scr[j, :, pl.ds(STATE_LANES, STATE_LANES)]
        for c in range(cpt):
            seq_rows = pl.ds(c, nb, stride=cpt)
            for i in range(n_st):
                ss_scr[i, seq_rows, :] = s_re[:, i * LANES:(i + 1) * LANES]
                ss_scr[n_st + i, seq_rows, :] = s_im[:, i * LANES:(i + 1) * LANES]
            z_re = jnp.concatenate([z_scr[i, seq_rows, :] for i in range(n_st)], axis=1)
            z_im = jnp.concatenate([z_scr[n_st + i, seq_rows, :] for i in range(n_st)], axis=1)
            s_re, s_im = (aqr * s_re - aqi * s_im + z_re,
                          aqr * s_im + aqi * s_re + z_im)
        carry_scr[j, :, pl.ds(0, STATE_LANES)] = s_re
        carry_scr[j, :, pl.ds(STATE_LANES, STATE_LANES)] = s_im
        ss = jnp.concatenate([ss_scr[i] for i in range(2 * n_st)], axis=1).astype(BF16)
        yj = (jnp.dot(xj, k_ref[j], preferred_element_type=F32)
              + jnp.dot(ss, n_ref[j], preferred_element_type=F32))
        for k in range(Q):
            y_scr[j, pl.ds(k, nchunk, stride=Q), :] = yj[:, k * LANES:(k + 1) * LANES]

    for j in range(N_LANE_TILES):
        lanes = pl.ds(j * LANES, LANES)
        y = y_scr[j] + d_ref[:, lanes] * u_scr[j]
        o_ref[:, :, lanes] = jax.nn.gelu(y).astype(BF16).reshape(nb, tt, LANES)


def _s5_call(x, part, g, wu, bu, mmat, kmat, nmat, aq_re, aq_im, d_skip):
    nb, seq = x.shape[0] // N_PARTS, x.shape[1]
    tt = S5_ROWS // nb
    rows = S5_ROWS
    nchunk = rows // Q
    return pl.pallas_call(
        _s5_body,
        out_shape=jax.ShapeDtypeStruct((nb, seq, SSM_WIDTH), BF16),
        grid=(seq // tt,),
        in_specs=[
            pl.BlockSpec((nb, tt, D_MODEL), lambda i: (part, i, 0)),
            _const_spec(g.shape), _const_spec(wu.shape), _const_spec(bu.shape),
            _const_spec(mmat.shape), _const_spec(kmat.shape), _const_spec(nmat.shape),
            _const_spec(aq_re.shape), _const_spec(aq_im.shape), _const_spec(d_skip.shape),
        ],
        out_specs=pl.BlockSpec((nb, tt, SSM_WIDTH), lambda i: (0, i, 0)),
        scratch_shapes=[
            pltpu.VMEM((N_LANE_TILES, rows, LANES), F32),
            pltpu.VMEM((N_LANE_TILES, rows, LANES), F32),
            pltpu.VMEM((2 * STATE_LANES // LANES, nchunk, LANES), F32),
            pltpu.VMEM((2 * STATE_LANES // LANES, nchunk, LANES), F32),
            pltpu.VMEM((N_LANE_TILES, nb, 2 * STATE_LANES), F32),
        ],
        compiler_params=pltpu.CompilerParams(
            dimension_semantics=("arbitrary",), vmem_limit_bytes=VMEM_LIMIT),
        name="s5_mixer",
    )(x, g, wu, bu, mmat, kmat, nmat, aq_re, aq_im, d_skip)


MIX_TM = 512


def _mix_body(x_ref, ys_ref, g_ref, wc_ref, bc_ref, wg_ref, bg_ref, cw_ref, cb_ref, wco_ref,
              wab_ref, wo_ref, g2_ref, wr_ref, wrh_ref, br_ref,
              x1_ref, h2p_ref, route_ref, route_t_ref, cnt_ref, carry_scr, cnt_scr, x1_scr,
              *, tiles_per_seq, n_tiles):
    i = pl.program_id(0)
    tm = x_ref.shape[1]

    @pl.when(i == 0)
    def _():
        cnt_scr[...] = jnp.zeros_like(cnt_scr)
        x1_scr[...] = jnp.zeros_like(x1_scr)
        carry_scr[...] = jnp.zeros_like(carry_scr)

    carry_scr[0] = jnp.where(i == n_tiles, carry_scr[0],
                             jnp.where(i % tiles_per_seq == 0, 0.0, carry_scr[1]))

    yab = jnp.dot(ys_ref[0], wab_ref[...], preferred_element_type=F32)

    h2 = _rms(x1_scr[...], g2_ref[...])
    h2_hi = h2.astype(BF16)
    h2p_ref[...] = _pack_bf16_pairs(h2_hi.astype(F32))
    h2_lo = (h2 - h2_hi.astype(F32)).astype(BF16)
    lg2 = jnp.dot(h2_hi, wr_ref[...], preferred_element_type=F32)
    logits = (lg2[:, 0:LANES] + lg2[:, LANES:2 * LANES]
              + jnp.dot(h2_lo, wrh_ref[...], preferred_element_type=F32) + br_ref[...])

    x = x_ref[0]
    h = _rms(x, g_ref[...]).astype(BF16)
    zc = jnp.dot(h, wc_ref[...], preferred_element_type=F32) + bc_ref[...]
    zg = jnp.dot(h, wg_ref[...], preferred_element_type=F32) + bg_ref[...]

    cnt = cnt_scr[...]
    route, new_cnt = _route_tile(logits, cnt)
    cnt = jnp.where(i > 0, new_cnt, cnt)
    cnt_scr[...] = cnt
    cnt_ref[...] = cnt
    route_ref[...] = route
    route_t_ref[...] = route.T[0:8, :]

    y_a = yab[:, 0:D_MODEL] * jax.nn.sigmoid(yab[:, D_MODEL:2 * D_MODEL])

    c_b = zc[:, 0:CONV_WIDTH]
    cv = zc[:, CONV_WIDTH:2 * CONV_WIDTH] * zc[:, 2 * CONV_WIDTH:3 * CONV_WIDTH]
    row = lax.broadcasted_iota(jnp.int32, (tm, CONV_WIDTH), 0)
    last1 = carry_scr[0, 7:8, :]
    last2 = carry_scr[0, 6:7, :]
    p1 = jnp.where(row == 0, last1, pltpu.roll(cv, 1, axis=0))
    p2 = jnp.where(row == 0, last2, jnp.where(row == 1, last1, pltpu.roll(cv, 2, axis=0)))
    carry_scr[1] = cv[tm - 8:tm, :]
    conv = cw_ref[0:1, :] * p2 + cw_ref[1:2, :] * p1 + cw_ref[2:3, :] * cv + cb_ref[...]
    y_b = jnp.dot((c_b * conv).astype(BF16), wco_ref[...], preferred_element_type=F32)

    mix = (jax.nn.sigmoid(zg[:, 0:D_MODEL]) * y_a
           + jax.nn.sigmoid(zg[:, D_MODEL:2 * D_MODEL]) * y_b)
    x1 = x + jnp.dot(mix.astype(BF16), wo_ref[...], preferred_element_type=F32)
    x1_ref[...] = x1
    x1_scr[...] = x1


def _route_tile(logits, cnt):
    tm = logits.shape[0]
    lane = lax.broadcasted_iota(jnp.int32, (tm, LANES), 1)
    neg = jnp.float32(-jnp.inf)
    big = jnp.int32(1 << 20)
    is_g = (lane >= N_EXPERTS) & (lane < N_EXPERTS + N_EXPERT_GROUPS)
    gl = jnp.where(is_g, logits, neg)
    gmax = jnp.max(gl, axis=1, keepdims=True)
    g_w = 1.0 / jnp.sum(jnp.exp(gl - gmax), axis=1, keepdims=True)
    g_idx = jnp.min(jnp.where(gl == gmax, lane - N_EXPERTS, big), axis=1, keepdims=True)
    lo = g_idx * EXPERTS_PER_GROUP
    el = jnp.where((lane >= lo) & (lane < lo + EXPERTS_PER_GROUP), logits, neg)
    m1 = jnp.max(el, axis=1, keepdims=True)
    i1 = jnp.min(jnp.where(el == m1, lane, big), axis=1, keepdims=True)
    el2 = jnp.where(lane == i1, neg, el)
    m2 = jnp.max(el2, axis=1, keepdims=True)
    i2 = jnp.min(jnp.where(el2 == m2, lane, big), axis=1, keepdims=True)
    r = jnp.exp(m2 - m1)
    w1 = g_w / (1.0 + r)
    w2 = g_w * r / (1.0 + r)

    picks = ((lane == i1) | (lane == i2)).astype(BF16)
    r_i = lax.broadcasted_iota(jnp.int32, (tm, tm), 0)
    c_i = lax.broadcasted_iota(jnp.int32, (tm, tm), 1)
    before = (c_i < r_i).astype(BF16)
    excl = jnp.dot(before, picks, preferred_element_type=F32) + cnt
    rank1 = jnp.sum(jnp.where(lane == i1, excl, 0.0), axis=1, keepdims=True)
    rank2 = jnp.sum(jnp.where(lane == i2, excl, 0.0), axis=1, keepdims=True)
    new_cnt = cnt + jnp.sum(picks.astype(F32), axis=0, keepdims=True)
    route = jnp.where(lane == 0, i1.astype(F32), 0.0)
    route = jnp.where(lane == 1, i2.astype(F32), route)
    route = jnp.where(lane == 2, w1, route)
    route = jnp.where(lane == 3, w2, route)
    route = jnp.where(lane == 4, rank1, route)
    route = jnp.where(lane == 5, rank2, route)
    return route, new_cnt


def _mix_call(x, part, ys, g, wc, bc, wg, bg, cw, cb, wco, wab, wo, g2, wr, wrh, br):
    nb, seq, _ = ys.shape
    T = nb * seq
    nl = seq // MIX_TM
    n = nb * nl
    consts = (g, wc, bc, wg, bg, cw, cb, wco, wab, wo, g2, wr, wrh, br)
    cur = lambda i: jnp.minimum(i, n - 1)
    prev = lambda i: jnp.maximum(i - 1, 0)
    return pl.pallas_call(
        functools.partial(_mix_body, tiles_per_seq=nl, n_tiles=n),
        out_shape=(jax.ShapeDtypeStruct((T, D_MODEL), F32),
                   jax.ShapeDtypeStruct((T, D_MODEL // 2), jnp.int32),
                   jax.ShapeDtypeStruct((T, LANES), F32),
                   jax.ShapeDtypeStruct((8, T), F32),
                   jax.ShapeDtypeStruct((1, LANES), F32)),
        grid=(n + 1,),
        in_specs=[pl.BlockSpec((1, MIX_TM, D_MODEL),
                               lambda i: (part * nb + cur(i) // nl, cur(i) % nl, 0)),
                  pl.BlockSpec((1, MIX_TM, SSM_WIDTH), lambda i: (cur(i) // nl, cur(i) % nl, 0))]
                 + [_const_spec(c.shape) for c in consts],
        out_specs=(pl.BlockSpec((MIX_TM, D_MODEL), lambda i: (cur(i), 0)),
                   pl.BlockSpec((MIX_TM, D_MODEL // 2), lambda i: (prev(i), 0)),
                   pl.BlockSpec((MIX_TM, LANES), lambda i: (prev(i), 0)),
                   pl.BlockSpec((8, MIX_TM), lambda i: (0, prev(i))),
                   pl.BlockSpec((1, LANES), lambda i: (0, 0))),
        scratch_shapes=[pltpu.VMEM((2, 8, CONV_WIDTH), F32), pltpu.VMEM((1, LANES), F32),
                        pltpu.VMEM((MIX_TM, D_MODEL), F32)],
        compiler_params=pltpu.CompilerParams(
            dimension_semantics=("arbitrary",), vmem_limit_bytes=VMEM_LIMIT),
        name="conv_glu_router",
    )(x, ys, *consts)


SC_CORES = 2
SC_SUBCORES = 16
SC_WORKERS = SC_CORES * SC_SUBCORES
SC_ROWS = 64


def _sc_mesh():
    return plsc.VectorSubcoreMesh(core_axis_name="c", subcore_axis_name="s")


def _sc_worker_id():
    return lax.axis_index("s") * SC_CORES + lax.axis_index("c")


def _sc_dispatch(rows, pos_a, pos_b, n_out):
    T, W = rows.shape
    per_w = T // SC_WORKERS
    nch = per_w // SC_ROWS
    assert per_w * SC_WORKERS == T and nch * SC_ROWS == per_w and nch % 2 == 0
    idx_a = pos_a.reshape(SC_WORKERS, nch, SC_ROWS)
    idx_b = pos_b.reshape(SC_WORKERS, nch, SC_ROWS)

    @functools.partial(
        pl.kernel, mesh=_sc_mesh(),
        out_type=jax.ShapeDtypeStruct((n_out, W), rows.dtype),
        scratch_types=[
            pltpu.VMEM((nch, SC_ROWS), jnp.int32),
            pltpu.VMEM((nch, SC_ROWS), jnp.int32),
            pltpu.VMEM((2, SC_ROWS, W), rows.dtype),
            pltpu.SemaphoreType.DMA((2,)),
            pltpu.SemaphoreType.DMA((2,)),
            pltpu.SemaphoreType.DMA((2,)),
        ],
        name="moe_dispatch",
    )
    def k(rows_hbm, ia_hbm, ib_hbm, out_hbm, ia_v, ib_v, buf, gsem, asem, bsem):
        wid = _sc_worker_id()
        base = wid * per_w
        pltpu.sync_copy(ia_hbm.at[wid], ia_v)
        pltpu.sync_copy(ib_hbm.at[wid], ib_v)

        def get(j, b):
            return pltpu.make_async_copy(
                rows_hbm.at[pl.ds(base + j * SC_ROWS, SC_ROWS)], buf.at[b], gsem.at[b])

        def put_a(j, b):
            return pltpu.make_async_copy(buf.at[b], out_hbm.at[ia_v.at[j]], asem.at[b])

        def put_b(j, b):
            return pltpu.make_async_copy(buf.at[b], out_hbm.at[ib_v.at[j]], bsem.at[b])

        get(0, 0).start()

        @pl.loop(0, nch, step=2)
        def _(j0):
            for b in range(2):
                j = j0 + b
                nb = 1 - b
                get(j, b).wait()

                @pl.when(j + 1 < nch)
                def _():
                    @pl.when(j >= 1)
                    def _():
                        put_a(j - 1, nb).wait()
                        put_b(j - 1, nb).wait()
                    get(j + 1, nb).start()

                put_a(j, b).start()
                put_b(j, b).start()

        for b in range(2):
            put_a(nch - 2 + b, b).wait()
            put_b(nch - 2 + b, b).wait()

    return k(rows, idx_a, idx_b)


def _sc_gather(table, idx):
    _, W = table.shape
    B = idx.shape[0]
    per_w = B // SC_WORKERS
    nch = per_w // SC_ROWS
    assert per_w * SC_WORKERS == B and nch * SC_ROWS == per_w and nch % 2 == 0
    idx3 = idx.reshape(SC_WORKERS, nch, SC_ROWS)

    @functools.partial(
        pl.kernel, mesh=_sc_mesh(),
        out_type=jax.ShapeDtypeStruct((B, W), table.dtype),
        scratch_types=[
            pltpu.VMEM((nch, SC_ROWS), jnp.int32),
            pltpu.VMEM((2, SC_ROWS, W), table.dtype),
            pltpu.SemaphoreType.DMA((2,)),
            pltpu.SemaphoreType.DMA((2,)),
        ],
        name="moe_combine_gather",
    )
    def k(table_hbm, idx_hbm, out_hbm, idx_v, buf, gsem, osem):
        wid = _sc_worker_id()
        base = wid * per_w
        pltpu.sync_copy(idx_hbm.at[wid], idx_v)

        def get(j, b):
            return pltpu.make_async_copy(table_hbm.at[idx_v.at[j]], buf.at[b], gsem.at[b])

        def put(j, b):
            return pltpu.make_async_copy(
                buf.at[b], out_hbm.at[pl.ds(base + j * SC_ROWS, SC_ROWS)], osem.at[b])

        get(0, 0).start()

        @pl.loop(0, nch, step=2)
        def _(j0):
            for b in range(2):
                j = j0 + b
                nb = 1 - b
                get(j, b).wait()

                @pl.when(j + 1 < nch)
                def _():
                    @pl.when(j >= 1)
                    def _():
                        put(j - 1, nb).wait()
                    get(j + 1, nb).start()

                put(j, b).start()

        for b in range(2):
            put(nch - 2 + b, b).wait()

    return k(table, idx3)


MOE_RT = 512


def _moe_body(te_ref, nt_ref, x_ref, wg_ref, wu_ref, wd_ref, o_ref, wg_scr, wu_scr, wd_scr):
    i = pl.program_id(0)
    prev = jnp.maximum(i - 1, 0)

    @pl.when((i == 0) | (te_ref[i] != te_ref[prev]))
    def _():
        wg_scr[...] = wg_ref[0].astype(BF16)
        wu_scr[...] = wu_ref[0].astype(BF16)
        wd_scr[...] = wd_ref[0].astype(BF16)

    @pl.when(i < nt_ref[0])
    def _():
        half = D_MODEL // 2
        lo, hi = _unpack_bf16_pairs(x_ref[...])
        lo = lo.astype(BF16)
        hi = hi.astype(BF16)
        gate = (jnp.dot(lo, wg_scr[0:half, :], preferred_element_type=F32)
                + jnp.dot(hi, wg_scr[half:D_MODEL, :], preferred_element_type=F32))
        up = (jnp.dot(lo, wu_scr[0:half, :], preferred_element_type=F32)
              + jnp.dot(hi, wu_scr[half:D_MODEL, :], preferred_element_type=F32))
        hid = (jax.nn.silu(gate) * up).astype(BF16)
        y = jnp.dot(hid, wd_scr[...], preferred_element_type=F32)
        o_ref[...] = _pack_bf16_pairs(y.astype(BF16).astype(F32))


def _moe_call(tile_expert, n_tiles, xs, wg, wu, wd):
    R = xs.shape[0]
    half = D_MODEL // 2
    row_map = lambda i, te, nt: (jnp.minimum(i, nt[0] - 1), 0)
    w_map = lambda i, te, nt: (te[i], 0, 0)
    return pl.pallas_call(
        _moe_body,
        out_shape=jax.ShapeDtypeStruct((R, half), jnp.int32),
        grid_spec=pltpu.PrefetchScalarGridSpec(
            num_scalar_prefetch=2,
            grid=(R // MOE_RT,),
            in_specs=[pl.BlockSpec((MOE_RT, half), row_map),
                      pl.BlockSpec((1, D_MODEL, EXPERT_HIDDEN), w_map),
                      pl.BlockSpec((1, D_MODEL, EXPERT_HIDDEN), w_map),
                      pl.BlockSpec((1, EXPERT_HIDDEN, D_MODEL), w_map)],
            out_specs=pl.BlockSpec((MOE_RT, half), row_map),
            scratch_shapes=[pltpu.VMEM((D_MODEL, EXPERT_HIDDEN), BF16),
                            pltpu.VMEM((D_MODEL, EXPERT_HIDDEN), BF16),
                            pltpu.VMEM((EXPERT_HIDDEN, D_MODEL), BF16)]),
        compiler_params=pltpu.CompilerParams(
            dimension_semantics=("arbitrary",), vmem_limit_bytes=VMEM_LIMIT),
        name="moe_experts",
    )(tile_expert, n_tiles, xs, wg, wu, wd)


def _moe_plan(route_t, counts, n_rows):
    cnt = counts[0, :N_EXPERTS].astype(jnp.int32)
    tiles = (cnt + MOE_RT - 1) // MOE_RT
    tile_end = jnp.cumsum(tiles)
    n_tiles = tile_end[-1:]
    row_start = (tile_end - tiles) * MOE_RT
    ids = route_t[0:2].astype(jnp.int32)
    ranks = route_t[4:6].astype(jnp.int32)
    experts = jnp.arange(N_EXPERTS, dtype=jnp.int32)[:, None, None]
    pos = ranks + jnp.sum(jnp.where(ids[None] == experts, row_start[:, None, None], 0), axis=0)
    tile_id = jnp.minimum(jnp.arange(n_rows // MOE_RT, dtype=jnp.int32), n_tiles - 1)
    tile_expert = jnp.sum((tile_id[:, None] >= tile_end[None, :]).astype(jnp.int32), axis=1)
    return pos, tile_expert, n_tiles


PLE_TM = 512
PLE_CHUNK = 16


def _ple_body(x_ref, ya_ref, yb_ref, route_ref, p_ref, g3_ref, wpg_ref, bpg_ref, wple_ref, gf_ref,
              *rest):
    o_ref = rest[-1]
    tm = x_ref.shape[0]
    g3, gf, bpg = g3_ref[...], gf_ref[...], bpg_ref[...]
    x2s, h3s = [], []
    for r in range(0, tm, PLE_CHUNK):
        rows = pl.ds(r, PLE_CHUNK)
        ya = jnp.concatenate(_unpack_bf16_pairs(ya_ref[rows, :]), axis=1)
        yb = jnp.concatenate(_unpack_bf16_pairs(yb_ref[rows, :]), axis=1)
        route = route_ref[rows, :]
        x2 = x_ref[rows, :] + route[:, 2:3] * ya + route[:, 3:4] * yb
        x2s.append(x2)
        h3s.append(_rms(x2, g3).astype(BF16))
    h3 = jnp.concatenate(h3s, axis=0)
    zg = jnp.dot(h3, wpg_ref[...], preferred_element_type=F32)
    pe = jnp.dot(p_ref[...].astype(BF16), wple_ref[...], preferred_element_type=F32)
    for i, r in enumerate(range(0, tm, PLE_CHUNK)):
        gate = jax.nn.sigmoid(zg[r:r + PLE_CHUNK, :] + bpg)
        x3 = x2s[i] + gate * pe[r:r + PLE_CHUNK, :]
        o_ref[pl.ds(r, PLE_CHUNK), :] = _rms(x3, gf)


def _ple_call(x1, y_picks, route, p, part, out_so_far, g3, wpg, bpg, wple, gf):
    tg = x1.shape[0]
    nt = tg // PLE_TM
    consts = (g3, wpg, bpg, wple, gf)
    full = lambda i: (part * nt + i, 0)
    operands = [x1, y_picks, y_picks, route, p, *consts]
    in_specs = ([pl.BlockSpec((PLE_TM, D_MODEL), lambda i: (i, 0)),
                 pl.BlockSpec((PLE_TM, D_MODEL // 2), lambda i: (i, 0)),
                 pl.BlockSpec((PLE_TM, D_MODEL // 2), lambda i: (i + nt, 0)),
                 pl.BlockSpec((PLE_TM, LANES), lambda i: (i, 0)),
                 pl.BlockSpec((PLE_TM, p.shape[1]), full)]
                + [_const_spec(c.shape) for c in consts])
    aliases = {}
    if out_so_far is not None:
        aliases = {len(operands): 0}
        operands.append(out_so_far)
        in_specs.append(pl.BlockSpec(memory_space=pl.ANY))
    return pl.pallas_call(
        _ple_body,
        out_shape=jax.ShapeDtypeStruct((p.shape[0], D_MODEL), F32),
        grid=(nt,),
        in_specs=in_specs,
        out_specs=pl.BlockSpec((PLE_TM, D_MODEL), full),
        input_output_aliases=aliases,
        compiler_params=pltpu.CompilerParams(
            dimension_semantics=("arbitrary",), vmem_limit_bytes=VMEM_LIMIT),
        name="ple_final",
    )(*operands)


def _layer(x, p, norm_mix, w_in, b_in, lam_re, lam_im, log_dt, b_re, b_im, c_re, c_im, d_skip,
           w_glu_a, w_glu_b, conv_w, conv_b, w_conv_out, w_o, norm_ffn, w_rg, b_rg, w_re, b_re_r,
           w_eg, w_eu, w_ed, norm_ple, w_ple, w_pg, b_pg, norm_out):
    nb, seq, d = x.shape
    T = nb * seq
    s0 = SSM_WIDTH
    s3 = s0 + 3 * CONV_WIDTH
    row = lambda v: v.reshape(1, -1).astype(F32)

    kmat, mmat, nmat, aq_re, aq_im = _s5_operators(lam_re, lam_im, log_dt, b_re, b_im, c_re, c_im)
    s5_consts = (row(norm_mix), w_in[:, :s0].astype(BF16), row(b_in[:s0]),
                 mmat, kmat, nmat, aq_re, aq_im, row(d_skip))

    w_r = jnp.zeros((d, LANES), F32).at[:, :N_EXPERTS].set(w_re)
    w_r = w_r.at[:, N_EXPERTS:N_EXPERTS + N_EXPERT_GROUPS].set(w_rg)
    b_r = jnp.zeros((1, LANES), F32).at[0, :N_EXPERTS].set(b_re_r)
    b_r = b_r.at[0, N_EXPERTS:N_EXPERTS + N_EXPERT_GROUPS].set(b_rg)
    w_r_hi = w_r.astype(BF16)
    w_r_lo = (w_r - w_r_hi.astype(F32)).astype(BF16)

    mix_consts = (row(norm_mix),
                  w_in[:, s0:s3].astype(BF16), row(b_in[s0:s3]),
                  w_in[:, s3:].astype(BF16), row(b_in[s3:]),
                  conv_w.astype(F32), row(conv_b), w_conv_out.astype(BF16),
                  jnp.concatenate([w_glu_a, w_glu_b], axis=1).astype(BF16), w_o.astype(BF16),
                  row(norm_ffn), jnp.concatenate([w_r_hi, w_r_lo], axis=1), w_r_hi, b_r)
    ple_consts = (row(norm_ple), w_pg.astype(BF16), row(b_pg), w_ple.astype(BF16), row(norm_out))
    p2d = p.reshape(T, -1)
    tg = T // N_PARTS
    n_rows = 2 * tg + N_EXPERTS * MOE_RT

    out = None
    for part in range(N_PARTS):
        ys = _s5_call(x, part, *s5_consts)
        x1, h2p, route, route_t, counts = _mix_call(x, part, ys, *mix_consts)
        pos, tile_expert, n_tiles = _moe_plan(route_t, counts, n_rows)
        xs = _sc_dispatch(h2p, pos[0], pos[1], n_rows)
        ysort = _moe_call(tile_expert, n_tiles, xs, w_eg, w_eu, w_ed)
        y_picks = _sc_gather(ysort, pos.reshape(-1))
        out = _ple_call(x1, y_picks, route, p2d, part, out, *ple_consts)
    return out.reshape(nb, seq, d)


def kernel(x, p, norm_mix, w_in, b_in, ssm_lam_re, ssm_lam_im, ssm_log_dt, ssm_b_re, ssm_b_im, ssm_c_re, ssm_c_im, ssm_d, w_glu_a, w_glu_b, conv_w, conv_b, w_conv_out, w_o, norm_ffn, w_router_group, b_router_group, w_router_expert, b_router_expert, w_exp_gate, w_exp_up, w_exp_down, norm_ple, w_ple, w_ple_gate, b_ple_gate, norm_final):
    assert p.shape[0] == 1, "the final RMSNorm is fused into the (single) layer's last kernel"
    i = 0
    return _layer(x, p[i], norm_mix[i], w_in[i], b_in[i], ssm_lam_re[i], ssm_lam_im[i],
                  ssm_log_dt[i], ssm_b_re[i], ssm_b_im[i], ssm_c_re[i], ssm_c_im[i], ssm_d[i],
                  w_glu_a[i], w_glu_b[i], conv_w[i], conv_b[i], w_conv_out[i], w_o[i],
                  norm_ffn[i], w_router_group[i], b_router_group[i], w_router_expert[i],
                  b_router_expert[i], w_exp_gate[i], w_exp_up[i], w_exp_down[i], norm_ple[i],
                  w_ple[i], w_ple_gate[i], b_ple_gate[i], norm_final)
```

```python
import functools
import math

import jax
import jax.numpy as jnp
from jax import lax
from jax.experimental import pallas as pl
from jax.experimental.pallas import tpu as pltpu
from jax.experimental.pallas import tpu_sc as plsc

F32 = jnp.float32
BF16 = jnp.bfloat16

D_MODEL = 1024
SSM_WIDTH = 512
SSM_GROUP = 16
SSM_GROUPS = 32
SSM_STATE = 64
CONV_WIDTH = 512
N_EXPERT_GROUPS = 4
EXPERTS_PER_GROUP = 8
N_EXPERTS = 32
EXPERT_HIDDEN = 256
NORM_EPS = 1e-6

LANES = 128
Q = 8
GROUPS_PER_LANE_TILE = LANES // SSM_GROUP
N_LANE_TILES = SSM_WIDTH // LANES
STATE_LANES = GROUPS_PER_LANE_TILE * SSM_STATE
S5_ROWS = 1024
N_PARTS = 1
VMEM_LIMIT = 56 * 1024 * 1024


def _rms(x, g):
    return x * lax.rsqrt(jnp.mean(x * x, axis=-1, keepdims=True) + NORM_EPS) * g


def _pack_bf16_pairs(a):
    w = a.shape[1] // 2
    lo = lax.shift_right_logical(lax.bitcast_convert_type(a[:, :w], jnp.int32), 16)
    hi = lax.bitcast_convert_type(a[:, w:], jnp.int32) & jnp.int32(-65536)
    return lo | hi


def _unpack_bf16_pairs(word):
    lo = lax.bitcast_convert_type(lax.shift_left(word, 16), F32)
    hi = lax.bitcast_convert_type(word & jnp.int32(-65536), F32)
    return lo, hi


def _const_spec(shape):
    n = len(shape)
    return pl.BlockSpec(shape, lambda *_: (0,) * n, pipeline_mode=pl.Buffered(1))


def _s5_operators(lam_re, lam_im, log_dt, b_re, b_im, c_re, c_im):
    G, P, H = SSM_GROUPS, SSM_STATE, SSM_GROUP
    J = N_LANE_TILES
    lr = lam_re.astype(F32)
    li = lam_im.astype(F32)
    dt = jnp.exp(log_dt.astype(F32))[:, None]

    def apow(n):
        n = n.astype(F32)[:, None, None]
        mag = jnp.exp(lr * dt * n)
        ang = li * dt * n
        return mag * jnp.cos(ang), mag * jnp.sin(ang)

    a1_re, a1_im = apow(jnp.ones((1,), F32))
    nr = a1_re[0] - 1.0
    ni = a1_im[0]
    den = lr * lr + li * li
    f_re = (nr * lr + ni * li) / den
    f_im = (ni * lr - nr * li) / den
    br = b_re.astype(F32)
    bi = b_im.astype(F32)
    bbar_re = f_re[:, :, None] * br - f_im[:, :, None] * bi
    bbar_im = f_re[:, :, None] * bi + f_im[:, :, None] * br
    to_rows = lambda v, perm: jnp.transpose(v, perm).reshape(H, G * P)
    bt_re = to_rows(bbar_re, (2, 0, 1))
    bt_im = to_rows(bbar_im, (2, 0, 1))
    ct_re = to_rows(c_re.astype(F32), (1, 0, 2))
    ct_im = to_rows(c_im.astype(F32), (1, 0, 2))
    ap_re, ap_im = apow(jnp.arange(Q + 1))
    ap_re = ap_re.reshape(Q + 1, G * P)
    ap_im = ap_im.reshape(Q + 1, G * P)

    blk = lambda r: pl.BlockSpec((r, STATE_LANES), lambda j: (0, j))
    mat = pl.BlockSpec((1, Q * LANES, Q * LANES), lambda j: (j, 0, 0))
    shape = jax.ShapeDtypeStruct((J, Q * LANES, Q * LANES), BF16)
    kmat, mmat, nmat = pl.pallas_call(
        _s5_ops_body,
        out_shape=(shape, shape, shape),
        grid=(J,),
        in_specs=[blk(Q + 1), blk(Q + 1), blk(H), blk(H), blk(H), blk(H)],
        out_specs=(mat, mat, mat),
        compiler_params=pltpu.CompilerParams(
            dimension_semantics=("arbitrary",), vmem_limit_bytes=VMEM_LIMIT),
        name="s5_operators",
    )(ap_re, ap_im, bt_re, bt_im, ct_re, ct_im)
    return kmat, mmat, nmat, ap_re[Q:Q + 1], ap_im[Q:Q + 1]


def _s5_ops_body(apr_ref, api_ref, btr_ref, bti_ref, ctr_ref, cti_ref, k_ref, m_ref, n_ref):
    ri = lax.broadcasted_iota(jnp.int32, (LANES, STATE_LANES), 0)
    li = lax.broadcasted_iota(jnp.int32, (LANES, STATE_LANES), 1)
    same_group = (ri // SSM_GROUP) == (li // SSM_STATE)

    def expand(ref):
        tiled = jnp.concatenate([ref[...]] * GROUPS_PER_LANE_TILE, axis=0)
        return jnp.where(same_group, tiled, 0.0)

    b_re, b_im, c_re, c_im = expand(btr_ref), expand(bti_ref), expand(ctr_ref), expand(cti_ref)

    def cmul(n, x_re, x_im):
        a_re = apr_ref[n:n + 1, :]
        a_im = api_ref[n:n + 1, :]
        return a_re * x_re - a_im * x_im, a_re * x_im + a_im * x_re

    m_blocks = []
    for k in range(Q):
        g_re, g_im = cmul(Q - 1 - k, b_re, b_im)
        m_blocks.append(jnp.concatenate([g_re, g_im], axis=1))
    m = jnp.concatenate(m_blocks, axis=0)

    nt_blocks = []
    for t in range(Q):
        g_re, g_im = cmul(t + 1, c_re, c_im)
        nt_blocks.append(jnp.concatenate([g_re, -g_im], axis=1))
    nt = jnp.concatenate(nt_blocks, axis=0)

    n0t = jnp.concatenate([c_re, -c_im], axis=1)
    p = lax.dot_general(m, n0t, (((1,), (1,)), ((), ())),
                        precision=lax.Precision.HIGHEST, preferred_element_type=F32)
    zeros = jnp.zeros((LANES, LANES), F32)
    cols = []
    for t in range(Q):
        cols.append(jnp.concatenate(
            [p[(Q - 1 - (t - k)) * LANES:(Q - (t - k)) * LANES, :] if t >= k else zeros
             for k in range(Q)], axis=0))
    k_ref[0] = jnp.concatenate(cols, axis=1).astype(BF16)
    m_ref[0] = m.astype(BF16)
    n_ref[0] = nt.T.astype(BF16)


def _s5_body(x_ref, g_ref, wu_ref, bu_ref, m_ref, k_ref, n_ref, aqr_ref, aqi_ref, d_ref,
             o_ref, u_scr, y_scr, z_scr, ss_scr, carry_scr):
    nb, tt = x_ref.shape[0], x_ref.shape[1]
    rows = nb * tt
    nchunk = rows // Q

    @pl.when(pl.program_id(0) == 0)
    def _():
        carry_scr[...] = jnp.zeros_like(carry_scr)

    hb = nb // 2
    for r in range(2):
        x = x_ref[r * hb:(r + 1) * hb].reshape(rows // 2, D_MODEL)
        h = _rms(x, g_ref[...]).astype(BF16)
        u = jnp.dot(h, wu_ref[...], preferred_element_type=F32) + bu_ref[...]
        for j in range(N_LANE_TILES):
            u_scr[j, pl.ds(r * (rows // 2), rows // 2), :] = u[:, j * LANES:(j + 1) * LANES]

    n_st = STATE_LANES // LANES
    cpt = tt // Q

    def chunk_matmuls(j):
        xj = jnp.concatenate(
            [u_scr[j, pl.ds(k, nchunk, stride=Q), :] for k in range(Q)], axis=1).astype(BF16)
        z = jnp.dot(xj, m_ref[j], preferred_element_type=F32)
        for i in range(2 * n_st):
            z_scr[j, i] = z[:, i * LANES:(i + 1) * LANES]
        return jnp.dot(xj, k_ref[j], preferred_element_type=F32)

    def scan_and_finish(j, y_intra):
        aqr = jnp.broadcast_to(aqr_ref[:, pl.ds(j * STATE_LANES, STATE_LANES)], (nb, STATE_LANES))
        aqi = jnp.broadcast_to(aqi_ref[:, pl.ds(j * STATE_LANES, STATE_LANES)], (nb, STATE_LANES))
        s_re = carry_scr[j, :, pl.ds(0, STATE_LANES)]
        s_im = carry_scr[j, :, pl.ds(STATE_LANES, STATE_LANES)]
        for c in range(cpt):
            seq_rows = pl.ds(c, nb, stride=cpt)
            for i in range(n_st):
                ss_scr[j, i, seq_rows, :] = s_re[:, i * LANES:(i + 1) * LANES]
                ss_scr[j, n_st + i, seq_rows, :] = s_im[:, i * LANES:(i + 1) * LANES]
            z_re = jnp.concatenate([z_scr[j, i, seq_rows, :] for i in range(n_st)], axis=1)
            z_im = jnp.concatenate([z_scr[j, n_st + i, seq_rows, :] for i in range(n_st)], axis=1)
            s_re, s_im = (aqr * s_re - aqi * s_im + z_re,
                          aqr * s_im + aqi * s_re + z_im)
        carry_scr[j, :, pl.ds(0, STATE_LANES)] = s_re
        carry_scr[j, :, pl.ds(STATE_LANES, STATE_LANES)] = s_im
        ss = jnp.concatenate([ss_scr[j, i] for i in range(2 * n_st)], axis=1).astype(BF16)
        yj = y_intra + jnp.dot(ss, n_ref[j], preferred_element_type=F32)
        for k in range(Q):
            y_scr[j, pl.ds(k, nchunk, stride=Q), :] = yj[:, k * LANES:(k + 1) * LANES]
        lanes = pl.ds(j * LANES, LANES)
        y = y_scr[j] + d_ref[:, lanes] * u_scr[j]
        o_ref[:, :, lanes] = jax.nn.gelu(y).astype(BF16).reshape(nb, tt, LANES)

    y_intra = {0: chunk_matmuls(0)}
    for j in range(N_LANE_TILES):
        if j + 1 < N_LANE_TILES:
            y_intra[j + 1] = chunk_matmuls(j + 1)
        scan_and_finish(j, y_intra.pop(j))


def _s5_call(x, part, g, wu, bu, mmat, kmat, nmat, aq_re, aq_im, d_skip):
    nb, seq = x.shape[0] // N_PARTS, x.shape[1]
    tt = S5_ROWS // nb
    rows = S5_ROWS
    nchunk = rows // Q
    return pl.pallas_call(
        _s5_body,
        out_shape=jax.ShapeDtypeStruct((nb, seq, SSM_WIDTH), BF16),
        grid=(seq // tt,),
        in_specs=[
            pl.BlockSpec((nb, tt, D_MODEL), lambda i: (part, i, 0)),
            _const_spec(g.shape), _const_spec(wu.shape), _const_spec(bu.shape),
            _const_spec(mmat.shape), _const_spec(kmat.shape), _const_spec(nmat.shape),
            _const_spec(aq_re.shape), _const_spec(aq_im.shape), _const_spec(d_skip.shape),
        ],
        out_specs=pl.BlockSpec((nb, tt, SSM_WIDTH), lambda i: (0, i, 0)),
        scratch_shapes=[
            pltpu.VMEM((N_LANE_TILES, rows, LANES), F32),
            pltpu.VMEM((N_LANE_TILES, rows, LANES), F32),
            pltpu.VMEM((N_LANE_TILES, 2 * STATE_LANES // LANES, nchunk, LANES), F32),
            pltpu.VMEM((N_LANE_TILES, 2 * STATE_LANES // LANES, nchunk, LANES), F32),
            pltpu.VMEM((N_LANE_TILES, nb, 2 * STATE_LANES), F32),
        ],
        compiler_params=pltpu.CompilerParams(
            dimension_semantics=("arbitrary",), vmem_limit_bytes=VMEM_LIMIT),
        name="s5_mixer",
    )(x, g, wu, bu, mmat, kmat, nmat, aq_re, aq_im, d_skip)


MIX_TM = 512


def _mix_body(x_ref, ys_ref, g_ref, wc_ref, bc_ref, wg_ref, bg_ref, cw_ref, cb_ref, wco_ref,
              wab_ref, wo_ref, g2_ref, wr_ref, wrh_ref, br_ref,
              x1_ref, h2p_ref, route_ref, route_t_ref, cnt_ref, carry_scr, cnt_scr, x1_scr,
              *, tiles_per_seq, n_tiles):
    i = pl.program_id(0)
    tm = x_ref.shape[1]

    @pl.when(i == 0)
    def _():
        cnt_scr[...] = jnp.zeros_like(cnt_scr)
        x1_scr[...] = jnp.zeros_like(x1_scr)
        carry_scr[...] = jnp.zeros_like(carry_scr)

    carry_scr[0] = jnp.where(i == n_tiles, carry_scr[0],
                             jnp.where(i % tiles_per_seq == 0, 0.0, carry_scr[1]))

    yab = jnp.dot(ys_ref[0], wab_ref[...], preferred_element_type=F32)

    h2 = _rms(x1_scr[...], g2_ref[...])
    h2_hi = h2.astype(BF16)
    h2p_ref[...] = _pack_bf16_pairs(h2_hi.astype(F32))
    h2_lo = (h2 - h2_hi.astype(F32)).astype(BF16)
    lg2 = jnp.dot(h2_hi, wr_ref[...], preferred_element_type=F32)
    logits = (lg2[:, 0:LANES] + lg2[:, LANES:2 * LANES]
              + jnp.dot(h2_lo, wrh_ref[...], preferred_element_type=F32) + br_ref[...])

    x = x_ref[0]
    h = _rms(x, g_ref[...]).astype(BF16)
    zc = jnp.dot(h, wc_ref[...], preferred_element_type=F32) + bc_ref[...]
    zg = jnp.dot(h, wg_ref[...], preferred_element_type=F32) + bg_ref[...]

    cnt = cnt_scr[...]
    route, new_cnt = _route_tile(logits, cnt)
    cnt = jnp.where(i > 0, new_cnt, cnt)
    cnt_scr[...] = cnt
    cnt_ref[...] = cnt
    route_ref[...] = route
    route_t_ref[...] = route.T[0:8, :]

    y_a = yab[:, 0:D_MODEL] * jax.nn.sigmoid(yab[:, D_MODEL:2 * D_MODEL])

    c_b = zc[:, 0:CONV_WIDTH]
    cv = zc[:, CONV_WIDTH:2 * CONV_WIDTH] * zc[:, 2 * CONV_WIDTH:3 * CONV_WIDTH]
    row = lax.broadcasted_iota(jnp.int32, (tm, CONV_WIDTH), 0)
    last1 = carry_scr[0, 7:8, :]
    last2 = carry_scr[0, 6:7, :]
    p1 = jnp.where(row == 0, last1, pltpu.roll(cv, 1, axis=0))
    p2 = jnp.where(row == 0, last2, jnp.where(row == 1, last1, pltpu.roll(cv, 2, axis=0)))
    carry_scr[1] = cv[tm - 8:tm, :]
    conv = cw_ref[0:1, :] * p2 + cw_ref[1:2, :] * p1 + cw_ref[2:3, :] * cv + cb_ref[...]
    y_b = jnp.dot((c_b * conv).astype(BF16), wco_ref[...], preferred_element_type=F32)

    mix = (jax.nn.sigmoid(zg[:, 0:D_MODEL]) * y_a
           + jax.nn.sigmoid(zg[:, D_MODEL:2 * D_MODEL]) * y_b)
    x1 = x + jnp.dot(mix.astype(BF16), wo_ref[...], preferred_element_type=F32)
    x1_ref[...] = x1
    x1_scr[...] = x1


def _route_tile(logits, cnt):
    tm = logits.shape[0]
    lane = lax.broadcasted_iota(jnp.int32, (tm, LANES), 1)
    neg = jnp.float32(-jnp.inf)
    big = jnp.int32(1 << 20)
    is_g = (lane >= N_EXPERTS) & (lane < N_EXPERTS + N_EXPERT_GROUPS)
    gl = jnp.where(is_g, logits, neg)
    gmax = jnp.max(gl, axis=1, keepdims=True)
    g_w = 1.0 / jnp.sum(jnp.exp(gl - gmax), axis=1, keepdims=True)
    g_idx = jnp.min(jnp.where(gl == gmax, lane - N_EXPERTS, big), axis=1, keepdims=True)
    lo = g_idx * EXPERTS_PER_GROUP
    el = jnp.where((lane >= lo) & (lane < lo + EXPERTS_PER_GROUP), logits, neg)
    m1 = jnp.max(el, axis=1, keepdims=True)
    i1 = jnp.min(jnp.where(el == m1, lane, big), axis=1, keepdims=True)
    el2 = jnp.where(lane == i1, neg, el)
    m2 = jnp.max(el2, axis=1, keepdims=True)
    i2 = jnp.min(jnp.where(el2 == m2, lane, big), axis=1, keepdims=True)
    r = jnp.exp(m2 - m1)
    w1 = g_w / (1.0 + r)
    w2 = g_w * r / (1.0 + r)

    picks = ((lane == i1) | (lane == i2)).astype(BF16)
    r_i = lax.broadcasted_iota(jnp.int32, (tm, tm), 0)
    c_i = lax.broadcasted_iota(jnp.int32, (tm, tm), 1)
    before = (c_i < r_i).astype(BF16)
    excl = jnp.dot(before, picks, preferred_element_type=F32) + cnt
    rank1 = jnp.sum(jnp.where(lane == i1, excl, 0.0), axis=1, keepdims=True)
    rank2 = jnp.sum(jnp.where(lane == i2, excl, 0.0), axis=1, keepdims=True)
    new_cnt = cnt + jnp.sum(picks.astype(F32), axis=0, keepdims=True)
    route = jnp.where(lane == 0, i1.astype(F32), 0.0)
    route = jnp.where(lane == 1, i2.astype(F32), route)
    route = jnp.where(lane == 2, w1, route)
    route = jnp.where(lane == 3, w2, route)
    route = jnp.where(lane == 4, rank1, route)
    route = jnp.where(lane == 5, rank2, route)
    return route, new_cnt


def _mix_call(x, part, ys, g, wc, bc, wg, bg, cw, cb, wco, wab, wo, g2, wr, wrh, br):
    nb, seq, _ = ys.shape
    T = nb * seq
    nl = seq // MIX_TM
    n = nb * nl
    consts = (g, wc, bc, wg, bg, cw, cb, wco, wab, wo, g2, wr, wrh, br)
    cur = lambda i: jnp.minimum(i, n - 1)
    prev = lambda i: jnp.maximum(i - 1, 0)
    return pl.pallas_call(
        functools.partial(_mix_body, tiles_per_seq=nl, n_tiles=n),
        out_shape=(jax.ShapeDtypeStruct((T, D_MODEL), F32),
                   jax.ShapeDtypeStruct((T, D_MODEL // 2), jnp.int32),
                   jax.ShapeDtypeStruct((T, LANES), F32),
                   jax.ShapeDtypeStruct((8, T), F32),
                   jax.ShapeDtypeStruct((1, LANES), F32)),
        grid=(n + 1,),
        in_specs=[pl.BlockSpec((1, MIX_TM, D_MODEL),
                               lambda i: (part * nb + cur(i) // nl, cur(i) % nl, 0)),
                  pl.BlockSpec((1, MIX_TM, SSM_WIDTH), lambda i: (cur(i) // nl, cur(i) % nl, 0))]
                 + [_const_spec(c.shape) for c in consts],
        out_specs=(pl.BlockSpec((MIX_TM, D_MODEL), lambda i: (cur(i), 0)),
                   pl.BlockSpec((MIX_TM, D_MODEL // 2), lambda i: (prev(i), 0)),
                   pl.BlockSpec((MIX_TM, LANES), lambda i: (prev(i), 0)),
                   pl.BlockSpec((8, MIX_TM), lambda i: (0, prev(i))),
                   pl.BlockSpec((1, LANES), lambda i: (0, 0))),
        scratch_shapes=[pltpu.VMEM((2, 8, CONV_WIDTH), F32), pltpu.VMEM((1, LANES), F32),
                        pltpu.VMEM((MIX_TM, D_MODEL), F32)],
        compiler_params=pltpu.CompilerParams(
            dimension_semantics=("arbitrary",), vmem_limit_bytes=VMEM_LIMIT),
        name="conv_glu_router",
    )(x, ys, *consts)


SC_CORES = 2
SC_SUBCORES = 16
SC_WORKERS = SC_CORES * SC_SUBCORES
SC_ROWS = 64


def _sc_mesh():
    return plsc.VectorSubcoreMesh(core_axis_name="c", subcore_axis_name="s")


def _sc_worker_id():
    return lax.axis_index("s") * SC_CORES + lax.axis_index("c")


def _sc_dispatch(rows, pos_a, pos_b, n_out):
    T, W = rows.shape
    per_w = T // SC_WORKERS
    nch = per_w // SC_ROWS
    assert per_w * SC_WORKERS == T and nch * SC_ROWS == per_w and nch % 2 == 0
    idx_a = pos_a.reshape(SC_WORKERS, nch, SC_ROWS)
    idx_b = pos_b.reshape(SC_WORKERS, nch, SC_ROWS)

    @functools.partial(
        pl.kernel, mesh=_sc_mesh(),
        out_type=jax.ShapeDtypeStruct((n_out, W), rows.dtype),
        scratch_types=[
            pltpu.VMEM((nch, SC_ROWS), jnp.int32),
            pltpu.VMEM((nch, SC_ROWS), jnp.int32),
            pltpu.VMEM((2, SC_ROWS, W), rows.dtype),
            pltpu.SemaphoreType.DMA((2,)),
            pltpu.SemaphoreType.DMA((2,)),
            pltpu.SemaphoreType.DMA((2,)),
        ],
        name="moe_dispatch",
    )
    def k(rows_hbm, ia_hbm, ib_hbm, out_hbm, ia_v, ib_v, buf, gsem, asem, bsem):
        wid = _sc_worker_id()
        base = wid * per_w
        pltpu.sync_copy(ia_hbm.at[wid], ia_v)
        pltpu.sync_copy(ib_hbm.at[wid], ib_v)

        def get(j, b):
            return pltpu.make_async_copy(
                rows_hbm.at[pl.ds(base + j * SC_ROWS, SC_ROWS)], buf.at[b], gsem.at[b])

        def put_a(j, b):
            return pltpu.make_async_copy(buf.at[b], out_hbm.at[ia_v.at[j]], asem.at[b])

        def put_b(j, b):
            return pltpu.make_async_copy(buf.at[b], out_hbm.at[ib_v.at[j]], bsem.at[b])

        get(0, 0).start()

        @pl.loop(0, nch, step=2)
        def _(j0):
            for b in range(2):
                j = j0 + b
                nb = 1 - b
                get(j, b).wait()

                @pl.when(j + 1 < nch)
                def _():
                    @pl.when(j >= 1)
                    def _():
                        put_a(j - 1, nb).wait()
                        put_b(j - 1, nb).wait()
                    get(j + 1, nb).start()

                put_a(j, b).start()
                put_b(j, b).start()

        for b in range(2):
            put_a(nch - 2 + b, b).wait()
            put_b(nch - 2 + b, b).wait()

    return k(rows, idx_a, idx_b)


def _sc_gather(table, idx):
    _, W = table.shape
    B = idx.shape[0]
    per_w = B // SC_WORKERS
    nch = per_w // SC_ROWS
    assert per_w * SC_WORKERS == B and nch * SC_ROWS == per_w and nch % 2 == 0
    idx3 = idx.reshape(SC_WORKERS, nch, SC_ROWS)

    @functools.partial(
        pl.kernel, mesh=_sc_mesh(),
        out_type=jax.ShapeDtypeStruct((B, W), table.dtype),
        scratch_types=[
            pltpu.VMEM((nch, SC_ROWS), jnp.int32),
            pltpu.VMEM((2, SC_ROWS, W), table.dtype),
            pltpu.SemaphoreType.DMA((2,)),
            pltpu.SemaphoreType.DMA((2,)),
        ],
        name="moe_combine_gather",
    )
    def k(table_hbm, idx_hbm, out_hbm, idx_v, buf, gsem, osem):
        wid = _sc_worker_id()
        base = wid * per_w
        pltpu.sync_copy(idx_hbm.at[wid], idx_v)

        def get(j, b):
            return pltpu.make_async_copy(table_hbm.at[idx_v.at[j]], buf.at[b], gsem.at[b])

        def put(j, b):
            return pltpu.make_async_copy(
                buf.at[b], out_hbm.at[pl.ds(base + j * SC_ROWS, SC_ROWS)], osem.at[b])

        get(0, 0).start()

        @pl.loop(0, nch, step=2)
        def _(j0):
            for b in range(2):
                j = j0 + b
                nb = 1 - b
                get(j, b).wait()

                @pl.when(j + 1 < nch)
                def _():
                    @pl.when(j >= 1)
                    def _():
                        put(j - 1, nb).wait()
                    get(j + 1, nb).start()

                put(j, b).start()

        for b in range(2):
            put(nch - 2 + b, b).wait()

    return k(table, idx3)


MOE_RT = 512
MOE_SUB = 256


def _moe_body(te_ref, nt_ref, x_ref, wg_ref, wu_ref, wd_ref, o_ref, wg_scr, wu_scr, wd_scr):
    i = pl.program_id(0)
    prev = jnp.maximum(i - 1, 0)

    @pl.when((i == 0) | (te_ref[i] != te_ref[prev]))
    def _():
        wg_scr[...] = wg_ref[0].astype(BF16)
        wu_scr[...] = wu_ref[0].astype(BF16)
        wd_scr[...] = wd_ref[0].astype(BF16)

    @pl.when(i < nt_ref[0])
    def _():
        half = D_MODEL // 2
        n_sub = x_ref.shape[0] // MOE_SUB

        def up_proj(s):
            lo, hi = _unpack_bf16_pairs(x_ref[pl.ds(s * MOE_SUB, MOE_SUB), :])
            lo = lo.astype(BF16)
            hi = hi.astype(BF16)
            gate = (jnp.dot(lo, wg_scr[0:half, :], preferred_element_type=F32)
                    + jnp.dot(hi, wg_scr[half:D_MODEL, :], preferred_element_type=F32))
            up = (jnp.dot(lo, wu_scr[0:half, :], preferred_element_type=F32)
                  + jnp.dot(hi, wu_scr[half:D_MODEL, :], preferred_element_type=F32))
            return gate, up

        def down_proj(gate, up):
            hid = (jax.nn.silu(gate) * up).astype(BF16)
            return jnp.dot(hid, wd_scr[...], preferred_element_type=F32)

        def store(s, y):
            o_ref[pl.ds(s * MOE_SUB, MOE_SUB), :] = _pack_bf16_pairs(y.astype(BF16).astype(F32))

        gu = {0: up_proj(0)}
        ys = {}
        for s in range(n_sub):
            if s + 1 < n_sub:
                gu[s + 1] = up_proj(s + 1)
            ys[s] = down_proj(*gu.pop(s))
            if s >= 1:
                store(s - 1, ys.pop(s - 1))
        store(n_sub - 1, ys.pop(n_sub - 1))


def _moe_call(tile_expert, n_tiles, xs, wg, wu, wd):
    R = xs.shape[0]
    half = D_MODEL // 2
    row_map = lambda i, te, nt: (jnp.minimum(i, nt[0] - 1), 0)
    w_map = lambda i, te, nt: (te[i], 0, 0)
    return pl.pallas_call(
        _moe_body,
        out_shape=jax.ShapeDtypeStruct((R, half), jnp.int32),
        grid_spec=pltpu.PrefetchScalarGridSpec(
            num_scalar_prefetch=2,
            grid=(R // MOE_RT,),
            in_specs=[pl.BlockSpec((MOE_RT, half), row_map),
                      pl.BlockSpec((1, D_MODEL, EXPERT_HIDDEN), w_map),
                      pl.BlockSpec((1, D_MODEL, EXPERT_HIDDEN), w_map),
                      pl.BlockSpec((1, EXPERT_HIDDEN, D_MODEL), w_map)],
            out_specs=pl.BlockSpec((MOE_RT, half), row_map),
            scratch_shapes=[pltpu.VMEM((D_MODEL, EXPERT_HIDDEN), BF16),
                            pltpu.VMEM((D_MODEL, EXPERT_HIDDEN), BF16),
                            pltpu.VMEM((EXPERT_HIDDEN, D_MODEL), BF16)]),
        compiler_params=pltpu.CompilerParams(
            dimension_semantics=("arbitrary",), vmem_limit_bytes=VMEM_LIMIT),
        name="moe_experts",
    )(tile_expert, n_tiles, xs, wg, wu, wd)


def _moe_plan(route_t, counts, n_rows):
    cnt = counts[0, :N_EXPERTS].astype(jnp.int32)
    tiles = (cnt + MOE_RT - 1) // MOE_RT
    tile_end = jnp.cumsum(tiles)
    n_tiles = tile_end[-1:]
    row_start = (tile_end - tiles) * MOE_RT
    ids = route_t[0:2].astype(jnp.int32)
    ranks = route_t[4:6].astype(jnp.int32)
    experts = jnp.arange(N_EXPERTS, dtype=jnp.int32)[:, None, None]
    pos = ranks + jnp.sum(jnp.where(ids[None] == experts, row_start[:, None, None], 0), axis=0)
    tile_id = jnp.minimum(jnp.arange(n_rows // MOE_RT, dtype=jnp.int32), n_tiles - 1)
    tile_expert = jnp.sum((tile_id[:, None] >= tile_end[None, :]).astype(jnp.int32), axis=1)
    return pos, tile_expert, n_tiles


PLE_TM = 1024
PLE_SUB = 256


def _ple_body(x_ref, ya_ref, yb_ref, route_ref, p_ref, g3_ref, wpg_ref, bpg_ref, wple_ref, gf_ref,
              *rest):
    o_ref = rest[-1]
    n_sub = x_ref.shape[0] // PLE_SUB
    g3, gf, bpg = g3_ref[...], gf_ref[...], bpg_ref[...]

    def head(s):
        rows = pl.ds(s * PLE_SUB, PLE_SUB)
        ya = jnp.concatenate(_unpack_bf16_pairs(ya_ref[rows, :]), axis=1)
        yb = jnp.concatenate(_unpack_bf16_pairs(yb_ref[rows, :]), axis=1)
        route = route_ref[rows, :]
        x2 = x_ref[rows, :] + route[:, 2:3] * ya + route[:, 3:4] * yb
        return x2, _rms(x2, g3).astype(BF16)

    def dots(s, h3):
        rows = pl.ds(s * PLE_SUB, PLE_SUB)
        zg = jnp.dot(h3, wpg_ref[...], preferred_element_type=F32)
        pe = jnp.dot(p_ref[rows, :].astype(BF16), wple_ref[...], preferred_element_type=F32)
        return zg, pe

    def tail(s, x2, zg, pe):
        x3 = x2 + jax.nn.sigmoid(zg + bpg) * pe
        o_ref[pl.ds(s * PLE_SUB, PLE_SUB), :] = _rms(x3, gf)

    x2s, mm = {}, {}
    x2s[0], h3 = head(0)
    for s in range(n_sub):
        mm[s] = dots(s, h3)
        if s + 1 < n_sub:
            x2s[s + 1], h3 = head(s + 1)
        if s >= 1:
            tail(s - 1, x2s.pop(s - 1), *mm.pop(s - 1))
    tail(n_sub - 1, x2s.pop(n_sub - 1), *mm.pop(n_sub - 1))


def _ple_call(x1, y_picks, route, p, part, out_so_far, g3, wpg, bpg, wple, gf):
    tg = x1.shape[0]
    nt = tg // PLE_TM
    consts = (g3, wpg, bpg, wple, gf)
    full = lambda i: (part * nt + i, 0)
    operands = [x1, y_picks, y_picks, route, p, *consts]
    in_specs = ([pl.BlockSpec((PLE_TM, D_MODEL), lambda i: (i, 0)),
                 pl.BlockSpec((PLE_TM, D_MODEL // 2), lambda i: (i, 0)),
                 pl.BlockSpec((PLE_TM, D_MODEL // 2), lambda i: (i + nt, 0)),
                 pl.BlockSpec((PLE_TM, LANES), lambda i: (i, 0)),
                 pl.BlockSpec((PLE_TM, p.shape[1]), full)]
                + [_const_spec(c.shape) for c in consts])
    aliases = {}
    if out_so_far is not None:
        aliases = {len(operands): 0}
        operands.append(out_so_far)
        in_specs.append(pl.BlockSpec(memory_space=pl.ANY))
    return pl.pallas_call(
        _ple_body,
        out_shape=jax.ShapeDtypeStruct((p.shape[0], D_MODEL), F32),
        grid=(nt,),
        in_specs=in_specs,
        out_specs=pl.BlockSpec((PLE_TM, D_MODEL), full),
        input_output_aliases=aliases,
        compiler_params=pltpu.CompilerParams(
            dimension_semantics=("arbitrary",), vmem_limit_bytes=VMEM_LIMIT),
        name="ple_final",
    )(*operands)


def _layer(x, p, norm_mix, w_in, b_in, lam_re, lam_im, log_dt, b_re, b_im, c_re, c_im, d_skip,
           w_glu_a, w_glu_b, conv_w, conv_b, w_conv_out, w_o, norm_ffn, w_rg, b_rg, w_re, b_re_r,
           w_eg, w_eu, w_ed, norm_ple, w_ple, w_pg, b_pg, norm_out):
    nb, seq, d = x.shape
    T = nb * seq
    s0 = SSM_WIDTH
    s3 = s0 + 3 * CONV_WIDTH
    row = lambda v: v.reshape(1, -1).astype(F32)

    kmat, mmat, nmat, aq_re, aq_im = _s5_operators(lam_re, lam_im, log_dt, b_re, b_im, c_re, c_im)
    s5_consts = (row(norm_mix), w_in[:, :s0].astype(BF16), row(b_in[:s0]),
                 mmat, kmat, nmat, aq_re, aq_im, row(d_skip))

    w_r = jnp.zeros((d, LANES), F32).at[:, :N_EXPERTS].set(w_re)
    w_r = w_r.at[:, N_EXPERTS:N_EXPERTS + N_EXPERT_GROUPS].set(w_rg)
    b_r = jnp.zeros((1, LANES), F32).at[0, :N_EXPERTS].set(b_re_r)
    b_r = b_r.at[0, N_EXPERTS:N_EXPERTS + N_EXPERT_GROUPS].set(b_rg)
    w_r_hi = w_r.astype(BF16)
    w_r_lo = (w_r - w_r_hi.astype(F32)).astype(BF16)

    mix_consts = (row(norm_mix),
                  w_in[:, s0:s3].astype(BF16), row(b_in[s0:s3]),
                  w_in[:, s3:].astype(BF16), row(b_in[s3:]),
                  conv_w.astype(F32), row(conv_b), w_conv_out.astype(BF16),
                  jnp.concatenate([w_glu_a, w_glu_b], axis=1).astype(BF16), w_o.astype(BF16),
                  row(norm_ffn), jnp.concatenate([w_r_hi, w_r_lo], axis=1), w_r_hi, b_r)
    ple_consts = (row(norm_ple), w_pg.astype(BF16), row(b_pg), w_ple.astype(BF16), row(norm_out))
    p2d = p.reshape(T, -1)
    tg = T // N_PARTS
    n_rows = 2 * tg + N_EXPERTS * MOE_RT

    out = None
    for part in range(N_PARTS):
        ys = _s5_call(x, part, *s5_consts)
        x1, h2p, route, route_t, counts = _mix_call(x, part, ys, *mix_consts)
        pos, tile_expert, n_tiles = _moe_plan(route_t, counts, n_rows)
        xs = _sc_dispatch(h2p, pos[0], pos[1], n_rows)
        ysort = _moe_call(tile_expert, n_tiles, xs, w_eg, w_eu, w_ed)
        y_picks = _sc_gather(ysort, pos.reshape(-1))
        out = _ple_call(x1, y_picks, route, p2d, part, out, *ple_consts)
    return out.reshape(nb, seq, d)


def kernel(x, p, norm_mix, w_in, b_in, ssm_lam_re, ssm_lam_im, ssm_log_dt, ssm_b_re, ssm_b_im, ssm_c_re, ssm_c_im, ssm_d, w_glu_a, w_glu_b, conv_w, conv_b, w_conv_out, w_o, norm_ffn, w_router_group, b_router_group, w_router_expert, b_router_expert, w_exp_gate, w_exp_up, w_exp_down, norm_ple, w_ple, w_ple_gate, b_ple_gate, norm_final):
    assert p.shape[0] == 1, "the final RMSNorm is fused into the (single) layer's last kernel"
    i = 0
    return _layer(x, p[i], norm_mix[i], w_in[i], b_in[i], ssm_lam_re[i], ssm_lam_im[i],
                  ssm_log_dt[i], ssm_b_re[i], ssm_b_im[i], ssm_c_re[i], ssm_c_im[i], ssm_d[i],
                  w_glu_a[i], w_glu_b[i], conv_w[i], conv_b[i], w_conv_out[i], w_o[i],
                  norm_ffn[i], w_router_group[i], b_router_group[i], w_router_expert[i],
                  b_router_expert[i], w_exp_gate[i], w_exp_up[i], w_exp_down[i], norm_ple[i],
                  w_ple[i], w_ple_gate[i], b_ple_gate[i], norm_final)
```

```python
import functools
import math

import jax
import jax.numpy as jnp
from jax import lax
from jax.experimental import pallas as pl
from jax.experimental.pallas import tpu as pltpu
from jax.experimental.pallas import tpu_sc as plsc

F32 = jnp.float32
BF16 = jnp.bfloat16

D_MODEL = 1024
SSM_WIDTH = 512
SSM_GROUP = 16
SSM_GROUPS = 32
SSM_STATE = 64
CONV_WIDTH = 512
N_EXPERT_GROUPS = 4
EXPERTS_PER_GROUP = 8
N_EXPERTS = 32
EXPERT_HIDDEN = 256
NORM_EPS = 1e-6

LANES = 128
Q = 8
GROUPS_PER_LANE_TILE = LANES // SSM_GROUP
N_LANE_TILES = SSM_WIDTH // LANES
STATE_LANES = GROUPS_PER_LANE_TILE * SSM_STATE
S5_ROWS = 1024
VMEM_LIMIT = 56 * 1024 * 1024


def _rms(x, g):
    return x * lax.rsqrt(jnp.mean(x * x, axis=-1, keepdims=True) + NORM_EPS) * g


def _pack_bf16_pairs(a):
    w = a.shape[1] // 2
    lo = lax.shift_right_logical(lax.bitcast_convert_type(a[:, :w], jnp.int32), 16)
    hi = lax.bitcast_convert_type(a[:, w:], jnp.int32) & jnp.int32(-65536)
    return lo | hi


def _unpack_bf16_pairs(word):
    lo = lax.bitcast_convert_type(lax.shift_left(word, 16), F32)
    hi = lax.bitcast_convert_type(word & jnp.int32(-65536), F32)
    return lo, hi


def _const_spec(shape):
    n = len(shape)
    return pl.BlockSpec(shape, lambda *_: (0,) * n, pipeline_mode=pl.Buffered(1))


def _s5_operators(lam_re, lam_im, log_dt, b_re, b_im, c_re, c_im):
    G, P, H = SSM_GROUPS, SSM_STATE, SSM_GROUP
    J = N_LANE_TILES
    lr = lam_re.astype(F32)
    li = lam_im.astype(F32)
    dt = jnp.exp(log_dt.astype(F32))[:, None]

    def apow(n):
        n = n.astype(F32)[:, None, None]
        mag = jnp.exp(lr * dt * n)
        ang = li * dt * n
        return mag * jnp.cos(ang), mag * jnp.sin(ang)

    a1_re, a1_im = apow(jnp.ones((1,), F32))
    nr = a1_re[0] - 1.0
    ni = a1_im[0]
    den = lr * lr + li * li
    f_re = (nr * lr + ni * li) / den
    f_im = (ni * lr - nr * li) / den
    br = b_re.astype(F32)
    bi = b_im.astype(F32)
    bbar_re = f_re[:, :, None] * br - f_im[:, :, None] * bi
    bbar_im = f_re[:, :, None] * bi + f_im[:, :, None] * br
    to_rows = lambda v, perm: jnp.transpose(v, perm).reshape(H, G * P)
    bt_re = to_rows(bbar_re, (2, 0, 1))
    bt_im = to_rows(bbar_im, (2, 0, 1))
    ct_re = to_rows(c_re.astype(F32), (1, 0, 2))
    ct_im = to_rows(c_im.astype(F32), (1, 0, 2))
    ap_re, ap_im = apow(jnp.arange(Q + 1))
    ap_re = ap_re.reshape(Q + 1, G * P)
    ap_im = ap_im.reshape(Q + 1, G * P)

    blk = lambda r: pl.BlockSpec((r, STATE_LANES), lambda j: (0, j))
    mat = lambda r, c: pl.BlockSpec((1, r, c), lambda j: (j, 0, 0))
    shape = lambda r, c: jax.ShapeDtypeStruct((J, r, c), BF16)
    qx, st = Q * LANES, 2 * STATE_LANES
    kmat, mmat, nmat = pl.pallas_call(
        _s5_ops_body,
        out_shape=(shape(qx, qx), shape(qx, st), shape(st, qx)),
        grid=(J,),
        in_specs=[blk(Q + 1), blk(Q + 1), blk(H), blk(H), blk(H), blk(H)],
        out_specs=(mat(qx, qx), mat(qx, st), mat(st, qx)),
        compiler_params=pltpu.CompilerParams(
            dimension_semantics=("arbitrary",), vmem_limit_bytes=VMEM_LIMIT),
        name="s5_operators",
    )(ap_re, ap_im, bt_re, bt_im, ct_re, ct_im)
    return kmat, mmat, nmat, ap_re[Q:Q + 1], ap_im[Q:Q + 1]


def _s5_ops_body(apr_ref, api_ref, btr_ref, bti_ref, ctr_ref, cti_ref, k_ref, m_ref, n_ref):
    ri = lax.broadcasted_iota(jnp.int32, (LANES, STATE_LANES), 0)
    li = lax.broadcasted_iota(jnp.int32, (LANES, STATE_LANES), 1)
    same_group = (ri // SSM_GROUP) == (li // SSM_STATE)

    def expand(ref):
        tiled = jnp.concatenate([ref[...]] * GROUPS_PER_LANE_TILE, axis=0)
        return jnp.where(same_group, tiled, 0.0)

    b_re, b_im, c_re, c_im = expand(btr_ref), expand(bti_ref), expand(ctr_ref), expand(cti_ref)

    def cmul(n, x_re, x_im):
        a_re = apr_ref[n:n + 1, :]
        a_im = api_ref[n:n + 1, :]
        return a_re * x_re - a_im * x_im, a_re * x_im + a_im * x_re

    m_blocks = []
    for k in range(Q):
        g_re, g_im = cmul(Q - 1 - k, b_re, b_im)
        m_blocks.append(jnp.concatenate([g_re, g_im], axis=1))
    m = jnp.concatenate(m_blocks, axis=0)

    nt_blocks = []
    for t in range(Q):
        g_re, g_im = cmul(t + 1, c_re, c_im)
        nt_blocks.append(jnp.concatenate([g_re, -g_im], axis=1))
    nt = jnp.concatenate(nt_blocks, axis=0)

    n0t = jnp.concatenate([c_re, -c_im], axis=1)
    p = lax.dot_general(m, n0t, (((1,), (1,)), ((), ())),
                        precision=lax.Precision.HIGHEST, preferred_element_type=F32)
    zeros = jnp.zeros((LANES, LANES), F32)
    cols = []
    for t in range(Q):
        cols.append(jnp.concatenate(
            [p[(Q - 1 - (t - k)) * LANES:(Q - (t - k)) * LANES, :] if t >= k else zeros
             for k in range(Q)], axis=0))
    k_ref[0] = jnp.concatenate(cols, axis=1).astype(BF16)
    m_ref[0] = m.astype(BF16)
    n_ref[0] = nt.T.astype(BF16)


def _s5_body(x_ref, g_ref, wu_ref, bu_ref, m_ref, k_ref, n_ref, aqr_ref, aqi_ref, d_ref,
             o_ref, u_scr, y_scr, z_scr, ss_scr, carry_scr):
    nb, tt = x_ref.shape[0], x_ref.shape[1]
    rows = nb * tt
    nchunk = rows // Q

    @pl.when(pl.program_id(0) == 0)
    def _():
        carry_scr[...] = jnp.zeros_like(carry_scr)

    hb = nb // 2
    for r in range(2):
        x = x_ref[r * hb:(r + 1) * hb].reshape(rows // 2, D_MODEL)
        h = _rms(x, g_ref[...]).astype(BF16)
        u = jnp.dot(h, wu_ref[...], preferred_element_type=F32) + bu_ref[...]
        for j in range(N_LANE_TILES):
            u_scr[j, pl.ds(r * (rows // 2), rows // 2), :] = u[:, j * LANES:(j + 1) * LANES]

    n_st = STATE_LANES // LANES
    cpt = tt // Q

    lane_tiles = range(N_LANE_TILES)


    xs = [jnp.concatenate([u_scr[j, pl.ds(k, nchunk, stride=Q), :] for k in range(Q)],
                          axis=1).astype(BF16) for j in lane_tiles]
    for j in lane_tiles:
        z = jnp.dot(xs[j], m_ref[j], preferred_element_type=F32)
        for i in range(2 * n_st):
            z_scr[j, i] = z[:, i * LANES:(i + 1) * LANES]
    y_intra = [jnp.dot(xs[j], k_ref[j], preferred_element_type=F32) for j in lane_tiles]

    aq = [(jnp.broadcast_to(aqr_ref[:, pl.ds(j * STATE_LANES, STATE_LANES)], (nb, STATE_LANES)),
           jnp.broadcast_to(aqi_ref[:, pl.ds(j * STATE_LANES, STATE_LANES)], (nb, STATE_LANES)))
          for j in lane_tiles]
    st = [(carry_scr[j, :, pl.ds(0, STATE_LANES)], carry_scr[j, :, pl.ds(STATE_LANES, STATE_LANES)])
          for j in lane_tiles]
    for c in range(cpt):
        seq_rows = pl.ds(c, nb, stride=cpt)
        for j in lane_tiles:
            s_re, s_im = st[j]
            aqr, aqi = aq[j]
            for i in range(n_st):
                ss_scr[j, i, seq_rows, :] = s_re[:, i * LANES:(i + 1) * LANES]
                ss_scr[j, n_st + i, seq_rows, :] = s_im[:, i * LANES:(i + 1) * LANES]
            z_re = jnp.concatenate([z_scr[j, i, seq_rows, :] for i in range(n_st)], axis=1)
            z_im = jnp.concatenate([z_scr[j, n_st + i, seq_rows, :] for i in range(n_st)], axis=1)
            st[j] = (aqr * s_re - aqi * s_im + z_re, aqr * s_im + aqi * s_re + z_im)
    for j in lane_tiles:
        carry_scr[j, :, pl.ds(0, STATE_LANES)] = st[j][0]
        carry_scr[j, :, pl.ds(STATE_LANES, STATE_LANES)] = st[j][1]

    def state_to_output(j):
        ss = jnp.concatenate([ss_scr[j, i] for i in range(2 * n_st)], axis=1).astype(BF16)
        return y_intra[j] + jnp.dot(ss, n_ref[j], preferred_element_type=F32)

    def finish(j, yj):
        for k in range(Q):
            y_scr[j, pl.ds(k, nchunk, stride=Q), :] = yj[:, k * LANES:(k + 1) * LANES]
        lanes = pl.ds(j * LANES, LANES)
        y = y_scr[j] + d_ref[:, lanes] * u_scr[j]
        o_ref[:, :, lanes] = jax.nn.gelu(y).astype(BF16).reshape(nb, tt, LANES)

    yj = state_to_output(0)
    for j in lane_tiles:
        y_next = state_to_output(j + 1) if j + 1 < N_LANE_TILES else None
        finish(j, yj)
        yj = y_next


def _s5_call(x, g, wu, bu, mmat, kmat, nmat, aq_re, aq_im, d_skip):
    nb, seq = x.shape[0], x.shape[1]
    tt = S5_ROWS // nb
    rows = S5_ROWS
    nchunk = rows // Q
    return pl.pallas_call(
        _s5_body,
        out_shape=jax.ShapeDtypeStruct((nb, seq, SSM_WIDTH), BF16),
        grid=(seq // tt,),
        in_specs=[
            pl.BlockSpec((nb, tt, D_MODEL), lambda i: (0, i, 0)),
            _const_spec(g.shape), _const_spec(wu.shape), _const_spec(bu.shape),
            _const_spec(mmat.shape), _const_spec(kmat.shape), _const_spec(nmat.shape),
            _const_spec(aq_re.shape), _const_spec(aq_im.shape), _const_spec(d_skip.shape),
        ],
        out_specs=pl.BlockSpec((nb, tt, SSM_WIDTH), lambda i: (0, i, 0)),
        scratch_shapes=[
            pltpu.VMEM((N_LANE_TILES, rows, LANES), F32),
            pltpu.VMEM((N_LANE_TILES, rows, LANES), F32),
            pltpu.VMEM((N_LANE_TILES, 2 * STATE_LANES // LANES, nchunk, LANES), F32),
            pltpu.VMEM((N_LANE_TILES, 2 * STATE_LANES // LANES, nchunk, LANES), F32),
            pltpu.VMEM((N_LANE_TILES, nb, 2 * STATE_LANES), F32),
        ],
        compiler_params=pltpu.CompilerParams(
            dimension_semantics=("arbitrary",), vmem_limit_bytes=VMEM_LIMIT),
        name="s5_mixer",
    )(x, g, wu, bu, mmat, kmat, nmat, aq_re, aq_im, d_skip)


MIX_TM = 512


def _mix_body(x_ref, ys_ref, g_ref, wc_ref, bc_ref, wg_ref, bg_ref, cw_ref, cb_ref, wco_ref,
              wab_ref, wo_ref, g2_ref, wr_ref, wrh_ref, br_ref,
              x1_ref, h2p_ref, route_ref, route_t_ref, cnt_ref, carry_scr, cnt_scr, x1_scr,
              *, tiles_per_seq, n_tiles):
    i = pl.program_id(0)
    tm = x_ref.shape[1]

    @pl.when(i == 0)
    def _():
        cnt_scr[...] = jnp.zeros_like(cnt_scr)
        x1_scr[...] = jnp.zeros_like(x1_scr)
        carry_scr[...] = jnp.zeros_like(carry_scr)

    carry_scr[0] = jnp.where(i == n_tiles, carry_scr[0],
                             jnp.where(i % tiles_per_seq == 0, 0.0, carry_scr[1]))

    yab = jnp.dot(ys_ref[0], wab_ref[...], preferred_element_type=F32)

    h2 = _rms(x1_scr[...], g2_ref[...])
    h2_hi = h2.astype(BF16)
    h2p_ref[...] = _pack_bf16_pairs(h2_hi.astype(F32))
    h2_lo = (h2 - h2_hi.astype(F32)).astype(BF16)
    lg2 = jnp.dot(h2_hi, wr_ref[...], preferred_element_type=F32)
    logits = (lg2[:, 0:LANES] + lg2[:, LANES:2 * LANES]
              + jnp.dot(h2_lo, wrh_ref[...], preferred_element_type=F32) + br_ref[...])

    x = x_ref[0]
    h = _rms(x, g_ref[...]).astype(BF16)
    zc = jnp.dot(h, wc_ref[...], preferred_element_type=F32) + bc_ref[...]
    zg = jnp.dot(h, wg_ref[...], preferred_element_type=F32) + bg_ref[...]

    cnt = cnt_scr[...]
    route, new_cnt = _route_tile(logits, cnt)
    cnt = jnp.where(i > 0, new_cnt, cnt)
    cnt_scr[...] = cnt
    cnt_ref[...] = cnt
    route_ref[...] = route
    route_t_ref[...] = route.T[0:8, :]

    y_a = yab[:, 0:D_MODEL] * jax.nn.sigmoid(yab[:, D_MODEL:2 * D_MODEL])

    c_b = zc[:, 0:CONV_WIDTH]
    cv = zc[:, CONV_WIDTH:2 * CONV_WIDTH] * zc[:, 2 * CONV_WIDTH:3 * CONV_WIDTH]
    row = lax.broadcasted_iota(jnp.int32, (tm, CONV_WIDTH), 0)
    last1 = carry_scr[0, 7:8, :]
    last2 = carry_scr[0, 6:7, :]
    p1 = jnp.where(row == 0, last1, pltpu.roll(cv, 1, axis=0))
    p2 = jnp.where(row == 0, last2, jnp.where(row == 1, last1, pltpu.roll(cv, 2, axis=0)))
    carry_scr[1] = cv[tm - 8:tm, :]
    conv = cw_ref[0:1, :] * p2 + cw_ref[1:2, :] * p1 + cw_ref[2:3, :] * cv + cb_ref[...]
    y_b = jnp.dot((c_b * conv).astype(BF16), wco_ref[...], preferred_element_type=F32)

    mix = (jax.nn.sigmoid(zg[:, 0:D_MODEL]) * y_a
           + jax.nn.sigmoid(zg[:, D_MODEL:2 * D_MODEL]) * y_b)
    x1 = x + jnp.dot(mix.astype(BF16), wo_ref[...], preferred_element_type=F32)
    x1_ref[...] = x1
    x1_scr[...] = x1


def _route_tile(logits, cnt):
    tm = logits.shape[0]
    lane = lax.broadcasted_iota(jnp.int32, (tm, LANES), 1)
    neg = jnp.float32(-jnp.inf)
    big = jnp.int32(1 << 20)
    is_g = (lane >= N_EXPERTS) & (lane < N_EXPERTS + N_EXPERT_GROUPS)
    gl = jnp.where(is_g, logits, neg)
    gmax = jnp.max(gl, axis=1, keepdims=True)
    g_w = 1.0 / jnp.sum(jnp.exp(gl - gmax), axis=1, keepdims=True)
    g_idx = jnp.min(jnp.where(gl == gmax, lane - N_EXPERTS, big), axis=1, keepdims=True)
    lo = g_idx * EXPERTS_PER_GROUP
    el = jnp.where((lane >= lo) & (lane < lo + EXPERTS_PER_GROUP), logits, neg)
    m1 = jnp.max(el, axis=1, keepdims=True)
    i1 = jnp.min(jnp.where(el == m1, lane, big), axis=1, keepdims=True)
    el2 = jnp.where(lane == i1, neg, el)
    m2 = jnp.max(el2, axis=1, keepdims=True)
    i2 = jnp.min(jnp.where(el2 == m2, lane, big), axis=1, keepdims=True)
    r = jnp.exp(m2 - m1)
    w1 = g_w / (1.0 + r)
    w2 = g_w * r / (1.0 + r)

    picks = ((lane == i1) | (lane == i2)).astype(BF16)
    r_i = lax.broadcasted_iota(jnp.int32, (tm, tm), 0)
    c_i = lax.broadcasted_iota(jnp.int32, (tm, tm), 1)
    before = (c_i < r_i).astype(BF16)
    excl = jnp.dot(before, picks, preferred_element_type=F32) + cnt
    rank1 = jnp.sum(jnp.where(lane == i1, excl, 0.0), axis=1, keepdims=True)
    rank2 = jnp.sum(jnp.where(lane == i2, excl, 0.0), axis=1, keepdims=True)
    new_cnt = cnt + jnp.sum(picks.astype(F32), axis=0, keepdims=True)
    route = jnp.where(lane == 0, i1.astype(F32), 0.0)
    route = jnp.where(lane == 1, i2.astype(F32), route)
    route = jnp.where(lane == 2, w1, route)
    route = jnp.where(lane == 3, w2, route)
    route = jnp.where(lane == 4, rank1, route)
    route = jnp.where(lane == 5, rank2, route)
    return route, new_cnt


def _mix_call(x, ys, g, wc, bc, wg, bg, cw, cb, wco, wab, wo, g2, wr, wrh, br):
    nb, seq, _ = ys.shape
    T = nb * seq
    nl = seq // MIX_TM
    n = nb * nl
    consts = (g, wc, bc, wg, bg, cw, cb, wco, wab, wo, g2, wr, wrh, br)
    cur = lambda i: jnp.minimum(i, n - 1)
    prev = lambda i: jnp.maximum(i - 1, 0)
    return pl.pallas_call(
        functools.partial(_mix_body, tiles_per_seq=nl, n_tiles=n),
        out_shape=(jax.ShapeDtypeStruct((T, D_MODEL), F32),
                   jax.ShapeDtypeStruct((T, D_MODEL // 2), jnp.int32),
                   jax.ShapeDtypeStruct((T, LANES), F32),
                   jax.ShapeDtypeStruct((8, T), F32),
                   jax.ShapeDtypeStruct((1, LANES), F32)),
        grid=(n + 1,),
        in_specs=[pl.BlockSpec((1, MIX_TM, D_MODEL),
                               lambda i: (cur(i) // nl, cur(i) % nl, 0)),
                  pl.BlockSpec((1, MIX_TM, SSM_WIDTH), lambda i: (cur(i) // nl, cur(i) % nl, 0))]
                 + [_const_spec(c.shape) for c in consts],
        out_specs=(pl.BlockSpec((MIX_TM, D_MODEL), lambda i: (cur(i), 0)),
                   pl.BlockSpec((MIX_TM, D_MODEL // 2), lambda i: (prev(i), 0)),
                   pl.BlockSpec((MIX_TM, LANES), lambda i: (prev(i), 0)),
                   pl.BlockSpec((8, MIX_TM), lambda i: (0, prev(i))),
                   pl.BlockSpec((1, LANES), lambda i: (0, 0))),
        scratch_shapes=[pltpu.VMEM((2, 8, CONV_WIDTH), F32), pltpu.VMEM((1, LANES), F32),
                        pltpu.VMEM((MIX_TM, D_MODEL), F32)],
        compiler_params=pltpu.CompilerParams(
            dimension_semantics=("arbitrary",), vmem_limit_bytes=VMEM_LIMIT),
        name="conv_glu_router",
    )(x, ys, *consts)


SC_CORES = 2
SC_SUBCORES = 16
SC_WORKERS = SC_CORES * SC_SUBCORES
SC_ROWS = 64
SC_NBUF = 2


def _sc_mesh():
    return plsc.VectorSubcoreMesh(core_axis_name="c", subcore_axis_name="s")


def _sc_worker_id():
    return lax.axis_index("s") * SC_CORES + lax.axis_index("c")


def _sc_ring(nch, get, puts):
    for b in range(SC_NBUF - 1):
        get(b, b).start()

    @pl.loop(0, nch, step=SC_NBUF)
    def _(j0):
        for b in range(SC_NBUF):
            j = j0 + b
            refill = (b - 1) % SC_NBUF
            get(j, b).wait()

            @pl.when(j + SC_NBUF - 1 < nch)
            def _():
                @pl.when(j >= 1)
                def _():
                    for c in puts(j - 1, refill):
                        c.wait()
                get(j + SC_NBUF - 1, refill).start()

            for c in puts(j, b):
                c.start()

    for b in range(SC_NBUF):
        for c in puts(nch - SC_NBUF + b, b):
            c.wait()


def _sc_dispatch(rows, pos_a, pos_b, n_out):
    T, W = rows.shape
    per_w = T // SC_WORKERS
    nch = per_w // SC_ROWS
    assert per_w * SC_WORKERS == T and nch * SC_ROWS == per_w and nch % SC_NBUF == 0
    idx_a = pos_a.reshape(SC_WORKERS, nch, SC_ROWS)
    idx_b = pos_b.reshape(SC_WORKERS, nch, SC_ROWS)

    @functools.partial(
        pl.kernel, mesh=_sc_mesh(),
        out_type=jax.ShapeDtypeStruct((n_out, W), rows.dtype),
        scratch_types=[
            pltpu.VMEM((nch, SC_ROWS), jnp.int32),
            pltpu.VMEM((nch, SC_ROWS), jnp.int32),
            pltpu.VMEM((SC_NBUF, SC_ROWS, W), rows.dtype),
            pltpu.SemaphoreType.DMA((SC_NBUF,)),
            pltpu.SemaphoreType.DMA((SC_NBUF,)),
            pltpu.SemaphoreType.DMA((SC_NBUF,)),
        ],
        name="moe_dispatch",
    )
    def k(rows_hbm, ia_hbm, ib_hbm, out_hbm, ia_v, ib_v, buf, gsem, asem, bsem):
        wid = _sc_worker_id()
        base = wid * per_w
        pltpu.sync_copy(ia_hbm.at[wid], ia_v)
        pltpu.sync_copy(ib_hbm.at[wid], ib_v)

        def get(j, b):
            return pltpu.make_async_copy(
                rows_hbm.at[pl.ds(base + j * SC_ROWS, SC_ROWS)], buf.at[b], gsem.at[b])

        def put_a(j, b):
            return pltpu.make_async_copy(buf.at[b], out_hbm.at[ia_v.at[j]], asem.at[b])

        def put_b(j, b):
            return pltpu.make_async_copy(buf.at[b], out_hbm.at[ib_v.at[j]], bsem.at[b])

        _sc_ring(nch, get, lambda j, b: (put_a(j, b), put_b(j, b)))

    return k(rows, idx_a, idx_b)


def _sc_gather(table, idx):
    _, W = table.shape
    B = idx.shape[0]
    per_w = B // SC_WORKERS
    nch = per_w // SC_ROWS
    assert per_w * SC_WORKERS == B and nch * SC_ROWS == per_w and nch % SC_NBUF == 0
    idx3 = idx.reshape(SC_WORKERS, nch, SC_ROWS)

    @functools.partial(
        pl.kernel, mesh=_sc_mesh(),
        out_type=jax.ShapeDtypeStruct((B, W), table.dtype),
        scratch_types=[
            pltpu.VMEM((nch, SC_ROWS), jnp.int32),
            pltpu.VMEM((SC_NBUF, SC_ROWS, W), table.dtype),
            pltpu.SemaphoreType.DMA((SC_NBUF,)),
            pltpu.SemaphoreType.DMA((SC_NBUF,)),
        ],
        name="moe_combine_gather",
    )
    def k(table_hbm, idx_hbm, out_hbm, idx_v, buf, gsem, osem):
        wid = _sc_worker_id()
        base = wid * per_w
        pltpu.sync_copy(idx_hbm.at[wid], idx_v)

        def get(j, b):
            return pltpu.make_async_copy(table_hbm.at[idx_v.at[j]], buf.at[b], gsem.at[b])

        def put(j, b):
            return pltpu.make_async_copy(
                buf.at[b], out_hbm.at[pl.ds(base + j * SC_ROWS, SC_ROWS)], osem.at[b])

        _sc_ring(nch, get, lambda j, b: (put(j, b),))

    return k(table, idx3)


MOE_RT = 512
MOE_SUB = 256


def _moe_body(te_ref, nt_ref, nx_ref, seg_ref, x_ref, wg_hbm, wu_hbm, wd_hbm, o_ref,
              wg_scr, wu_scr, wd_scr, wg_buf, wu_buf, wd_buf, sem):
    i = pl.program_id(0)
    expert = te_ref[i]
    slot = seg_ref[i] % 2

    def weight_copies(e, b):
        return (pltpu.make_async_copy(wg_hbm.at[e], wg_buf.at[b], sem.at[b, 0]),
                pltpu.make_async_copy(wu_hbm.at[e], wu_buf.at[b], sem.at[b, 1]),
                pltpu.make_async_copy(wd_hbm.at[e], wd_buf.at[b], sem.at[b, 2]))

    @pl.when(i == 0)
    def _():
        for c in weight_copies(expert, 0):
            c.start()

    @pl.when((i == 0) | (expert != te_ref[jnp.maximum(i - 1, 0)]))
    def _():
        for c in weight_copies(expert, slot):
            c.wait()
        wg_scr[...] = wg_buf[slot].astype(BF16)
        wu_scr[...] = wu_buf[slot].astype(BF16)
        wd_scr[...] = wd_buf[slot].astype(BF16)

        @pl.when(nx_ref[i] >= 0)
        def _():
            for c in weight_copies(nx_ref[i], 1 - slot):
                c.start()

    @pl.when(i < nt_ref[0])
    def _():
        half = D_MODEL // 2
        n_sub = x_ref.shape[0] // MOE_SUB

        def up_proj(s):
            lo, hi = _unpack_bf16_pairs(x_ref[pl.ds(s * MOE_SUB, MOE_SUB), :])
            lo = lo.astype(BF16)
            hi = hi.astype(BF16)
            gate = (jnp.dot(lo, wg_scr[0:half, :], preferred_element_type=F32)
                    + jnp.dot(hi, wg_scr[half:D_MODEL, :], preferred_element_type=F32))
            up = (jnp.dot(lo, wu_scr[0:half, :], preferred_element_type=F32)
                  + jnp.dot(hi, wu_scr[half:D_MODEL, :], preferred_element_type=F32))
            return gate, up

        def down_proj(gate, up):
            hid = (jax.nn.silu(gate) * up).astype(BF16)
            return jnp.dot(hid, wd_scr[...], preferred_element_type=F32)

        def store(s, y):
            o_ref[pl.ds(s * MOE_SUB, MOE_SUB), :] = _pack_bf16_pairs(y.astype(BF16).astype(F32))

        gu = {0: up_proj(0)}
        ys = {}
        for s in range(n_sub):
            if s + 1 < n_sub:
                gu[s + 1] = up_proj(s + 1)
            ys[s] = down_proj(*gu.pop(s))
            if s >= 1:
                store(s - 1, ys.pop(s - 1))
        store(n_sub - 1, ys.pop(n_sub - 1))


def _moe_call(plan, xs, wg, wu, wd):
    R = xs.shape[0]
    half = D_MODEL // 2
    row_map = lambda i, te, nt, nx, seg: (jnp.minimum(i, nt[0] - 1), 0)
    hbm = pl.BlockSpec(memory_space=pl.ANY)
    return pl.pallas_call(
        _moe_body,
        out_shape=jax.ShapeDtypeStruct((R, half), jnp.int32),
        grid_spec=pltpu.PrefetchScalarGridSpec(
            num_scalar_prefetch=4,
            grid=(R // MOE_RT,),
            in_specs=[pl.BlockSpec((MOE_RT, half), row_map), hbm, hbm, hbm],
            out_specs=pl.BlockSpec((MOE_RT, half), row_map),
            scratch_shapes=[pltpu.VMEM((D_MODEL, EXPERT_HIDDEN), BF16),
                            pltpu.VMEM((D_MODEL, EXPERT_HIDDEN), BF16),
                            pltpu.VMEM((EXPERT_HIDDEN, D_MODEL), BF16),
                            pltpu.VMEM((2, D_MODEL, EXPERT_HIDDEN), F32),
                            pltpu.VMEM((2, D_MODEL, EXPERT_HIDDEN), F32),
                            pltpu.VMEM((2, EXPERT_HIDDEN, D_MODEL), F32),
                            pltpu.SemaphoreType.DMA((2, 3))]),
        compiler_params=pltpu.CompilerParams(
            dimension_semantics=("arbitrary",), vmem_limit_bytes=VMEM_LIMIT),
        name="moe_experts",
    )(*plan, xs, wg, wu, wd)


def _moe_plan(route_t, counts, n_rows):
    cnt = counts[0, :N_EXPERTS].astype(jnp.int32)
    tiles = (cnt + MOE_RT - 1) // MOE_RT
    tile_end = jnp.cumsum(tiles)
    n_tiles = tile_end[-1:]
    row_start = (tile_end - tiles) * MOE_RT
    ids = route_t[0:2].astype(jnp.int32)
    ranks = route_t[4:6].astype(jnp.int32)
    experts = jnp.arange(N_EXPERTS, dtype=jnp.int32)[:, None, None]
    pos = ranks + jnp.sum(jnp.where(ids[None] == experts, row_start[:, None, None], 0), axis=0)
    tile_id = jnp.minimum(jnp.arange(n_rows // MOE_RT, dtype=jnp.int32), n_tiles - 1)
    tile_expert = jnp.sum((tile_id[:, None] >= tile_end[None, :]).astype(jnp.int32), axis=1)
    e_ids = jnp.arange(N_EXPERTS, dtype=jnp.int32)
    later = (e_ids[None, :] > e_ids[:, None]) & (tiles[None, :] > 0)
    nxt = jnp.min(jnp.where(later, e_ids[None, :], N_EXPERTS), axis=1)
    nxt = jnp.where(nxt == N_EXPERTS, -1, nxt)
    seg = jnp.cumsum((tiles > 0).astype(jnp.int32)) - 1
    pick = tile_expert[:, None] == e_ids[None, :]
    tile_next = jnp.sum(jnp.where(pick, nxt[None, :], 0), axis=1)
    tile_seg = jnp.sum(jnp.where(pick, seg[None, :], 0), axis=1)
    return pos, (tile_expert, n_tiles, tile_next, tile_seg)


PLE_TM = 1024
PLE_SUB = 256


def _ple_body(x_ref, ya_ref, yb_ref, route_ref, p_ref, g3_ref, wpg_ref, bpg_ref, wple_ref, gf_ref,
              o_ref):
    n_sub = x_ref.shape[0] // PLE_SUB
    g3, gf, bpg = g3_ref[...], gf_ref[...], bpg_ref[...]

    def head(s):
        rows = pl.ds(s * PLE_SUB, PLE_SUB)
        ya = jnp.concatenate(_unpack_bf16_pairs(ya_ref[rows, :]), axis=1)
        yb = jnp.concatenate(_unpack_bf16_pairs(yb_ref[rows, :]), axis=1)
        route = route_ref[rows, :]
        x2 = x_ref[rows, :] + route[:, 2:3] * ya + route[:, 3:4] * yb
        return x2, _rms(x2, g3).astype(BF16)

    def dots(s, h3):
        rows = pl.ds(s * PLE_SUB, PLE_SUB)
        zg = jnp.dot(h3, wpg_ref[...], preferred_element_type=F32)
        pe = jnp.dot(p_ref[rows, :].astype(BF16), wple_ref[...], preferred_element_type=F32)
        return zg, pe

    def tail(s, x2, zg, pe):
        x3 = x2 + jax.nn.sigmoid(zg + bpg) * pe
        o_ref[pl.ds(s * PLE_SUB, PLE_SUB), :] = _rms(x3, gf)

    x2s, mm = {}, {}
    x2s[0], h3 = head(0)
    for s in range(n_sub):
        mm[s] = dots(s, h3)
        if s + 1 < n_sub:
            x2s[s + 1], h3 = head(s + 1)
        if s >= 1:
            tail(s - 1, x2s.pop(s - 1), *mm.pop(s - 1))
    tail(n_sub - 1, x2s.pop(n_sub - 1), *mm.pop(n_sub - 1))


def _ple_call(x1, y_picks, route, p, g3, wpg, bpg, wple, gf):
    T = x1.shape[0]
    nt = T // PLE_TM
    consts = (g3, wpg, bpg, wple, gf)
    tok = lambda i: (i, 0)
    return pl.pallas_call(
        _ple_body,
        out_shape=jax.ShapeDtypeStruct((T, D_MODEL), F32),
        grid=(nt,),
        in_specs=[pl.BlockSpec((PLE_TM, D_MODEL), tok),
                  pl.BlockSpec((PLE_TM, D_MODEL // 2), tok),
                  pl.BlockSpec((PLE_TM, D_MODEL // 2), lambda i: (i + nt, 0)),
                  pl.BlockSpec((PLE_TM, LANES), tok),
                  pl.BlockSpec((PLE_TM, p.shape[1]), tok)]
                 + [_const_spec(c.shape) for c in consts],
        out_specs=pl.BlockSpec((PLE_TM, D_MODEL), tok),
        compiler_params=pltpu.CompilerParams(
            dimension_semantics=("arbitrary",), vmem_limit_bytes=VMEM_LIMIT),
        name="ple_final",
    )(x1, y_picks, y_picks, route, p, *consts)


def _layer(x, p, norm_mix, w_in, b_in, lam_re, lam_im, log_dt, b_re, b_im, c_re, c_im, d_skip,
           w_glu_a, w_glu_b, conv_w, conv_b, w_conv_out, w_o, norm_ffn, w_rg, b_rg, w_re, b_re_r,
           w_eg, w_eu, w_ed, norm_ple, w_ple, w_pg, b_pg, norm_out):
    nb, seq, d = x.shape
    T = nb * seq
    s0 = SSM_WIDTH
    s3 = s0 + 3 * CONV_WIDTH
    row = lambda v: v.reshape(1, -1).astype(F32)

    kmat, mmat, nmat, aq_re, aq_im = _s5_operators(lam_re, lam_im, log_dt, b_re, b_im, c_re, c_im)
    s5_consts = (row(norm_mix), w_in[:, :s0].astype(BF16), row(b_in[:s0]),
                 mmat, kmat, nmat, aq_re, aq_im, row(d_skip))

    w_r = jnp.zeros((d, LANES), F32).at[:, :N_EXPERTS].set(w_re)
    w_r = w_r.at[:, N_EXPERTS:N_EXPERTS + N_EXPERT_GROUPS].set(w_rg)
    b_r = jnp.zeros((1, LANES), F32).at[0, :N_EXPERTS].set(b_re_r)
    b_r = b_r.at[0, N_EXPERTS:N_EXPERTS + N_EXPERT_GROUPS].set(b_rg)
    w_r_hi = w_r.astype(BF16)
    w_r_lo = (w_r - w_r_hi.astype(F32)).astype(BF16)

    mix_consts = (row(norm_mix),
                  w_in[:, s0:s3].astype(BF16), row(b_in[s0:s3]),
                  w_in[:, s3:].astype(BF16), row(b_in[s3:]),
                  conv_w.astype(F32), row(conv_b), w_conv_out.astype(BF16),
                  jnp.concatenate([w_glu_a, w_glu_b], axis=1).astype(BF16), w_o.astype(BF16),
                  row(norm_ffn), jnp.concatenate([w_r_hi, w_r_lo], axis=1), w_r_hi, b_r)
    ple_consts = (row(norm_ple), w_pg.astype(BF16), row(b_pg), w_ple.astype(BF16), row(norm_out))
    p2d = p.reshape(T, -1)
    n_rows = 2 * T + N_EXPERTS * MOE_RT

    ys = _s5_call(x, *s5_consts)
    x1, h2p, route, route_t, counts = _mix_call(x, ys, *mix_consts)
    pos, plan = _moe_plan(route_t, counts, n_rows)
    xs = _sc_dispatch(h2p, pos[0], pos[1], n_rows)
    ysort = _moe_call(plan, xs, w_eg, w_eu, w_ed)
    y_picks = _sc_gather(ysort, pos.reshape(-1))
    out = _ple_call(x1, y_picks, route, p2d, *ple_consts)
    return out.reshape(nb, seq, d)


def kernel(x, p, norm_mix, w_in, b_in, ssm_lam_re, ssm_lam_im, ssm_log_dt, ssm_b_re, ssm_b_im, ssm_c_re, ssm_c_im, ssm_d, w_glu_a, w_glu_b, conv_w, conv_b, w_conv_out, w_o, norm_ffn, w_router_group, b_router_group, w_router_expert, b_router_expert, w_exp_gate, w_exp_up, w_exp_down, norm_ple, w_ple, w_ple_gate, b_ple_gate, norm_final):
    assert p.shape[0] == 1, "the final RMSNorm is fused into the (single) layer's last kernel"
    i = 0
    return _layer(x, p[i], norm_mix[i], w_in[i], b_in[i], ssm_lam_re[i], ssm_lam_im[i],
                  ssm_log_dt[i], ssm_b_re[i], ssm_b_im[i], ssm_c_re[i], ssm_c_im[i], ssm_d[i],
                  w_glu_a[i], w_glu_b[i], conv_w[i], conv_b[i], w_conv_out[i], w_o[i],
                  norm_ffn[i], w_router_group[i], b_router_group[i], w_router_expert[i],
                  b_router_expert[i], w_exp_gate[i], w_exp_up[i], w_exp_down[i], norm_ple[i],
                  w_ple[i], w_ple_gate[i], b_ple_gate[i], norm_final)
```

```python
import functools
import math

import jax
import jax.numpy as jnp
from jax import lax
from jax.experimental import pallas as pl
from jax.experimental.pallas import tpu as pltpu
from jax.experimental.pallas import tpu_sc as plsc

F32 = jnp.float32
BF16 = jnp.bfloat16

D_MODEL = 1024
SSM_WIDTH = 512
SSM_GROUP = 16
SSM_GROUPS = 32
SSM_STATE = 64
CONV_WIDTH = 512
N_EXPERT_GROUPS = 4
EXPERTS_PER_GROUP = 8
N_EXPERTS = 32
EXPERT_HIDDEN = 256
NORM_EPS = 1e-6

LANES = 128
Q = 8
GROUPS_PER_LANE_TILE = LANES // SSM_GROUP
N_LANE_TILES = SSM_WIDTH // LANES
STATE_LANES = GROUPS_PER_LANE_TILE * SSM_STATE
S5_ROWS = 1024
VMEM_LIMIT = 56 * 1024 * 1024


def _rms(x, g):
    return x * lax.rsqrt(jnp.mean(x * x, axis=-1, keepdims=True) + NORM_EPS) * g


def _pack_bf16_pairs(a):
    w = a.shape[1] // 2
    lo = lax.shift_right_logical(lax.bitcast_convert_type(a[:, :w], jnp.int32), 16)
    hi = lax.bitcast_convert_type(a[:, w:], jnp.int32) & jnp.int32(-65536)
    return lo | hi


def _unpack_bf16_pairs(word):
    lo = lax.bitcast_convert_type(lax.shift_left(word, 16), F32)
    hi = lax.bitcast_convert_type(word & jnp.int32(-65536), F32)
    return lo, hi


def _const_spec(shape):
    n = len(shape)
    return pl.BlockSpec(shape, lambda *_: (0,) * n, pipeline_mode=pl.Buffered(1))


def _s5_operators(lam_re, lam_im, log_dt, b_re, b_im, c_re, c_im):
    G, P, H = SSM_GROUPS, SSM_STATE, SSM_GROUP
    J = N_LANE_TILES
    lr = lam_re.astype(F32)
    li = lam_im.astype(F32)
    dt = jnp.exp(log_dt.astype(F32))[:, None]

    def apow(n):
        n = n.astype(F32)[:, None, None]
        mag = jnp.exp(lr * dt * n)
        ang = li * dt * n
        return mag * jnp.cos(ang), mag * jnp.sin(ang)

    a1_re, a1_im = apow(jnp.ones((1,), F32))
    nr = a1_re[0] - 1.0
    ni = a1_im[0]
    den = lr * lr + li * li
    f_re = (nr * lr + ni * li) / den
    f_im = (ni * lr - nr * li) / den
    br = b_re.astype(F32)
    bi = b_im.astype(F32)
    bbar_re = f_re[:, :, None] * br - f_im[:, :, None] * bi
    bbar_im = f_re[:, :, None] * bi + f_im[:, :, None] * br
    to_rows = lambda v, perm: jnp.transpose(v, perm).reshape(H, G * P)
    bt_re = to_rows(bbar_re, (2, 0, 1))
    bt_im = to_rows(bbar_im, (2, 0, 1))
    ct_re = to_rows(c_re.astype(F32), (1, 0, 2))
    ct_im = to_rows(c_im.astype(F32), (1, 0, 2))
    ap_re, ap_im = apow(jnp.arange(Q + 1))
    ap_re = ap_re.reshape(Q + 1, G * P)
    ap_im = ap_im.reshape(Q + 1, G * P)

    blk = lambda r: pl.BlockSpec((r, STATE_LANES), lambda j: (0, j))
    mat = lambda r, c: pl.BlockSpec((1, r, c), lambda j: (j, 0, 0))
    shape = lambda r, c: jax.ShapeDtypeStruct((J, r, c), BF16)
    qx, st = Q * LANES, 2 * STATE_LANES
    kmat, mmat, nmat = pl.pallas_call(
        _s5_ops_body,
        out_shape=(shape(qx, qx), shape(qx, st), shape(st, qx)),
        grid=(J,),
        in_specs=[blk(Q + 1), blk(Q + 1), blk(H), blk(H), blk(H), blk(H)],
        out_specs=(mat(qx, qx), mat(qx, st), mat(st, qx)),
        compiler_params=pltpu.CompilerParams(
            dimension_semantics=("arbitrary",), vmem_limit_bytes=VMEM_LIMIT),
        name="s5_operators",
    )(ap_re, ap_im, bt_re, bt_im, ct_re, ct_im)
    return kmat, mmat, nmat, ap_re[Q:Q + 1], ap_im[Q:Q + 1]


def _s5_ops_body(apr_ref, api_ref, btr_ref, bti_ref, ctr_ref, cti_ref, k_ref, m_ref, n_ref):
    ri = lax.broadcasted_iota(jnp.int32, (LANES, STATE_LANES), 0)
    li = lax.broadcasted_iota(jnp.int32, (LANES, STATE_LANES), 1)
    same_group = (ri // SSM_GROUP) == (li // SSM_STATE)

    def expand(ref):
        tiled = jnp.concatenate([ref[...]] * GROUPS_PER_LANE_TILE, axis=0)
        return jnp.where(same_group, tiled, 0.0)

    b_re, b_im, c_re, c_im = expand(btr_ref), expand(bti_ref), expand(ctr_ref), expand(cti_ref)

    def cmul(n, x_re, x_im):
        a_re = apr_ref[n:n + 1, :]
        a_im = api_ref[n:n + 1, :]
        return a_re * x_re - a_im * x_im, a_re * x_im + a_im * x_re

    m_blocks = []
    for k in range(Q):
        g_re, g_im = cmul(Q - 1 - k, b_re, b_im)
        m_blocks.append(jnp.concatenate([g_re, g_im], axis=1))
    m = jnp.concatenate(m_blocks, axis=0)

    nt_blocks = []
    for t in range(Q):
        g_re, g_im = cmul(t + 1, c_re, c_im)
        nt_blocks.append(jnp.concatenate([g_re, -g_im], axis=1))
    nt = jnp.concatenate(nt_blocks, axis=0)

    n0t = jnp.concatenate([c_re, -c_im], axis=1)
    p = lax.dot_general(m, n0t, (((1,), (1,)), ((), ())),
                        precision=lax.Precision.HIGHEST, preferred_element_type=F32)
    zeros = jnp.zeros((LANES, LANES), F32)
    cols = []
    for t in range(Q):
        cols.append(jnp.concatenate(
            [p[(Q - 1 - (t - k)) * LANES:(Q - (t - k)) * LANES, :] if t >= k else zeros
             for k in range(Q)], axis=0))
    k_ref[0] = jnp.concatenate(cols, axis=1).astype(BF16)
    m_ref[0] = m.astype(BF16)
    n_ref[0] = nt.T.astype(BF16)


def _s5_body(x_ref, g_ref, wu_ref, bu_ref, m_ref, k_ref, n_ref, aqr_ref, aqi_ref, d_ref,
             o_ref, u_scr, y_scr, z_scr, ss_scr, carry_scr):
    nb, tt = x_ref.shape[0], x_ref.shape[1]
    rows = nb * tt
    nchunk = rows // Q

    @pl.when(pl.program_id(0) == 0)
    def _():
        carry_scr[...] = jnp.zeros_like(carry_scr)

    hb = nb // 2
    for r in range(2):
        x = x_ref[r * hb:(r + 1) * hb].reshape(rows // 2, D_MODEL)
        h = _rms(x, g_ref[...]).astype(BF16)
        u = jnp.dot(h, wu_ref[...], preferred_element_type=F32) + bu_ref[...]
        for j in range(N_LANE_TILES):
            u_scr[j, pl.ds(r * (rows // 2), rows // 2), :] = u[:, j * LANES:(j + 1) * LANES]

    n_st = STATE_LANES // LANES
    cpt = tt // Q

    lane_tiles = range(N_LANE_TILES)


    xs = [jnp.concatenate([u_scr[j, pl.ds(k, nchunk, stride=Q), :] for k in range(Q)],
                          axis=1).astype(BF16) for j in lane_tiles]
    for j in lane_tiles:
        z = jnp.dot(xs[j], m_ref[j], preferred_element_type=F32)
        for i in range(2 * n_st):
            z_scr[j, i] = z[:, i * LANES:(i + 1) * LANES]
    y_intra = [jnp.dot(xs[j], k_ref[j], preferred_element_type=F32) for j in lane_tiles]

    aq = [(jnp.broadcast_to(aqr_ref[:, pl.ds(j * STATE_LANES, STATE_LANES)], (nb, STATE_LANES)),
           jnp.broadcast_to(aqi_ref[:, pl.ds(j * STATE_LANES, STATE_LANES)], (nb, STATE_LANES)))
          for j in lane_tiles]
    st = [(carry_scr[j, :, pl.ds(0, STATE_LANES)], carry_scr[j, :, pl.ds(STATE_LANES, STATE_LANES)])
          for j in lane_tiles]
    for c in range(cpt):
        seq_rows = pl.ds(c, nb, stride=cpt)
        for j in lane_tiles:
            s_re, s_im = st[j]
            aqr, aqi = aq[j]
            for i in range(n_st):
                ss_scr[j, i, seq_rows, :] = s_re[:, i * LANES:(i + 1) * LANES]
                ss_scr[j, n_st + i, seq_rows, :] = s_im[:, i * LANES:(i + 1) * LANES]
            z_re = jnp.concatenate([z_scr[j, i, seq_rows, :] for i in range(n_st)], axis=1)
            z_im = jnp.concatenate([z_scr[j, n_st + i, seq_rows, :] for i in range(n_st)], axis=1)
            st[j] = (aqr * s_re - aqi * s_im + z_re, aqr * s_im + aqi * s_re + z_im)
    for j in lane_tiles:
        carry_scr[j, :, pl.ds(0, STATE_LANES)] = st[j][0]
        carry_scr[j, :, pl.ds(STATE_LANES, STATE_LANES)] = st[j][1]

    def state_to_output(j):
        ss = jnp.concatenate([ss_scr[j, i] for i in range(2 * n_st)], axis=1).astype(BF16)
        return y_intra[j] + jnp.dot(ss, n_ref[j], preferred_element_type=F32)

    def finish(j, yj):
        for k in range(Q):
            y_scr[j, pl.ds(k, nchunk, stride=Q), :] = yj[:, k * LANES:(k + 1) * LANES]
        lanes = pl.ds(j * LANES, LANES)
        y = y_scr[j] + d_ref[:, lanes] * u_scr[j]
        o_ref[:, :, lanes] = jax.nn.gelu(y).astype(BF16).reshape(nb, tt, LANES)

    yj = state_to_output(0)
    for j in lane_tiles:
        y_next = state_to_output(j + 1) if j + 1 < N_LANE_TILES else None
        finish(j, yj)
        yj = y_next


def _s5_call(x, g, wu, bu, mmat, kmat, nmat, aq_re, aq_im, d_skip):
    nb, seq = x.shape[0], x.shape[1]
    tt = S5_ROWS // nb
    rows = S5_ROWS
    nchunk = rows // Q
    return pl.pallas_call(
        _s5_body,
        out_shape=jax.ShapeDtypeStruct((nb, seq, SSM_WIDTH), BF16),
        grid=(seq // tt,),
        in_specs=[
            pl.BlockSpec((nb, tt, D_MODEL), lambda i: (0, i, 0)),
            _const_spec(g.shape), _const_spec(wu.shape), _const_spec(bu.shape),
            _const_spec(mmat.shape), _const_spec(kmat.shape), _const_spec(nmat.shape),
            _const_spec(aq_re.shape), _const_spec(aq_im.shape), _const_spec(d_skip.shape),
        ],
        out_specs=pl.BlockSpec((nb, tt, SSM_WIDTH), lambda i: (0, i, 0)),
        scratch_shapes=[
            pltpu.VMEM((N_LANE_TILES, rows, LANES), F32),
            pltpu.VMEM((N_LANE_TILES, rows, LANES), F32),
            pltpu.VMEM((N_LANE_TILES, 2 * STATE_LANES // LANES, nchunk, LANES), F32),
            pltpu.VMEM((N_LANE_TILES, 2 * STATE_LANES // LANES, nchunk, LANES), F32),
            pltpu.VMEM((N_LANE_TILES, nb, 2 * STATE_LANES), F32),
        ],
        compiler_params=pltpu.CompilerParams(
            dimension_semantics=("arbitrary",), vmem_limit_bytes=VMEM_LIMIT),
        name="s5_mixer",
    )(x, g, wu, bu, mmat, kmat, nmat, aq_re, aq_im, d_skip)


MIX_TM = 512


def _mix_body(x_ref, ys_ref, g_ref, wc_ref, bc_ref, wg_ref, bg_ref, cw_ref, cb_ref, wco_ref,
              wab_ref, wo_ref, g2_ref, wr_ref, wrh_ref, br_ref, eg_ref, eu_ref, ed_ref,
              x1_ref, h2p_ref, route_ref, route_t_ref, cnt_ref, eg_out, eu_out, ed_out,
              carry_scr, cnt_scr, x1_scr, *, tiles_per_seq, n_tiles):
    i = pl.program_id(0)
    tm = x_ref.shape[1]

    @pl.when(i == 0)
    def _():
        cnt_scr[...] = jnp.zeros_like(cnt_scr)
        x1_scr[...] = jnp.zeros_like(x1_scr)
        carry_scr[...] = jnp.zeros_like(carry_scr)

    carry_scr[0] = jnp.where(i == n_tiles, carry_scr[0],
                             jnp.where(i % tiles_per_seq == 0, 0.0, carry_scr[1]))

    yab = jnp.dot(ys_ref[0], wab_ref[...], preferred_element_type=F32)

    eg_out[...] = eg_ref[...].astype(BF16)
    eu_out[...] = eu_ref[...].astype(BF16)
    ed_out[...] = ed_ref[...].astype(BF16)

    h2 = _rms(x1_scr[...], g2_ref[...])
    h2_hi = h2.astype(BF16)
    h2p_ref[...] = _pack_bf16_pairs(h2_hi.astype(F32))
    h2_lo = (h2 - h2_hi.astype(F32)).astype(BF16)
    lg2 = jnp.dot(h2_hi, wr_ref[...], preferred_element_type=F32)
    logits = (lg2[:, 0:LANES] + lg2[:, LANES:2 * LANES]
              + jnp.dot(h2_lo, wrh_ref[...], preferred_element_type=F32) + br_ref[...])

    x = x_ref[0]
    h = _rms(x, g_ref[...]).astype(BF16)
    zc = jnp.dot(h, wc_ref[...], preferred_element_type=F32) + bc_ref[...]
    zg = jnp.dot(h, wg_ref[...], preferred_element_type=F32) + bg_ref[...]

    cnt = cnt_scr[...]
    route, new_cnt = _route_tile(logits, cnt)
    cnt = jnp.where(i > 0, new_cnt, cnt)
    cnt_scr[...] = cnt
    cnt_ref[...] = cnt
    route_ref[...] = route
    route_t_ref[...] = route.T[0:8, :]

    y_a = yab[:, 0:D_MODEL] * jax.nn.sigmoid(yab[:, D_MODEL:2 * D_MODEL])

    c_b = zc[:, 0:CONV_WIDTH]
    cv = zc[:, CONV_WIDTH:2 * CONV_WIDTH] * zc[:, 2 * CONV_WIDTH:3 * CONV_WIDTH]
    row = lax.broadcasted_iota(jnp.int32, (tm, CONV_WIDTH), 0)
    last1 = carry_scr[0, 7:8, :]
    last2 = carry_scr[0, 6:7, :]
    p1 = jnp.where(row == 0, last1, pltpu.roll(cv, 1, axis=0))
    p2 = jnp.where(row == 0, last2, jnp.where(row == 1, last1, pltpu.roll(cv, 2, axis=0)))
    carry_scr[1] = cv[tm - 8:tm, :]
    conv = cw_ref[0:1, :] * p2 + cw_ref[1:2, :] * p1 + cw_ref[2:3, :] * cv + cb_ref[...]
    y_b = jnp.dot((c_b * conv).astype(BF16), wco_ref[...], preferred_element_type=F32)

    mix = (jax.nn.sigmoid(zg[:, 0:D_MODEL]) * y_a
           + jax.nn.sigmoid(zg[:, D_MODEL:2 * D_MODEL]) * y_b)
    x1 = x + jnp.dot(mix.astype(BF16), wo_ref[...], preferred_element_type=F32)
    x1_ref[...] = x1
    x1_scr[...] = x1


def _route_tile(logits, cnt):
    tm = logits.shape[0]
    lane = lax.broadcasted_iota(jnp.int32, (tm, LANES), 1)
    neg = jnp.float32(-jnp.inf)
    big = jnp.int32(1 << 20)
    is_g = (lane >= N_EXPERTS) & (lane < N_EXPERTS + N_EXPERT_GROUPS)
    gl = jnp.where(is_g, logits, neg)
    gmax = jnp.max(gl, axis=1, keepdims=True)
    g_w = 1.0 / jnp.sum(jnp.exp(gl - gmax), axis=1, keepdims=True)
    g_idx = jnp.min(jnp.where(gl == gmax, lane - N_EXPERTS, big), axis=1, keepdims=True)
    lo = g_idx * EXPERTS_PER_GROUP
    el = jnp.where((lane >= lo) & (lane < lo + EXPERTS_PER_GROUP), logits, neg)
    m1 = jnp.max(el, axis=1, keepdims=True)
    i1 = jnp.min(jnp.where(el == m1, lane, big), axis=1, keepdims=True)
    el2 = jnp.where(lane == i1, neg, el)
    m2 = jnp.max(el2, axis=1, keepdims=True)
    i2 = jnp.min(jnp.where(el2 == m2, lane, big), axis=1, keepdims=True)
    r = jnp.exp(m2 - m1)
    w1 = g_w / (1.0 + r)
    w2 = g_w * r / (1.0 + r)

    picks = ((lane == i1) | (lane == i2)).astype(BF16)
    r_i = lax.broadcasted_iota(jnp.int32, (tm, tm), 0)
    c_i = lax.broadcasted_iota(jnp.int32, (tm, tm), 1)
    before = (c_i < r_i).astype(BF16)
    excl = jnp.dot(before, picks, preferred_element_type=F32) + cnt
    rank1 = jnp.sum(jnp.where(lane == i1, excl, 0.0), axis=1, keepdims=True)
    rank2 = jnp.sum(jnp.where(lane == i2, excl, 0.0), axis=1, keepdims=True)
    new_cnt = cnt + jnp.sum(picks.astype(F32), axis=0, keepdims=True)
    route = jnp.where(lane == 0, i1.astype(F32), 0.0)
    route = jnp.where(lane == 1, i2.astype(F32), route)
    route = jnp.where(lane == 2, w1, route)
    route = jnp.where(lane == 3, w2, route)
    route = jnp.where(lane == 4, rank1, route)
    route = jnp.where(lane == 5, rank2, route)
    return route, new_cnt


def _mix_call(x, ys, g, wc, bc, wg, bg, cw, cb, wco, wab, wo, g2, wr, wrh, br, w_eg, w_eu, w_ed):
    nb, seq, _ = ys.shape
    T = nb * seq
    nl = seq // MIX_TM
    n = nb * nl
    consts = (g, wc, bc, wg, bg, cw, cb, wco, wab, wo, g2, wr, wrh, br)
    cur = lambda i: jnp.minimum(i, n - 1)
    prev = lambda i: jnp.maximum(i - 1, 0)
    spe = n // N_EXPERTS
    assert spe * N_EXPERTS == n
    wslice = lambda i: (cur(i) // spe, cur(i) % spe, 0)
    up_spec = pl.BlockSpec((1, D_MODEL // spe, EXPERT_HIDDEN), wslice)
    down_spec = pl.BlockSpec((1, EXPERT_HIDDEN // spe, D_MODEL), wslice)
    return pl.pallas_call(
        functools.partial(_mix_body, tiles_per_seq=nl, n_tiles=n),
        out_shape=(jax.ShapeDtypeStruct((T, D_MODEL), F32),
                   jax.ShapeDtypeStruct((T, D_MODEL // 2), jnp.int32),
                   jax.ShapeDtypeStruct((T, LANES), F32),
                   jax.ShapeDtypeStruct((8, T), F32),
                   jax.ShapeDtypeStruct((1, LANES), F32),
                   jax.ShapeDtypeStruct(w_eg.shape, BF16),
                   jax.ShapeDtypeStruct(w_eu.shape, BF16),
                   jax.ShapeDtypeStruct(w_ed.shape, BF16)),
        grid=(n + 1,),
        in_specs=[pl.BlockSpec((1, MIX_TM, D_MODEL),
                               lambda i: (cur(i) // nl, cur(i) % nl, 0)),
                  pl.BlockSpec((1, MIX_TM, SSM_WIDTH), lambda i: (cur(i) // nl, cur(i) % nl, 0))]
                 + [_const_spec(c.shape) for c in consts] + [up_spec, up_spec, down_spec],
        out_specs=(pl.BlockSpec((MIX_TM, D_MODEL), lambda i: (cur(i), 0)),
                   pl.BlockSpec((MIX_TM, D_MODEL // 2), lambda i: (prev(i), 0)),
                   pl.BlockSpec((MIX_TM, LANES), lambda i: (prev(i), 0)),
                   pl.BlockSpec((8, MIX_TM), lambda i: (0, prev(i))),
                   pl.BlockSpec((1, LANES), lambda i: (0, 0)),
                   up_spec, up_spec, down_spec),
        scratch_shapes=[pltpu.VMEM((2, 8, CONV_WIDTH), F32), pltpu.VMEM((1, LANES), F32),
                        pltpu.VMEM((MIX_TM, D_MODEL), F32)],
        compiler_params=pltpu.CompilerParams(
            dimension_semantics=("arbitrary",), vmem_limit_bytes=VMEM_LIMIT),
        name="conv_glu_router",
    )(x, ys, *consts, w_eg, w_eu, w_ed)


SC_CORES = 2
SC_SUBCORES = 16
SC_WORKERS = SC_CORES * SC_SUBCORES
SC_ROWS = 64
SC_NBUF = 2


def _sc_mesh():
    return plsc.VectorSubcoreMesh(core_axis_name="c", subcore_axis_name="s")


def _sc_worker_id():
    return lax.axis_index("s") * SC_CORES + lax.axis_index("c")


def _sc_ring(nch, get, puts):
    for b in range(SC_NBUF - 1):
        get(b, b).start()

    @pl.loop(0, nch, step=SC_NBUF)
    def _(j0):
        for b in range(SC_NBUF):
            j = j0 + b
            refill = (b - 1) % SC_NBUF
            get(j, b).wait()

            @pl.when(j + SC_NBUF - 1 < nch)
            def _():
                @pl.when(j >= 1)
                def _():
                    for c in puts(j - 1, refill):
                        c.wait()
                get(j + SC_NBUF - 1, refill).start()

            for c in puts(j, b):
                c.start()

    for b in range(SC_NBUF):
        for c in puts(nch - SC_NBUF + b, b):
            c.wait()


def _sc_dispatch(rows, pos_a, pos_b, n_out):
    T, W = rows.shape
    per_w = T // SC_WORKERS
    nch = per_w // SC_ROWS
    assert per_w * SC_WORKERS == T and nch * SC_ROWS == per_w and nch % SC_NBUF == 0
    idx_a = pos_a.reshape(SC_WORKERS, nch, SC_ROWS)
    idx_b = pos_b.reshape(SC_WORKERS, nch, SC_ROWS)

    @functools.partial(
        pl.kernel, mesh=_sc_mesh(),
        out_type=jax.ShapeDtypeStruct((n_out, W), rows.dtype),
        scratch_types=[
            pltpu.VMEM((nch, SC_ROWS), jnp.int32),
            pltpu.VMEM((nch, SC_ROWS), jnp.int32),
            pltpu.VMEM((SC_NBUF, SC_ROWS, W), rows.dtype),
            pltpu.SemaphoreType.DMA((SC_NBUF,)),
            pltpu.SemaphoreType.DMA((SC_NBUF,)),
            pltpu.SemaphoreType.DMA((SC_NBUF,)),
        ],
        name="moe_dispatch",
    )
    def k(rows_hbm, ia_hbm, ib_hbm, out_hbm, ia_v, ib_v, buf, gsem, asem, bsem):
        wid = _sc_worker_id()
        base = wid * per_w
        pltpu.sync_copy(ia_hbm.at[wid], ia_v)
        pltpu.sync_copy(ib_hbm.at[wid], ib_v)

        def get(j, b):
            return pltpu.make_async_copy(
                rows_hbm.at[pl.ds(base + j * SC_ROWS, SC_ROWS)], buf.at[b], gsem.at[b])

        def put_a(j, b):
            return pltpu.make_async_copy(buf.at[b], out_hbm.at[ia_v.at[j]], asem.at[b])

        def put_b(j, b):
            return pltpu.make_async_copy(buf.at[b], out_hbm.at[ib_v.at[j]], bsem.at[b])

        _sc_ring(nch, get, lambda j, b: (put_a(j, b), put_b(j, b)))

    return k(rows, idx_a, idx_b)


def _sc_gather(table, idx):
    _, W = table.shape
    B = idx.shape[0]
    per_w = B // SC_WORKERS
    nch = per_w // SC_ROWS
    assert per_w * SC_WORKERS == B and nch * SC_ROWS == per_w and nch % SC_NBUF == 0
    idx3 = idx.reshape(SC_WORKERS, nch, SC_ROWS)

    @functools.partial(
        pl.kernel, mesh=_sc_mesh(),
        out_type=jax.ShapeDtypeStruct((B, W), table.dtype),
        scratch_types=[
            pltpu.VMEM((nch, SC_ROWS), jnp.int32),
            pltpu.VMEM((SC_NBUF, SC_ROWS, W), table.dtype),
            pltpu.SemaphoreType.DMA((SC_NBUF,)),
            pltpu.SemaphoreType.DMA((SC_NBUF,)),
        ],
        name="moe_combine_gather",
    )
    def k(table_hbm, idx_hbm, out_hbm, idx_v, buf, gsem, osem):
        wid = _sc_worker_id()
        base = wid * per_w
        pltpu.sync_copy(idx_hbm.at[wid], idx_v)

        def get(j, b):
            return pltpu.make_async_copy(table_hbm.at[idx_v.at[j]], buf.at[b], gsem.at[b])

        def put(j, b):
            return pltpu.make_async_copy(
                buf.at[b], out_hbm.at[pl.ds(base + j * SC_ROWS, SC_ROWS)], osem.at[b])

        _sc_ring(nch, get, lambda j, b: (put(j, b),))

    return k(table, idx3)


MOE_RT = 512
MOE_SUB = 256


def _moe_body(te_ref, nt_ref, nx_ref, seg_ref, x_ref, wg_hbm, wu_hbm, wd_hbm, o_ref,
              wg_scr, wu_scr, wd_scr, wg_buf, wu_buf, wd_buf, sem):
    i = pl.program_id(0)
    expert = te_ref[i]
    slot = seg_ref[i] % 2

    def weight_copies(e, b):
        return (pltpu.make_async_copy(wg_hbm.at[e], wg_buf.at[b], sem.at[b, 0]),
                pltpu.make_async_copy(wu_hbm.at[e], wu_buf.at[b], sem.at[b, 1]),
                pltpu.make_async_copy(wd_hbm.at[e], wd_buf.at[b], sem.at[b, 2]))

    @pl.when(i == 0)
    def _():
        for c in weight_copies(expert, 0):
            c.start()

    @pl.when((i == 0) | (expert != te_ref[jnp.maximum(i - 1, 0)]))
    def _():
        for c in weight_copies(expert, slot):
            c.wait()
        wg_scr[...] = wg_buf[slot]
        wu_scr[...] = wu_buf[slot]
        wd_scr[...] = wd_buf[slot]

        @pl.when(nx_ref[i] >= 0)
        def _():
            for c in weight_copies(nx_ref[i], 1 - slot):
                c.start()

    @pl.when(i < nt_ref[0])
    def _():
        half = D_MODEL // 2
        n_sub = x_ref.shape[0] // MOE_SUB

        def up_proj(s):
            lo, hi = _unpack_bf16_pairs(x_ref[pl.ds(s * MOE_SUB, MOE_SUB), :])
            lo = lo.astype(BF16)
            hi = hi.astype(BF16)
            gate = (jnp.dot(lo, wg_scr[0:half, :], preferred_element_type=F32)
                    + jnp.dot(hi, wg_scr[half:D_MODEL, :], preferred_element_type=F32))
            up = (jnp.dot(lo, wu_scr[0:half, :], preferred_element_type=F32)
                  + jnp.dot(hi, wu_scr[half:D_MODEL, :], preferred_element_type=F32))
            return gate, up

        def down_proj(gate, up):
            hid = (jax.nn.silu(gate) * up).astype(BF16)
            return jnp.dot(hid, wd_scr[...], preferred_element_type=F32)

        def store(s, y):
            o_ref[pl.ds(s * MOE_SUB, MOE_SUB), :] = _pack_bf16_pairs(y.astype(BF16).astype(F32))

        gu = {0: up_proj(0)}
        ys = {}
        for s in range(n_sub):
            if s + 1 < n_sub:
                gu[s + 1] = up_proj(s + 1)
            ys[s] = down_proj(*gu.pop(s))
            if s >= 1:
                store(s - 1, ys.pop(s - 1))
        store(n_sub - 1, ys.pop(n_sub - 1))


def _moe_call(plan, xs, wg, wu, wd):
    R = xs.shape[0]
    half = D_MODEL // 2
    row_map = lambda i, te, nt, nx, seg: (jnp.minimum(i, nt[0] - 1), 0)
    hbm = pl.BlockSpec(memory_space=pl.ANY)
    return pl.pallas_call(
        _moe_body,
        out_shape=jax.ShapeDtypeStruct((R, half), jnp.int32),
        grid_spec=pltpu.PrefetchScalarGridSpec(
            num_scalar_prefetch=4,
            grid=(R // MOE_RT,),
            in_specs=[pl.BlockSpec((MOE_RT, half), row_map), hbm, hbm, hbm],
            out_specs=pl.BlockSpec((MOE_RT, half), row_map),
            scratch_shapes=[pltpu.VMEM((D_MODEL, EXPERT_HIDDEN), BF16),
                            pltpu.VMEM((D_MODEL, EXPERT_HIDDEN), BF16),
                            pltpu.VMEM((EXPERT_HIDDEN, D_MODEL), BF16),
                            pltpu.VMEM((2, D_MODEL, EXPERT_HIDDEN), BF16),
                            pltpu.VMEM((2, D_MODEL, EXPERT_HIDDEN), BF16),
                            pltpu.VMEM((2, EXPERT_HIDDEN, D_MODEL), BF16),
                            pltpu.SemaphoreType.DMA((2, 3))]),
        compiler_params=pltpu.CompilerParams(
            dimension_semantics=("arbitrary",), vmem_limit_bytes=VMEM_LIMIT),
        name="moe_experts",
    )(*plan, xs, wg, wu, wd)


def _moe_plan(route_t, counts, n_rows):
    cnt = counts[0, :N_EXPERTS].astype(jnp.int32)
    tiles = (cnt + MOE_RT - 1) // MOE_RT
    tile_end = jnp.cumsum(tiles)
    n_tiles = tile_end[-1:]
    row_start = (tile_end - tiles) * MOE_RT
    ids = route_t[0:2].astype(jnp.int32)
    ranks = route_t[4:6].astype(jnp.int32)
    experts = jnp.arange(N_EXPERTS, dtype=jnp.int32)[:, None, None]
    pos = ranks + jnp.sum(jnp.where(ids[None] == experts, row_start[:, None, None], 0), axis=0)
    tile_id = jnp.minimum(jnp.arange(n_rows // MOE_RT, dtype=jnp.int32), n_tiles - 1)
    tile_expert = jnp.sum((tile_id[:, None] >= tile_end[None, :]).astype(jnp.int32), axis=1)
    e_ids = jnp.arange(N_EXPERTS, dtype=jnp.int32)
    later = (e_ids[None, :] > e_ids[:, None]) & (tiles[None, :] > 0)
    nxt = jnp.min(jnp.where(later, e_ids[None, :], N_EXPERTS), axis=1)
    nxt = jnp.where(nxt == N_EXPERTS, -1, nxt)
    seg = jnp.cumsum((tiles > 0).astype(jnp.int32)) - 1
    pick = tile_expert[:, None] == e_ids[None, :]
    tile_next = jnp.sum(jnp.where(pick, nxt[None, :], 0), axis=1)
    tile_seg = jnp.sum(jnp.where(pick, seg[None, :], 0), axis=1)
    return pos, (tile_expert, n_tiles, tile_next, tile_seg)


PLE_TM = 1024
PLE_SUB = 256


def _ple_body(x_ref, ya_ref, yb_ref, route_ref, p_ref, g3_ref, wpg_ref, bpg_ref, wple_ref, gf_ref,
              o_ref):
    n_sub = x_ref.shape[0] // PLE_SUB
    g3, gf, bpg = g3_ref[...], gf_ref[...], bpg_ref[...]

    def head(s):
        rows = pl.ds(s * PLE_SUB, PLE_SUB)
        ya = jnp.concatenate(_unpack_bf16_pairs(ya_ref[rows, :]), axis=1)
        yb = jnp.concatenate(_unpack_bf16_pairs(yb_ref[rows, :]), axis=1)
        route = route_ref[rows, :]
        x2 = x_ref[rows, :] + route[:, 2:3] * ya + route[:, 3:4] * yb
        return x2, _rms(x2, g3).astype(BF16)

    def dots(s, h3):
        rows = pl.ds(s * PLE_SUB, PLE_SUB)
        zg = jnp.dot(h3, wpg_ref[...], preferred_element_type=F32)
        pe = jnp.dot(p_ref[rows, :].astype(BF16), wple_ref[...], preferred_element_type=F32)
        return zg, pe

    def tail(s, x2, zg, pe):
        x3 = x2 + jax.nn.sigmoid(zg + bpg) * pe
        o_ref[pl.ds(s * PLE_SUB, PLE_SUB), :] = _rms(x3, gf)

    x2s, mm = {}, {}
    x2s[0], h3 = head(0)
    for s in range(n_sub):
        mm[s] = dots(s, h3)
        if s + 1 < n_sub:
            x2s[s + 1], h3 = head(s + 1)
        if s >= 1:
            tail(s - 1, x2s.pop(s - 1), *mm.pop(s - 1))
    tail(n_sub - 1, x2s.pop(n_sub - 1), *mm.pop(n_sub - 1))


def _ple_call(x1, y_picks, route, p, g3, wpg, bpg, wple, gf):
    T = x1.shape[0]
    nt = T // PLE_TM
    consts = (g3, wpg, bpg, wple, gf)
    tok = lambda i: (i, 0)
    return pl.pallas_call(
        _ple_body,
        out_shape=jax.ShapeDtypeStruct((T, D_MODEL), F32),
        grid=(nt,),
        in_specs=[pl.BlockSpec((PLE_TM, D_MODEL), tok),
                  pl.BlockSpec((PLE_TM, D_MODEL // 2), tok),
                  pl.BlockSpec((PLE_TM, D_MODEL // 2), lambda i: (i + nt, 0)),
                  pl.BlockSpec((PLE_TM, LANES), tok),
                  pl.BlockSpec((PLE_TM, p.shape[1]), tok)]
                 + [_const_spec(c.shape) for c in consts],
        out_specs=pl.BlockSpec((PLE_TM, D_MODEL), tok),
        compiler_params=pltpu.CompilerParams(
            dimension_semantics=("arbitrary",), vmem_limit_bytes=VMEM_LIMIT),
        name="ple_final",
    )(x1, y_picks, y_picks, route, p, *consts)


def _layer(x, p, norm_mix, w_in, b_in, lam_re, lam_im, log_dt, b_re, b_im, c_re, c_im, d_skip,
           w_glu_a, w_glu_b, conv_w, conv_b, w_conv_out, w_o, norm_ffn, w_rg, b_rg, w_re, b_re_r,
           w_eg, w_eu, w_ed, norm_ple, w_ple, w_pg, b_pg, norm_out):
    nb, seq, d = x.shape
    T = nb * seq
    s0 = SSM_WIDTH
    s3 = s0 + 3 * CONV_WIDTH
    row = lambda v: v.reshape(1, -1).astype(F32)

    kmat, mmat, nmat, aq_re, aq_im = _s5_operators(lam_re, lam_im, log_dt, b_re, b_im, c_re, c_im)
    s5_consts = (row(norm_mix), w_in[:, :s0].astype(BF16), row(b_in[:s0]),
                 mmat, kmat, nmat, aq_re, aq_im, row(d_skip))

    w_r = jnp.zeros((d, LANES), F32).at[:, :N_EXPERTS].set(w_re)
    w_r = w_r.at[:, N_EXPERTS:N_EXPERTS + N_EXPERT_GROUPS].set(w_rg)
    b_r = jnp.zeros((1, LANES), F32).at[0, :N_EXPERTS].set(b_re_r)
    b_r = b_r.at[0, N_EXPERTS:N_EXPERTS + N_EXPERT_GROUPS].set(b_rg)
    w_r_hi = w_r.astype(BF16)
    w_r_lo = (w_r - w_r_hi.astype(F32)).astype(BF16)

    mix_consts = (row(norm_mix),
                  w_in[:, s0:s3].astype(BF16), row(b_in[s0:s3]),
                  w_in[:, s3:].astype(BF16), row(b_in[s3:]),
                  conv_w.astype(F32), row(conv_b), w_conv_out.astype(BF16),
                  jnp.concatenate([w_glu_a, w_glu_b], axis=1).astype(BF16), w_o.astype(BF16),
                  row(norm_ffn), jnp.concatenate([w_r_hi, w_r_lo], axis=1), w_r_hi, b_r)
    ple_consts = (row(norm_ple), w_pg.astype(BF16), row(b_pg), w_ple.astype(BF16), row(norm_out))
    p2d = p.reshape(T, -1)
    n_rows = 2 * T + N_EXPERTS * MOE_RT

    ys = _s5_call(x, *s5_consts)
    x1, h2p, route, route_t, counts, eg16, eu16, ed16 = _mix_call(
        x, ys, *mix_consts, w_eg, w_eu, w_ed)
    pos, plan = _moe_plan(route_t, counts, n_rows)
    xs = _sc_dispatch(h2p, pos[0], pos[1], n_rows)
    ysort = _moe_call(plan, xs, eg16, eu16, ed16)
    y_picks = _sc_gather(ysort, pos.reshape(-1))
    out = _ple_call(x1, y_picks, route, p2d, *ple_consts)
    return out.reshape(nb, seq, d)


def kernel(x, p, norm_mix, w_in, b_in, ssm_lam_re, ssm_lam_im, ssm_log_dt, ssm_b_re, ssm_b_im, ssm_c_re, ssm_c_im, ssm_d, w_glu_a, w_glu_b, conv_w, conv_b, w_conv_out, w_o, norm_ffn, w_router_group, b_router_group, w_router_expert, b_router_expert, w_exp_gate, w_exp_up, w_exp_down, norm_ple, w_ple, w_ple_gate, b_ple_gate, norm_final):
    assert p.shape[0] == 1, "the final RMSNorm is fused into the (single) layer's last kernel"
    i = 0
    return _layer(x, p[i], norm_mix[i], w_in[i], b_in[i], ssm_lam_re[i], ssm_lam_im[i],
                  ssm_log_dt[i], ssm_b_re[i], ssm_b_im[i], ssm_c_re[i], ssm_c_im[i], ssm_d[i],
                  w_glu_a[i], w_glu_b[i], conv_w[i], conv_b[i], w_conv_out[i], w_o[i],
                  norm_ffn[i], w_router_group[i], b_router_group[i], w_router_expert[i],
                  b_router_expert[i], w_exp_gate[i], w_exp_up[i], w_exp_down[i], norm_ple[i],
                  w_ple[i], w_ple_gate[i], b_ple_gate[i], norm_final)
```

```python
import functools
import math

import jax
import jax.numpy as jnp
from jax import lax
from jax.experimental import pallas as pl
from jax.experimental.pallas import tpu as pltpu
from jax.experimental.pallas import tpu_sc as plsc

F32 = jnp.float32
BF16 = jnp.bfloat16

D_MODEL = 1024
SSM_WIDTH = 512
SSM_GROUP = 16
SSM_GROUPS = 32
SSM_STATE = 64
CONV_WIDTH = 512
N_EXPERT_GROUPS = 4
EXPERTS_PER_GROUP = 8
N_EXPERTS = 32
EXPERT_HIDDEN = 256
NORM_EPS = 1e-6

LANES = 128
Q = 8
GROUPS_PER_LANE_TILE = LANES // SSM_GROUP
N_LANE_TILES = SSM_WIDTH // LANES
STATE_LANES = GROUPS_PER_LANE_TILE * SSM_STATE
S5_ROWS = 1024
VMEM_LIMIT = 56 * 1024 * 1024


def _rms(x, g):
    return x * lax.rsqrt(jnp.mean(x * x, axis=-1, keepdims=True) + NORM_EPS) * g


def _pack_bf16_pairs(a):
    w = a.shape[1] // 2
    lo = lax.shift_right_logical(lax.bitcast_convert_type(a[:, :w], jnp.int32), 16)
    hi = lax.bitcast_convert_type(a[:, w:], jnp.int32) & jnp.int32(-65536)
    return lo | hi


def _unpack_bf16_pairs(word):
    lo = lax.bitcast_convert_type(lax.shift_left(word, 16), F32)
    hi = lax.bitcast_convert_type(word & jnp.int32(-65536), F32)
    return lo, hi


def _const_spec(shape):
    n = len(shape)
    return pl.BlockSpec(shape, lambda *_: (0,) * n, pipeline_mode=pl.Buffered(1))


def _s5_operators(lam_re, lam_im, log_dt, b_re, b_im, c_re, c_im):
    G, P, H = SSM_GROUPS, SSM_STATE, SSM_GROUP
    J = N_LANE_TILES
    lr = lam_re.astype(F32)
    li = lam_im.astype(F32)
    dt = jnp.exp(log_dt.astype(F32))[:, None]

    def apow(n):
        n = n.astype(F32)[:, None, None]
        mag = jnp.exp(lr * dt * n)
        ang = li * dt * n
        return mag * jnp.cos(ang), mag * jnp.sin(ang)

    a1_re, a1_im = apow(jnp.ones((1,), F32))
    nr = a1_re[0] - 1.0
    ni = a1_im[0]
    den = lr * lr + li * li
    f_re = (nr * lr + ni * li) / den
    f_im = (ni * lr - nr * li) / den
    br = b_re.astype(F32)
    bi = b_im.astype(F32)
    bbar_re = f_re[:, :, None] * br - f_im[:, :, None] * bi
    bbar_im = f_re[:, :, None] * bi + f_im[:, :, None] * br
    to_rows = lambda v, perm: jnp.transpose(v, perm).reshape(H, G * P)
    bt_re = to_rows(bbar_re, (2, 0, 1))
    bt_im = to_rows(bbar_im, (2, 0, 1))
    ct_re = to_rows(c_re.astype(F32), (1, 0, 2))
    ct_im = to_rows(c_im.astype(F32), (1, 0, 2))
    ap_re, ap_im = apow(jnp.arange(Q + 1))
    ap_re = ap_re.reshape(Q + 1, G * P)
    ap_im = ap_im.reshape(Q + 1, G * P)

    blk = lambda r: pl.BlockSpec((r, STATE_LANES), lambda j: (0, j))
    mat = lambda r, c: pl.BlockSpec((1, r, c), lambda j: (j, 0, 0))
    shape = lambda r, c: jax.ShapeDtypeStruct((J, r, c), BF16)
    qx, st = Q * LANES, 2 * STATE_LANES
    kmat, mmat, nmat = pl.pallas_call(
        _s5_ops_body,
        out_shape=(shape(qx, qx), shape(qx, st), shape(st, qx)),
        grid=(J,),
        in_specs=[blk(Q + 1), blk(Q + 1), blk(H), blk(H), blk(H), blk(H)],
        out_specs=(mat(qx, qx), mat(qx, st), mat(st, qx)),
        compiler_params=pltpu.CompilerParams(
            dimension_semantics=("arbitrary",), vmem_limit_bytes=VMEM_LIMIT),
        name="s5_operators",
    )(ap_re, ap_im, bt_re, bt_im, ct_re, ct_im)
    return kmat, mmat, nmat, ap_re[Q:Q + 1], ap_im[Q:Q + 1]


def _s5_ops_body(apr_ref, api_ref, btr_ref, bti_ref, ctr_ref, cti_ref, k_ref, m_ref, n_ref):
    ri = lax.broadcasted_iota(jnp.int32, (LANES, STATE_LANES), 0)
    li = lax.broadcasted_iota(jnp.int32, (LANES, STATE_LANES), 1)
    same_group = (ri // SSM_GROUP) == (li // SSM_STATE)

    def expand(ref):
        tiled = jnp.concatenate([ref[...]] * GROUPS_PER_LANE_TILE, axis=0)
        return jnp.where(same_group, tiled, 0.0)

    b_re, b_im, c_re, c_im = expand(btr_ref), expand(bti_ref), expand(ctr_ref), expand(cti_ref)

    def cmul(n, x_re, x_im):
        a_re = apr_ref[n:n + 1, :]
        a_im = api_ref[n:n + 1, :]
        return a_re * x_re - a_im * x_im, a_re * x_im + a_im * x_re

    m_blocks = []
    for k in range(Q):
        g_re, g_im = cmul(Q - 1 - k, b_re, b_im)
        m_blocks.append(jnp.concatenate([g_re, g_im], axis=1))
    m = jnp.concatenate(m_blocks, axis=0)

    nt_blocks = []
    for t in range(Q):
        g_re, g_im = cmul(t + 1, c_re, c_im)
        nt_blocks.append(jnp.concatenate([g_re, -g_im], axis=1))
    nt = jnp.concatenate(nt_blocks, axis=0)

    n0t = jnp.concatenate([c_re, -c_im], axis=1)
    p = lax.dot_general(m, n0t, (((1,), (1,)), ((), ())),
                        precision=lax.Precision.HIGHEST, preferred_element_type=F32)
    zeros = jnp.zeros((LANES, LANES), F32)
    cols = []
    for t in range(Q):
        cols.append(jnp.concatenate(
            [p[(Q - 1 - (t - k)) * LANES:(Q - (t - k)) * LANES, :] if t >= k else zeros
             for k in range(Q)], axis=0))
    k_ref[0] = jnp.concatenate(cols, axis=1).astype(BF16)
    m_ref[0] = m.astype(BF16)
    n_ref[0] = nt.T.astype(BF16)


def _s5_body(x_ref, g_ref, wu_ref, bu_ref, m_ref, k_ref, n_ref, aqr_ref, aqi_ref, d_ref,
             o_ref, u_scr, y_scr, z_scr, ss_scr, carry_scr):
    nb, tt = x_ref.shape[0], x_ref.shape[1]
    rows = nb * tt
    nchunk = rows // Q

    @pl.when(pl.program_id(0) == 0)
    def _():
        carry_scr[...] = jnp.zeros_like(carry_scr)

    hb = nb // 2
    for r in range(2):
        x = x_ref[r * hb:(r + 1) * hb].reshape(rows // 2, D_MODEL)
        h = _rms(x, g_ref[...]).astype(BF16)
        u = jnp.dot(h, wu_ref[...], preferred_element_type=F32) + bu_ref[...]
        for j in range(N_LANE_TILES):
            u_scr[j, pl.ds(r * (rows // 2), rows // 2), :] = u[:, j * LANES:(j + 1) * LANES]

    n_st = STATE_LANES // LANES
    cpt = tt // Q

    lane_tiles = range(N_LANE_TILES)


    xs = [jnp.concatenate([u_scr[j, pl.ds(k, nchunk, stride=Q), :] for k in range(Q)],
                          axis=1).astype(BF16) for j in lane_tiles]
    for j in lane_tiles:
        z = jnp.dot(xs[j], m_ref[j], preferred_element_type=F32)
        for i in range(2 * n_st):
            z_scr[j, i] = z[:, i * LANES:(i + 1) * LANES]
    y_intra = [jnp.dot(xs[j], k_ref[j], preferred_element_type=F32) for j in lane_tiles]

    aq = [(jnp.broadcast_to(aqr_ref[:, pl.ds(j * STATE_LANES, STATE_LANES)], (nb, STATE_LANES)),
           jnp.broadcast_to(aqi_ref[:, pl.ds(j * STATE_LANES, STATE_LANES)], (nb, STATE_LANES)))
          for j in lane_tiles]
    st = [(carry_scr[j, :, pl.ds(0, STATE_LANES)], carry_scr[j, :, pl.ds(STATE_LANES, STATE_LANES)])
          for j in lane_tiles]
    for c in range(cpt):
        seq_rows = pl.ds(c, nb, stride=cpt)
        for j in lane_tiles:
            s_re, s_im = st[j]
            aqr, aqi = aq[j]
            for i in range(n_st):
                ss_scr[j, i, seq_rows, :] = s_re[:, i * LANES:(i + 1) * LANES]
                ss_scr[j, n_st + i, seq_rows, :] = s_im[:, i * LANES:(i + 1) * LANES]
            z_re = jnp.concatenate([z_scr[j, i, seq_rows, :] for i in range(n_st)], axis=1)
            z_im = jnp.concatenate([z_scr[j, n_st + i, seq_rows, :] for i in range(n_st)], axis=1)
            st[j] = (aqr * s_re - aqi * s_im + z_re, aqr * s_im + aqi * s_re + z_im)
    for j in lane_tiles:
        carry_scr[j, :, pl.ds(0, STATE_LANES)] = st[j][0]
        carry_scr[j, :, pl.ds(STATE_LANES, STATE_LANES)] = st[j][1]

    def state_to_output(j):
        ss = jnp.concatenate([ss_scr[j, i] for i in range(2 * n_st)], axis=1).astype(BF16)
        return y_intra[j] + jnp.dot(ss, n_ref[j], preferred_element_type=F32)

    def finish(j, yj):
        for k in range(Q):
            y_scr[j, pl.ds(k, nchunk, stride=Q), :] = yj[:, k * LANES:(k + 1) * LANES]
        lanes = pl.ds(j * LANES, LANES)
        y = y_scr[j] + d_ref[:, lanes] * u_scr[j]
        o_ref[:, :, lanes] = jax.nn.gelu(y).astype(BF16).reshape(nb, tt, LANES)

    yj = state_to_output(0)
    for j in lane_tiles:
        y_next = state_to_output(j + 1) if j + 1 < N_LANE_TILES else None
        finish(j, yj)
        yj = y_next


def _s5_call(x, g, wu, bu, mmat, kmat, nmat, aq_re, aq_im, d_skip):
    nb, seq = x.shape[0], x.shape[1]
    tt = S5_ROWS // nb
    rows = S5_ROWS
    nchunk = rows // Q
    return pl.pallas_call(
        _s5_body,
        out_shape=jax.ShapeDtypeStruct((nb, seq, SSM_WIDTH), BF16),
        grid=(seq // tt,),
        in_specs=[
            pl.BlockSpec((nb, tt, D_MODEL), lambda i: (0, i, 0)),
            _const_spec(g.shape), _const_spec(wu.shape), _const_spec(bu.shape),
            _const_spec(mmat.shape), _const_spec(kmat.shape), _const_spec(nmat.shape),
            _const_spec(aq_re.shape), _const_spec(aq_im.shape), _const_spec(d_skip.shape),
        ],
        out_specs=pl.BlockSpec((nb, tt, SSM_WIDTH), lambda i: (0, i, 0)),
        scratch_shapes=[
            pltpu.VMEM((N_LANE_TILES, rows, LANES), F32),
            pltpu.VMEM((N_LANE_TILES, rows, LANES), F32),
            pltpu.VMEM((N_LANE_TILES, 2 * STATE_LANES // LANES, nchunk, LANES), F32),
            pltpu.VMEM((N_LANE_TILES, 2 * STATE_LANES // LANES, nchunk, LANES), F32),
            pltpu.VMEM((N_LANE_TILES, nb, 2 * STATE_LANES), F32),
        ],
        compiler_params=pltpu.CompilerParams(
            dimension_semantics=("arbitrary",), vmem_limit_bytes=VMEM_LIMIT),
        name="s5_mixer",
    )(x, g, wu, bu, mmat, kmat, nmat, aq_re, aq_im, d_skip)


MIX_TM = 512


def _mix_body(x_ref, ys_ref, g_ref, wc_ref, bc_ref, wg_ref, bg_ref, cw_ref, cb_ref, wco_ref,
              wab_ref, wo_ref, g2_ref, wr_ref, wrh_ref, br_ref,
              x1_ref, h2p_ref, route_ref, route_t_ref, cnt_ref, carry_scr, cnt_scr, x1_scr,
              *, tiles_per_seq, n_tiles):
    i = pl.program_id(0)
    tm = x_ref.shape[1]

    @pl.when(i == 0)
    def _():
        cnt_scr[...] = jnp.zeros_like(cnt_scr)
        x1_scr[...] = jnp.zeros_like(x1_scr)
        carry_scr[...] = jnp.zeros_like(carry_scr)

    carry_scr[0] = jnp.where(i == n_tiles, carry_scr[0],
                             jnp.where(i % tiles_per_seq == 0, 0.0, carry_scr[1]))

    yab = jnp.dot(ys_ref[0], wab_ref[...], preferred_element_type=F32)

    h2 = _rms(x1_scr[...], g2_ref[...])
    h2_hi = h2.astype(BF16)
    h2p_ref[...] = _pack_bf16_pairs(h2_hi.astype(F32))
    h2_lo = (h2 - h2_hi.astype(F32)).astype(BF16)
    lg2 = jnp.dot(h2_hi, wr_ref[...], preferred_element_type=F32)
    logits = (lg2[:, 0:LANES] + lg2[:, LANES:2 * LANES]
              + jnp.dot(h2_lo, wrh_ref[...], preferred_element_type=F32) + br_ref[...])

    x = x_ref[0]
    h = _rms(x, g_ref[...]).astype(BF16)
    zc = jnp.dot(h, wc_ref[...], preferred_element_type=F32) + bc_ref[...]
    zg = jnp.dot(h, wg_ref[...], preferred_element_type=F32) + bg_ref[...]

    cnt = cnt_scr[...]
    route, new_cnt = _route_tile(logits, cnt)
    cnt = jnp.where(i > 0, new_cnt, cnt)
    cnt_scr[...] = cnt
    cnt_ref[...] = cnt
    route_ref[...] = route
    route_t_ref[...] = route.T[0:8, :]

    y_a = yab[:, 0:D_MODEL] * jax.nn.sigmoid(yab[:, D_MODEL:2 * D_MODEL])

    c_b = zc[:, 0:CONV_WIDTH]
    cv = zc[:, CONV_WIDTH:2 * CONV_WIDTH] * zc[:, 2 * CONV_WIDTH:3 * CONV_WIDTH]
    row = lax.broadcasted_iota(jnp.int32, (tm, CONV_WIDTH), 0)
    last1 = carry_scr[0, 7:8, :]
    last2 = carry_scr[0, 6:7, :]
    p1 = jnp.where(row == 0, last1, pltpu.roll(cv, 1, axis=0))
    p2 = jnp.where(row == 0, last2, jnp.where(row == 1, last1, pltpu.roll(cv, 2, axis=0)))
    carry_scr[1] = cv[tm - 8:tm, :]
    conv = cw_ref[0:1, :] * p2 + cw_ref[1:2, :] * p1 + cw_ref[2:3, :] * cv + cb_ref[...]
    y_b = jnp.dot((c_b * conv).astype(BF16), wco_ref[...], preferred_element_type=F32)

    mix = (jax.nn.sigmoid(zg[:, 0:D_MODEL]) * y_a
           + jax.nn.sigmoid(zg[:, D_MODEL:2 * D_MODEL]) * y_b)
    x1 = x + jnp.dot(mix.astype(BF16), wo_ref[...], preferred_element_type=F32)
    x1_ref[...] = x1
    x1_scr[...] = x1


def _route_tile(logits, cnt):
    tm = logits.shape[0]
    lane = lax.broadcasted_iota(jnp.int32, (tm, LANES), 1)
    neg = jnp.float32(-jnp.inf)
    big = jnp.int32(1 << 20)
    is_g = (lane >= N_EXPERTS) & (lane < N_EXPERTS + N_EXPERT_GROUPS)
    gl = jnp.where(is_g, logits, neg)
    gmax = jnp.max(gl, axis=1, keepdims=True)
    g_w = 1.0 / jnp.sum(jnp.exp(gl - gmax), axis=1, keepdims=True)
    g_idx = jnp.min(jnp.where(gl == gmax, lane - N_EXPERTS, big), axis=1, keepdims=True)
    lo = g_idx * EXPERTS_PER_GROUP
    el = jnp.where((lane >= lo) & (lane < lo + EXPERTS_PER_GROUP), logits, neg)
    m1 = jnp.max(el, axis=1, keepdims=True)
    i1 = jnp.min(jnp.where(el == m1, lane, big), axis=1, keepdims=True)
    el2 = jnp.where(lane == i1, neg, el)
    m2 = jnp.max(el2, axis=1, keepdims=True)
    i2 = jnp.min(jnp.where(el2 == m2, lane, big), axis=1, keepdims=True)
    r = jnp.exp(m2 - m1)
    w1 = g_w / (1.0 + r)
    w2 = g_w * r / (1.0 + r)

    picks = ((lane == i1) | (lane == i2)).astype(BF16)
    r_i = lax.broadcasted_iota(jnp.int32, (tm, tm), 0)
    c_i = lax.broadcasted_iota(jnp.int32, (tm, tm), 1)
    before = (c_i < r_i).astype(BF16)
    excl = jnp.dot(before, picks, preferred_element_type=F32) + cnt
    rank1 = jnp.sum(jnp.where(lane == i1, excl, 0.0), axis=1, keepdims=True)
    rank2 = jnp.sum(jnp.where(lane == i2, excl, 0.0), axis=1, keepdims=True)
    new_cnt = cnt + jnp.sum(picks.astype(F32), axis=0, keepdims=True)
    route = jnp.where(lane == 0, i1.astype(F32), 0.0)
    route = jnp.where(lane == 1, i2.astype(F32), route)
    route = jnp.where(lane == 2, w1, route)
    route = jnp.where(lane == 3, w2, route)
    route = jnp.where(lane == 4, rank1, route)
    route = jnp.where(lane == 5, rank2, route)
    return route, new_cnt


def _mix_call(x, ys, g, wc, bc, wg, bg, cw, cb, wco, wab, wo, g2, wr, wrh, br):
    nb, seq, _ = ys.shape
    T = nb * seq
    nl = seq // MIX_TM
    n = nb * nl
    consts = (g, wc, bc, wg, bg, cw, cb, wco, wab, wo, g2, wr, wrh, br)
    cur = lambda i: jnp.minimum(i, n - 1)
    prev = lambda i: jnp.maximum(i - 1, 0)
    return pl.pallas_call(
        functools.partial(_mix_body, tiles_per_seq=nl, n_tiles=n),
        out_shape=(jax.ShapeDtypeStruct((T, D_MODEL), F32),
                   jax.ShapeDtypeStruct((T, D_MODEL // 2), jnp.int32),
                   jax.ShapeDtypeStruct((T, LANES), F32),
                   jax.ShapeDtypeStruct((8, T), F32),
                   jax.ShapeDtypeStruct((1, LANES), F32)),
        grid=(n + 1,),
        in_specs=[pl.BlockSpec((1, MIX_TM, D_MODEL),
                               lambda i: (cur(i) // nl, cur(i) % nl, 0)),
                  pl.BlockSpec((1, MIX_TM, SSM_WIDTH), lambda i: (cur(i) // nl, cur(i) % nl, 0))]
                 + [_const_spec(c.shape) for c in consts],
        out_specs=(pl.BlockSpec((MIX_TM, D_MODEL), lambda i: (cur(i), 0)),
                   pl.BlockSpec((MIX_TM, D_MODEL // 2), lambda i: (prev(i), 0)),
                   pl.BlockSpec((MIX_TM, LANES), lambda i: (prev(i), 0)),
                   pl.BlockSpec((8, MIX_TM), lambda i: (0, prev(i))),
                   pl.BlockSpec((1, LANES), lambda i: (0, 0))),
        scratch_shapes=[pltpu.VMEM((2, 8, CONV_WIDTH), F32), pltpu.VMEM((1, LANES), F32),
                        pltpu.VMEM((MIX_TM, D_MODEL), F32)],
        compiler_params=pltpu.CompilerParams(
            dimension_semantics=("arbitrary",), vmem_limit_bytes=VMEM_LIMIT),
        name="conv_glu_router",
    )(x, ys, *consts)


SC_CORES = 2
SC_SUBCORES = 16
SC_WORKERS = SC_CORES * SC_SUBCORES
SC_ROWS = 64
SC_NBUF = 2


def _sc_mesh():
    return plsc.VectorSubcoreMesh(core_axis_name="c", subcore_axis_name="s")


def _sc_worker_id():
    return lax.axis_index("s") * SC_CORES + lax.axis_index("c")


def _sc_ring(nch, get, puts):
    for b in range(SC_NBUF - 1):
        get(b, b).start()

    @pl.loop(0, nch, step=SC_NBUF)
    def _(j0):
        for b in range(SC_NBUF):
            j = j0 + b
            refill = (b - 1) % SC_NBUF
            get(j, b).wait()

            @pl.when(j + SC_NBUF - 1 < nch)
            def _():
                @pl.when(j >= 1)
                def _():
                    for c in puts(j - 1, refill):
                        c.wait()
                get(j + SC_NBUF - 1, refill).start()

            for c in puts(j, b):
                c.start()

    for b in range(SC_NBUF):
        for c in puts(nch - SC_NBUF + b, b):
            c.wait()


def _sc_dispatch(rows, pos_a, pos_b, n_out):
    T, W = rows.shape
    per_w = T // SC_WORKERS
    nch = per_w // SC_ROWS
    assert per_w * SC_WORKERS == T and nch * SC_ROWS == per_w and nch % SC_NBUF == 0
    idx_a = pos_a.reshape(SC_WORKERS, nch, SC_ROWS)
    idx_b = pos_b.reshape(SC_WORKERS, nch, SC_ROWS)

    @functools.partial(
        pl.kernel, mesh=_sc_mesh(),
        out_type=jax.ShapeDtypeStruct((n_out, W), rows.dtype),
        scratch_types=[
            pltpu.VMEM((nch, SC_ROWS), jnp.int32),
            pltpu.VMEM((nch, SC_ROWS), jnp.int32),
            pltpu.VMEM((SC_NBUF, SC_ROWS, W), rows.dtype),
            pltpu.SemaphoreType.DMA((SC_NBUF,)),
            pltpu.SemaphoreType.DMA((SC_NBUF,)),
            pltpu.SemaphoreType.DMA((SC_NBUF,)),
        ],
        name="moe_dispatch",
    )
    def k(rows_hbm, ia_hbm, ib_hbm, out_hbm, ia_v, ib_v, buf, gsem, asem, bsem):
        wid = _sc_worker_id()
        base = wid * per_w
        pltpu.sync_copy(ia_hbm.at[wid], ia_v)
        pltpu.sync_copy(ib_hbm.at[wid], ib_v)

        def get(j, b):
            return pltpu.make_async_copy(
                rows_hbm.at[pl.ds(base + j * SC_ROWS, SC_ROWS)], buf.at[b], gsem.at[b])

        def put_a(j, b):
            return pltpu.make_async_copy(buf.at[b], out_hbm.at[ia_v.at[j]], asem.at[b])

        def put_b(j, b):
            return pltpu.make_async_copy(buf.at[b], out_hbm.at[ib_v.at[j]], bsem.at[b])

        _sc_ring(nch, get, lambda j, b: (put_a(j, b), put_b(j, b)))

    return k(rows, idx_a, idx_b)


def _sc_gather(table, idx):
    _, W = table.shape
    B = idx.shape[0]
    per_w = B // SC_WORKERS
    nch = per_w // SC_ROWS
    assert per_w * SC_WORKERS == B and nch * SC_ROWS == per_w and nch % SC_NBUF == 0
    idx3 = idx.reshape(SC_WORKERS, nch, SC_ROWS)

    @functools.partial(
        pl.kernel, mesh=_sc_mesh(),
        out_type=jax.ShapeDtypeStruct((B, W), table.dtype),
        scratch_types=[
            pltpu.VMEM((nch, SC_ROWS), jnp.int32),
            pltpu.VMEM((SC_NBUF, SC_ROWS, W), table.dtype),
            pltpu.SemaphoreType.DMA((SC_NBUF,)),
            pltpu.SemaphoreType.DMA((SC_NBUF,)),
        ],
        name="moe_combine_gather",
    )
    def k(table_hbm, idx_hbm, out_hbm, idx_v, buf, gsem, osem):
        wid = _sc_worker_id()
        base = wid * per_w
        pltpu.sync_copy(idx_hbm.at[wid], idx_v)

        def get(j, b):
            return pltpu.make_async_copy(table_hbm.at[idx_v.at[j]], buf.at[b], gsem.at[b])

        def put(j, b):
            return pltpu.make_async_copy(
                buf.at[b], out_hbm.at[pl.ds(base + j * SC_ROWS, SC_ROWS)], osem.at[b])

        _sc_ring(nch, get, lambda j, b: (put(j, b),))

    return k(table, idx3)


MOE_RT = 1024
MOE_SUB = 256


def _moe_body(te_ref, nt_ref, nx_ref, seg_ref, x_ref, wg_hbm, wu_hbm, wd_hbm, o_ref,
              wg_scr, wu_scr, wd_scr, wg_buf, wu_buf, wd_buf, sem):
    i = pl.program_id(0)
    expert = te_ref[i]
    slot = seg_ref[i] % 2

    def weight_copies(e, b):
        return (pltpu.make_async_copy(wg_hbm.at[e], wg_buf.at[b], sem.at[b, 0]),
                pltpu.make_async_copy(wu_hbm.at[e], wu_buf.at[b], sem.at[b, 1]),
                pltpu.make_async_copy(wd_hbm.at[e], wd_buf.at[b], sem.at[b, 2]))

    @pl.when(i == 0)
    def _():
        for c in weight_copies(expert, 0):
            c.start()

    @pl.when((i == 0) | (expert != te_ref[jnp.maximum(i - 1, 0)]))
    def _():
        for c in weight_copies(expert, slot):
            c.wait()
        wg_scr[...] = wg_buf[slot].astype(BF16)
        wu_scr[...] = wu_buf[slot].astype(BF16)
        wd_scr[...] = wd_buf[slot].astype(BF16)

        @pl.when(nx_ref[i] >= 0)
        def _():
            for c in weight_copies(nx_ref[i], 1 - slot):
                c.start()

    @pl.when(i < nt_ref[0])
    def _():
        half = D_MODEL // 2
        n_sub = x_ref.shape[0] // MOE_SUB

        def up_proj(s):
            lo, hi = _unpack_bf16_pairs(x_ref[pl.ds(s * MOE_SUB, MOE_SUB), :])
            lo = lo.astype(BF16)
            hi = hi.astype(BF16)
            gate = (jnp.dot(lo, wg_scr[0:half, :], preferred_element_type=F32)
                    + jnp.dot(hi, wg_scr[half:D_MODEL, :], preferred_element_type=F32))
            up = (jnp.dot(lo, wu_scr[0:half, :], preferred_element_type=F32)
                  + jnp.dot(hi, wu_scr[half:D_MODEL, :], preferred_element_type=F32))
            return gate, up

        def down_proj(gate, up):
            hid = (jax.nn.silu(gate) * up).astype(BF16)
            return jnp.dot(hid, wd_scr[...], preferred_element_type=F32)

        def store(s, y):
            o_ref[pl.ds(s * MOE_SUB, MOE_SUB), :] = _pack_bf16_pairs(y.astype(BF16).astype(F32))

        gu = {0: up_proj(0)}
        ys = {}
        for s in range(n_sub):
            if s + 1 < n_sub:
                gu[s + 1] = up_proj(s + 1)
            ys[s] = down_proj(*gu.pop(s))
            if s >= 1:
                store(s - 1, ys.pop(s - 1))
        store(n_sub - 1, ys.pop(n_sub - 1))


def _moe_call(plan, xs, wg, wu, wd):
    R = xs.shape[0]
    half = D_MODEL // 2
    row_map = lambda i, te, nt, nx, seg: (jnp.minimum(i, nt[0] - 1), 0)
    hbm = pl.BlockSpec(memory_space=pl.ANY)
    return pl.pallas_call(
        _moe_body,
        out_shape=jax.ShapeDtypeStruct((R, half), jnp.int32),
        grid_spec=pltpu.PrefetchScalarGridSpec(
            num_scalar_prefetch=4,
            grid=(R // MOE_RT,),
            in_specs=[pl.BlockSpec((MOE_RT, half), row_map), hbm, hbm, hbm],
            out_specs=pl.BlockSpec((MOE_RT, half), row_map),
            scratch_shapes=[pltpu.VMEM((D_MODEL, EXPERT_HIDDEN), BF16),
                            pltpu.VMEM((D_MODEL, EXPERT_HIDDEN), BF16),
                            pltpu.VMEM((EXPERT_HIDDEN, D_MODEL), BF16),
                            pltpu.VMEM((2, D_MODEL, EXPERT_HIDDEN), F32),
                            pltpu.VMEM((2, D_MODEL, EXPERT_HIDDEN), F32),
                            pltpu.VMEM((2, EXPERT_HIDDEN, D_MODEL), F32),
                            pltpu.SemaphoreType.DMA((2, 3))]),
        compiler_params=pltpu.CompilerParams(
            dimension_semantics=("arbitrary",), vmem_limit_bytes=VMEM_LIMIT),
        name="moe_experts",
    )(*plan, xs, wg, wu, wd)


def _moe_plan(route_t, counts, n_rows):
    cnt = counts[0, :N_EXPERTS].astype(jnp.int32)
    tiles = (cnt + MOE_RT - 1) // MOE_RT
    tile_end = jnp.cumsum(tiles)
    n_tiles = tile_end[-1:]
    row_start = (tile_end - tiles) * MOE_RT
    ids = route_t[0:2].astype(jnp.int32)
    ranks = route_t[4:6].astype(jnp.int32)
    experts = jnp.arange(N_EXPERTS, dtype=jnp.int32)[:, None, None]
    pos = ranks + jnp.sum(jnp.where(ids[None] == experts, row_start[:, None, None], 0), axis=0)
    tile_id = jnp.minimum(jnp.arange(n_rows // MOE_RT, dtype=jnp.int32), n_tiles - 1)
    tile_expert = jnp.sum((tile_id[:, None] >= tile_end[None, :]).astype(jnp.int32), axis=1)
    e_ids = jnp.arange(N_EXPERTS, dtype=jnp.int32)
    later = (e_ids[None, :] > e_ids[:, None]) & (tiles[None, :] > 0)
    nxt = jnp.min(jnp.where(later, e_ids[None, :], N_EXPERTS), axis=1)
    nxt = jnp.where(nxt == N_EXPERTS, -1, nxt)
    seg = jnp.cumsum((tiles > 0).astype(jnp.int32)) - 1
    pick = tile_expert[:, None] == e_ids[None, :]
    tile_next = jnp.sum(jnp.where(pick, nxt[None, :], 0), axis=1)
    tile_seg = jnp.sum(jnp.where(pick, seg[None, :], 0), axis=1)
    return pos, (tile_expert, n_tiles, tile_next, tile_seg)


PLE_TM = 1024
PLE_SUB = 256


def _ple_body(x_ref, ya_ref, yb_ref, route_ref, p_ref, g3_ref, wpg_ref, bpg_ref, wple_ref, gf_ref,
              o_ref):
    n_sub = x_ref.shape[0] // PLE_SUB
    g3, gf, bpg = g3_ref[...], gf_ref[...], bpg_ref[...]

    def head(s):
        rows = pl.ds(s * PLE_SUB, PLE_SUB)
        ya = jnp.concatenate(_unpack_bf16_pairs(ya_ref[rows, :]), axis=1)
        yb = jnp.concatenate(_unpack_bf16_pairs(yb_ref[rows, :]), axis=1)
        route = route_ref[rows, :]
        x2 = x_ref[rows, :] + route[:, 2:3] * ya + route[:, 3:4] * yb
        return x2, _rms(x2, g3).astype(BF16)

    def dots(s, h3):
        rows = pl.ds(s * PLE_SUB, PLE_SUB)
        zg = jnp.dot(h3, wpg_ref[...], preferred_element_type=F32)
        pe = jnp.dot(p_ref[rows, :].astype(BF16), wple_ref[...], preferred_element_type=F32)
        return zg, pe

    def tail(s, x2, zg, pe):
        x3 = x2 + jax.nn.sigmoid(zg + bpg) * pe
        o_ref[pl.ds(s * PLE_SUB, PLE_SUB), :] = _rms(x3, gf)

    x2s, mm = {}, {}
    x2s[0], h3 = head(0)
    for s in range(n_sub):
        mm[s] = dots(s, h3)
        if s + 1 < n_sub:
            x2s[s + 1], h3 = head(s + 1)
        if s >= 1:
            tail(s - 1, x2s.pop(s - 1), *mm.pop(s - 1))
    tail(n_sub - 1, x2s.pop(n_sub - 1), *mm.pop(n_sub - 1))


def _ple_call(x1, y_picks, route, p, g3, wpg, bpg, wple, gf):
    T = x1.shape[0]
    nt = T // PLE_TM
    consts = (g3, wpg, bpg, wple, gf)
    tok = lambda i: (i, 0)
    return pl.pallas_call(
        _ple_body,
        out_shape=jax.ShapeDtypeStruct((T, D_MODEL), F32),
        grid=(nt,),
        in_specs=[pl.BlockSpec((PLE_TM, D_MODEL), tok),
                  pl.BlockSpec((PLE_TM, D_MODEL // 2), tok),
                  pl.BlockSpec((PLE_TM, D_MODEL // 2), lambda i: (i + nt, 0)),
                  pl.BlockSpec((PLE_TM, LANES), tok),
                  pl.BlockSpec((PLE_TM, p.shape[1]), tok)]
                 + [_const_spec(c.shape) for c in consts],
        out_specs=pl.BlockSpec((PLE_TM, D_MODEL), tok),
        compiler_params=pltpu.CompilerParams(
            dimension_semantics=("arbitrary",), vmem_limit_bytes=VMEM_LIMIT),
        name="ple_final",
    )(x1, y_picks, y_picks, route, p, *consts)


def _layer(x, p, norm_mix, w_in, b_in, lam_re, lam_im, log_dt, b_re, b_im, c_re, c_im, d_skip,
           w_glu_a, w_glu_b, conv_w, conv_b, w_conv_out, w_o, norm_ffn, w_rg, b_rg, w_re, b_re_r,
           w_eg, w_eu, w_ed, norm_ple, w_ple, w_pg, b_pg, norm_out):
    nb, seq, d = x.shape
    T = nb * seq
    s0 = SSM_WIDTH
    s3 = s0 + 3 * CONV_WIDTH
    row = lambda v: v.reshape(1, -1).astype(F32)

    kmat, mmat, nmat, aq_re, aq_im = _s5_operators(lam_re, lam_im, log_dt, b_re, b_im, c_re, c_im)
    s5_consts = (row(norm_mix), w_in[:, :s0].astype(BF16), row(b_in[:s0]),
                 mmat, kmat, nmat, aq_re, aq_im, row(d_skip))

    w_r = jnp.zeros((d, LANES), F32).at[:, :N_EXPERTS].set(w_re)
    w_r = w_r.at[:, N_EXPERTS:N_EXPERTS + N_EXPERT_GROUPS].set(w_rg)
    b_r = jnp.zeros((1, LANES), F32).at[0, :N_EXPERTS].set(b_re_r)
    b_r = b_r.at[0, N_EXPERTS:N_EXPERTS + N_EXPERT_GROUPS].set(b_rg)
    w_r_hi = w_r.astype(BF16)
    w_r_lo = (w_r - w_r_hi.astype(F32)).astype(BF16)

    mix_consts = (row(norm_mix),
                  w_in[:, s0:s3].astype(BF16), row(b_in[s0:s3]),
                  w_in[:, s3:].astype(BF16), row(b_in[s3:]),
                  conv_w.astype(F32), row(conv_b), w_conv_out.astype(BF16),
                  jnp.concatenate([w_glu_a, w_glu_b], axis=1).astype(BF16), w_o.astype(BF16),
                  row(norm_ffn), jnp.concatenate([w_r_hi, w_r_lo], axis=1), w_r_hi, b_r)
    ple_consts = (row(norm_ple), w_pg.astype(BF16), row(b_pg), w_ple.astype(BF16), row(norm_out))
    p2d = p.reshape(T, -1)
    n_rows = 2 * T + N_EXPERTS * MOE_RT

    ys = _s5_call(x, *s5_consts)
    x1, h2p, route, route_t, counts = _mix_call(x, ys, *mix_consts)
    pos, plan = _moe_plan(route_t, counts, n_rows)
    xs = _sc_dispatch(h2p, pos[0], pos[1], n_rows)
    ysort = _moe_call(plan, xs, w_eg, w_eu, w_ed)
    y_picks = _sc_gather(ysort, pos.reshape(-1))
    out = _ple_call(x1, y_picks, route, p2d, *ple_consts)
    return out.reshape(nb, seq, d)


def kernel(x, p, norm_mix, w_in, b_in, ssm_lam_re, ssm_lam_im, ssm_log_dt, ssm_b_re, ssm_b_im, ssm_c_re, ssm_c_im, ssm_d, w_glu_a, w_glu_b, conv_w, conv_b, w_conv_out, w_o, norm_ffn, w_router_group, b_router_group, w_router_expert, b_router_expert, w_exp_gate, w_exp_up, w_exp_down, norm_ple, w_ple, w_ple_gate, b_ple_gate, norm_final):
    assert p.shape[0] == 1, "the final RMSNorm is fused into the (single) layer's last kernel"
    i = 0
    return _layer(x, p[i], norm_mix[i], w_in[i], b_in[i], ssm_lam_re[i], ssm_lam_im[i],
                  ssm_log_dt[i], ssm_b_re[i], ssm_b_im[i], ssm_c_re[i], ssm_c_im[i], ssm_d[i],
                  w_glu_a[i], w_glu_b[i], conv_w[i], conv_b[i], w_conv_out[i], w_o[i],
                  norm_ffn[i], w_router_group[i], b_router_group[i], w_router_expert[i],
                  b_router_expert[i], w_exp_gate[i], w_exp_up[i], w_exp_down[i], norm_ple[i],
                  w_ple[i], w_ple_gate[i], b_ple_gate[i], norm_final)
```

```python
import functools
import math

import jax
import jax.numpy as jnp
from jax import lax
from jax.experimental import pallas as pl
from jax.experimental.pallas import tpu as pltpu
from jax.experimental.pallas import tpu_sc as plsc

F32 = jnp.float32
BF16 = jnp.bfloat16

D_MODEL = 1024
SSM_WIDTH = 512
SSM_GROUP = 16
SSM_GROUPS = 32
SSM_STATE = 64
CONV_WIDTH = 512
N_EXPERT_GROUPS = 4
EXPERTS_PER_GROUP = 8
N_EXPERTS = 32
EXPERT_HIDDEN = 256
NORM_EPS = 1e-6

LANES = 128
Q = 8
GROUPS_PER_LANE_TILE = LANES // SSM_GROUP
N_LANE_TILES = SSM_WIDTH // LANES
STATE_LANES = GROUPS_PER_LANE_TILE * SSM_STATE
S5_ROWS = 1024
VMEM_LIMIT = 56 * 1024 * 1024


def _rms(x, g):
    return x * lax.rsqrt(jnp.mean(x * x, axis=-1, keepdims=True) + NORM_EPS) * g


def _pack_bf16_pairs(a):
    w = a.shape[1] // 2
    lo = lax.shift_right_logical(lax.bitcast_convert_type(a[:, :w], jnp.int32), 16)
    hi = lax.bitcast_convert_type(a[:, w:], jnp.int32) & jnp.int32(-65536)
    return lo | hi


def _unpack_bf16_pairs(word):
    lo = lax.bitcast_convert_type(lax.shift_left(word, 16), F32)
    hi = lax.bitcast_convert_type(word & jnp.int32(-65536), F32)
    return lo, hi


def _const_spec(shape):
    n = len(shape)
    return pl.BlockSpec(shape, lambda *_: (0,) * n, pipeline_mode=pl.Buffered(1))


def _s5_operators(lam_re, lam_im, log_dt, b_re, b_im, c_re, c_im):
    G, P, H = SSM_GROUPS, SSM_STATE, SSM_GROUP
    J = N_LANE_TILES
    lr = lam_re.astype(F32)
    li = lam_im.astype(F32)
    dt = jnp.exp(log_dt.astype(F32))[:, None]

    def apow(n):
        n = n.astype(F32)[:, None, None]
        mag = jnp.exp(lr * dt * n)
        ang = li * dt * n
        return mag * jnp.cos(ang), mag * jnp.sin(ang)

    a1_re, a1_im = apow(jnp.ones((1,), F32))
    nr = a1_re[0] - 1.0
    ni = a1_im[0]
    den = lr * lr + li * li
    f_re = (nr * lr + ni * li) / den
    f_im = (ni * lr - nr * li) / den
    br = b_re.astype(F32)
    bi = b_im.astype(F32)
    bbar_re = f_re[:, :, None] * br - f_im[:, :, None] * bi
    bbar_im = f_re[:, :, None] * bi + f_im[:, :, None] * br
    to_rows = lambda v, perm: jnp.transpose(v, perm).reshape(H, G * P)
    bt_re = to_rows(bbar_re, (2, 0, 1))
    bt_im = to_rows(bbar_im, (2, 0, 1))
    ct_re = to_rows(c_re.astype(F32), (1, 0, 2))
    ct_im = to_rows(c_im.astype(F32), (1, 0, 2))
    ap_re, ap_im = apow(jnp.arange(Q + 1))
    ap_re = ap_re.reshape(Q + 1, G * P)
    ap_im = ap_im.reshape(Q + 1, G * P)

    blk = lambda r: pl.BlockSpec((r, STATE_LANES), lambda j: (0, j))
    mat = lambda r, c: pl.BlockSpec((1, r, c), lambda j: (j, 0, 0))
    shape = lambda r, c: jax.ShapeDtypeStruct((J, r, c), BF16)
    qx, st = Q * LANES, 2 * STATE_LANES
    kmat, mmat, nmat = pl.pallas_call(
        _s5_ops_body,
        out_shape=(shape(qx, qx), shape(qx, st), shape(st, qx)),
        grid=(J,),
        in_specs=[blk(Q + 1), blk(Q + 1), blk(H), blk(H), blk(H), blk(H)],
        out_specs=(mat(qx, qx), mat(qx, st), mat(st, qx)),
        compiler_params=pltpu.CompilerParams(
            dimension_semantics=("arbitrary",), vmem_limit_bytes=VMEM_LIMIT),
        name="s5_operators",
    )(ap_re, ap_im, bt_re, bt_im, ct_re, ct_im)
    return kmat, mmat, nmat, ap_re[Q:Q + 1], ap_im[Q:Q + 1]


def _s5_ops_body(apr_ref, api_ref, btr_ref, bti_ref, ctr_ref, cti_ref, k_ref, m_ref, n_ref):
    ri = lax.broadcasted_iota(jnp.int32, (LANES, STATE_LANES), 0)
    li = lax.broadcasted_iota(jnp.int32, (LANES, STATE_LANES), 1)
    same_group = (ri // SSM_GROUP) == (li // SSM_STATE)

    def expand(ref):
        tiled = jnp.concatenate([ref[...]] * GROUPS_PER_LANE_TILE, axis=0)
        return jnp.where(same_group, tiled, 0.0)

    b_re, b_im, c_re, c_im = expand(btr_ref), expand(bti_ref), expand(ctr_ref), expand(cti_ref)

    def cmul(n, x_re, x_im):
        a_re = apr_ref[n:n + 1, :]
        a_im = api_ref[n:n + 1, :]
        return a_re * x_re - a_im * x_im, a_re * x_im + a_im * x_re

    m_blocks = []
    for k in range(Q):
        g_re, g_im = cmul(Q - 1 - k, b_re, b_im)
        m_blocks.append(jnp.concatenate([g_re, g_im], axis=1))
    m = jnp.concatenate(m_blocks, axis=0)

    nt_blocks = []
    for t in range(Q):
        g_re, g_im = cmul(t + 1, c_re, c_im)
        nt_blocks.append(jnp.concatenate([g_re, -g_im], axis=1))
    nt = jnp.concatenate(nt_blocks, axis=0)

    n0t = jnp.concatenate([c_re, -c_im], axis=1)
    p = lax.dot_general(m, n0t, (((1,), (1,)), ((), ())),
                        precision=lax.Precision.HIGHEST, preferred_element_type=F32)
    zeros = jnp.zeros((LANES, LANES), F32)
    cols = []
    for t in range(Q):
        cols.append(jnp.concatenate(
            [p[(Q - 1 - (t - k)) * LANES:(Q - (t - k)) * LANES, :] if t >= k else zeros
             for k in range(Q)], axis=0))
    k_ref[0] = jnp.concatenate(cols, axis=1).astype(BF16)
    m_ref[0] = m.astype(BF16)
    n_ref[0] = nt.T.astype(BF16)


def _s5_body(x_ref, g_ref, wu_ref, bu_ref, m_ref, k_ref, n_ref, aqr_ref, aqi_ref, d_ref,
             o_ref, u_scr, y_scr, z_scr, ss_scr, carry_scr):
    nb, tt = x_ref.shape[0], x_ref.shape[1]
    rows = nb * tt
    nchunk = rows // Q

    @pl.when(pl.program_id(0) == 0)
    def _():
        carry_scr[...] = jnp.zeros_like(carry_scr)

    hb = nb // 2
    for r in range(2):
        x = x_ref[r * hb:(r + 1) * hb].reshape(rows // 2, D_MODEL)
        h = _rms(x, g_ref[...]).astype(BF16)
        u = jnp.dot(h, wu_ref[...], preferred_element_type=F32) + bu_ref[...]
        for j in range(N_LANE_TILES):
            u_scr[j, pl.ds(r * (rows // 2), rows // 2), :] = u[:, j * LANES:(j + 1) * LANES]

    n_st = STATE_LANES // LANES
    cpt = tt // Q

    lane_tiles = range(N_LANE_TILES)


    xs = [jnp.concatenate([u_scr[j, pl.ds(k, nchunk, stride=Q), :] for k in range(Q)],
                          axis=1).astype(BF16) for j in lane_tiles]
    for j in lane_tiles:
        z = jnp.dot(xs[j], m_ref[j], preferred_element_type=F32)
        for i in range(2 * n_st):
            z_scr[j, i] = z[:, i * LANES:(i + 1) * LANES]
    y_intra = [jnp.dot(xs[j], k_ref[j], preferred_element_type=F32) for j in lane_tiles]

    aq = [(jnp.broadcast_to(aqr_ref[:, pl.ds(j * STATE_LANES, STATE_LANES)], (nb, STATE_LANES)),
           jnp.broadcast_to(aqi_ref[:, pl.ds(j * STATE_LANES, STATE_LANES)], (nb, STATE_LANES)))
          for j in lane_tiles]
    st = [(carry_scr[j, :, pl.ds(0, STATE_LANES)], carry_scr[j, :, pl.ds(STATE_LANES, STATE_LANES)])
          for j in lane_tiles]
    for c in range(cpt):
        seq_rows = pl.ds(c, nb, stride=cpt)
        for j in lane_tiles:
            s_re, s_im = st[j]
            aqr, aqi = aq[j]
            for i in range(n_st):
                ss_scr[j, i, seq_rows, :] = s_re[:, i * LANES:(i + 1) * LANES]
                ss_scr[j, n_st + i, seq_rows, :] = s_im[:, i * LANES:(i + 1) * LANES]
            z_re = jnp.concatenate([z_scr[j, i, seq_rows, :] for i in range(n_st)], axis=1)
            z_im = jnp.concatenate([z_scr[j, n_st + i, seq_rows, :] for i in range(n_st)], axis=1)
            st[j] = (aqr * s_re - aqi * s_im + z_re, aqr * s_im + aqi * s_re + z_im)
    for j in lane_tiles:
        carry_scr[j, :, pl.ds(0, STATE_LANES)] = st[j][0]
        carry_scr[j, :, pl.ds(STATE_LANES, STATE_LANES)] = st[j][1]

    def state_to_output(j):
        ss = jnp.concatenate([ss_scr[j, i] for i in range(2 * n_st)], axis=1).astype(BF16)
        return y_intra[j] + jnp.dot(ss, n_ref[j], preferred_element_type=F32)

    def finish(j, yj):
        for k in range(Q):
            y_scr[j, pl.ds(k, nchunk, stride=Q), :] = yj[:, k * LANES:(k + 1) * LANES]
        lanes = pl.ds(j * LANES, LANES)
        y = y_scr[j] + d_ref[:, lanes] * u_scr[j]
        o_ref[:, :, lanes] = jax.nn.gelu(y).astype(BF16).reshape(nb, tt, LANES)

    yj = state_to_output(0)
    for j in lane_tiles:
        y_next = state_to_output(j + 1) if j + 1 < N_LANE_TILES else None
        finish(j, yj)
        yj = y_next


def _col_block(rows, width, block):
    return pl.BlockSpec((rows, width), lambda *_: (0, block), pipeline_mode=pl.Buffered(1))


def _s5_call(x, g, w_in, b_in, mmat, kmat, nmat, aq_re, aq_im, d_skip):
    nb, seq = x.shape[0], x.shape[1]
    tt = S5_ROWS // nb
    rows = S5_ROWS
    nchunk = rows // Q
    return pl.pallas_call(
        _s5_body,
        out_shape=jax.ShapeDtypeStruct((nb, seq, SSM_WIDTH), BF16),
        grid=(seq // tt,),
        in_specs=[
            pl.BlockSpec((nb, tt, D_MODEL), lambda i: (0, i, 0)),
            _const_spec(g.shape), _col_block(D_MODEL, SSM_WIDTH, 0), _col_block(1, SSM_WIDTH, 0),
            _const_spec(mmat.shape), _const_spec(kmat.shape), _const_spec(nmat.shape),
            _const_spec(aq_re.shape), _const_spec(aq_im.shape), _const_spec(d_skip.shape),
        ],
        out_specs=pl.BlockSpec((nb, tt, SSM_WIDTH), lambda i: (0, i, 0)),
        scratch_shapes=[
            pltpu.VMEM((N_LANE_TILES, rows, LANES), F32),
            pltpu.VMEM((N_LANE_TILES, rows, LANES), F32),
            pltpu.VMEM((N_LANE_TILES, 2 * STATE_LANES // LANES, nchunk, LANES), F32),
            pltpu.VMEM((N_LANE_TILES, 2 * STATE_LANES // LANES, nchunk, LANES), F32),
            pltpu.VMEM((N_LANE_TILES, nb, 2 * STATE_LANES), F32),
        ],
        compiler_params=pltpu.CompilerParams(
            dimension_semantics=("arbitrary",), vmem_limit_bytes=VMEM_LIMIT),
        name="s5_mixer",
    )(x, g, w_in, b_in, mmat, kmat, nmat, aq_re, aq_im, d_skip)


MIX_TM = 512


def _mix_body(x_ref, ys_ref, wcb_ref, wcc_ref, wcv_ref, wg_ref, bcb_ref, bcc_ref, bcv_ref, bg_ref,
              g_ref, cw_ref, cb_ref, wco_ref, wab_ref, wo_ref, g2_ref, wr_ref, wrh_ref, br_ref,
              x1_ref, h2p_ref, route_ref, route_t_ref, cnt_ref, carry_scr, cnt_scr, x1_scr,
              *, tiles_per_seq, n_tiles):
    i = pl.program_id(0)
    tm = x_ref.shape[1]
    wc_refs = (wcb_ref, wcc_ref, wcv_ref)
    bc_refs = (bcb_ref, bcc_ref, bcv_ref)

    @pl.when(i == 0)
    def _():
        cnt_scr[...] = jnp.zeros_like(cnt_scr)
        x1_scr[...] = jnp.zeros_like(x1_scr)
        carry_scr[...] = jnp.zeros_like(carry_scr)

    carry_scr[0] = jnp.where(i == n_tiles, carry_scr[0],
                             jnp.where(i % tiles_per_seq == 0, 0.0, carry_scr[1]))

    yab = jnp.dot(ys_ref[0], wab_ref[...], preferred_element_type=F32)

    h2 = _rms(x1_scr[...], g2_ref[...])
    h2_hi = h2.astype(BF16)
    h2p_ref[...] = _pack_bf16_pairs(h2_hi.astype(F32))
    h2_lo = (h2 - h2_hi.astype(F32)).astype(BF16)
    lg2 = jnp.dot(h2_hi, wr_ref[...], preferred_element_type=F32)
    logits = (lg2[:, 0:LANES] + lg2[:, LANES:2 * LANES]
              + jnp.dot(h2_lo, wrh_ref[...], preferred_element_type=F32) + br_ref[...])

    x = x_ref[0]
    h = _rms(x, g_ref[...]).astype(BF16)
    c_b, c_c, c_v = [jnp.dot(h, w[...], preferred_element_type=F32) + b[...]
                     for w, b in zip(wc_refs, bc_refs)]
    zg = jnp.dot(h, wg_ref[...], preferred_element_type=F32) + bg_ref[...]

    cnt = cnt_scr[...]
    route, new_cnt = _route_tile(logits, cnt)
    cnt = jnp.where(i > 0, new_cnt, cnt)
    cnt_scr[...] = cnt
    cnt_ref[...] = cnt
    route_ref[...] = route
    route_t_ref[...] = route.T[0:8, :]

    y_a = yab[:, 0:D_MODEL] * jax.nn.sigmoid(yab[:, D_MODEL:2 * D_MODEL])

    cv = c_c * c_v
    row = lax.broadcasted_iota(jnp.int32, (tm, CONV_WIDTH), 0)
    last1 = carry_scr[0, 7:8, :]
    last2 = carry_scr[0, 6:7, :]
    p1 = jnp.where(row == 0, last1, pltpu.roll(cv, 1, axis=0))
    p2 = jnp.where(row == 0, last2, jnp.where(row == 1, last1, pltpu.roll(cv, 2, axis=0)))
    carry_scr[1] = cv[tm - 8:tm, :]
    conv = cw_ref[0:1, :] * p2 + cw_ref[1:2, :] * p1 + cw_ref[2:3, :] * cv + cb_ref[...]
    y_b = jnp.dot((c_b * conv).astype(BF16), wco_ref[...], preferred_element_type=F32)

    mix = (jax.nn.sigmoid(zg[:, 0:D_MODEL]) * y_a
           + jax.nn.sigmoid(zg[:, D_MODEL:2 * D_MODEL]) * y_b)
    x1 = x + jnp.dot(mix.astype(BF16), wo_ref[...], preferred_element_type=F32)
    x1_ref[...] = x1
    x1_scr[...] = x1


def _route_tile(logits, cnt):
    tm = logits.shape[0]
    lane = lax.broadcasted_iota(jnp.int32, (tm, LANES), 1)
    neg = jnp.float32(-jnp.inf)
    big = jnp.int32(1 << 20)
    is_g = (lane >= N_EXPERTS) & (lane < N_EXPERTS + N_EXPERT_GROUPS)
    gl = jnp.where(is_g, logits, neg)
    gmax = jnp.max(gl, axis=1, keepdims=True)
    g_w = 1.0 / jnp.sum(jnp.exp(gl - gmax), axis=1, keepdims=True)
    g_idx = jnp.min(jnp.where(gl == gmax, lane - N_EXPERTS, big), axis=1, keepdims=True)
    lo = g_idx * EXPERTS_PER_GROUP
    el = jnp.where((lane >= lo) & (lane < lo + EXPERTS_PER_GROUP), logits, neg)
    m1 = jnp.max(el, axis=1, keepdims=True)
    i1 = jnp.min(jnp.where(el == m1, lane, big), axis=1, keepdims=True)
    el2 = jnp.where(lane == i1, neg, el)
    m2 = jnp.max(el2, axis=1, keepdims=True)
    i2 = jnp.min(jnp.where(el2 == m2, lane, big), axis=1, keepdims=True)
    r = jnp.exp(m2 - m1)
    w1 = g_w / (1.0 + r)
    w2 = g_w * r / (1.0 + r)

    picks = ((lane == i1) | (lane == i2)).astype(BF16)
    r_i = lax.broadcasted_iota(jnp.int32, (tm, tm), 0)
    c_i = lax.broadcasted_iota(jnp.int32, (tm, tm), 1)
    before = (c_i < r_i).astype(BF16)
    excl = jnp.dot(before, picks, preferred_element_type=F32) + cnt
    rank1 = jnp.sum(jnp.where(lane == i1, excl, 0.0), axis=1, keepdims=True)
    rank2 = jnp.sum(jnp.where(lane == i2, excl, 0.0), axis=1, keepdims=True)
    new_cnt = cnt + jnp.sum(picks.astype(F32), axis=0, keepdims=True)
    route = jnp.where(lane == 0, i1.astype(F32), 0.0)
    route = jnp.where(lane == 1, i2.astype(F32), route)
    route = jnp.where(lane == 2, w1, route)
    route = jnp.where(lane == 3, w2, route)
    route = jnp.where(lane == 4, rank1, route)
    route = jnp.where(lane == 5, rank2, route)
    return route, new_cnt


def _mix_call(x, ys, w_in, b_in, g, cw, cb, wco, wab, wo, g2, wr, wrh, br):
    nb, seq, _ = ys.shape
    T = nb * seq
    nl = seq // MIX_TM
    n = nb * nl
    consts = (g, cw, cb, wco, wab, wo, g2, wr, wrh, br)
    gates = 2 * D_MODEL
    in_proj_specs = ([_col_block(D_MODEL, CONV_WIDTH, b) for b in (1, 2, 3)]
                     + [_col_block(D_MODEL, gates, 1)]
                     + [_col_block(1, CONV_WIDTH, b) for b in (1, 2, 3)]
                     + [_col_block(1, gates, 1)])
    cur = lambda i: jnp.minimum(i, n - 1)
    prev = lambda i: jnp.maximum(i - 1, 0)
    return pl.pallas_call(
        functools.partial(_mix_body, tiles_per_seq=nl, n_tiles=n),
        out_shape=(jax.ShapeDtypeStruct((T, D_MODEL), F32),
                   jax.ShapeDtypeStruct((T, D_MODEL // 2), jnp.int32),
                   jax.ShapeDtypeStruct((T, LANES), F32),
                   jax.ShapeDtypeStruct((8, T), F32),
                   jax.ShapeDtypeStruct((1, LANES), F32)),
        grid=(n + 1,),
        in_specs=[pl.BlockSpec((1, MIX_TM, D_MODEL),
                               lambda i: (cur(i) // nl, cur(i) % nl, 0)),
                  pl.BlockSpec((1, MIX_TM, SSM_WIDTH), lambda i: (cur(i) // nl, cur(i) % nl, 0))]
                 + in_proj_specs + [_const_spec(c.shape) for c in consts],
        out_specs=(pl.BlockSpec((MIX_TM, D_MODEL), lambda i: (cur(i), 0)),
                   pl.BlockSpec((MIX_TM, D_MODEL // 2), lambda i: (prev(i), 0)),
                   pl.BlockSpec((MIX_TM, LANES), lambda i: (prev(i), 0)),
                   pl.BlockSpec((8, MIX_TM), lambda i: (0, prev(i))),
                   pl.BlockSpec((1, LANES), lambda i: (0, 0))),
        scratch_shapes=[pltpu.VMEM((2, 8, CONV_WIDTH), F32), pltpu.VMEM((1, LANES), F32),
                        pltpu.VMEM((MIX_TM, D_MODEL), F32)],
        compiler_params=pltpu.CompilerParams(
            dimension_semantics=("arbitrary",), vmem_limit_bytes=VMEM_LIMIT),
        name="conv_glu_router",
    )(x, ys, *([w_in] * 4), *([b_in] * 4), *consts)


SC_CORES = 2
SC_SUBCORES = 16
SC_WORKERS = SC_CORES * SC_SUBCORES
SC_ROWS = 64
SC_NBUF = 2


def _sc_mesh():
    return plsc.VectorSubcoreMesh(core_axis_name="c", subcore_axis_name="s")


def _sc_worker_id():
    return lax.axis_index("s") * SC_CORES + lax.axis_index("c")


def _sc_ring(nch, get, puts):
    for b in range(SC_NBUF - 1):
        get(b, b).start()

    @pl.loop(0, nch, step=SC_NBUF)
    def _(j0):
        for b in range(SC_NBUF):
            j = j0 + b
            refill = (b - 1) % SC_NBUF
            get(j, b).wait()

            @pl.when(j + SC_NBUF - 1 < nch)
            def _():
                @pl.when(j >= 1)
                def _():
                    for c in puts(j - 1, refill):
                        c.wait()
                get(j + SC_NBUF - 1, refill).start()

            for c in puts(j, b):
                c.start()

    for b in range(SC_NBUF):
        for c in puts(nch - SC_NBUF + b, b):
            c.wait()


def _sc_dispatch(rows, pos_a, pos_b, n_out):
    T, W = rows.shape
    per_w = T // SC_WORKERS
    nch = per_w // SC_ROWS
    assert per_w * SC_WORKERS == T and nch * SC_ROWS == per_w and nch % SC_NBUF == 0
    idx_a = pos_a.reshape(SC_WORKERS, nch, SC_ROWS)
    idx_b = pos_b.reshape(SC_WORKERS, nch, SC_ROWS)

    @functools.partial(
        pl.kernel, mesh=_sc_mesh(),
        out_type=jax.ShapeDtypeStruct((n_out, W), rows.dtype),
        scratch_types=[
            pltpu.VMEM((nch, SC_ROWS), jnp.int32),
            pltpu.VMEM((nch, SC_ROWS), jnp.int32),
            pltpu.VMEM((SC_NBUF, SC_ROWS, W), rows.dtype),
            pltpu.SemaphoreType.DMA((SC_NBUF,)),
            pltpu.SemaphoreType.DMA((SC_NBUF,)),
            pltpu.SemaphoreType.DMA((SC_NBUF,)),
        ],
        name="moe_dispatch",
    )
    def k(rows_hbm, ia_hbm, ib_hbm, out_hbm, ia_v, ib_v, buf, gsem, asem, bsem):
        wid = _sc_worker_id()
        base = wid * per_w
        pltpu.sync_copy(ia_hbm.at[wid], ia_v)
        pltpu.sync_copy(ib_hbm.at[wid], ib_v)

        def get(j, b):
            return pltpu.make_async_copy(
                rows_hbm.at[pl.ds(base + j * SC_ROWS, SC_ROWS)], buf.at[b], gsem.at[b])

        def put_a(j, b):
            return pltpu.make_async_copy(buf.at[b], out_hbm.at[ia_v.at[j]], asem.at[b])

        def put_b(j, b):
            return pltpu.make_async_copy(buf.at[b], out_hbm.at[ib_v.at[j]], bsem.at[b])

        _sc_ring(nch, get, lambda j, b: (put_a(j, b), put_b(j, b)))

    return k(rows, idx_a, idx_b)


def _sc_gather(table, idx):
    _, W = table.shape
    B = idx.shape[0]
    per_w = B // SC_WORKERS
    nch = per_w // SC_ROWS
    assert per_w * SC_WORKERS == B and nch * SC_ROWS == per_w and nch % SC_NBUF == 0
    idx3 = idx.reshape(SC_WORKERS, nch, SC_ROWS)

    @functools.partial(
        pl.kernel, mesh=_sc_mesh(),
        out_type=jax.ShapeDtypeStruct((B, W), table.dtype),
        scratch_types=[
            pltpu.VMEM((nch, SC_ROWS), jnp.int32),
            pltpu.VMEM((SC_NBUF, SC_ROWS, W), table.dtype),
            pltpu.SemaphoreType.DMA((SC_NBUF,)),
            pltpu.SemaphoreType.DMA((SC_NBUF,)),
        ],
        name="moe_combine_gather",
    )
    def k(table_hbm, idx_hbm, out_hbm, idx_v, buf, gsem, osem):
        wid = _sc_worker_id()
        base = wid * per_w
        pltpu.sync_copy(idx_hbm.at[wid], idx_v)

        def get(j, b):
            return pltpu.make_async_copy(table_hbm.at[idx_v.at[j]], buf.at[b], gsem.at[b])

        def put(j, b):
            return pltpu.make_async_copy(
                buf.at[b], out_hbm.at[pl.ds(base + j * SC_ROWS, SC_ROWS)], osem.at[b])

        _sc_ring(nch, get, lambda j, b: (put(j, b),))

    return k(table, idx3)


MOE_RT = 1024
MOE_SUB = 256


def _moe_body(te_ref, nt_ref, nx_ref, seg_ref, x_ref, wg_hbm, wu_hbm, wd_hbm, o_ref,
              wg_scr, wu_scr, wd_scr, wg_buf, wu_buf, wd_buf, sem):
    i = pl.program_id(0)
    expert = te_ref[i]
    slot = seg_ref[i] % 2

    def weight_copies(e, b):
        return (pltpu.make_async_copy(wg_hbm.at[e], wg_buf.at[b], sem.at[b, 0]),
                pltpu.make_async_copy(wu_hbm.at[e], wu_buf.at[b], sem.at[b, 1]),
                pltpu.make_async_copy(wd_hbm.at[e], wd_buf.at[b], sem.at[b, 2]))

    @pl.when(i == 0)
    def _():
        for c in weight_copies(expert, 0):
            c.start()

    @pl.when((i == 0) | (expert != te_ref[jnp.maximum(i - 1, 0)]))
    def _():
        for c in weight_copies(expert, slot):
            c.wait()
        wg_scr[...] = wg_buf[slot].astype(BF16)
        wu_scr[...] = wu_buf[slot].astype(BF16)
        wd_scr[...] = wd_buf[slot].astype(BF16)

        @pl.when(nx_ref[i] >= 0)
        def _():
            for c in weight_copies(nx_ref[i], 1 - slot):
                c.start()

    @pl.when(i < nt_ref[0])
    def _():
        half = D_MODEL // 2
        n_sub = x_ref.shape[0] // MOE_SUB

        def up_proj(s):
            lo, hi = _unpack_bf16_pairs(x_ref[pl.ds(s * MOE_SUB, MOE_SUB), :])
            lo = lo.astype(BF16)
            hi = hi.astype(BF16)
            gate = (jnp.dot(lo, wg_scr[0:half, :], preferred_element_type=F32)
                    + jnp.dot(hi, wg_scr[half:D_MODEL, :], preferred_element_type=F32))
            up = (jnp.dot(lo, wu_scr[0:half, :], preferred_element_type=F32)
                  + jnp.dot(hi, wu_scr[half:D_MODEL, :], preferred_element_type=F32))
            return gate, up

        def down_proj(gate, up):
            hid = (jax.nn.silu(gate) * up).astype(BF16)
            return jnp.dot(hid, wd_scr[...], preferred_element_type=F32)

        def store(s, y):
            o_ref[pl.ds(s * MOE_SUB, MOE_SUB), :] = _pack_bf16_pairs(y.astype(BF16).astype(F32))

        gu = {0: up_proj(0)}
        ys = {}
        for s in range(n_sub):
            if s + 1 < n_sub:
                gu[s + 1] = up_proj(s + 1)
            ys[s] = down_proj(*gu.pop(s))
            if s >= 1:
                store(s - 1, ys.pop(s - 1))
        store(n_sub - 1, ys.pop(n_sub - 1))


def _moe_call(plan, xs, wg, wu, wd):
    R = xs.shape[0]
    half = D_MODEL // 2
    row_map = lambda i, te, nt, nx, seg: (jnp.minimum(i, nt[0] - 1), 0)
    hbm = pl.BlockSpec(memory_space=pl.ANY)
    return pl.pallas_call(
        _moe_body,
        out_shape=jax.ShapeDtypeStruct((R, half), jnp.int32),
        grid_spec=pltpu.PrefetchScalarGridSpec(
            num_scalar_prefetch=4,
            grid=(R // MOE_RT,),
            in_specs=[pl.BlockSpec((MOE_RT, half), row_map), hbm, hbm, hbm],
            out_specs=pl.BlockSpec((MOE_RT, half), row_map),
            scratch_shapes=[pltpu.VMEM((D_MODEL, EXPERT_HIDDEN), BF16),
                            pltpu.VMEM((D_MODEL, EXPERT_HIDDEN), BF16),
                            pltpu.VMEM((EXPERT_HIDDEN, D_MODEL), BF16),
                            pltpu.VMEM((2, D_MODEL, EXPERT_HIDDEN), F32),
                            pltpu.VMEM((2, D_MODEL, EXPERT_HIDDEN), F32),
                            pltpu.VMEM((2, EXPERT_HIDDEN, D_MODEL), F32),
                            pltpu.SemaphoreType.DMA((2, 3))]),
        compiler_params=pltpu.CompilerParams(
            dimension_semantics=("arbitrary",), vmem_limit_bytes=VMEM_LIMIT),
        name="moe_experts",
    )(*plan, xs, wg, wu, wd)


def _moe_plan(route_t, counts, n_rows):
    cnt = counts[0, :N_EXPERTS].astype(jnp.int32)
    tiles = (cnt + MOE_RT - 1) // MOE_RT
    tile_end = jnp.cumsum(tiles)
    n_tiles = tile_end[-1:]
    row_start = (tile_end - tiles) * MOE_RT
    ids = route_t[0:2].astype(jnp.int32)
    ranks = route_t[4:6].astype(jnp.int32)
    experts = jnp.arange(N_EXPERTS, dtype=jnp.int32)[:, None, None]
    pos = ranks + jnp.sum(jnp.where(ids[None] == experts, row_start[:, None, None], 0), axis=0)
    tile_id = jnp.minimum(jnp.arange(n_rows // MOE_RT, dtype=jnp.int32), n_tiles - 1)
    tile_expert = jnp.sum((tile_id[:, None] >= tile_end[None, :]).astype(jnp.int32), axis=1)
    e_ids = jnp.arange(N_EXPERTS, dtype=jnp.int32)
    later = (e_ids[None, :] > e_ids[:, None]) & (tiles[None, :] > 0)
    nxt = jnp.min(jnp.where(later, e_ids[None, :], N_EXPERTS), axis=1)
    nxt = jnp.where(nxt == N_EXPERTS, -1, nxt)
    seg = jnp.cumsum((tiles > 0).astype(jnp.int32)) - 1
    pick = tile_expert[:, None] == e_ids[None, :]
    tile_next = jnp.sum(jnp.where(pick, nxt[None, :], 0), axis=1)
    tile_seg = jnp.sum(jnp.where(pick, seg[None, :], 0), axis=1)
    return pos, (tile_expert, n_tiles, tile_next, tile_seg)


PLE_TM = 1024
PLE_SUB = 256


def _ple_body(x_ref, ya_ref, yb_ref, route_ref, p_ref, g3_ref, wpg_ref, bpg_ref, wple_ref, gf_ref,
              o_ref):
    n_sub = x_ref.shape[0] // PLE_SUB
    g3, gf, bpg = g3_ref[...], gf_ref[...], bpg_ref[...]

    def head(s):
        rows = pl.ds(s * PLE_SUB, PLE_SUB)
        ya = jnp.concatenate(_unpack_bf16_pairs(ya_ref[rows, :]), axis=1)
        yb = jnp.concatenate(_unpack_bf16_pairs(yb_ref[rows, :]), axis=1)
        route = route_ref[rows, :]
        x2 = x_ref[rows, :] + route[:, 2:3] * ya + route[:, 3:4] * yb
        return x2, _rms(x2, g3).astype(BF16)

    def dots(s, h3):
        rows = pl.ds(s * PLE_SUB, PLE_SUB)
        zg = jnp.dot(h3, wpg_ref[...], preferred_element_type=F32)
        pe = jnp.dot(p_ref[rows, :].astype(BF16), wple_ref[...], preferred_element_type=F32)
        return zg, pe

    def tail(s, x2, zg, pe):
        x3 = x2 + jax.nn.sigmoid(zg + bpg) * pe
        o_ref[pl.ds(s * PLE_SUB, PLE_SUB), :] = _rms(x3, gf)

    x2s, mm = {}, {}
    x2s[0], h3 = head(0)
    for s in range(n_sub):
        mm[s] = dots(s, h3)
        if s + 1 < n_sub:
            x2s[s + 1], h3 = head(s + 1)
        if s >= 1:
            tail(s - 1, x2s.pop(s - 1), *mm.pop(s - 1))
    tail(n_sub - 1, x2s.pop(n_sub - 1), *mm.pop(n_sub - 1))


def _ple_call(x1, y_picks, route, p, g3, wpg, bpg, wple, gf):
    T = x1.shape[0]
    nt = T // PLE_TM
    consts = (g3, wpg, bpg, wple, gf)
    tok = lambda i: (i, 0)
    return pl.pallas_call(
        _ple_body,
        out_shape=jax.ShapeDtypeStruct((T, D_MODEL), F32),
        grid=(nt,),
        in_specs=[pl.BlockSpec((PLE_TM, D_MODEL), tok),
                  pl.BlockSpec((PLE_TM, D_MODEL // 2), tok),
                  pl.BlockSpec((PLE_TM, D_MODEL // 2), lambda i: (i + nt, 0)),
                  pl.BlockSpec((PLE_TM, LANES), tok),
                  pl.BlockSpec((PLE_TM, p.shape[1]), tok)]
                 + [_const_spec(c.shape) for c in consts],
        out_specs=pl.BlockSpec((PLE_TM, D_MODEL), tok),
        compiler_params=pltpu.CompilerParams(
            dimension_semantics=("arbitrary",), vmem_limit_bytes=VMEM_LIMIT),
        name="ple_final",
    )(x1, y_picks, y_picks, route, p, *consts)


def _layer(x, p, norm_mix, w_in, b_in, lam_re, lam_im, log_dt, b_re, b_im, c_re, c_im, d_skip,
           w_glu_a, w_glu_b, conv_w, conv_b, w_conv_out, w_o, norm_ffn, w_rg, b_rg, w_re, b_re_r,
           w_eg, w_eu, w_ed, norm_ple, w_ple, w_pg, b_pg, norm_out):
    nb, seq, d = x.shape
    T = nb * seq
    row = lambda v: v.reshape(1, -1).astype(F32)
    assert w_in.shape[1] == SSM_WIDTH + 3 * CONV_WIDTH + 2 * D_MODEL and SSM_WIDTH == CONV_WIDTH
    w_in16 = w_in.astype(BF16)
    b_in_row = row(b_in)

    kmat, mmat, nmat, aq_re, aq_im = _s5_operators(lam_re, lam_im, log_dt, b_re, b_im, c_re, c_im)
    s5_consts = (row(norm_mix), w_in16, b_in_row, mmat, kmat, nmat, aq_re, aq_im, row(d_skip))

    lane_pad = LANES - N_EXPERTS - N_EXPERT_GROUPS
    w_r = jnp.pad(jnp.concatenate([w_re, w_rg], axis=1).astype(F32), ((0, 0), (0, lane_pad)))
    b_r = jnp.pad(jnp.concatenate([b_re_r, b_rg]).astype(F32), (0, lane_pad)).reshape(1, LANES)
    w_r_hi = w_r.astype(BF16)
    w_r_lo = (w_r - w_r_hi.astype(F32)).astype(BF16)

    mix_consts = (w_in16, b_in_row, row(norm_mix),
                  conv_w.astype(F32), row(conv_b), w_conv_out.astype(BF16),
                  jnp.concatenate([w_glu_a, w_glu_b], axis=1).astype(BF16), w_o.astype(BF16),
                  row(norm_ffn), jnp.concatenate([w_r_hi, w_r_lo], axis=1), w_r_hi, b_r)
    ple_consts = (row(norm_ple), w_pg.astype(BF16), row(b_pg), w_ple.astype(BF16), row(norm_out))
    p2d = p.reshape(T, -1)
    n_rows = 2 * T + N_EXPERTS * MOE_RT

    ys = _s5_call(x, *s5_consts)
    x1, h2p, route, route_t, counts = _mix_call(x, ys, *mix_consts)
    pos, plan = _moe_plan(route_t, counts, n_rows)
    xs = _sc_dispatch(h2p, pos[0], pos[1], n_rows)
    ysort = _moe_call(plan, xs, w_eg, w_eu, w_ed)
    y_picks = _sc_gather(ysort, pos.reshape(-1))
    out = _ple_call(x1, y_picks, route, p2d, *ple_consts)
    return out.reshape(nb, seq, d)


def kernel(x, p, norm_mix, w_in, b_in, ssm_lam_re, ssm_lam_im, ssm_log_dt, ssm_b_re, ssm_b_im, ssm_c_re, ssm_c_im, ssm_d, w_glu_a, w_glu_b, conv_w, conv_b, w_conv_out, w_o, norm_ffn, w_router_group, b_router_group, w_router_expert, b_router_expert, w_exp_gate, w_exp_up, w_exp_down, norm_ple, w_ple, w_ple_gate, b_ple_gate, norm_final):
    assert p.shape[0] == 1, "the final RMSNorm is fused into the (single) layer's last kernel"
    i = 0
    return _layer(x, p[i], norm_mix[i], w_in[i], b_in[i], ssm_lam_re[i], ssm_lam_im[i],
                  ssm_log_dt[i], ssm_b_re[i], ssm_b_im[i], ssm_c_re[i], ssm_c_im[i], ssm_d[i],
                  w_glu_a[i], w_glu_b[i], conv_w[i], conv_b[i], w_conv_out[i], w_o[i],
                  norm_ffn[i], w_router_group[i], b_router_group[i], w_router_expert[i],
                  b_router_expert[i], w_exp_gate[i], w_exp_up[i], w_exp_down[i], norm_ple[i],
                  w_ple[i], w_ple_gate[i], b_ple_gate[i], norm_final)
```

```python
import functools

import jax
import jax.numpy as jnp
from jax import lax
from jax.experimental import pallas as pl
from jax.experimental.pallas import tpu as pltpu
from jax.experimental.pallas import tpu_sc as plsc

F32 = jnp.float32
BF16 = jnp.bfloat16

D_MODEL = 1024
SSM_WIDTH = 512
SSM_GROUP = 16
SSM_GROUPS = 32
SSM_STATE = 64
CONV_WIDTH = 512
N_EXPERT_GROUPS = 4
EXPERTS_PER_GROUP = 8
N_EXPERTS = 32
EXPERT_HIDDEN = 256
NORM_EPS = 1e-6

LANES = 128
Q = 8
GROUPS_PER_LANE_TILE = LANES // SSM_GROUP
N_LANE_TILES = SSM_WIDTH // LANES
STATE_LANES = GROUPS_PER_LANE_TILE * SSM_STATE
S5_ROWS = 1024
VMEM_LIMIT = 56 * 1024 * 1024


def _rms(x, g):
    return x * lax.rsqrt(jnp.mean(x * x, axis=-1, keepdims=True) + NORM_EPS) * g


def _pack_bf16_pairs(a):
    w = a.shape[1] // 2
    lo = lax.shift_right_logical(lax.bitcast_convert_type(a[:, :w], jnp.int32), 16)
    hi = lax.bitcast_convert_type(a[:, w:], jnp.int32) & jnp.int32(-65536)
    return lo | hi


def _unpack_bf16_pairs(word):
    lo = lax.bitcast_convert_type(lax.shift_left(word, 16), F32)
    hi = lax.bitcast_convert_type(word & jnp.int32(-65536), F32)
    return lo, hi


def _const_spec(shape):
    n = len(shape)
    return pl.BlockSpec(shape, lambda *_: (0,) * n, pipeline_mode=pl.Buffered(1))


def _s5_operators(lam_re, lam_im, log_dt, b_re, b_im, c_re, c_im):
    G, P, H = SSM_GROUPS, SSM_STATE, SSM_GROUP
    J = N_LANE_TILES
    lr = lam_re.astype(F32)
    li = lam_im.astype(F32)
    dt = jnp.exp(log_dt.astype(F32))[:, None]

    def apow(n):
        n = n.astype(F32)[:, None, None]
        mag = jnp.exp(lr * dt * n)
        ang = li * dt * n
        return mag * jnp.cos(ang), mag * jnp.sin(ang)

    a1_re, a1_im = apow(jnp.ones((1,), F32))
    nr = a1_re[0] - 1.0
    ni = a1_im[0]
    den = lr * lr + li * li
    f_re = (nr * lr + ni * li) / den
    f_im = (ni * lr - nr * li) / den
    br = b_re.astype(F32)
    bi = b_im.astype(F32)
    bbar_re = f_re[:, :, None] * br - f_im[:, :, None] * bi
    bbar_im = f_re[:, :, None] * bi + f_im[:, :, None] * br
    to_rows = lambda v, perm: jnp.transpose(v, perm).reshape(H, G * P)
    bt_re = to_rows(bbar_re, (2, 0, 1))
    bt_im = to_rows(bbar_im, (2, 0, 1))
    ct_re = to_rows(c_re.astype(F32), (1, 0, 2))
    ct_im = to_rows(c_im.astype(F32), (1, 0, 2))
    ap_re, ap_im = apow(jnp.arange(Q + 1))
    ap_re = ap_re.reshape(Q + 1, G * P)
    ap_im = ap_im.reshape(Q + 1, G * P)

    blk = lambda r: pl.BlockSpec((r, STATE_LANES), lambda j: (0, j))
    mat = lambda r, c: pl.BlockSpec((1, r, c), lambda j: (j, 0, 0))
    shape = lambda r, c: jax.ShapeDtypeStruct((J, r, c), BF16)
    qx, st = Q * LANES, 2 * STATE_LANES
    kmat, mmat, nmat = pl.pallas_call(
        _s5_ops_body,
        out_shape=(shape(qx, qx), shape(qx, st), shape(st, qx)),
        grid=(J,),
        in_specs=[blk(Q + 1), blk(Q + 1), blk(H), blk(H), blk(H), blk(H)],
        out_specs=(mat(qx, qx), mat(qx, st), mat(st, qx)),
        compiler_params=pltpu.CompilerParams(
            dimension_semantics=("arbitrary",), vmem_limit_bytes=VMEM_LIMIT),
        name="s5_operators",
    )(ap_re, ap_im, bt_re, bt_im, ct_re, ct_im)
    return kmat, mmat, nmat, ap_re[Q:Q + 1], ap_im[Q:Q + 1]


def _s5_ops_body(apr_ref, api_ref, btr_ref, bti_ref, ctr_ref, cti_ref, k_ref, m_ref, n_ref):
    ri = lax.broadcasted_iota(jnp.int32, (LANES, STATE_LANES), 0)
    li = lax.broadcasted_iota(jnp.int32, (LANES, STATE_LANES), 1)
    same_group = (ri // SSM_GROUP) == (li // SSM_STATE)

    def expand(ref):
        tiled = jnp.concatenate([ref[...]] * GROUPS_PER_LANE_TILE, axis=0)
        return jnp.where(same_group, tiled, 0.0)

    b_re, b_im, c_re, c_im = expand(btr_ref), expand(bti_ref), expand(ctr_ref), expand(cti_ref)

    def cmul(n, x_re, x_im):
        a_re = apr_ref[n:n + 1, :]
        a_im = api_ref[n:n + 1, :]
        return a_re * x_re - a_im * x_im, a_re * x_im + a_im * x_re

    m_blocks = []
    for k in range(Q):
        g_re, g_im = cmul(Q - 1 - k, b_re, b_im)
        m_blocks.append(jnp.concatenate([g_re, g_im], axis=1))
    m = jnp.concatenate(m_blocks, axis=0)

    nt_blocks = []
    for t in range(Q):
        g_re, g_im = cmul(t + 1, c_re, c_im)
        nt_blocks.append(jnp.concatenate([g_re, -g_im], axis=1))
    nt = jnp.concatenate(nt_blocks, axis=0)

    n0t = jnp.concatenate([c_re, -c_im], axis=1)
    p = lax.dot_general(m, n0t, (((1,), (1,)), ((), ())),
                        precision=lax.Precision.HIGHEST, preferred_element_type=F32)
    zeros = jnp.zeros((LANES, LANES), F32)
    cols = []
    for t in range(Q):
        cols.append(jnp.concatenate(
            [p[(Q - 1 - (t - k)) * LANES:(Q - (t - k)) * LANES, :] if t >= k else zeros
             for k in range(Q)], axis=0))
    k_ref[0] = jnp.concatenate(cols, axis=1).astype(BF16)
    m_ref[0] = m.astype(BF16)
    n_ref[0] = nt.T.astype(BF16)


def _s5_body(x_ref, g_ref, wu_ref, bu_ref, m_ref, k_ref, n_ref, aqr_ref, aqi_ref, d_ref,
             o_ref, u_scr, y_scr, z_scr, ss_scr, carry_scr, wu_scr):
    nb, tt = x_ref.shape[0], x_ref.shape[1]
    rows = nb * tt
    nchunk = rows // Q

    @pl.when(pl.program_id(0) == 0)
    def _():
        carry_scr[...] = jnp.zeros_like(carry_scr)
        wu_scr[...] = wu_ref[...].astype(BF16)

    hb = nb // 2
    for r in range(2):
        x = x_ref[r * hb:(r + 1) * hb].reshape(rows // 2, D_MODEL)
        h = _rms(x, g_ref[...]).astype(BF16)
        u = jnp.dot(h, wu_scr[...], preferred_element_type=F32) + bu_ref[...]
        for j in range(N_LANE_TILES):
            u_scr[j, pl.ds(r * (rows // 2), rows // 2), :] = u[:, j * LANES:(j + 1) * LANES]

    n_st = STATE_LANES // LANES
    cpt = tt // Q

    lane_tiles = range(N_LANE_TILES)


    xs = [jnp.concatenate([u_scr[j, pl.ds(k, nchunk, stride=Q), :] for k in range(Q)],
                          axis=1).astype(BF16) for j in lane_tiles]
    for j in lane_tiles:
        z = jnp.dot(xs[j], m_ref[j], preferred_element_type=F32)
        for i in range(2 * n_st):
            z_scr[j, i] = z[:, i * LANES:(i + 1) * LANES]
    y_intra = [jnp.dot(xs[j], k_ref[j], preferred_element_type=F32) for j in lane_tiles]

    aq = [(jnp.broadcast_to(aqr_ref[:, pl.ds(j * STATE_LANES, STATE_LANES)], (nb, STATE_LANES)),
           jnp.broadcast_to(aqi_ref[:, pl.ds(j * STATE_LANES, STATE_LANES)], (nb, STATE_LANES)))
          for j in lane_tiles]
    st = [(carry_scr[j, :, pl.ds(0, STATE_LANES)], carry_scr[j, :, pl.ds(STATE_LANES, STATE_LANES)])
          for j in lane_tiles]
    for c in range(cpt):
        seq_rows = pl.ds(c, nb, stride=cpt)
        for j in lane_tiles:
            s_re, s_im = st[j]
            aqr, aqi = aq[j]
            for i in range(n_st):
                ss_scr[j, i, seq_rows, :] = s_re[:, i * LANES:(i + 1) * LANES]
                ss_scr[j, n_st + i, seq_rows, :] = s_im[:, i * LANES:(i + 1) * LANES]
            z_re = jnp.concatenate([z_scr[j, i, seq_rows, :] for i in range(n_st)], axis=1)
            z_im = jnp.concatenate([z_scr[j, n_st + i, seq_rows, :] for i in range(n_st)], axis=1)
            st[j] = (aqr * s_re - aqi * s_im + z_re, aqr * s_im + aqi * s_re + z_im)
    for j in lane_tiles:
        carry_scr[j, :, pl.ds(0, STATE_LANES)] = st[j][0]
        carry_scr[j, :, pl.ds(STATE_LANES, STATE_LANES)] = st[j][1]

    def state_to_output(j):
        ss = jnp.concatenate([ss_scr[j, i] for i in range(2 * n_st)], axis=1).astype(BF16)
        return y_intra[j] + jnp.dot(ss, n_ref[j], preferred_element_type=F32)

    def finish(j, yj):
        for k in range(Q):
            y_scr[j, pl.ds(k, nchunk, stride=Q), :] = yj[:, k * LANES:(k + 1) * LANES]
        lanes = pl.ds(j * LANES, LANES)
        y = y_scr[j] + d_ref[:, lanes] * u_scr[j]
        o_ref[:, :, lanes] = jax.nn.gelu(y).astype(BF16).reshape(nb, tt, LANES)

    yj = state_to_output(0)
    for j in lane_tiles:
        y_next = state_to_output(j + 1) if j + 1 < N_LANE_TILES else None
        finish(j, yj)
        yj = y_next


def _col_block(rows, width, block):
    return pl.BlockSpec((rows, width), lambda *_: (0, block), pipeline_mode=pl.Buffered(1))


def _s5_call(x, g, w_in, b_in, mmat, kmat, nmat, aq_re, aq_im, d_skip):
    nb, seq = x.shape[0], x.shape[1]
    tt = S5_ROWS // nb
    rows = S5_ROWS
    nchunk = rows // Q
    return pl.pallas_call(
        _s5_body,
        out_shape=jax.ShapeDtypeStruct((nb, seq, SSM_WIDTH), BF16),
        grid=(seq // tt,),
        in_specs=[
            pl.BlockSpec((nb, tt, D_MODEL), lambda i: (0, i, 0)),
            _const_spec(g.shape), _col_block(D_MODEL, SSM_WIDTH, 0), _col_block(1, SSM_WIDTH, 0),
            _const_spec(mmat.shape), _const_spec(kmat.shape), _const_spec(nmat.shape),
            _const_spec(aq_re.shape), _const_spec(aq_im.shape), _const_spec(d_skip.shape),
        ],
        out_specs=pl.BlockSpec((nb, tt, SSM_WIDTH), lambda i: (0, i, 0)),
        scratch_shapes=[
            pltpu.VMEM((N_LANE_TILES, rows, LANES), F32),
            pltpu.VMEM((N_LANE_TILES, rows, LANES), F32),
            pltpu.VMEM((N_LANE_TILES, 2 * STATE_LANES // LANES, nchunk, LANES), F32),
            pltpu.VMEM((N_LANE_TILES, 2 * STATE_LANES // LANES, nchunk, LANES), F32),
            pltpu.VMEM((N_LANE_TILES, nb, 2 * STATE_LANES), F32),
            pltpu.VMEM((D_MODEL, SSM_WIDTH), BF16),
        ],
        compiler_params=pltpu.CompilerParams(
            dimension_semantics=("arbitrary",), vmem_limit_bytes=VMEM_LIMIT),
        name="s5_mixer",
    )(x, g, w_in, b_in, mmat, kmat, nmat, aq_re, aq_im, d_skip)


MIX_TM = 512


def _mix_body(x_ref, ys_ref, wcb_ref, wcc_ref, wcv_ref, wg_ref, bcb_ref, bcc_ref, bcv_ref, bg_ref,
              g_ref, cw_ref, cb_ref, wco_ref, wab_ref, wo_ref, g2_ref, wr_ref, wrh_ref, br_ref,
              x1_ref, h2p_ref, route_ref, route_t_ref, cnt_ref, carry_scr, cnt_scr, x1_scr,
              wcb_scr, wcc_scr, wcv_scr, wg_scr, *, tiles_per_seq, n_tiles):
    i = pl.program_id(0)
    tm = x_ref.shape[1]
    wc_refs = (wcb_scr, wcc_scr, wcv_scr)
    bc_refs = (bcb_ref, bcc_ref, bcv_ref)

    @pl.when(i == 0)
    def _():
        cnt_scr[...] = jnp.zeros_like(cnt_scr)
        x1_scr[...] = jnp.zeros_like(x1_scr)
        carry_scr[...] = jnp.zeros_like(carry_scr)
        for src, dst in ((wcb_ref, wcb_scr), (wcc_ref, wcc_scr), (wcv_ref, wcv_scr),
                         (wg_ref, wg_scr)):
            dst[...] = src[...].astype(BF16)

    carry_scr[0] = jnp.where(i == n_tiles, carry_scr[0],
                             jnp.where(i % tiles_per_seq == 0, 0.0, carry_scr[1]))

    yab = jnp.dot(ys_ref[0], wab_ref[...], preferred_element_type=F32)

    h2 = _rms(x1_scr[...], g2_ref[...])
    h2_hi = h2.astype(BF16)
    h2p_ref[...] = _pack_bf16_pairs(h2_hi.astype(F32))
    h2_lo = (h2 - h2_hi.astype(F32)).astype(BF16)
    lg2 = jnp.dot(h2_hi, wr_ref[...], preferred_element_type=F32)
    logits = (lg2[:, 0:LANES] + lg2[:, LANES:2 * LANES]
              + jnp.dot(h2_lo, wrh_ref[...], preferred_element_type=F32) + br_ref[...])

    x = x_ref[0]
    h = _rms(x, g_ref[...]).astype(BF16)
    c_b, c_c, c_v = [jnp.dot(h, w[...], preferred_element_type=F32) + b[...]
                     for w, b in zip(wc_refs, bc_refs)]
    zg = jnp.dot(h, wg_scr[...], preferred_element_type=F32) + bg_ref[...]

    cnt = cnt_scr[...]
    route, new_cnt = _route_tile(logits, cnt)
    cnt = jnp.where(i > 0, new_cnt, cnt)
    cnt_scr[...] = cnt
    cnt_ref[...] = cnt
    route_ref[...] = route
    route_t_ref[...] = route.T[0:8, :]

    y_a = yab[:, 0:D_MODEL] * jax.nn.sigmoid(yab[:, D_MODEL:2 * D_MODEL])

    cv = c_c * c_v
    row = lax.broadcasted_iota(jnp.int32, (tm, CONV_WIDTH), 0)
    last1 = carry_scr[0, 7:8, :]
    last2 = carry_scr[0, 6:7, :]
    p1 = jnp.where(row == 0, last1, pltpu.roll(cv, 1, axis=0))
    p2 = jnp.where(row == 0, last2, jnp.where(row == 1, last1, pltpu.roll(cv, 2, axis=0)))
    carry_scr[1] = cv[tm - 8:tm, :]
    conv = cw_ref[0:1, :] * p2 + cw_ref[1:2, :] * p1 + cw_ref[2:3, :] * cv + cb_ref[...]
    y_b = jnp.dot((c_b * conv).astype(BF16), wco_ref[...], preferred_element_type=F32)

    mix = (jax.nn.sigmoid(zg[:, 0:D_MODEL]) * y_a
           + jax.nn.sigmoid(zg[:, D_MODEL:2 * D_MODEL]) * y_b)
    x1 = x + jnp.dot(mix.astype(BF16), wo_ref[...], preferred_element_type=F32)
    x1_ref[...] = x1
    x1_scr[...] = x1


def _route_tile(logits, cnt):
    tm = logits.shape[0]
    lane = lax.broadcasted_iota(jnp.int32, (tm, LANES), 1)
    neg = jnp.float32(-jnp.inf)
    big = jnp.int32(1 << 20)
    is_g = (lane >= N_EXPERTS) & (lane < N_EXPERTS + N_EXPERT_GROUPS)
    gl = jnp.where(is_g, logits, neg)
    gmax = jnp.max(gl, axis=1, keepdims=True)
    g_w = 1.0 / jnp.sum(jnp.exp(gl - gmax), axis=1, keepdims=True)
    g_idx = jnp.min(jnp.where(gl == gmax, lane - N_EXPERTS, big), axis=1, keepdims=True)
    lo = g_idx * EXPERTS_PER_GROUP
    el = jnp.where((lane >= lo) & (lane < lo + EXPERTS_PER_GROUP), logits, neg)
    m1 = jnp.max(el, axis=1, keepdims=True)
    i1 = jnp.min(jnp.where(el == m1, lane, big), axis=1, keepdims=True)
    el2 = jnp.where(lane == i1, neg, el)
    m2 = jnp.max(el2, axis=1, keepdims=True)
    i2 = jnp.min(jnp.where(el2 == m2, lane, big), axis=1, keepdims=True)
    r = jnp.exp(m2 - m1)
    w1 = g_w / (1.0 + r)
    w2 = g_w * r / (1.0 + r)

    picks = ((lane == i1) | (lane == i2)).astype(BF16)
    r_i = lax.broadcasted_iota(jnp.int32, (tm, tm), 0)
    c_i = lax.broadcasted_iota(jnp.int32, (tm, tm), 1)
    before = (c_i < r_i).astype(BF16)
    excl = jnp.dot(before, picks, preferred_element_type=F32) + cnt
    rank1 = jnp.sum(jnp.where(lane == i1, excl, 0.0), axis=1, keepdims=True)
    rank2 = jnp.sum(jnp.where(lane == i2, excl, 0.0), axis=1, keepdims=True)
    new_cnt = cnt + jnp.sum(picks.astype(F32), axis=0, keepdims=True)
    route = jnp.where(lane == 0, i1.astype(F32), 0.0)
    route = jnp.where(lane == 1, i2.astype(F32), route)
    route = jnp.where(lane == 2, w1, route)
    route = jnp.where(lane == 3, w2, route)
    route = jnp.where(lane == 4, rank1, route)
    route = jnp.where(lane == 5, rank2, route)
    return route, new_cnt


def _mix_call(x, ys, w_in, b_in, g, cw, cb, wco, wab, wo, g2, wr, wrh, br):
    nb, seq, _ = ys.shape
    T = nb * seq
    nl = seq // MIX_TM
    n = nb * nl
    consts = (g, cw, cb, wco, wab, wo, g2, wr, wrh, br)
    gates = 2 * D_MODEL
    in_proj_specs = ([_col_block(D_MODEL, CONV_WIDTH, b) for b in (1, 2, 3)]
                     + [_col_block(D_MODEL, gates, 1)]
                     + [_col_block(1, CONV_WIDTH, b) for b in (1, 2, 3)]
                     + [_col_block(1, gates, 1)])
    cur = lambda i: jnp.minimum(i, n - 1)
    prev = lambda i: jnp.maximum(i - 1, 0)
    return pl.pallas_call(
        functools.partial(_mix_body, tiles_per_seq=nl, n_tiles=n),
        out_shape=(jax.ShapeDtypeStruct((T, D_MODEL), F32),
                   jax.ShapeDtypeStruct((T, D_MODEL // 2), jnp.int32),
                   jax.ShapeDtypeStruct((T, LANES), F32),
                   jax.ShapeDtypeStruct((8, T), F32),
                   jax.ShapeDtypeStruct((1, LANES), F32)),
        grid=(n + 1,),
        in_specs=[pl.BlockSpec((1, MIX_TM, D_MODEL),
                               lambda i: (cur(i) // nl, cur(i) % nl, 0)),
                  pl.BlockSpec((1, MIX_TM, SSM_WIDTH), lambda i: (cur(i) // nl, cur(i) % nl, 0))]
                 + in_proj_specs + [_const_spec(c.shape) for c in consts],
        out_specs=(pl.BlockSpec((MIX_TM, D_MODEL), lambda i: (cur(i), 0)),
                   pl.BlockSpec((MIX_TM, D_MODEL // 2), lambda i: (prev(i), 0)),
                   pl.BlockSpec((MIX_TM, LANES), lambda i: (prev(i), 0)),
                   pl.BlockSpec((8, MIX_TM), lambda i: (0, prev(i))),
                   pl.BlockSpec((1, LANES), lambda i: (0, 0))),
        scratch_shapes=[pltpu.VMEM((2, 8, CONV_WIDTH), F32), pltpu.VMEM((1, LANES), F32),
                        pltpu.VMEM((MIX_TM, D_MODEL), F32)]
                       + [pltpu.VMEM((D_MODEL, CONV_WIDTH), BF16)] * 3
                       + [pltpu.VMEM((D_MODEL, gates), BF16)],
        compiler_params=pltpu.CompilerParams(
            dimension_semantics=("arbitrary",), vmem_limit_bytes=VMEM_LIMIT),
        name="conv_glu_router",
    )(x, ys, *([w_in] * 4), *([b_in] * 4), *consts)


SC_CORES = 2
SC_SUBCORES = 16
SC_WORKERS = SC_CORES * SC_SUBCORES
SC_ROWS = 64
SC_NBUF = 2


def _sc_mesh():
    return plsc.VectorSubcoreMesh(core_axis_name="c", subcore_axis_name="s")


def _sc_worker_id():
    return lax.axis_index("s") * SC_CORES + lax.axis_index("c")


def _sc_ring(nch, get, puts):
    for b in range(SC_NBUF - 1):
        get(b, b).start()

    @pl.loop(0, nch, step=SC_NBUF)
    def _(j0):
        for b in range(SC_NBUF):
            j = j0 + b
            refill = (b - 1) % SC_NBUF
            get(j, b).wait()

            @pl.when(j + SC_NBUF - 1 < nch)
            def _():
                @pl.when(j >= 1)
                def _():
                    for c in puts(j - 1, refill):
                        c.wait()
                get(j + SC_NBUF - 1, refill).start()

            for c in puts(j, b):
                c.start()

    for b in range(SC_NBUF):
        for c in puts(nch - SC_NBUF + b, b):
            c.wait()


def _sc_dispatch(rows, pos_a, pos_b, n_out):
    T, W = rows.shape
    per_w = T // SC_WORKERS
    nch = per_w // SC_ROWS
    assert per_w * SC_WORKERS == T and nch * SC_ROWS == per_w and nch % SC_NBUF == 0
    idx_a = pos_a.reshape(SC_WORKERS, nch, SC_ROWS)
    idx_b = pos_b.reshape(SC_WORKERS, nch, SC_ROWS)

    @functools.partial(
        pl.kernel, mesh=_sc_mesh(),
        out_type=jax.ShapeDtypeStruct((n_out, W), rows.dtype),
        scratch_types=[
            pltpu.VMEM((nch, SC_ROWS), jnp.int32),
            pltpu.VMEM((nch, SC_ROWS), jnp.int32),
            pltpu.VMEM((SC_NBUF, SC_ROWS, W), rows.dtype),
            pltpu.SemaphoreType.DMA((SC_NBUF,)),
            pltpu.SemaphoreType.DMA((SC_NBUF,)),
            pltpu.SemaphoreType.DMA((SC_NBUF,)),
        ],
        name="moe_dispatch",
    )
    def k(rows_hbm, ia_hbm, ib_hbm, out_hbm, ia_v, ib_v, buf, gsem, asem, bsem):
        wid = _sc_worker_id()
        base = wid * per_w
        pltpu.sync_copy(ia_hbm.at[wid], ia_v)
        pltpu.sync_copy(ib_hbm.at[wid], ib_v)

        def get(j, b):
            return pltpu.make_async_copy(
                rows_hbm.at[pl.ds(base + j * SC_ROWS, SC_ROWS)], buf.at[b], gsem.at[b])

        def put_a(j, b):
            return pltpu.make_async_copy(buf.at[b], out_hbm.at[ia_v.at[j]], asem.at[b])

        def put_b(j, b):
            return pltpu.make_async_copy(buf.at[b], out_hbm.at[ib_v.at[j]], bsem.at[b])

        _sc_ring(nch, get, lambda j, b: (put_a(j, b), put_b(j, b)))

    return k(rows, idx_a, idx_b)


def _sc_gather(table, idx):
    _, W = table.shape
    B = idx.shape[0]
    per_w = B // SC_WORKERS
    nch = per_w // SC_ROWS
    assert per_w * SC_WORKERS == B and nch * SC_ROWS == per_w and nch % SC_NBUF == 0
    idx3 = idx.reshape(SC_WORKERS, nch, SC_ROWS)

    @functools.partial(
        pl.kernel, mesh=_sc_mesh(),
        out_type=jax.ShapeDtypeStruct((B, W), table.dtype),
        scratch_types=[
            pltpu.VMEM((nch, SC_ROWS), jnp.int32),
            pltpu.VMEM((SC_NBUF, SC_ROWS, W), table.dtype),
            pltpu.SemaphoreType.DMA((SC_NBUF,)),
            pltpu.SemaphoreType.DMA((SC_NBUF,)),
        ],
        name="moe_combine_gather",
    )
    def k(table_hbm, idx_hbm, out_hbm, idx_v, buf, gsem, osem):
        wid = _sc_worker_id()
        base = wid * per_w
        pltpu.sync_copy(idx_hbm.at[wid], idx_v)

        def get(j, b):
            return pltpu.make_async_copy(table_hbm.at[idx_v.at[j]], buf.at[b], gsem.at[b])

        def put(j, b):
            return pltpu.make_async_copy(
                buf.at[b], out_hbm.at[pl.ds(base + j * SC_ROWS, SC_ROWS)], osem.at[b])

        _sc_ring(nch, get, lambda j, b: (put(j, b),))

    return k(table, idx3)


MOE_RT = 1024
MOE_SUB = 256


def _moe_body(te_ref, nt_ref, nx_ref, seg_ref, x_ref, wg_hbm, wu_hbm, wd_hbm, o_ref,
              wg_scr, wu_scr, wd_scr, wg_buf, wu_buf, wd_buf, sem):
    i = pl.program_id(0)
    expert = te_ref[i]
    slot = seg_ref[i] % 2

    def weight_copies(e, b):
        return (pltpu.make_async_copy(wg_hbm.at[e], wg_buf.at[b], sem.at[b, 0]),
                pltpu.make_async_copy(wu_hbm.at[e], wu_buf.at[b], sem.at[b, 1]),
                pltpu.make_async_copy(wd_hbm.at[e], wd_buf.at[b], sem.at[b, 2]))

    @pl.when(i == 0)
    def _():
        for c in weight_copies(expert, 0):
            c.start()

    @pl.when((i == 0) | (expert != te_ref[jnp.maximum(i - 1, 0)]))
    def _():
        for c in weight_copies(expert, slot):
            c.wait()
        wg_scr[...] = wg_buf[slot].astype(BF16)
        wu_scr[...] = wu_buf[slot].astype(BF16)
        wd_scr[...] = wd_buf[slot].astype(BF16)

        @pl.when(nx_ref[i] >= 0)
        def _():
            for c in weight_copies(nx_ref[i], 1 - slot):
                c.start()

    @pl.when(i < nt_ref[0])
    def _():
        half = D_MODEL // 2
        n_sub = x_ref.shape[0] // MOE_SUB

        def up_proj(s):
            lo, hi = _unpack_bf16_pairs(x_ref[pl.ds(s * MOE_SUB, MOE_SUB), :])
            lo = lo.astype(BF16)
            hi = hi.astype(BF16)
            gate = (jnp.dot(lo, wg_scr[0:half, :], preferred_element_type=F32)
                    + jnp.dot(hi, wg_scr[half:D_MODEL, :], preferred_element_type=F32))
            up = (jnp.dot(lo, wu_scr[0:half, :], preferred_element_type=F32)
                  + jnp.dot(hi, wu_scr[half:D_MODEL, :], preferred_element_type=F32))
            return gate, up

        def down_proj(gate, up):
            hid = (jax.nn.silu(gate) * up).astype(BF16)
            return jnp.dot(hid, wd_scr[...], preferred_element_type=F32)

        def store(s, y):
            o_ref[pl.ds(s * MOE_SUB, MOE_SUB), :] = _pack_bf16_pairs(y.astype(BF16).astype(F32))

        gu = {0: up_proj(0)}
        ys = {}
        for s in range(n_sub):
            if s + 1 < n_sub:
                gu[s + 1] = up_proj(s + 1)
            ys[s] = down_proj(*gu.pop(s))
            if s >= 1:
                store(s - 1, ys.pop(s - 1))
        store(n_sub - 1, ys.pop(n_sub - 1))


def _moe_call(plan, xs, wg, wu, wd):
    R = xs.shape[0]
    half = D_MODEL // 2
    row_map = lambda i, te, nt, nx, seg: (jnp.minimum(i, nt[0] - 1), 0)
    hbm = pl.BlockSpec(memory_space=pl.ANY)
    return pl.pallas_call(
        _moe_body,
        out_shape=jax.ShapeDtypeStruct((R, half), jnp.int32),
        grid_spec=pltpu.PrefetchScalarGridSpec(
            num_scalar_prefetch=4,
            grid=(R // MOE_RT,),
            in_specs=[pl.BlockSpec((MOE_RT, half), row_map), hbm, hbm, hbm],
            out_specs=pl.BlockSpec((MOE_RT, half), row_map),
            scratch_shapes=[pltpu.VMEM((D_MODEL, EXPERT_HIDDEN), BF16),
                            pltpu.VMEM((D_MODEL, EXPERT_HIDDEN), BF16),
                            pltpu.VMEM((EXPERT_HIDDEN, D_MODEL), BF16),
                            pltpu.VMEM((2, D_MODEL, EXPERT_HIDDEN), F32),
                            pltpu.VMEM((2, D_MODEL, EXPERT_HIDDEN), F32),
                            pltpu.VMEM((2, EXPERT_HIDDEN, D_MODEL), F32),
                            pltpu.SemaphoreType.DMA((2, 3))]),
        compiler_params=pltpu.CompilerParams(
            dimension_semantics=("arbitrary",), vmem_limit_bytes=VMEM_LIMIT),
        name="moe_experts",
    )(*plan, xs, wg, wu, wd)


def _moe_plan(route_t, counts, n_rows):
    cnt = counts[0, :N_EXPERTS].astype(jnp.int32)
    tiles = (cnt + MOE_RT - 1) // MOE_RT
    tile_end = jnp.cumsum(tiles)
    n_tiles = tile_end[-1:]
    row_start = (tile_end - tiles) * MOE_RT
    ids = route_t[0:2].astype(jnp.int32)
    ranks = route_t[4:6].astype(jnp.int32)
    experts = jnp.arange(N_EXPERTS, dtype=jnp.int32)[:, None, None]
    pos = ranks + jnp.sum(jnp.where(ids[None] == experts, row_start[:, None, None], 0), axis=0)
    tile_id = jnp.minimum(jnp.arange(n_rows // MOE_RT, dtype=jnp.int32), n_tiles - 1)
    tile_expert = jnp.sum((tile_id[:, None] >= tile_end[None, :]).astype(jnp.int32), axis=1)
    e_ids = jnp.arange(N_EXPERTS, dtype=jnp.int32)
    later = (e_ids[None, :] > e_ids[:, None]) & (tiles[None, :] > 0)
    nxt = jnp.min(jnp.where(later, e_ids[None, :], N_EXPERTS), axis=1)
    nxt = jnp.where(nxt == N_EXPERTS, -1, nxt)
    seg = jnp.cumsum((tiles > 0).astype(jnp.int32)) - 1
    pick = tile_expert[:, None] == e_ids[None, :]
    tile_next = jnp.sum(jnp.where(pick, nxt[None, :], 0), axis=1)
    tile_seg = jnp.sum(jnp.where(pick, seg[None, :], 0), axis=1)
    return pos, (tile_expert, n_tiles, tile_next, tile_seg)


PLE_TM = 1024
PLE_SUB = 256


def _ple_body(x_ref, ya_ref, yb_ref, route_ref, p_ref, g3_ref, wpg_ref, bpg_ref, wple_ref, gf_ref,
              o_ref, wpg_scr, wple_scr):
    @pl.when(pl.program_id(0) == 0)
    def _():
        wpg_scr[...] = wpg_ref[...].astype(BF16)
        wple_scr[...] = wple_ref[...].astype(BF16)

    n_sub = x_ref.shape[0] // PLE_SUB
    g3, gf, bpg = g3_ref[...], gf_ref[...], bpg_ref[...]

    def head(s):
        rows = pl.ds(s * PLE_SUB, PLE_SUB)
        ya = jnp.concatenate(_unpack_bf16_pairs(ya_ref[rows, :]), axis=1)
        yb = jnp.concatenate(_unpack_bf16_pairs(yb_ref[rows, :]), axis=1)
        route = route_ref[rows, :]
        x2 = x_ref[rows, :] + route[:, 2:3] * ya + route[:, 3:4] * yb
        return x2, _rms(x2, g3).astype(BF16)

    def dots(s, h3):
        rows = pl.ds(s * PLE_SUB, PLE_SUB)
        zg = jnp.dot(h3, wpg_scr[...], preferred_element_type=F32)
        pe = jnp.dot(p_ref[rows, :].astype(BF16), wple_scr[...], preferred_element_type=F32)
        return zg, pe

    def tail(s, x2, zg, pe):
        x3 = x2 + jax.nn.sigmoid(zg + bpg) * pe
        o_ref[pl.ds(s * PLE_SUB, PLE_SUB), :] = _rms(x3, gf)

    x2s, mm = {}, {}
    x2s[0], h3 = head(0)
    for s in range(n_sub):
        mm[s] = dots(s, h3)
        if s + 1 < n_sub:
            x2s[s + 1], h3 = head(s + 1)
        if s >= 1:
            tail(s - 1, x2s.pop(s - 1), *mm.pop(s - 1))
    tail(n_sub - 1, x2s.pop(n_sub - 1), *mm.pop(n_sub - 1))


def _ple_call(x1, y_picks, route, p, g3, wpg, bpg, wple, gf):
    T = x1.shape[0]
    nt = T // PLE_TM
    consts = (g3, wpg, bpg, wple, gf)
    tok = lambda i: (i, 0)
    return pl.pallas_call(
        _ple_body,
        out_shape=jax.ShapeDtypeStruct((T, D_MODEL), F32),
        grid=(nt,),
        in_specs=[pl.BlockSpec((PLE_TM, D_MODEL), tok),
                  pl.BlockSpec((PLE_TM, D_MODEL // 2), tok),
                  pl.BlockSpec((PLE_TM, D_MODEL // 2), lambda i: (i + nt, 0)),
                  pl.BlockSpec((PLE_TM, LANES), tok),
                  pl.BlockSpec((PLE_TM, p.shape[1]), tok)]
                 + [_const_spec(c.shape) for c in consts],
        out_specs=pl.BlockSpec((PLE_TM, D_MODEL), tok),
        scratch_shapes=[pltpu.VMEM(wpg.shape, BF16), pltpu.VMEM(wple.shape, BF16)],
        compiler_params=pltpu.CompilerParams(
            dimension_semantics=("arbitrary",), vmem_limit_bytes=VMEM_LIMIT),
        name="ple_final",
    )(x1, y_picks, y_picks, route, p, *consts)


def _layer(x, p, norm_mix, w_in, b_in, lam_re, lam_im, log_dt, b_re, b_im, c_re, c_im, d_skip,
           w_glu_a, w_glu_b, conv_w, conv_b, w_conv_out, w_o, norm_ffn, w_rg, b_rg, w_re, b_re_r,
           w_eg, w_eu, w_ed, norm_ple, w_ple, w_pg, b_pg, norm_out):
    nb, seq, d = x.shape
    T = nb * seq
    row = lambda v: v.reshape(1, -1).astype(F32)
    assert w_in.shape[1] == SSM_WIDTH + 3 * CONV_WIDTH + 2 * D_MODEL and SSM_WIDTH == CONV_WIDTH
    w_in32 = w_in.astype(F32)
    b_in_row = row(b_in)

    kmat, mmat, nmat, aq_re, aq_im = _s5_operators(lam_re, lam_im, log_dt, b_re, b_im, c_re, c_im)
    s5_consts = (row(norm_mix), w_in32, b_in_row, mmat, kmat, nmat, aq_re, aq_im, row(d_skip))

    lane_pad = LANES - N_EXPERTS - N_EXPERT_GROUPS
    w_r = jnp.pad(jnp.concatenate([w_re, w_rg], axis=1).astype(F32), ((0, 0), (0, lane_pad)))
    b_r = jnp.pad(jnp.concatenate([b_re_r, b_rg]).astype(F32), (0, lane_pad)).reshape(1, LANES)
    w_r_hi = w_r.astype(BF16)
    w_r_lo = (w_r - w_r_hi.astype(F32)).astype(BF16)

    mix_consts = (w_in32, b_in_row, row(norm_mix),
                  conv_w.astype(F32), row(conv_b), w_conv_out.astype(BF16),
                  jnp.concatenate([w_glu_a, w_glu_b], axis=1).astype(BF16), w_o.astype(BF16),
                  row(norm_ffn), jnp.concatenate([w_r_hi, w_r_lo], axis=1), w_r_hi, b_r)
    ple_consts = (row(norm_ple), w_pg.astype(F32), row(b_pg), w_ple.astype(F32), row(norm_out))
    p2d = p.reshape(T, -1)
    n_rows = 2 * T + N_EXPERTS * MOE_RT

    ys = _s5_call(x, *s5_consts)
    x1, h2p, route, route_t, counts = _mix_call(x, ys, *mix_consts)
    pos, plan = _moe_plan(route_t, counts, n_rows)
    xs = _sc_dispatch(h2p, pos[0], pos[1], n_rows)
    ysort = _moe_call(plan, xs, w_eg, w_eu, w_ed)
    y_picks = _sc_gather(ysort, pos.reshape(-1))
    out = _ple_call(x1, y_picks, route, p2d, *ple_consts)
    return out.reshape(nb, seq, d)


def kernel(x, p, norm_mix, w_in, b_in, ssm_lam_re, ssm_lam_im, ssm_log_dt, ssm_b_re, ssm_b_im, ssm_c_re, ssm_c_im, ssm_d, w_glu_a, w_glu_b, conv_w, conv_b, w_conv_out, w_o, norm_ffn, w_router_group, b_router_group, w_router_expert, b_router_expert, w_exp_gate, w_exp_up, w_exp_down, norm_ple, w_ple, w_ple_gate, b_ple_gate, norm_final):
    assert p.shape[0] == 1, "the final RMSNorm is fused into the (single) layer's last kernel"
    i = 0
    return _layer(x, p[i], norm_mix[i], w_in[i], b_in[i], ssm_lam_re[i], ssm_lam_im[i],
                  ssm_log_dt[i], ssm_b_re[i], ssm_b_im[i], ssm_c_re[i], ssm_c_im[i], ssm_d[i],
                  w_glu_a[i], w_glu_b[i], conv_w[i], conv_b[i], w_conv_out[i], w_o[i],
                  norm_ffn[i], w_router_group[i], b_router_group[i], w_router_expert[i],
                  b_router_expert[i], w_exp_gate[i], w_exp_up[i], w_exp_down[i], norm_ple[i],
                  w_ple[i], w_ple_gate[i], b_ple_gate[i], norm_final)
```

```python
import functools
import math

import jax
import jax.numpy as jnp
from jax import lax
from jax.experimental import pallas as pl
from jax.experimental.pallas import tpu as pltpu
from jax.experimental.pallas import tpu_sc as plsc

F32 = jnp.float32
BF16 = jnp.bfloat16

D_MODEL = 1024
SSM_WIDTH = 512
SSM_GROUP = 16
SSM_GROUPS = 32
SSM_STATE = 64
CONV_WIDTH = 512
N_EXPERT_GROUPS = 4
EXPERTS_PER_GROUP = 8
N_EXPERTS = 32
EXPERT_HIDDEN = 256
NORM_EPS = 1e-6

LANES = 128
Q = 8
GROUPS_PER_LANE_TILE = LANES // SSM_GROUP
N_LANE_TILES = SSM_WIDTH // LANES
STATE_LANES = GROUPS_PER_LANE_TILE * SSM_STATE
S5_ROWS = 1024
VMEM_LIMIT = 56 * 1024 * 1024


def _rms(x, g):
    return x * lax.rsqrt(jnp.mean(x * x, axis=-1, keepdims=True) + NORM_EPS) * g


def _pack_bf16_pairs(a):
    w = a.shape[1] // 2
    lo = lax.shift_right_logical(lax.bitcast_convert_type(a[:, :w], jnp.int32), 16)
    hi = lax.bitcast_convert_type(a[:, w:], jnp.int32) & jnp.int32(-65536)
    return lo | hi


def _unpack_bf16_pairs(word):
    lo = lax.bitcast_convert_type(lax.shift_left(word, 16), F32)
    hi = lax.bitcast_convert_type(word & jnp.int32(-65536), F32)
    return lo, hi


def _const_spec(shape):
    n = len(shape)
    return pl.BlockSpec(shape, lambda *_: (0,) * n, pipeline_mode=pl.Buffered(1))


def _s5_operators(lam_re, lam_im, log_dt, b_re, b_im, c_re, c_im):
    G, P, H = SSM_GROUPS, SSM_STATE, SSM_GROUP
    J = N_LANE_TILES
    lr = lam_re.astype(F32)
    li = lam_im.astype(F32)
    dt = jnp.exp(log_dt.astype(F32))[:, None]

    def apow(n):
        n = n.astype(F32)[:, None, None]
        mag = jnp.exp(lr * dt * n)
        ang = li * dt * n
        return mag * jnp.cos(ang), mag * jnp.sin(ang)

    a1_re, a1_im = apow(jnp.ones((1,), F32))
    nr = a1_re[0] - 1.0
    ni = a1_im[0]
    den = lr * lr + li * li
    f_re = (nr * lr + ni * li) / den
    f_im = (ni * lr - nr * li) / den
    br = b_re.astype(F32)
    bi = b_im.astype(F32)
    bbar_re = f_re[:, :, None] * br - f_im[:, :, None] * bi
    bbar_im = f_re[:, :, None] * bi + f_im[:, :, None] * br
    to_rows = lambda v, perm: jnp.transpose(v, perm).reshape(H, G * P)
    bt_re = to_rows(bbar_re, (2, 0, 1))
    bt_im = to_rows(bbar_im, (2, 0, 1))
    ct_re = to_rows(c_re.astype(F32), (1, 0, 2))
    ct_im = to_rows(c_im.astype(F32), (1, 0, 2))
    ap_re, ap_im = apow(jnp.arange(Q + 1))
    ap_re = ap_re.reshape(Q + 1, G * P)
    ap_im = ap_im.reshape(Q + 1, G * P)

    blk = lambda r: pl.BlockSpec((r, STATE_LANES), lambda j: (0, j))
    mat = lambda r, c: pl.BlockSpec((1, r, c), lambda j: (j, 0, 0))
    shape = lambda r, c: jax.ShapeDtypeStruct((J, r, c), BF16)
    qx, st = Q * LANES, 2 * STATE_LANES
    kmat, mmat, nmat = pl.pallas_call(
        _s5_ops_body,
        out_shape=(shape(qx, qx), shape(qx, st), shape(st, qx)),
        grid=(J,),
        in_specs=[blk(Q + 1), blk(Q + 1), blk(H), blk(H), blk(H), blk(H)],
        out_specs=(mat(qx, qx), mat(qx, st), mat(st, qx)),
        compiler_params=pltpu.CompilerParams(
            dimension_semantics=("arbitrary",), vmem_limit_bytes=VMEM_LIMIT),
        name="s5_operators",
    )(ap_re, ap_im, bt_re, bt_im, ct_re, ct_im)
    return kmat, mmat, nmat, ap_re[Q:Q + 1], ap_im[Q:Q + 1]


def _s5_ops_body(apr_ref, api_ref, btr_ref, bti_ref, ctr_ref, cti_ref, k_ref, m_ref, n_ref):
    ri = lax.broadcasted_iota(jnp.int32, (LANES, STATE_LANES), 0)
    li = lax.broadcasted_iota(jnp.int32, (LANES, STATE_LANES), 1)
    same_group = (ri // SSM_GROUP) == (li // SSM_STATE)

    def expand(ref):
        tiled = jnp.concatenate([ref[...]] * GROUPS_PER_LANE_TILE, axis=0)
        return jnp.where(same_group, tiled, 0.0)

    b_re, b_im, c_re, c_im = expand(btr_ref), expand(bti_ref), expand(ctr_ref), expand(cti_ref)

    def cmul(n, x_re, x_im):
        a_re = apr_ref[n:n + 1, :]
        a_im = api_ref[n:n + 1, :]
        return a_re * x_re - a_im * x_im, a_re * x_im + a_im * x_re

    m_blocks = []
    for k in range(Q):
        g_re, g_im = cmul(Q - 1 - k, b_re, b_im)
        m_blocks.append(jnp.concatenate([g_re, g_im], axis=1))
    m = jnp.concatenate(m_blocks, axis=0)

    nt_blocks = []
    for t in range(Q):
        g_re, g_im = cmul(t + 1, c_re, c_im)
        nt_blocks.append(jnp.concatenate([g_re, -g_im], axis=1))
    nt = jnp.concatenate(nt_blocks, axis=0)

    n0t = jnp.concatenate([c_re, -c_im], axis=1)
    p = lax.dot_general(m, n0t, (((1,), (1,)), ((), ())),
                        precision=lax.Precision.HIGHEST, preferred_element_type=F32)
    zeros = jnp.zeros((LANES, LANES), F32)
    cols = []
    for t in range(Q):
        cols.append(jnp.concatenate(
            [p[(Q - 1 - (t - k)) * LANES:(Q - (t - k)) * LANES, :] if t >= k else zeros
             for k in range(Q)], axis=0))
    k_ref[0] = jnp.concatenate(cols, axis=1).astype(BF16)
    m_ref[0] = m.astype(BF16)
    n_ref[0] = nt.T.astype(BF16)


def _s5_body(x_ref, g_ref, wu_ref, bu_ref, m_ref, k_ref, n_ref, aqr_ref, aqi_ref, d_ref,
             o_ref, u_scr, y_scr, z_scr, ss_scr, carry_scr):
    nb, tt = x_ref.shape[0], x_ref.shape[1]
    rows = nb * tt
    nchunk = rows // Q

    @pl.when(pl.program_id(0) == 0)
    def _():
        carry_scr[...] = jnp.zeros_like(carry_scr)

    hb = nb // 2
    for r in range(2):
        x = x_ref[r * hb:(r + 1) * hb].reshape(rows // 2, D_MODEL)
        h = _rms(x, g_ref[...]).astype(BF16)
        u = jnp.dot(h, wu_ref[...], preferred_element_type=F32) + bu_ref[...]
        for j in range(N_LANE_TILES):
            u_scr[j, pl.ds(r * (rows // 2), rows // 2), :] = u[:, j * LANES:(j + 1) * LANES]

    n_st = STATE_LANES // LANES
    cpt = tt // Q

    lane_tiles = range(N_LANE_TILES)


    xs = [jnp.concatenate([u_scr[j, pl.ds(k, nchunk, stride=Q), :] for k in range(Q)],
                          axis=1).astype(BF16) for j in lane_tiles]
    for j in lane_tiles:
        z = jnp.dot(xs[j], m_ref[j], preferred_element_type=F32)
        for i in range(2 * n_st):
            z_scr[j, i] = z[:, i * LANES:(i + 1) * LANES]
    y_intra = [jnp.dot(xs[j], k_ref[j], preferred_element_type=F32) for j in lane_tiles]

    aq = [(jnp.broadcast_to(aqr_ref[:, pl.ds(j * STATE_LANES, STATE_LANES)], (nb, STATE_LANES)),
           jnp.broadcast_to(aqi_ref[:, pl.ds(j * STATE_LANES, STATE_LANES)], (nb, STATE_LANES)))
          for j in lane_tiles]
    st = [(carry_scr[j, :, pl.ds(0, STATE_LANES)], carry_scr[j, :, pl.ds(STATE_LANES, STATE_LANES)])
          for j in lane_tiles]
    for c in range(cpt):
        seq_rows = pl.ds(c, nb, stride=cpt)
        for j in lane_tiles:
            s_re, s_im = st[j]
            aqr, aqi = aq[j]
            for i in range(n_st):
                ss_scr[j, i, seq_rows, :] = s_re[:, i * LANES:(i + 1) * LANES]
                ss_scr[j, n_st + i, seq_rows, :] = s_im[:, i * LANES:(i + 1) * LANES]
            z_re = jnp.concatenate([z_scr[j, i, seq_rows, :] for i in range(n_st)], axis=1)
            z_im = jnp.concatenate([z_scr[j, n_st + i, seq_rows, :] for i in range(n_st)], axis=1)
            st[j] = (aqr * s_re - aqi * s_im + z_re, aqr * s_im + aqi * s_re + z_im)
    for j in lane_tiles:
        carry_scr[j, :, pl.ds(0, STATE_LANES)] = st[j][0]
        carry_scr[j, :, pl.ds(STATE_LANES, STATE_LANES)] = st[j][1]

    def state_to_output(j):
        ss = jnp.concatenate([ss_scr[j, i] for i in range(2 * n_st)], axis=1).astype(BF16)
        return y_intra[j] + jnp.dot(ss, n_ref[j], preferred_element_type=F32)

    def finish(j, yj):
        for k in range(Q):
            y_scr[j, pl.ds(k, nchunk, stride=Q), :] = yj[:, k * LANES:(k + 1) * LANES]
        lanes = pl.ds(j * LANES, LANES)
        y = y_scr[j] + d_ref[:, lanes] * u_scr[j]
        o_ref[:, :, lanes] = jax.nn.gelu(y).astype(BF16).reshape(nb, tt, LANES)

    yj = state_to_output(0)
    for j in lane_tiles:
        y_next = state_to_output(j + 1) if j + 1 < N_LANE_TILES else None
        finish(j, yj)
        yj = y_next


def _col_block(rows, width, block):
    return pl.BlockSpec((rows, width), lambda *_: (0, block), pipeline_mode=pl.Buffered(1))


def _s5_call(x, g, w_in, b_in, mmat, kmat, nmat, aq_re, aq_im, d_skip):
    nb, seq = x.shape[0], x.shape[1]
    tt = S5_ROWS // nb
    rows = S5_ROWS
    nchunk = rows // Q
    return pl.pallas_call(
        _s5_body,
        out_shape=jax.ShapeDtypeStruct((nb, seq, SSM_WIDTH), BF16),
        grid=(seq // tt,),
        in_specs=[
            pl.BlockSpec((nb, tt, D_MODEL), lambda i: (0, i, 0)),
            _const_spec(g.shape), _col_block(D_MODEL, SSM_WIDTH, 0), _col_block(1, SSM_WIDTH, 0),
            _const_spec(mmat.shape), _const_spec(kmat.shape), _const_spec(nmat.shape),
            _const_spec(aq_re.shape), _const_spec(aq_im.shape), _const_spec(d_skip.shape),
        ],
        out_specs=pl.BlockSpec((nb, tt, SSM_WIDTH), lambda i: (0, i, 0)),
        scratch_shapes=[
            pltpu.VMEM((N_LANE_TILES, rows, LANES), F32),
            pltpu.VMEM((N_LANE_TILES, rows, LANES), F32),
            pltpu.VMEM((N_LANE_TILES, 2 * STATE_LANES // LANES, nchunk, LANES), F32),
            pltpu.VMEM((N_LANE_TILES, 2 * STATE_LANES // LANES, nchunk, LANES), F32),
            pltpu.VMEM((N_LANE_TILES, nb, 2 * STATE_LANES), F32),
        ],
        compiler_params=pltpu.CompilerParams(
            dimension_semantics=("arbitrary",), vmem_limit_bytes=VMEM_LIMIT),
        name="s5_mixer",
    )(x, g, w_in, b_in, mmat, kmat, nmat, aq_re, aq_im, d_skip)


MIX_TM = 512


def _mix_body(x_ref, ys_ref, wcb_ref, wcc_ref, wcv_ref, wg_ref, bcb_ref, bcc_ref, bcv_ref, bg_ref,
              g_ref, cw_ref, cb_ref, wco_ref, wab_ref, wo_ref, g2_ref, wr_ref, br_ref,
              x1_ref, h2p_ref, route_ref, route_t_ref, cnt_ref, carry_scr, cnt_scr, x1_scr,
              *, tiles_per_seq, n_tiles):
    i = pl.program_id(0)
    tm = x_ref.shape[1]
    wc_refs = (wcb_ref, wcc_ref, wcv_ref)
    bc_refs = (bcb_ref, bcc_ref, bcv_ref)

    @pl.when(i == 0)
    def _():
        cnt_scr[...] = jnp.zeros_like(cnt_scr)
        x1_scr[...] = jnp.zeros_like(x1_scr)
        carry_scr[...] = jnp.zeros_like(carry_scr)

    carry_scr[0] = jnp.where(i == n_tiles, carry_scr[0],
                             jnp.where(i % tiles_per_seq == 0, 0.0, carry_scr[1]))

    yab = jnp.dot(ys_ref[0], wab_ref[...], preferred_element_type=F32)

    h2_hi = _rms(x1_scr[...], g2_ref[...]).astype(BF16)
    h2p_ref[...] = _pack_bf16_pairs(h2_hi.astype(F32))
    lg2 = jnp.dot(h2_hi, wr_ref[...], preferred_element_type=F32)
    logits = lg2[:, 0:LANES] + lg2[:, LANES:2 * LANES] + br_ref[...]

    x = x_ref[0]
    h = _rms(x, g_ref[...]).astype(BF16)
    c_b, c_c, c_v = [jnp.dot(h, w[...], preferred_element_type=F32) + b[...]
                     for w, b in zip(wc_refs, bc_refs)]
    zg = jnp.dot(h, wg_ref[...], preferred_element_type=F32) + bg_ref[...]

    cnt = cnt_scr[...]
    route, new_cnt = _route_tile(logits, cnt)
    cnt = jnp.where(i > 0, new_cnt, cnt)
    cnt_scr[...] = cnt
    cnt_ref[...] = cnt
    route_ref[...] = route
    route_t_ref[...] = route.T[0:8, :]

    y_a = yab[:, 0:D_MODEL] * jax.nn.sigmoid(yab[:, D_MODEL:2 * D_MODEL])

    cv = c_c * c_v
    row = lax.broadcasted_iota(jnp.int32, (tm, CONV_WIDTH), 0)
    last1 = carry_scr[0, 7:8, :]
    last2 = carry_scr[0, 6:7, :]
    p1 = jnp.where(row == 0, last1, pltpu.roll(cv, 1, axis=0))
    p2 = jnp.where(row == 0, last2, jnp.where(row == 1, last1, pltpu.roll(cv, 2, axis=0)))
    carry_scr[1] = cv[tm - 8:tm, :]
    conv = cw_ref[0:1, :] * p2 + cw_ref[1:2, :] * p1 + cw_ref[2:3, :] * cv + cb_ref[...]
    y_b = jnp.dot((c_b * conv).astype(BF16), wco_ref[...], preferred_element_type=F32)

    mix = (jax.nn.sigmoid(zg[:, 0:D_MODEL]) * y_a
           + jax.nn.sigmoid(zg[:, D_MODEL:2 * D_MODEL]) * y_b)
    x1 = x + jnp.dot(mix.astype(BF16), wo_ref[...], preferred_element_type=F32)
    x1_ref[...] = x1
    x1_scr[...] = x1


def _route_tile(logits, cnt):
    tm = logits.shape[0]
    lane = lax.broadcasted_iota(jnp.int32, (tm, LANES), 1)
    neg = jnp.float32(-jnp.inf)
    big = jnp.int32(1 << 20)
    is_g = (lane >= N_EXPERTS) & (lane < N_EXPERTS + N_EXPERT_GROUPS)
    gl = jnp.where(is_g, logits, neg)
    gmax = jnp.max(gl, axis=1, keepdims=True)
    g_w = 1.0 / jnp.sum(jnp.exp(gl - gmax), axis=1, keepdims=True)
    g_idx = jnp.min(jnp.where(gl == gmax, lane - N_EXPERTS, big), axis=1, keepdims=True)
    lo = g_idx * EXPERTS_PER_GROUP
    el = jnp.where((lane >= lo) & (lane < lo + EXPERTS_PER_GROUP), logits, neg)
    m1 = jnp.max(el, axis=1, keepdims=True)
    i1 = jnp.min(jnp.where(el == m1, lane, big), axis=1, keepdims=True)
    el2 = jnp.where(lane == i1, neg, el)
    m2 = jnp.max(el2, axis=1, keepdims=True)
    i2 = jnp.min(jnp.where(el2 == m2, lane, big), axis=1, keepdims=True)
    r = jnp.exp(m2 - m1)
    w1 = g_w / (1.0 + r)
    w2 = g_w * r / (1.0 + r)

    picks = ((lane == i1) | (lane == i2)).astype(BF16)
    r_i = lax.broadcasted_iota(jnp.int32, (tm, tm), 0)
    c_i = lax.broadcasted_iota(jnp.int32, (tm, tm), 1)
    before = (c_i < r_i).astype(BF16)
    excl = jnp.dot(before, picks, preferred_element_type=F32) + cnt
    rank1 = jnp.sum(jnp.where(lane == i1, excl, 0.0), axis=1, keepdims=True)
    rank2 = jnp.sum(jnp.where(lane == i2, excl, 0.0), axis=1, keepdims=True)
    new_cnt = cnt + jnp.sum(picks.astype(F32), axis=0, keepdims=True)
    route = jnp.where(lane == 0, i1.astype(F32), 0.0)
    route = jnp.where(lane == 1, i2.astype(F32), route)
    route = jnp.where(lane == 2, w1, route)
    route = jnp.where(lane == 3, w2, route)
    route = jnp.where(lane == 4, rank1, route)
    route = jnp.where(lane == 5, rank2, route)
    return route, new_cnt


def _mix_call(x, ys, w_in, b_in, g, cw, cb, wco, wab, wo, g2, wr, br):
    nb, seq, _ = ys.shape
    T = nb * seq
    nl = seq // MIX_TM
    n = nb * nl
    consts = (g, cw, cb, wco, wab, wo, g2, wr, br)
    gates = 2 * D_MODEL
    in_proj_specs = ([_col_block(D_MODEL, CONV_WIDTH, b) for b in (1, 2, 3)]
                     + [_col_block(D_MODEL, gates, 1)]
                     + [_col_block(1, CONV_WIDTH, b) for b in (1, 2, 3)]
                     + [_col_block(1, gates, 1)])
    cur = lambda i: jnp.minimum(i, n - 1)
    prev = lambda i: jnp.maximum(i - 1, 0)
    return pl.pallas_call(
        functools.partial(_mix_body, tiles_per_seq=nl, n_tiles=n),
        out_shape=(jax.ShapeDtypeStruct((T, D_MODEL), F32),
                   jax.ShapeDtypeStruct((T, D_MODEL // 2), jnp.int32),
                   jax.ShapeDtypeStruct((T, LANES), F32),
                   jax.ShapeDtypeStruct((8, T), F32),
                   jax.ShapeDtypeStruct((1, LANES), F32)),
        grid=(n + 1,),
        in_specs=[pl.BlockSpec((1, MIX_TM, D_MODEL),
                               lambda i: (cur(i) // nl, cur(i) % nl, 0)),
                  pl.BlockSpec((1, MIX_TM, SSM_WIDTH), lambda i: (cur(i) // nl, cur(i) % nl, 0))]
                 + in_proj_specs + [_const_spec(c.shape) for c in consts],
        out_specs=(pl.BlockSpec((MIX_TM, D_MODEL), lambda i: (cur(i), 0)),
                   pl.BlockSpec((MIX_TM, D_MODEL // 2), lambda i: (prev(i), 0)),
                   pl.BlockSpec((MIX_TM, LANES), lambda i: (prev(i), 0)),
                   pl.BlockSpec((8, MIX_TM), lambda i: (0, prev(i))),
                   pl.BlockSpec((1, LANES), lambda i: (0, 0))),
        scratch_shapes=[pltpu.VMEM((2, 8, CONV_WIDTH), F32), pltpu.VMEM((1, LANES), F32),
                        pltpu.VMEM((MIX_TM, D_MODEL), F32)],
        compiler_params=pltpu.CompilerParams(
            dimension_semantics=("arbitrary",), vmem_limit_bytes=VMEM_LIMIT),
        name="conv_glu_router",
    )(x, ys, *([w_in] * 4), *([b_in] * 4), *consts)


SC_CORES = 2
SC_SUBCORES = 16
SC_WORKERS = SC_CORES * SC_SUBCORES
SC_ROWS = 64
SC_NBUF = 2


def _sc_mesh():
    return plsc.VectorSubcoreMesh(core_axis_name="c", subcore_axis_name="s")


def _sc_worker_id():
    return lax.axis_index("s") * SC_CORES + lax.axis_index("c")


def _sc_ring(nch, get, puts):
    for b in range(SC_NBUF - 1):
        get(b, b).start()

    @pl.loop(0, nch, step=SC_NBUF)
    def _(j0):
        for b in range(SC_NBUF):
            j = j0 + b
            refill = (b - 1) % SC_NBUF
            get(j, b).wait()

            @pl.when(j + SC_NBUF - 1 < nch)
            def _():
                @pl.when(j >= 1)
                def _():
                    for c in puts(j - 1, refill):
                        c.wait()
                get(j + SC_NBUF - 1, refill).start()

            for c in puts(j, b):
                c.start()

    for b in range(SC_NBUF):
        for c in puts(nch - SC_NBUF + b, b):
            c.wait()


def _sc_dispatch(rows, pos_a, pos_b, n_out):
    T, W = rows.shape
    per_w = T // SC_WORKERS
    nch = per_w // SC_ROWS
    assert per_w * SC_WORKERS == T and nch * SC_ROWS == per_w and nch % SC_NBUF == 0
    idx_a = pos_a.reshape(SC_WORKERS, nch, SC_ROWS)
    idx_b = pos_b.reshape(SC_WORKERS, nch, SC_ROWS)

    @functools.partial(
        pl.kernel, mesh=_sc_mesh(),
        out_type=jax.ShapeDtypeStruct((n_out, W), rows.dtype),
        scratch_types=[
            pltpu.VMEM((nch, SC_ROWS), jnp.int32),
            pltpu.VMEM((nch, SC_ROWS), jnp.int32),
            pltpu.VMEM((SC_NBUF, SC_ROWS, W), rows.dtype),
            pltpu.SemaphoreType.DMA((SC_NBUF,)),
            pltpu.SemaphoreType.DMA((SC_NBUF,)),
            pltpu.SemaphoreType.DMA((SC_NBUF,)),
        ],
        name="moe_dispatch",
    )
    def k(rows_hbm, ia_hbm, ib_hbm, out_hbm, ia_v, ib_v, buf, gsem, asem, bsem):
        wid = _sc_worker_id()
        base = wid * per_w
        pltpu.sync_copy(ia_hbm.at[wid], ia_v)
        pltpu.sync_copy(ib_hbm.at[wid], ib_v)

        def get(j, b):
            return pltpu.make_async_copy(
                rows_hbm.at[pl.ds(base + j * SC_ROWS, SC_ROWS)], buf.at[b], gsem.at[b])

        def put_a(j, b):
            return pltpu.make_async_copy(buf.at[b], out_hbm.at[ia_v.at[j]], asem.at[b])

        def put_b(j, b):
            return pltpu.make_async_copy(buf.at[b], out_hbm.at[ib_v.at[j]], bsem.at[b])

        _sc_ring(nch, get, lambda j, b: (put_a(j, b), put_b(j, b)))

    return k(rows, idx_a, idx_b)


def _sc_gather(table, idx):
    _, W = table.shape
    B = idx.shape[0]
    per_w = B // SC_WORKERS
    nch = per_w // SC_ROWS
    assert per_w * SC_WORKERS == B and nch * SC_ROWS == per_w and nch % SC_NBUF == 0
    idx3 = idx.reshape(SC_WORKERS, nch, SC_ROWS)

    @functools.partial(
        pl.kernel, mesh=_sc_mesh(),
        out_type=jax.ShapeDtypeStruct((B, W), table.dtype),
        scratch_types=[
            pltpu.VMEM((nch, SC_ROWS), jnp.int32),
            pltpu.VMEM((SC_NBUF, SC_ROWS, W), table.dtype),
            pltpu.SemaphoreType.DMA((SC_NBUF,)),
            pltpu.SemaphoreType.DMA((SC_NBUF,)),
        ],
        name="moe_combine_gather",
    )
    def k(table_hbm, idx_hbm, out_hbm, idx_v, buf, gsem, osem):
        wid = _sc_worker_id()
        base = wid * per_w
        pltpu.sync_copy(idx_hbm.at[wid], idx_v)

        def get(j, b):
            return pltpu.make_async_copy(table_hbm.at[idx_v.at[j]], buf.at[b], gsem.at[b])

        def put(j, b):
            return pltpu.make_async_copy(
                buf.at[b], out_hbm.at[pl.ds(base + j * SC_ROWS, SC_ROWS)], osem.at[b])

        _sc_ring(nch, get, lambda j, b: (put(j, b),))

    return k(table, idx3)


MOE_RT = 1024
MOE_SUB = 256


def _moe_body(te_ref, nt_ref, nx_ref, seg_ref, x_ref, wg_hbm, wu_hbm, wd_hbm, o_ref,
              wg_scr, wu_scr, wd_scr, wg_buf, wu_buf, wd_buf, sem):
    i = pl.program_id(0)
    expert = te_ref[i]
    slot = seg_ref[i] % 2

    def weight_copies(e, b):
        return (pltpu.make_async_copy(wg_hbm.at[e], wg_buf.at[b], sem.at[b, 0]),
                pltpu.make_async_copy(wu_hbm.at[e], wu_buf.at[b], sem.at[b, 1]),
                pltpu.make_async_copy(wd_hbm.at[e], wd_buf.at[b], sem.at[b, 2]))

    @pl.when(i == 0)
    def _():
        for c in weight_copies(expert, 0):
            c.start()

    @pl.when((i == 0) | (expert != te_ref[jnp.maximum(i - 1, 0)]))
    def _():
        for c in weight_copies(expert, slot):
            c.wait()
        wg_scr[...] = wg_buf[slot].astype(BF16)
        wu_scr[...] = wu_buf[slot].astype(BF16)
        wd_scr[...] = wd_buf[slot].astype(BF16)

        @pl.when(nx_ref[i] >= 0)
        def _():
            for c in weight_copies(nx_ref[i], 1 - slot):
                c.start()

    @pl.when(i < nt_ref[0])
    def _():
        half = D_MODEL // 2
        n_sub = x_ref.shape[0] // MOE_SUB

        def up_proj(s):
            lo, hi = _unpack_bf16_pairs(x_ref[pl.ds(s * MOE_SUB, MOE_SUB), :])
            lo = lo.astype(BF16)
            hi = hi.astype(BF16)
            gate = (jnp.dot(lo, wg_scr[0:half, :], preferred_element_type=F32)
                    + jnp.dot(hi, wg_scr[half:D_MODEL, :], preferred_element_type=F32))
            up = (jnp.dot(lo, wu_scr[0:half, :], preferred_element_type=F32)
                  + jnp.dot(hi, wu_scr[half:D_MODEL, :], preferred_element_type=F32))
            return gate, up

        def down_proj(gate, up):
            hid = (jax.nn.silu(gate) * up).astype(BF16)
            return jnp.dot(hid, wd_scr[...], preferred_element_type=F32)

        def store(s, y):
            o_ref[pl.ds(s * MOE_SUB, MOE_SUB), :] = _pack_bf16_pairs(y.astype(BF16).astype(F32))

        gu = {0: up_proj(0)}
        ys = {}
        for s in range(n_sub):
            if s + 1 < n_sub:
                gu[s + 1] = up_proj(s + 1)
            ys[s] = down_proj(*gu.pop(s))
            if s >= 1:
                store(s - 1, ys.pop(s - 1))
        store(n_sub - 1, ys.pop(n_sub - 1))


def _moe_call(plan, xs, wg, wu, wd):
    R = xs.shape[0]
    half = D_MODEL // 2
    row_map = lambda i, te, nt, nx, seg: (jnp.minimum(i, nt[0] - 1), 0)
    hbm = pl.BlockSpec(memory_space=pl.ANY)
    return pl.pallas_call(
        _moe_body,
        out_shape=jax.ShapeDtypeStruct((R, half), jnp.int32),
        grid_spec=pltpu.PrefetchScalarGridSpec(
            num_scalar_prefetch=4,
            grid=(R // MOE_RT,),
            in_specs=[pl.BlockSpec((MOE_RT, half), row_map), hbm, hbm, hbm],
            out_specs=pl.BlockSpec((MOE_RT, half), row_map),
            scratch_shapes=[pltpu.VMEM((D_MODEL, EXPERT_HIDDEN), BF16),
                            pltpu.VMEM((D_MODEL, EXPERT_HIDDEN), BF16),
                            pltpu.VMEM((EXPERT_HIDDEN, D_MODEL), BF16),
                            pltpu.VMEM((2, D_MODEL, EXPERT_HIDDEN), F32),
                            pltpu.VMEM((2, D_MODEL, EXPERT_HIDDEN), F32),
                            pltpu.VMEM((2, EXPERT_HIDDEN, D_MODEL), F32),
                            pltpu.SemaphoreType.DMA((2, 3))]),
        compiler_params=pltpu.CompilerParams(
            dimension_semantics=("arbitrary",), vmem_limit_bytes=VMEM_LIMIT),
        name="moe_experts",
    )(*plan, xs, wg, wu, wd)


def _moe_plan(route_t, counts, n_rows):
    cnt = counts[0, :N_EXPERTS].astype(jnp.int32)
    tiles = (cnt + MOE_RT - 1) // MOE_RT
    tile_end = jnp.cumsum(tiles)
    n_tiles = tile_end[-1:]
    row_start = (tile_end - tiles) * MOE_RT
    ids = route_t[0:2].astype(jnp.int32)
    ranks = route_t[4:6].astype(jnp.int32)
    experts = jnp.arange(N_EXPERTS, dtype=jnp.int32)[:, None, None]
    pos = ranks + jnp.sum(jnp.where(ids[None] == experts, row_start[:, None, None], 0), axis=0)
    tile_id = jnp.minimum(jnp.arange(n_rows // MOE_RT, dtype=jnp.int32), n_tiles - 1)
    tile_expert = jnp.sum((tile_id[:, None] >= tile_end[None, :]).astype(jnp.int32), axis=1)
    e_ids = jnp.arange(N_EXPERTS, dtype=jnp.int32)
    later = (e_ids[None, :] > e_ids[:, None]) & (tiles[None, :] > 0)
    nxt = jnp.min(jnp.where(later, e_ids[None, :], N_EXPERTS), axis=1)
    nxt = jnp.where(nxt == N_EXPERTS, -1, nxt)
    seg = jnp.cumsum((tiles > 0).astype(jnp.int32)) - 1
    pick = tile_expert[:, None] == e_ids[None, :]
    tile_next = jnp.sum(jnp.where(pick, nxt[None, :], 0), axis=1)
    tile_seg = jnp.sum(jnp.where(pick, seg[None, :], 0), axis=1)
    return pos, (tile_expert, n_tiles, tile_next, tile_seg)


PLE_TM = 1024
PLE_SUB = 256


def _ple_body(x_ref, ya_ref, yb_ref, route_ref, p_ref, g3_ref, wpg_ref, bpg_ref, wple_ref, gf_ref,
              o_ref):
    n_sub = x_ref.shape[0] // PLE_SUB
    g3, gf, bpg = g3_ref[...], gf_ref[...], bpg_ref[...]

    def head(s):
        rows = pl.ds(s * PLE_SUB, PLE_SUB)
        ya = jnp.concatenate(_unpack_bf16_pairs(ya_ref[rows, :]), axis=1)
        yb = jnp.concatenate(_unpack_bf16_pairs(yb_ref[rows, :]), axis=1)
        route = route_ref[rows, :]
        x2 = x_ref[rows, :] + route[:, 2:3] * ya + route[:, 3:4] * yb
        return x2, _rms(x2, g3).astype(BF16)

    def dots(s, h3):
        rows = pl.ds(s * PLE_SUB, PLE_SUB)
        zg = jnp.dot(h3, wpg_ref[...], preferred_element_type=F32)
        pe = jnp.dot(p_ref[rows, :].astype(BF16), wple_ref[...], preferred_element_type=F32)
        return zg, pe

    def tail(s, x2, zg, pe):
        x3 = x2 + jax.nn.sigmoid(zg + bpg) * pe
        o_ref[pl.ds(s * PLE_SUB, PLE_SUB), :] = _rms(x3, gf)

    x2s, mm = {}, {}
    x2s[0], h3 = head(0)
    for s in range(n_sub):
        mm[s] = dots(s, h3)
        if s + 1 < n_sub:
            x2s[s + 1], h3 = head(s + 1)
        if s >= 1:
            tail(s - 1, x2s.pop(s - 1), *mm.pop(s - 1))
    tail(n_sub - 1, x2s.pop(n_sub - 1), *mm.pop(n_sub - 1))


def _ple_call(x1, y_picks, route, p, g3, wpg, bpg, wple, gf):
    T = x1.shape[0]
    nt = T // PLE_TM
    consts = (g3, wpg, bpg, wple, gf)
    tok = lambda i: (i, 0)
    return pl.pallas_call(
        _ple_body,
        out_shape=jax.ShapeDtypeStruct((T, D_MODEL), F32),
        grid=(nt,),
        in_specs=[pl.BlockSpec((PLE_TM, D_MODEL), tok),
                  pl.BlockSpec((PLE_TM, D_MODEL // 2), tok),
                  pl.BlockSpec((PLE_TM, D_MODEL // 2), lambda i: (i + nt, 0)),
                  pl.BlockSpec((PLE_TM, LANES), tok),
                  pl.BlockSpec((PLE_TM, p.shape[1]), tok)]
                 + [_const_spec(c.shape) for c in consts],
        out_specs=pl.BlockSpec((PLE_TM, D_MODEL), tok),
        compiler_params=pltpu.CompilerParams(
            dimension_semantics=("arbitrary",), vmem_limit_bytes=VMEM_LIMIT),
        name="ple_final",
    )(x1, y_picks, y_picks, route, p, *consts)


def _layer(x, p, norm_mix, w_in, b_in, lam_re, lam_im, log_dt, b_re, b_im, c_re, c_im, d_skip,
           w_glu_a, w_glu_b, conv_w, conv_b, w_conv_out, w_o, norm_ffn, w_rg, b_rg, w_re, b_re_r,
           w_eg, w_eu, w_ed, norm_ple, w_ple, w_pg, b_pg, norm_out):
    nb, seq, d = x.shape
    T = nb * seq
    row = lambda v: v.reshape(1, -1).astype(F32)
    assert w_in.shape[1] == SSM_WIDTH + 3 * CONV_WIDTH + 2 * D_MODEL and SSM_WIDTH == CONV_WIDTH
    w_in16 = w_in.astype(BF16)
    b_in_row = row(b_in)

    kmat, mmat, nmat, aq_re, aq_im = _s5_operators(lam_re, lam_im, log_dt, b_re, b_im, c_re, c_im)
    s5_consts = (row(norm_mix), w_in16, b_in_row, mmat, kmat, nmat, aq_re, aq_im, row(d_skip))

    lane_pad = LANES - N_EXPERTS - N_EXPERT_GROUPS
    w_r = jnp.pad(jnp.concatenate([w_re, w_rg], axis=1).astype(F32), ((0, 0), (0, lane_pad)))
    b_r = jnp.pad(jnp.concatenate([b_re_r, b_rg]).astype(F32), (0, lane_pad)).reshape(1, LANES)
    w_r_hi = w_r.astype(BF16)
    w_r_lo = (w_r - w_r_hi.astype(F32)).astype(BF16)

    mix_consts = (w_in16, b_in_row, row(norm_mix),
                  conv_w.astype(F32), row(conv_b), w_conv_out.astype(BF16),
                  jnp.concatenate([w_glu_a, w_glu_b], axis=1).astype(BF16), w_o.astype(BF16),
                  row(norm_ffn), jnp.concatenate([w_r_hi, w_r_lo], axis=1), b_r)
    ple_consts = (row(norm_ple), w_pg.astype(BF16), row(b_pg), w_ple.astype(BF16), row(norm_out))
    p2d = p.reshape(T, -1)
    n_rows = 2 * T + N_EXPERTS * MOE_RT

    ys = _s5_call(x, *s5_consts)
    x1, h2p, route, route_t, counts = _mix_call(x, ys, *mix_consts)
    pos, plan = _moe_plan(route_t, counts, n_rows)
    xs = _sc_dispatch(h2p, pos[0], pos[1], n_rows)
    ysort = _moe_call(plan, xs, w_eg, w_eu, w_ed)
    y_picks = _sc_gather(ysort, pos.reshape(-1))
    out = _ple_call(x1, y_picks, route, p2d, *ple_consts)
    return out.reshape(nb, seq, d)


def kernel(x, p, norm_mix, w_in, b_in, ssm_lam_re, ssm_lam_im, ssm_log_dt, ssm_b_re, ssm_b_im, ssm_c_re, ssm_c_im, ssm_d, w_glu_a, w_glu_b, conv_w, conv_b, w_conv_out, w_o, norm_ffn, w_router_group, b_router_group, w_router_expert, b_router_expert, w_exp_gate, w_exp_up, w_exp_down, norm_ple, w_ple, w_ple_gate, b_ple_gate, norm_final):
    assert p.shape[0] == 1, "the final RMSNorm is fused into the (single) layer's last kernel"
    i = 0
    return _layer(x, p[i], norm_mix[i], w_in[i], b_in[i], ssm_lam_re[i], ssm_lam_im[i],
                  ssm_log_dt[i], ssm_b_re[i], ssm_b_im[i], ssm_c_re[i], ssm_c_im[i], ssm_d[i],
                  w_glu_a[i], w_glu_b[i], conv_w[i], conv_b[i], w_conv_out[i], w_o[i],
                  norm_ffn[i], w_router_group[i], b_router_group[i], w_router_expert[i],
                  b_router_expert[i], w_exp_gate[i], w_exp_up[i], w_exp_down[i], norm_ple[i],
                  w_ple[i], w_ple_gate[i], b_ple_gate[i], norm_final)
```

```python
import functools
import math

import jax
import jax.numpy as jnp
from jax import lax
from jax.experimental import pallas as pl
from jax.experimental.pallas import tpu as pltpu
from jax.experimental.pallas import tpu_sc as plsc

F32 = jnp.float32
BF16 = jnp.bfloat16

D_MODEL = 1024
SSM_WIDTH = 512
SSM_GROUP = 16
SSM_GROUPS = 32
SSM_STATE = 64
CONV_WIDTH = 512
N_EXPERT_GROUPS = 4
EXPERTS_PER_GROUP = 8
N_EXPERTS = 32
EXPERT_HIDDEN = 256
NORM_EPS = 1e-6

LANES = 128
Q = 8
GROUPS_PER_LANE_TILE = LANES // SSM_GROUP
N_LANE_TILES = SSM_WIDTH // LANES
STATE_LANES = GROUPS_PER_LANE_TILE * SSM_STATE
S5_ROWS = 1024
VMEM_LIMIT = 56 * 1024 * 1024


def _rms(x, g):
    return x * lax.rsqrt(jnp.mean(x * x, axis=-1, keepdims=True) + NORM_EPS) * g


def _pack_bf16_pairs(a):
    w = a.shape[1] // 2
    lo = lax.shift_right_logical(lax.bitcast_convert_type(a[:, :w], jnp.int32), 16)
    hi = lax.bitcast_convert_type(a[:, w:], jnp.int32) & jnp.int32(-65536)
    return lo | hi


def _unpack_bf16_pairs(word):
    lo = lax.bitcast_convert_type(lax.shift_left(word, 16), F32)
    hi = lax.bitcast_convert_type(word & jnp.int32(-65536), F32)
    return lo, hi


def _const_spec(shape):
    n = len(shape)
    return pl.BlockSpec(shape, lambda *_: (0,) * n, pipeline_mode=pl.Buffered(1))


def _s5_operators(lam_re, lam_im, log_dt, b_re, b_im, c_re, c_im):
    G, P, H = SSM_GROUPS, SSM_STATE, SSM_GROUP
    J = N_LANE_TILES
    lr = lam_re.astype(F32)
    li = lam_im.astype(F32)
    dt = jnp.exp(log_dt.astype(F32))[:, None]

    def apow(n):
        n = n.astype(F32)[:, None, None]
        mag = jnp.exp(lr * dt * n)
        ang = li * dt * n
        return mag * jnp.cos(ang), mag * jnp.sin(ang)

    a1_re, a1_im = apow(jnp.ones((1,), F32))
    nr = a1_re[0] - 1.0
    ni = a1_im[0]
    den = lr * lr + li * li
    f_re = (nr * lr + ni * li) / den
    f_im = (ni * lr - nr * li) / den
    br = b_re.astype(F32)
    bi = b_im.astype(F32)
    bbar_re = f_re[:, :, None] * br - f_im[:, :, None] * bi
    bbar_im = f_re[:, :, None] * bi + f_im[:, :, None] * br
    to_rows = lambda v, perm: jnp.transpose(v, perm).reshape(H, G * P)
    bt_re = to_rows(bbar_re, (2, 0, 1))
    bt_im = to_rows(bbar_im, (2, 0, 1))
    ct_re = to_rows(c_re.astype(F32), (1, 0, 2))
    ct_im = to_rows(c_im.astype(F32), (1, 0, 2))
    ap_re, ap_im = apow(jnp.arange(Q + 1))
    ap_re = ap_re.reshape(Q + 1, G * P)
    ap_im = ap_im.reshape(Q + 1, G * P)

    blk = lambda r: pl.BlockSpec((r, STATE_LANES), lambda j: (0, j))
    mat = lambda r, c: pl.BlockSpec((1, r, c), lambda j: (j, 0, 0))
    shape = lambda r, c: jax.ShapeDtypeStruct((J, r, c), BF16)
    qx, st = Q * LANES, 2 * STATE_LANES
    kmat, mmat, nmat = pl.pallas_call(
        _s5_ops_body,
        out_shape=(shape(qx, qx), shape(qx, st), shape(st, qx)),
        grid=(J,),
        in_specs=[blk(Q + 1), blk(Q + 1), blk(H), blk(H), blk(H), blk(H)],
        out_specs=(mat(qx, qx), mat(qx, st), mat(st, qx)),
        compiler_params=pltpu.CompilerParams(
            dimension_semantics=("arbitrary",), vmem_limit_bytes=VMEM_LIMIT),
        name="s5_operators",
    )(ap_re, ap_im, bt_re, bt_im, ct_re, ct_im)
    return kmat, mmat, nmat, ap_re[Q:Q + 1], ap_im[Q:Q + 1]


def _s5_ops_body(apr_ref, api_ref, btr_ref, bti_ref, ctr_ref, cti_ref, k_ref, m_ref, n_ref):
    ri = lax.broadcasted_iota(jnp.int32, (LANES, STATE_LANES), 0)
    li = lax.broadcasted_iota(jnp.int32, (LANES, STATE_LANES), 1)
    same_group = (ri // SSM_GROUP) == (li // SSM_STATE)

    def expand(ref):
        tiled = jnp.concatenate([ref[...]] * GROUPS_PER_LANE_TILE, axis=0)
        return jnp.where(same_group, tiled, 0.0)

    b_re, b_im, c_re, c_im = expand(btr_ref), expand(bti_ref), expand(ctr_ref), expand(cti_ref)

    def cmul(n, x_re, x_im):
        a_re = apr_ref[n:n + 1, :]
        a_im = api_ref[n:n + 1, :]
        return a_re * x_re - a_im * x_im, a_re * x_im + a_im * x_re

    m_blocks = []
    for k in range(Q):
        g_re, g_im = cmul(Q - 1 - k, b_re, b_im)
        m_blocks.append(jnp.concatenate([g_re, g_im], axis=1))
    m = jnp.concatenate(m_blocks, axis=0)

    nt_blocks = []
    for t in range(Q):
        g_re, g_im = cmul(t + 1, c_re, c_im)
        nt_blocks.append(jnp.concatenate([g_re, -g_im], axis=1))
    nt = jnp.concatenate(nt_blocks, axis=0)

    n0t = jnp.concatenate([c_re, -c_im], axis=1)
    p = lax.dot_general(m, n0t, (((1,), (1,)), ((), ())),
                        precision=lax.Precision.HIGHEST, preferred_element_type=F32)
    zeros = jnp.zeros((LANES, LANES), F32)
    cols = []
    for t in range(Q):
        cols.append(jnp.concatenate(
            [p[(Q - 1 - (t - k)) * LANES:(Q - (t - k)) * LANES, :] if t >= k else zeros
             for k in range(Q)], axis=0))
    k_ref[0] = jnp.concatenate(cols, axis=1).astype(BF16)
    m_ref[0] = m.astype(BF16)
    n_ref[0] = nt.T.astype(BF16)


def _s5_body(x_ref, g_ref, wu_ref, bu_ref, m_ref, k_ref, n_ref, aqr_ref, aqi_ref, d_ref,
             o_ref, u_scr, y_scr, z_scr, ss_scr, carry_scr):
    nb, tt = x_ref.shape[0], x_ref.shape[1]
    rows = nb * tt
    nchunk = rows // Q

    @pl.when(pl.program_id(0) == 0)
    def _():
        carry_scr[...] = jnp.zeros_like(carry_scr)

    hb = nb // 2
    for r in range(2):
        x = x_ref[r * hb:(r + 1) * hb].reshape(rows // 2, D_MODEL)
        h = _rms(x, g_ref[...]).astype(BF16)
        u = jnp.dot(h, wu_ref[...], preferred_element_type=F32) + bu_ref[...]
        for j in range(N_LANE_TILES):
            u_scr[j, pl.ds(r * (rows // 2), rows // 2), :] = u[:, j * LANES:(j + 1) * LANES]

    n_st = STATE_LANES // LANES
    cpt = tt // Q

    lane_tiles = range(N_LANE_TILES)


    xs = [jnp.concatenate([u_scr[j, pl.ds(k, nchunk, stride=Q), :] for k in range(Q)],
                          axis=1).astype(BF16) for j in lane_tiles]
    for j in lane_tiles:
        z = jnp.dot(xs[j], m_ref[j], preferred_element_type=F32)
        for i in range(2 * n_st):
            z_scr[j, i] = z[:, i * LANES:(i + 1) * LANES]
    y_intra = [jnp.dot(xs[j], k_ref[j], preferred_element_type=F32) for j in lane_tiles]

    aq = [(jnp.broadcast_to(aqr_ref[:, pl.ds(j * STATE_LANES, STATE_LANES)], (nb, STATE_LANES)),
           jnp.broadcast_to(aqi_ref[:, pl.ds(j * STATE_LANES, STATE_LANES)], (nb, STATE_LANES)))
          for j in lane_tiles]
    st = [(carry_scr[j, :, pl.ds(0, STATE_LANES)], carry_scr[j, :, pl.ds(STATE_LANES, STATE_LANES)])
          for j in lane_tiles]
    for c in range(cpt):
        seq_rows = pl.ds(c, nb, stride=cpt)
        for j in lane_tiles:
            s_re, s_im = st[j]
            aqr, aqi = aq[j]
            for i in range(n_st):
                ss_scr[j, i, seq_rows, :] = s_re[:, i * LANES:(i + 1) * LANES]
                ss_scr[j, n_st + i, seq_rows, :] = s_im[:, i * LANES:(i + 1) * LANES]
            z_re = jnp.concatenate([z_scr[j, i, seq_rows, :] for i in range(n_st)], axis=1)
            z_im = jnp.concatenate([z_scr[j, n_st + i, seq_rows, :] for i in range(n_st)], axis=1)
            st[j] = (aqr * s_re - aqi * s_im + z_re, aqr * s_im + aqi * s_re + z_im)
    for j in lane_tiles:
        carry_scr[j, :, pl.ds(0, STATE_LANES)] = st[j][0]
        carry_scr[j, :, pl.ds(STATE_LANES, STATE_LANES)] = st[j][1]

    def state_to_output(j):
        ss = jnp.concatenate([ss_scr[j, i] for i in range(2 * n_st)], axis=1).astype(BF16)
        return y_intra[j] + jnp.dot(ss, n_ref[j], preferred_element_type=F32)

    def finish(j, yj):
        for k in range(Q):
            y_scr[j, pl.ds(k, nchunk, stride=Q), :] = yj[:, k * LANES:(k + 1) * LANES]
        lanes = pl.ds(j * LANES, LANES)
        y = y_scr[j] + d_ref[:, lanes] * u_scr[j]
        o_ref[:, :, lanes] = jax.nn.gelu(y).astype(BF16).reshape(nb, tt, LANES)

    yj = state_to_output(0)
    for j in lane_tiles:
        y_next = state_to_output(j + 1) if j + 1 < N_LANE_TILES else None
        finish(j, yj)
        yj = y_next


def _col_block(rows, width, block):
    return pl.BlockSpec((rows, width), lambda *_: (0, block), pipeline_mode=pl.Buffered(1))


def _s5_call(x, g, w_in, b_in, mmat, kmat, nmat, aq_re, aq_im, d_skip):
    nb, seq = x.shape[0], x.shape[1]
    tt = S5_ROWS // nb
    rows = S5_ROWS
    nchunk = rows // Q
    return pl.pallas_call(
        _s5_body,
        out_shape=jax.ShapeDtypeStruct((nb, seq, SSM_WIDTH), BF16),
        grid=(seq // tt,),
        in_specs=[
            pl.BlockSpec((nb, tt, D_MODEL), lambda i: (0, i, 0)),
            _const_spec(g.shape), _col_block(D_MODEL, SSM_WIDTH, 0), _col_block(1, SSM_WIDTH, 0),
            _const_spec(mmat.shape), _const_spec(kmat.shape), _const_spec(nmat.shape),
            _const_spec(aq_re.shape), _const_spec(aq_im.shape), _const_spec(d_skip.shape),
        ],
        out_specs=pl.BlockSpec((nb, tt, SSM_WIDTH), lambda i: (0, i, 0)),
        scratch_shapes=[
            pltpu.VMEM((N_LANE_TILES, rows, LANES), F32),
            pltpu.VMEM((N_LANE_TILES, rows, LANES), F32),
            pltpu.VMEM((N_LANE_TILES, 2 * STATE_LANES // LANES, nchunk, LANES), F32),
            pltpu.VMEM((N_LANE_TILES, 2 * STATE_LANES // LANES, nchunk, LANES), F32),
            pltpu.VMEM((N_LANE_TILES, nb, 2 * STATE_LANES), F32),
        ],
        compiler_params=pltpu.CompilerParams(
            dimension_semantics=("arbitrary",), vmem_limit_bytes=VMEM_LIMIT),
        name="s5_mixer",
    )(x, g, w_in, b_in, mmat, kmat, nmat, aq_re, aq_im, d_skip)


MIX_TM = 512


def _mix_body(x_ref, ys_ref, wcb_ref, wcc_ref, wcv_ref, wg_ref, bcb_ref, bcc_ref, bcv_ref, bg_ref,
              g_ref, cw_ref, cb_ref, wco_ref, wab_ref, wo_ref, g2_ref, wr_ref, br_ref,
              x1_ref, h2p_ref, route_ref, route_t_ref, cnt_ref, carry_scr, cnt_scr, x1_scr,
              *, tiles_per_seq, n_tiles):
    i = pl.program_id(0)
    tm = x_ref.shape[1]
    wc_refs = (wcb_ref, wcc_ref, wcv_ref)
    bc_refs = (bcb_ref, bcc_ref, bcv_ref)

    @pl.when(i == 0)
    def _():
        cnt_scr[...] = jnp.zeros_like(cnt_scr)
        x1_scr[...] = jnp.zeros_like(x1_scr)
        carry_scr[...] = jnp.zeros_like(carry_scr)

    carry_scr[0] = jnp.where(i == n_tiles, carry_scr[0],
                             jnp.where(i % tiles_per_seq == 0, 0.0, carry_scr[1]))

    yab = jnp.dot(ys_ref[0], wab_ref[...], preferred_element_type=F32)

    h2_hi = _rms(x1_scr[...], g2_ref[...]).astype(BF16)
    h2p_ref[...] = _pack_bf16_pairs(h2_hi.astype(F32))
    lg2 = jnp.dot(h2_hi, wr_ref[...], preferred_element_type=F32)
    logits = lg2[:, 0:LANES] + lg2[:, LANES:2 * LANES] + br_ref[...]

    x = x_ref[0]
    h = _rms(x, g_ref[...]).astype(BF16)
    c_b, c_c, c_v = [jnp.dot(h, w[...], preferred_element_type=F32) + b[...]
                     for w, b in zip(wc_refs, bc_refs)]
    zg = jnp.dot(h, wg_ref[...], preferred_element_type=F32) + bg_ref[...]

    cnt = cnt_scr[...]
    route, new_cnt = _route_tile(logits, cnt)
    cnt = jnp.where(i > 0, new_cnt, cnt)
    cnt_scr[...] = cnt
    cnt_ref[...] = cnt
    route_ref[...] = route
    route_t_ref[...] = route.T[0:8, :]

    y_a = yab[:, 0:D_MODEL] * jax.nn.sigmoid(yab[:, D_MODEL:2 * D_MODEL])

    cv = c_c * c_v
    row = lax.broadcasted_iota(jnp.int32, (tm, CONV_WIDTH), 0)
    last1 = carry_scr[0, 7:8, :]
    last2 = carry_scr[0, 6:7, :]
    p1 = jnp.where(row == 0, last1, pltpu.roll(cv, 1, axis=0))
    p2 = jnp.where(row == 0, last2, jnp.where(row == 1, last1, pltpu.roll(cv, 2, axis=0)))
    carry_scr[1] = cv[tm - 8:tm, :]
    conv = cw_ref[0:1, :] * p2 + cw_ref[1:2, :] * p1 + cw_ref[2:3, :] * cv + cb_ref[...]
    y_b = jnp.dot((c_b * conv).astype(BF16), wco_ref[...], preferred_element_type=F32)

    mix = (jax.nn.sigmoid(zg[:, 0:D_MODEL]) * y_a
           + jax.nn.sigmoid(zg[:, D_MODEL:2 * D_MODEL]) * y_b)
    x1 = x + jnp.dot(mix.astype(BF16), wo_ref[...], preferred_element_type=F32)
    x1_ref[...] = x1
    x1_scr[...] = x1


def _route_tile(logits, cnt):
    tm = logits.shape[0]
    lane = lax.broadcasted_iota(jnp.int32, (tm, LANES), 1)
    neg = jnp.float32(-jnp.inf)
    big = jnp.int32(1 << 20)
    is_g = (lane >= N_EXPERTS) & (lane < N_EXPERTS + N_EXPERT_GROUPS)
    gl = jnp.where(is_g, logits, neg)
    gmax = jnp.max(gl, axis=1, keepdims=True)
    g_w = 1.0 / jnp.sum(jnp.exp(gl - gmax), axis=1, keepdims=True)
    g_idx = jnp.min(jnp.where(gl == gmax, lane - N_EXPERTS, big), axis=1, keepdims=True)
    lo = g_idx * EXPERTS_PER_GROUP
    el = jnp.where((lane >= lo) & (lane < lo + EXPERTS_PER_GROUP), logits, neg)
    m1 = jnp.max(el, axis=1, keepdims=True)
    i1 = jnp.min(jnp.where(el == m1, lane, big), axis=1, keepdims=True)
    el2 = jnp.where(lane == i1, neg, el)
    m2 = jnp.max(el2, axis=1, keepdims=True)
    i2 = jnp.min(jnp.where(el2 == m2, lane, big), axis=1, keepdims=True)
    r = jnp.exp(m2 - m1)
    w1 = g_w / (1.0 + r)
    w2 = g_w * r / (1.0 + r)

    picks = ((lane == i1) | (lane == i2)).astype(BF16)
    r_i = lax.broadcasted_iota(jnp.int32, (tm, tm), 0)
    c_i = lax.broadcasted_iota(jnp.int32, (tm, tm), 1)
    before = (c_i < r_i).astype(BF16)
    excl = jnp.dot(before, picks, preferred_element_type=F32) + cnt
    rank1 = jnp.sum(jnp.where(lane == i1, excl, 0.0), axis=1, keepdims=True)
    rank2 = jnp.sum(jnp.where(lane == i2, excl, 0.0), axis=1, keepdims=True)
    new_cnt = cnt + jnp.sum(picks.astype(F32), axis=0, keepdims=True)
    route = jnp.where(lane == 0, i1.astype(F32), 0.0)
    route = jnp.where(lane == 1, i2.astype(F32), route)
    route = jnp.where(lane == 2, w1, route)
    route = jnp.where(lane == 3, w2, route)
    route = jnp.where(lane == 4, rank1, route)
    route = jnp.where(lane == 5, rank2, route)
    return route, new_cnt


def _mix_call(x, ys, w_in, b_in, g, cw, cb, wco, wab, wo, g2, wr, br):
    nb, seq, _ = ys.shape
    T = nb * seq
    nl = seq // MIX_TM
    n = nb * nl
    consts = (g, cw, cb, wco, wab, wo, g2, wr, br)
    gates = 2 * D_MODEL
    in_proj_specs = ([_col_block(D_MODEL, CONV_WIDTH, b) for b in (1, 2, 3)]
                     + [_col_block(D_MODEL, gates, 1)]
                     + [_col_block(1, CONV_WIDTH, b) for b in (1, 2, 3)]
                     + [_col_block(1, gates, 1)])
    cur = lambda i: jnp.minimum(i, n - 1)
    prev = lambda i: jnp.maximum(i - 1, 0)
    return pl.pallas_call(
        functools.partial(_mix_body, tiles_per_seq=nl, n_tiles=n),
        out_shape=(jax.ShapeDtypeStruct((T, D_MODEL), F32),
                   jax.ShapeDtypeStruct((T, D_MODEL // 2), jnp.int32),
                   jax.ShapeDtypeStruct((T, LANES), F32),
                   jax.ShapeDtypeStruct((8, T), F32),
                   jax.ShapeDtypeStruct((1, LANES), F32)),
        grid=(n + 1,),
        in_specs=[pl.BlockSpec((1, MIX_TM, D_MODEL),
                               lambda i: (cur(i) // nl, cur(i) % nl, 0)),
                  pl.BlockSpec((1, MIX_TM, SSM_WIDTH), lambda i: (cur(i) // nl, cur(i) % nl, 0))]
                 + in_proj_specs + [_const_spec(c.shape) for c in consts],
        out_specs=(pl.BlockSpec((MIX_TM, D_MODEL), lambda i: (cur(i), 0)),
                   pl.BlockSpec((MIX_TM, D_MODEL // 2), lambda i: (prev(i), 0)),
                   pl.BlockSpec((MIX_TM, LANES), lambda i: (prev(i), 0)),
                   pl.BlockSpec((8, MIX_TM), lambda i: (0, prev(i))),
                   pl.BlockSpec((1, LANES), lambda i: (0, 0))),
        scratch_shapes=[pltpu.VMEM((2, 8, CONV_WIDTH), F32), pltpu.VMEM((1, LANES), F32),
                        pltpu.VMEM((MIX_TM, D_MODEL), F32)],
        compiler_params=pltpu.CompilerParams(
            dimension_semantics=("arbitrary",), vmem_limit_bytes=VMEM_LIMIT),
        name="conv_glu_router",
    )(x, ys, *([w_in] * 4), *([b_in] * 4), *consts)


SC_CORES = 2
SC_SUBCORES = 16
SC_WORKERS = SC_CORES * SC_SUBCORES
SC_ROWS = 64
SC_NBUF = 2


def _sc_mesh():
    return plsc.VectorSubcoreMesh(core_axis_name="c", subcore_axis_name="s")


def _sc_worker_id():
    return lax.axis_index("s") * SC_CORES + lax.axis_index("c")


def _sc_ring(nch, get, puts):
    for b in range(SC_NBUF - 1):
        get(b, b).start()

    @pl.loop(0, nch, step=SC_NBUF)
    def _(j0):
        for b in range(SC_NBUF):
            j = j0 + b
            refill = (b - 1) % SC_NBUF
            get(j, b).wait()

            @pl.when(j + SC_NBUF - 1 < nch)
            def _():
                @pl.when(j >= 1)
                def _():
                    for c in puts(j - 1, refill):
                        c.wait()
                get(j + SC_NBUF - 1, refill).start()

            for c in puts(j, b):
                c.start()

    for b in range(SC_NBUF):
        for c in puts(nch - SC_NBUF + b, b):
            c.wait()


def _sc_dispatch(rows, pos_a, pos_b, n_out):
    T, W = rows.shape
    per_w = T // SC_WORKERS
    nch = per_w // SC_ROWS
    assert per_w * SC_WORKERS == T and nch * SC_ROWS == per_w and nch % SC_NBUF == 0
    idx_a = pos_a.reshape(SC_WORKERS, nch, SC_ROWS)
    idx_b = pos_b.reshape(SC_WORKERS, nch, SC_ROWS)

    @functools.partial(
        pl.kernel, mesh=_sc_mesh(),
        out_type=jax.ShapeDtypeStruct((n_out, W), rows.dtype),
        scratch_types=[
            pltpu.VMEM((nch, SC_ROWS), jnp.int32),
            pltpu.VMEM((nch, SC_ROWS), jnp.int32),
            pltpu.VMEM((SC_NBUF, SC_ROWS, W), rows.dtype),
            pltpu.SemaphoreType.DMA((SC_NBUF,)),
            pltpu.SemaphoreType.DMA((SC_NBUF,)),
            pltpu.SemaphoreType.DMA((SC_NBUF,)),
        ],
        name="moe_dispatch",
    )
    def k(rows_hbm, ia_hbm, ib_hbm, out_hbm, ia_v, ib_v, buf, gsem, asem, bsem):
        wid = _sc_worker_id()
        base = wid * per_w
        pltpu.sync_copy(ia_hbm.at[wid], ia_v)
        pltpu.sync_copy(ib_hbm.at[wid], ib_v)

        def get(j, b):
            return pltpu.make_async_copy(
                rows_hbm.at[pl.ds(base + j * SC_ROWS, SC_ROWS)], buf.at[b], gsem.at[b])

        def put_a(j, b):
            return pltpu.make_async_copy(buf.at[b], out_hbm.at[ia_v.at[j]], asem.at[b])

        def put_b(j, b):
            return pltpu.make_async_copy(buf.at[b], out_hbm.at[ib_v.at[j]], bsem.at[b])

        _sc_ring(nch, get, lambda j, b: (put_a(j, b), put_b(j, b)))

    return k(rows, idx_a, idx_b)


def _sc_gather(table, idx):
    _, W = table.shape
    B = idx.shape[0]
    per_w = B // SC_WORKERS
    nch = per_w // SC_ROWS
    assert per_w * SC_WORKERS == B and nch * SC_ROWS == per_w and nch % SC_NBUF == 0
    idx3 = idx.reshape(SC_WORKERS, nch, SC_ROWS)

    @functools.partial(
        pl.kernel, mesh=_sc_mesh(),
        out_type=jax.ShapeDtypeStruct((B, W), table.dtype),
        scratch_types=[
            pltpu.VMEM((nch, SC_ROWS), jnp.int32),
            pltpu.VMEM((SC_NBUF, SC_ROWS, W), table.dtype),
            pltpu.SemaphoreType.DMA((SC_NBUF,)),
            pltpu.SemaphoreType.DMA((SC_NBUF,)),
        ],
        name="moe_combine_gather",
    )
    def k(table_hbm, idx_hbm, out_hbm, idx_v, buf, gsem, osem):
        wid = _sc_worker_id()
        base = wid * per_w
        pltpu.sync_copy(idx_hbm.at[wid], idx_v)

        def get(j, b):
            return pltpu.make_async_copy(table_hbm.at[idx_v.at[j]], buf.at[b], gsem.at[b])

        def put(j, b):
            return pltpu.make_async_copy(
                buf.at[b], out_hbm.at[pl.ds(base + j * SC_ROWS, SC_ROWS)], osem.at[b])

        _sc_ring(nch, get, lambda j, b: (put(j, b),))

    return k(table, idx3)


MOE_RT = 1024
MOE_SUB = 256
MOE_IN_BUFFERS = 3


def _moe_tile(i, te_ref, nt_ref, nx_ref, seg_ref, x_ref, wg_hbm, wu_hbm, wd_hbm, o_ref,
              wg_scr, wu_scr, wd_scr, wg_buf, wu_buf, wd_buf, sem):
    expert = te_ref[i]
    slot = seg_ref[i] % 2

    def weight_copies(e, b):
        return (pltpu.make_async_copy(wg_hbm.at[e], wg_buf.at[b], sem.at[b, 0]),
                pltpu.make_async_copy(wu_hbm.at[e], wu_buf.at[b], sem.at[b, 1]),
                pltpu.make_async_copy(wd_hbm.at[e], wd_buf.at[b], sem.at[b, 2]))

    @pl.when(i == 0)
    def _():
        for c in weight_copies(expert, 0):
            c.start()

    @pl.when((i == 0) | (expert != te_ref[jnp.maximum(i - 1, 0)]))
    def _():
        for c in weight_copies(expert, slot):
            c.wait()
        wg_scr[...] = wg_buf[slot].astype(BF16)
        wu_scr[...] = wu_buf[slot].astype(BF16)
        wd_scr[...] = wd_buf[slot].astype(BF16)

        @pl.when(nx_ref[i] >= 0)
        def _():
            for c in weight_copies(nx_ref[i], 1 - slot):
                c.start()

    @pl.when(i < nt_ref[0])
    def _():
        half = D_MODEL // 2
        n_sub = x_ref.shape[0] // MOE_SUB

        def up_proj(s):
            lo, hi = _unpack_bf16_pairs(x_ref[pl.ds(s * MOE_SUB, MOE_SUB), :])
            lo = lo.astype(BF16)
            hi = hi.astype(BF16)
            gate = (jnp.dot(lo, wg_scr[0:half, :], preferred_element_type=F32)
                    + jnp.dot(hi, wg_scr[half:D_MODEL, :], preferred_element_type=F32))
            up = (jnp.dot(lo, wu_scr[0:half, :], preferred_element_type=F32)
                  + jnp.dot(hi, wu_scr[half:D_MODEL, :], preferred_element_type=F32))
            return gate, up

        def down_proj(gate, up):
            hid = (jax.nn.silu(gate) * up).astype(BF16)
            return jnp.dot(hid, wd_scr[...], preferred_element_type=F32)

        def store(s, y):
            o_ref[pl.ds(s * MOE_SUB, MOE_SUB), :] = _pack_bf16_pairs(y.astype(BF16).astype(F32))

        gu = {0: up_proj(0)}
        ys = {}
        for s in range(n_sub):
            if s + 1 < n_sub:
                gu[s + 1] = up_proj(s + 1)
            ys[s] = down_proj(*gu.pop(s))
            if s >= 1:
                store(s - 1, ys.pop(s - 1))
        store(n_sub - 1, ys.pop(n_sub - 1))


def _moe_call(plan, xs, wg, wu, wd):
    R = xs.shape[0]
    half = D_MODEL // 2

    def outer(te_ref, nt_ref, nx_ref, seg_ref, x_hbm, wg_hbm, wu_hbm, wd_hbm, o_hbm,
              wg_scr, wu_scr, wd_scr, wg_buf, wu_buf, wd_buf, sem, step_ref):
        row_map = lambda i: (jnp.minimum(i, nt_ref[0] - 1), 0)
        step_ref[0] = 0

        def tile(x_ref, o_ref):
            i = step_ref[0]
            step_ref[0] = i + 1
            _moe_tile(i, te_ref, nt_ref, nx_ref, seg_ref, x_ref, wg_hbm, wu_hbm, wd_hbm, o_ref,
                      wg_scr, wu_scr, wd_scr, wg_buf, wu_buf, wd_buf, sem)

        pltpu.emit_pipeline(
            tile, grid=(R // MOE_RT,),
            in_specs=[pl.BlockSpec((MOE_RT, half), row_map,
                                   pipeline_mode=pl.Buffered(MOE_IN_BUFFERS))],
            out_specs=[pl.BlockSpec((MOE_RT, half), row_map)],
        )(x_hbm, o_hbm)

    hbm = pl.BlockSpec(memory_space=pl.ANY)
    smem = pl.BlockSpec(memory_space=pltpu.SMEM)
    return pl.pallas_call(
        outer,
        out_shape=jax.ShapeDtypeStruct((R, half), jnp.int32),
        in_specs=[smem] * 4 + [hbm] * 4,
        out_specs=hbm,
        scratch_shapes=[pltpu.VMEM((D_MODEL, EXPERT_HIDDEN), BF16),
                        pltpu.VMEM((D_MODEL, EXPERT_HIDDEN), BF16),
                        pltpu.VMEM((EXPERT_HIDDEN, D_MODEL), BF16),
                        pltpu.VMEM((2, D_MODEL, EXPERT_HIDDEN), F32),
                        pltpu.VMEM((2, D_MODEL, EXPERT_HIDDEN), F32),
                        pltpu.VMEM((2, EXPERT_HIDDEN, D_MODEL), F32),
                        pltpu.SemaphoreType.DMA((2, 3)),
                        pltpu.SMEM((1,), jnp.int32)],
        compiler_params=pltpu.CompilerParams(vmem_limit_bytes=VMEM_LIMIT),
        name="moe_experts",
    )(*plan, xs, wg, wu, wd)


def _moe_plan(route_t, counts, n_rows):
    cnt = counts[0, :N_EXPERTS].astype(jnp.int32)
    tiles = (cnt + MOE_RT - 1) // MOE_RT
    tile_end = jnp.cumsum(tiles)
    n_tiles = tile_end[-1:]
    row_start = (tile_end - tiles) * MOE_RT
    ids = route_t[0:2].astype(jnp.int32)
    ranks = route_t[4:6].astype(jnp.int32)
    experts = jnp.arange(N_EXPERTS, dtype=jnp.int32)[:, None, None]
    pos = ranks + jnp.sum(jnp.where(ids[None] == experts, row_start[:, None, None], 0), axis=0)
    tile_id = jnp.minimum(jnp.arange(n_rows // MOE_RT, dtype=jnp.int32), n_tiles - 1)
    tile_expert = jnp.sum((tile_id[:, None] >= tile_end[None, :]).astype(jnp.int32), axis=1)
    e_ids = jnp.arange(N_EXPERTS, dtype=jnp.int32)
    later = (e_ids[None, :] > e_ids[:, None]) & (tiles[None, :] > 0)
    nxt = jnp.min(jnp.where(later, e_ids[None, :], N_EXPERTS), axis=1)
    nxt = jnp.where(nxt == N_EXPERTS, -1, nxt)
    seg = jnp.cumsum((tiles > 0).astype(jnp.int32)) - 1
    pick = tile_expert[:, None] == e_ids[None, :]
    tile_next = jnp.sum(jnp.where(pick, nxt[None, :], 0), axis=1)
    tile_seg = jnp.sum(jnp.where(pick, seg[None, :], 0), axis=1)
    return pos, (tile_expert, n_tiles, tile_next, tile_seg)


PLE_TM = 1024
PLE_SUB = 256


def _ple_body(x_ref, ya_ref, yb_ref, route_ref, p_ref, g3_ref, wpg_ref, bpg_ref, wple_ref, gf_ref,
              o_ref):
    n_sub = x_ref.shape[0] // PLE_SUB
    g3, gf, bpg = g3_ref[...], gf_ref[...], bpg_ref[...]

    def head(s):
        rows = pl.ds(s * PLE_SUB, PLE_SUB)
        ya = jnp.concatenate(_unpack_bf16_pairs(ya_ref[rows, :]), axis=1)
        yb = jnp.concatenate(_unpack_bf16_pairs(yb_ref[rows, :]), axis=1)
        route = route_ref[rows, :]
        x2 = x_ref[rows, :] + route[:, 2:3] * ya + route[:, 3:4] * yb
        return x2, _rms(x2, g3).astype(BF16)

    def dots(s, h3):
        rows = pl.ds(s * PLE_SUB, PLE_SUB)
        zg = jnp.dot(h3, wpg_ref[...], preferred_element_type=F32)
        pe = jnp.dot(p_ref[rows, :].astype(BF16), wple_ref[...], preferred_element_type=F32)
        return zg, pe

    def tail(s, x2, zg, pe):
        x3 = x2 + jax.nn.sigmoid(zg + bpg) * pe
        o_ref[pl.ds(s * PLE_SUB, PLE_SUB), :] = _rms(x3, gf)

    x2s, mm = {}, {}
    x2s[0], h3 = head(0)
    for s in range(n_sub):
        mm[s] = dots(s, h3)
        if s + 1 < n_sub:
            x2s[s + 1], h3 = head(s + 1)
        if s >= 1:
            tail(s - 1, x2s.pop(s - 1), *mm.pop(s - 1))
    tail(n_sub - 1, x2s.pop(n_sub - 1), *mm.pop(n_sub - 1))


def _ple_call(x1, y_picks, route, p, g3, wpg, bpg, wple, gf):
    T = x1.shape[0]
    nt = T // PLE_TM
    consts = (g3, wpg, bpg, wple, gf)
    tok = lambda i: (i, 0)
    return pl.pallas_call(
        _ple_body,
        out_shape=jax.ShapeDtypeStruct((T, D_MODEL), F32),
        grid=(nt,),
        in_specs=[pl.BlockSpec((PLE_TM, D_MODEL), tok),
                  pl.BlockSpec((PLE_TM, D_MODEL // 2), tok),
                  pl.BlockSpec((PLE_TM, D_MODEL // 2), lambda i: (i + nt, 0)),
                  pl.BlockSpec((PLE_TM, LANES), tok),
                  pl.BlockSpec((PLE_TM, p.shape[1]), tok)]
                 + [_const_spec(c.shape) for c in consts],
        out_specs=pl.BlockSpec((PLE_TM, D_MODEL), tok),
        compiler_params=pltpu.CompilerParams(
            dimension_semantics=("arbitrary",), vmem_limit_bytes=VMEM_LIMIT),
        name="ple_final",
    )(x1, y_picks, y_picks, route, p, *consts)


def _layer(x, p, norm_mix, w_in, b_in, lam_re, lam_im, log_dt, b_re, b_im, c_re, c_im, d_skip,
           w_glu_a, w_glu_b, conv_w, conv_b, w_conv_out, w_o, norm_ffn, w_rg, b_rg, w_re, b_re_r,
           w_eg, w_eu, w_ed, norm_ple, w_ple, w_pg, b_pg, norm_out):
    nb, seq, d = x.shape
    T = nb * seq
    row = lambda v: v.reshape(1, -1).astype(F32)
    assert w_in.shape[1] == SSM_WIDTH + 3 * CONV_WIDTH + 2 * D_MODEL and SSM_WIDTH == CONV_WIDTH
    w_in16 = w_in.astype(BF16)
    b_in_row = row(b_in)

    kmat, mmat, nmat, aq_re, aq_im = _s5_operators(lam_re, lam_im, log_dt, b_re, b_im, c_re, c_im)
    s5_consts = (row(norm_mix), w_in16, b_in_row, mmat, kmat, nmat, aq_re, aq_im, row(d_skip))

    lane_pad = LANES - N_EXPERTS - N_EXPERT_GROUPS
    w_r = jnp.pad(jnp.concatenate([w_re, w_rg], axis=1).astype(F32), ((0, 0), (0, lane_pad)))
    b_r = jnp.pad(jnp.concatenate([b_re_r, b_rg]).astype(F32), (0, lane_pad)).reshape(1, LANES)
    w_r_hi = w_r.astype(BF16)
    w_r_lo = (w_r - w_r_hi.astype(F32)).astype(BF16)

    mix_consts = (w_in16, b_in_row, row(norm_mix),
                  conv_w.astype(F32), row(conv_b), w_conv_out.astype(BF16),
                  jnp.concatenate([w_glu_a, w_glu_b], axis=1).astype(BF16), w_o.astype(BF16),
                  row(norm_ffn), jnp.concatenate([w_r_hi, w_r_lo], axis=1), b_r)
    ple_consts = (row(norm_ple), w_pg.astype(BF16), row(b_pg), w_ple.astype(BF16), row(norm_out))
    p2d = p.reshape(T, -1)
    n_rows = 2 * T + N_EXPERTS * MOE_RT

    ys = _s5_call(x, *s5_consts)
    x1, h2p, route, route_t, counts = _mix_call(x, ys, *mix_consts)
    pos, plan = _moe_plan(route_t, counts, n_rows)
    xs = _sc_dispatch(h2p, pos[0], pos[1], n_rows)
    ysort = _moe_call(plan, xs, w_eg, w_eu, w_ed)
    y_picks = _sc_gather(ysort, pos.reshape(-1))
    out = _ple_call(x1, y_picks, route, p2d, *ple_consts)
    return out.reshape(nb, seq, d)


def kernel(x, p, norm_mix, w_in, b_in, ssm_lam_re, ssm_lam_im, ssm_log_dt, ssm_b_re, ssm_b_im, ssm_c_re, ssm_c_im, ssm_d, w_glu_a, w_glu_b, conv_w, conv_b, w_conv_out, w_o, norm_ffn, w_router_group, b_router_group, w_router_expert, b_router_expert, w_exp_gate, w_exp_up, w_exp_down, norm_ple, w_ple, w_ple_gate, b_ple_gate, norm_final):
    assert p.shape[0] == 1, "the final RMSNorm is fused into the (single) layer's last kernel"
    i = 0
    return _layer(x, p[i], norm_mix[i], w_in[i], b_in[i], ssm_lam_re[i], ssm_lam_im[i],
                  ssm_log_dt[i], ssm_b_re[i], ssm_b_im[i], ssm_c_re[i], ssm_c_im[i], ssm_d[i],
                  w_glu_a[i], w_glu_b[i], conv_w[i], conv_b[i], w_conv_out[i], w_o[i],
                  norm_ffn[i], w_router_group[i], b_router_group[i], w_router_expert[i],
                  b_router_expert[i], w_exp_gate[i], w_exp_up[i], w_exp_down[i], norm_ple[i],
                  w_ple[i], w_ple_gate[i], b_ple_gate[i], norm_final)
```

```python
import functools
import math

import jax
import jax.numpy as jnp
from jax import lax
from jax.experimental import pallas as pl
from jax.experimental.pallas import tpu as pltpu
from jax.experimental.pallas import tpu_sc as plsc

F32 = jnp.float32
BF16 = jnp.bfloat16

D_MODEL = 1024
SSM_WIDTH = 512
SSM_GROUP = 16
SSM_GROUPS = 32
SSM_STATE = 64
CONV_WIDTH = 512
N_EXPERT_GROUPS = 4
EXPERTS_PER_GROUP = 8
N_EXPERTS = 32
EXPERT_HIDDEN = 256
NORM_EPS = 1e-6

LANES = 128
Q = 8
GROUPS_PER_LANE_TILE = LANES // SSM_GROUP
N_LANE_TILES = SSM_WIDTH // LANES
STATE_LANES = GROUPS_PER_LANE_TILE * SSM_STATE
S5_ROWS = 1024
VMEM_LIMIT = 56 * 1024 * 1024


def _rms(x, g):
    return x * lax.rsqrt(jnp.mean(x * x, axis=-1, keepdims=True) + NORM_EPS) * g


def _pack_bf16_pairs(a):
    w = a.shape[1] // 2
    lo = lax.shift_right_logical(lax.bitcast_convert_type(a[:, :w], jnp.int32), 16)
    hi = lax.bitcast_convert_type(a[:, w:], jnp.int32) & jnp.int32(-65536)
    return lo | hi


def _unpack_bf16_pairs(word):
    lo = lax.bitcast_convert_type(lax.shift_left(word, 16), F32)
    hi = lax.bitcast_convert_type(word & jnp.int32(-65536), F32)
    return lo, hi


def _const_spec(shape):
    n = len(shape)
    return pl.BlockSpec(shape, lambda *_: (0,) * n, pipeline_mode=pl.Buffered(1))


def _s5_operators(lam_re, lam_im, log_dt, b_re, b_im, c_re, c_im):
    G, P, H = SSM_GROUPS, SSM_STATE, SSM_GROUP
    J = N_LANE_TILES
    lr = lam_re.astype(F32)
    li = lam_im.astype(F32)
    dt = jnp.exp(log_dt.astype(F32))[:, None]

    def apow(n):
        n = n.astype(F32)[:, None, None]
        mag = jnp.exp(lr * dt * n)
        ang = li * dt * n
        return mag * jnp.cos(ang), mag * jnp.sin(ang)

    a1_re, a1_im = apow(jnp.ones((1,), F32))
    nr = a1_re[0] - 1.0
    ni = a1_im[0]
    den = lr * lr + li * li
    f_re = (nr * lr + ni * li) / den
    f_im = (ni * lr - nr * li) / den
    br = b_re.astype(F32)
    bi = b_im.astype(F32)
    bbar_re = f_re[:, :, None] * br - f_im[:, :, None] * bi
    bbar_im = f_re[:, :, None] * bi + f_im[:, :, None] * br
    to_rows = lambda v, perm: jnp.transpose(v, perm).reshape(H, G * P)
    bt_re = to_rows(bbar_re, (2, 0, 1))
    bt_im = to_rows(bbar_im, (2, 0, 1))
    ct_re = to_rows(c_re.astype(F32), (1, 0, 2))
    ct_im = to_rows(c_im.astype(F32), (1, 0, 2))
    ap_re, ap_im = apow(jnp.arange(Q + 1))
    ap_re = ap_re.reshape(Q + 1, G * P)
    ap_im = ap_im.reshape(Q + 1, G * P)

    blk = lambda r: pl.BlockSpec((r, STATE_LANES), lambda j: (0, j))
    mat = lambda r, c: pl.BlockSpec((1, r, c), lambda j: (j, 0, 0))
    shape = lambda r, c: jax.ShapeDtypeStruct((J, r, c), BF16)
    qx, st = Q * LANES, 2 * STATE_LANES
    kmat, mmat, nmat = pl.pallas_call(
        _s5_ops_body,
        out_shape=(shape(qx, qx), shape(qx, st), shape(st, qx)),
        grid=(J,),
        in_specs=[blk(Q + 1), blk(Q + 1), blk(H), blk(H), blk(H), blk(H)],
        out_specs=(mat(qx, qx), mat(qx, st), mat(st, qx)),
        compiler_params=pltpu.CompilerParams(
            dimension_semantics=("arbitrary",), vmem_limit_bytes=VMEM_LIMIT),
        name="s5_operators",
    )(ap_re, ap_im, bt_re, bt_im, ct_re, ct_im)
    return kmat, mmat, nmat, ap_re[Q:Q + 1], ap_im[Q:Q + 1]


def _s5_ops_body(apr_ref, api_ref, btr_ref, bti_ref, ctr_ref, cti_ref, k_ref, m_ref, n_ref):
    ri = lax.broadcasted_iota(jnp.int32, (LANES, STATE_LANES), 0)
    li = lax.broadcasted_iota(jnp.int32, (LANES, STATE_LANES), 1)
    same_group = (ri // SSM_GROUP) == (li // SSM_STATE)

    def expand(ref):
        tiled = jnp.concatenate([ref[...]] * GROUPS_PER_LANE_TILE, axis=0)
        return jnp.where(same_group, tiled, 0.0)

    b_re, b_im, c_re, c_im = expand(btr_ref), expand(bti_ref), expand(ctr_ref), expand(cti_ref)

    def cmul(n, x_re, x_im):
        a_re = apr_ref[n:n + 1, :]
        a_im = api_ref[n:n + 1, :]
        return a_re * x_re - a_im * x_im, a_re * x_im + a_im * x_re

    m_blocks = []
    for k in range(Q):
        g_re, g_im = cmul(Q - 1 - k, b_re, b_im)
        m_blocks.append(jnp.concatenate([g_re, g_im], axis=1))
    m = jnp.concatenate(m_blocks, axis=0)

    nt_blocks = []
    for t in range(Q):
        g_re, g_im = cmul(t + 1, c_re, c_im)
        nt_blocks.append(jnp.concatenate([g_re, -g_im], axis=1))
    nt = jnp.concatenate(nt_blocks, axis=0)

    n0t = jnp.concatenate([c_re, -c_im], axis=1)
    p = lax.dot_general(m, n0t, (((1,), (1,)), ((), ())),
                        precision=lax.Precision.HIGHEST, preferred_element_type=F32)
    zeros = jnp.zeros((LANES, LANES), F32)
    cols = []
    for t in range(Q):
        cols.append(jnp.concatenate(
            [p[(Q - 1 - (t - k)) * LANES:(Q - (t - k)) * LANES, :] if t >= k else zeros
             for k in range(Q)], axis=0))
    k_ref[0] = jnp.concatenate(cols, axis=1).astype(BF16)
    m_ref[0] = m.astype(BF16)
    n_ref[0] = nt.T.astype(BF16)


def _s5_body(x_ref, g_ref, wu_ref, bu_ref, m_ref, k_ref, n_ref, aqr_ref, aqi_ref, d_ref,
             o_ref, u_scr, y_scr, z_scr, ss_scr, carry_scr):
    nb, tt = x_ref.shape[0], x_ref.shape[1]
    rows = nb * tt
    nchunk = rows // Q

    @pl.when(pl.program_id(0) == 0)
    def _():
        carry_scr[...] = jnp.zeros_like(carry_scr)

    hb = nb // 2
    for r in range(2):
        x = x_ref[r * hb:(r + 1) * hb].reshape(rows // 2, D_MODEL)
        h = _rms(x, g_ref[...]).astype(BF16)
        u = jnp.dot(h, wu_ref[...], preferred_element_type=F32) + bu_ref[...]
        for j in range(N_LANE_TILES):
            u_scr[j, pl.ds(r * (rows // 2), rows // 2), :] = u[:, j * LANES:(j + 1) * LANES]

    n_st = STATE_LANES // LANES
    cpt = tt // Q

    lane_tiles = range(N_LANE_TILES)


    xs = [jnp.concatenate([u_scr[j, pl.ds(k, nchunk, stride=Q), :] for k in range(Q)],
                          axis=1).astype(BF16) for j in lane_tiles]
    for j in lane_tiles:
        z = jnp.dot(xs[j], m_ref[j], preferred_element_type=F32)
        for i in range(2 * n_st):
            z_scr[j, i] = z[:, i * LANES:(i + 1) * LANES]
    y_intra = [jnp.dot(xs[j], k_ref[j], preferred_element_type=F32) for j in lane_tiles]

    aq = [(jnp.broadcast_to(aqr_ref[:, pl.ds(j * STATE_LANES, STATE_LANES)], (nb, STATE_LANES)),
           jnp.broadcast_to(aqi_ref[:, pl.ds(j * STATE_LANES, STATE_LANES)], (nb, STATE_LANES)))
          for j in lane_tiles]
    st = [(carry_scr[j, :, pl.ds(0, STATE_LANES)], carry_scr[j, :, pl.ds(STATE_LANES, STATE_LANES)])
          for j in lane_tiles]
    for c in range(cpt):
        seq_rows = pl.ds(c, nb, stride=cpt)
        for j in lane_tiles:
            s_re, s_im = st[j]
            aqr, aqi = aq[j]
            for i in range(n_st):
                ss_scr[j, i, seq_rows, :] = s_re[:, i * LANES:(i + 1) * LANES]
                ss_scr[j, n_st + i, seq_rows, :] = s_im[:, i * LANES:(i + 1) * LANES]
            z_re = jnp.concatenate([z_scr[j, i, seq_rows, :] for i in range(n_st)], axis=1)
            z_im = jnp.concatenate([z_scr[j, n_st + i, seq_rows, :] for i in range(n_st)], axis=1)
            st[j] = (aqr * s_re - aqi * s_im + z_re, aqr * s_im + aqi * s_re + z_im)
    for j in lane_tiles:
        carry_scr[j, :, pl.ds(0, STATE_LANES)] = st[j][0]
        carry_scr[j, :, pl.ds(STATE_LANES, STATE_LANES)] = st[j][1]

    def state_to_output(j):
        ss = jnp.concatenate([ss_scr[j, i] for i in range(2 * n_st)], axis=1).astype(BF16)
        return y_intra[j] + jnp.dot(ss, n_ref[j], preferred_element_type=F32)

    def finish(j, yj):
        for k in range(Q):
            y_scr[j, pl.ds(k, nchunk, stride=Q), :] = yj[:, k * LANES:(k + 1) * LANES]
        lanes = pl.ds(j * LANES, LANES)
        y = y_scr[j] + d_ref[:, lanes] * u_scr[j]
        o_ref[:, :, lanes] = jax.nn.gelu(y).astype(BF16).reshape(nb, tt, LANES)

    yj = state_to_output(0)
    for j in lane_tiles:
        y_next = state_to_output(j + 1) if j + 1 < N_LANE_TILES else None
        finish(j, yj)
        yj = y_next


def _col_block(rows, width, block):
    return pl.BlockSpec((rows, width), lambda *_: (0, block), pipeline_mode=pl.Buffered(1))


def _s5_call(x, g, w_in, b_in, mmat, kmat, nmat, aq_re, aq_im, d_skip):
    nb, seq = x.shape[0], x.shape[1]
    tt = S5_ROWS // nb
    rows = S5_ROWS
    nchunk = rows // Q
    return pl.pallas_call(
        _s5_body,
        out_shape=jax.ShapeDtypeStruct((nb, seq, SSM_WIDTH), BF16),
        grid=(seq // tt,),
        in_specs=[
            pl.BlockSpec((nb, tt, D_MODEL), lambda i: (0, i, 0)),
            _const_spec(g.shape), _col_block(D_MODEL, SSM_WIDTH, 0), _col_block(1, SSM_WIDTH, 0),
            _const_spec(mmat.shape), _const_spec(kmat.shape), _const_spec(nmat.shape),
            _const_spec(aq_re.shape), _const_spec(aq_im.shape), _const_spec(d_skip.shape),
        ],
        out_specs=pl.BlockSpec((nb, tt, SSM_WIDTH), lambda i: (0, i, 0)),
        scratch_shapes=[
            pltpu.VMEM((N_LANE_TILES, rows, LANES), F32),
            pltpu.VMEM((N_LANE_TILES, rows, LANES), F32),
            pltpu.VMEM((N_LANE_TILES, 2 * STATE_LANES // LANES, nchunk, LANES), F32),
            pltpu.VMEM((N_LANE_TILES, 2 * STATE_LANES // LANES, nchunk, LANES), F32),
            pltpu.VMEM((N_LANE_TILES, nb, 2 * STATE_LANES), F32),
        ],
        compiler_params=pltpu.CompilerParams(
            dimension_semantics=("arbitrary",), vmem_limit_bytes=VMEM_LIMIT),
        name="s5_mixer",
    )(x, g, w_in, b_in, mmat, kmat, nmat, aq_re, aq_im, d_skip)


MIX_TM = 512


def _mix_body(x_ref, ys_ref, wcb_ref, wcc_ref, wcv_ref, wg_ref, bcb_ref, bcc_ref, bcv_ref, bg_ref,
              g_ref, cw_ref, cb_ref, wco_ref, wab_ref, wo_ref, g2_ref, wr_ref, br_ref,
              x1_ref, h2p_ref, route_ref, route_t_ref, cnt_ref, carry_scr, cnt_scr, x1_scr,
              *, tiles_per_seq, n_tiles):
    i = pl.program_id(0)
    tm = x_ref.shape[1]
    wc_refs = (wcb_ref, wcc_ref, wcv_ref)
    bc_refs = (bcb_ref, bcc_ref, bcv_ref)

    @pl.when(i == 0)
    def _():
        cnt_scr[...] = jnp.zeros_like(cnt_scr)
        x1_scr[...] = jnp.zeros_like(x1_scr)
        carry_scr[...] = jnp.zeros_like(carry_scr)

    carry_scr[0] = jnp.where(i == n_tiles, carry_scr[0],
                             jnp.where(i % tiles_per_seq == 0, 0.0, carry_scr[1]))

    yab = jnp.dot(ys_ref[0], wab_ref[...], preferred_element_type=F32)

    h2_hi = _rms(x1_scr[...], g2_ref[...]).astype(BF16)
    h2p_ref[...] = _pack_bf16_pairs(h2_hi.astype(F32))
    lg2 = jnp.dot(h2_hi, wr_ref[...], preferred_element_type=F32)
    logits = lg2[:, 0:LANES] + lg2[:, LANES:2 * LANES] + br_ref[...]

    x = x_ref[0]
    h = _rms(x, g_ref[...]).astype(BF16)
    c_b, c_c, c_v = [jnp.dot(h, w[...], preferred_element_type=F32) + b[...]
                     for w, b in zip(wc_refs, bc_refs)]
    zg = jnp.dot(h, wg_ref[...], preferred_element_type=F32) + bg_ref[...]

    cnt = cnt_scr[...]
    route, new_cnt = _route_tile(logits, cnt)
    cnt = jnp.where(i > 0, new_cnt, cnt)
    cnt_scr[...] = cnt
    cnt_ref[...] = cnt
    route_ref[...] = route
    route_t_ref[...] = route.T[0:8, :]

    y_a = yab[:, 0:D_MODEL] * jax.nn.sigmoid(yab[:, D_MODEL:2 * D_MODEL])

    cv = c_c * c_v
    row = lax.broadcasted_iota(jnp.int32, (tm, CONV_WIDTH), 0)
    last1 = carry_scr[0, 7:8, :]
    last2 = carry_scr[0, 6:7, :]
    p1 = jnp.where(row == 0, last1, pltpu.roll(cv, 1, axis=0))
    p2 = jnp.where(row == 0, last2, jnp.where(row == 1, last1, pltpu.roll(cv, 2, axis=0)))
    carry_scr[1] = cv[tm - 8:tm, :]
    conv = cw_ref[0:1, :] * p2 + cw_ref[1:2, :] * p1 + cw_ref[2:3, :] * cv + cb_ref[...]
    y_b = jnp.dot((c_b * conv).astype(BF16), wco_ref[...], preferred_element_type=F32)

    mix = (jax.nn.sigmoid(zg[:, 0:D_MODEL]) * y_a
           + jax.nn.sigmoid(zg[:, D_MODEL:2 * D_MODEL]) * y_b)
    x1 = x + jnp.dot(mix.astype(BF16), wo_ref[...], preferred_element_type=F32)
    x1_ref[...] = x1
    x1_scr[...] = x1


def _route_tile(logits, cnt):
    tm = logits.shape[0]
    lane = lax.broadcasted_iota(jnp.int32, (tm, LANES), 1)
    neg = jnp.float32(-jnp.inf)
    big = jnp.int32(1 << 20)
    is_g = (lane >= N_EXPERTS) & (lane < N_EXPERTS + N_EXPERT_GROUPS)
    gl = jnp.where(is_g, logits, neg)
    gmax = jnp.max(gl, axis=1, keepdims=True)
    g_w = 1.0 / jnp.sum(jnp.exp(gl - gmax), axis=1, keepdims=True)
    g_idx = jnp.min(jnp.where(gl == gmax, lane - N_EXPERTS, big), axis=1, keepdims=True)
    lo = g_idx * EXPERTS_PER_GROUP
    el = jnp.where((lane >= lo) & (lane < lo + EXPERTS_PER_GROUP), logits, neg)
    m1 = jnp.max(el, axis=1, keepdims=True)
    i1 = jnp.min(jnp.where(el == m1, lane, big), axis=1, keepdims=True)
    el2 = jnp.where(lane == i1, neg, el)
    m2 = jnp.max(el2, axis=1, keepdims=True)
    i2 = jnp.min(jnp.where(el2 == m2, lane, big), axis=1, keepdims=True)
    r = jnp.exp(m2 - m1)
    w1 = g_w / (1.0 + r)
    w2 = g_w * r / (1.0 + r)

    picks = ((lane == i1) | (lane == i2)).astype(BF16)
    r_i = lax.broadcasted_iota(jnp.int32, (tm, tm), 0)
    c_i = lax.broadcasted_iota(jnp.int32, (tm, tm), 1)
    before = (c_i < r_i).astype(BF16)
    excl = jnp.dot(before, picks, preferred_element_type=F32) + cnt
    rank1 = jnp.sum(jnp.where(lane == i1, excl, 0.0), axis=1, keepdims=True)
    rank2 = jnp.sum(jnp.where(lane == i2, excl, 0.0), axis=1, keepdims=True)
    new_cnt = cnt + jnp.sum(picks.astype(F32), axis=0, keepdims=True)
    route = jnp.where(lane == 0, i1.astype(F32), 0.0)
    route = jnp.where(lane == 1, i2.astype(F32), route)
    route = jnp.where(lane == 2, w1, route)
    route = jnp.where(lane == 3, w2, route)
    route = jnp.where(lane == 4, rank1, route)
    route = jnp.where(lane == 5, rank2, route)
    return route, new_cnt


def _mix_call(x, ys, w_in, b_in, g, cw, cb, wco, wab, wo, g2, wr, br):
    nb, seq, _ = ys.shape
    T = nb * seq
    nl = seq // MIX_TM
    n = nb * nl
    consts = (g, cw, cb, wco, wab, wo, g2, wr, br)
    gates = 2 * D_MODEL
    in_proj_specs = ([_col_block(D_MODEL, CONV_WIDTH, b) for b in (1, 2, 3)]
                     + [_col_block(D_MODEL, gates, 1)]
                     + [_col_block(1, CONV_WIDTH, b) for b in (1, 2, 3)]
                     + [_col_block(1, gates, 1)])
    cur = lambda i: jnp.minimum(i, n - 1)
    prev = lambda i: jnp.maximum(i - 1, 0)
    return pl.pallas_call(
        functools.partial(_mix_body, tiles_per_seq=nl, n_tiles=n),
        out_shape=(jax.ShapeDtypeStruct((T, D_MODEL), F32),
                   jax.ShapeDtypeStruct((T, D_MODEL // 2), jnp.int32),
                   jax.ShapeDtypeStruct((T, LANES), F32),
                   jax.ShapeDtypeStruct((8, T), F32),
                   jax.ShapeDtypeStruct((1, LANES), F32)),
        grid=(n + 1,),
        in_specs=[pl.BlockSpec((1, MIX_TM, D_MODEL),
                               lambda i: (cur(i) // nl, cur(i) % nl, 0)),
                  pl.BlockSpec((1, MIX_TM, SSM_WIDTH), lambda i: (cur(i) // nl, cur(i) % nl, 0))]
                 + in_proj_specs + [_const_spec(c.shape) for c in consts],
        out_specs=(pl.BlockSpec((MIX_TM, D_MODEL), lambda i: (cur(i), 0)),
                   pl.BlockSpec((MIX_TM, D_MODEL // 2), lambda i: (prev(i), 0)),
                   pl.BlockSpec((MIX_TM, LANES), lambda i: (prev(i), 0)),
                   pl.BlockSpec((8, MIX_TM), lambda i: (0, prev(i))),
                   pl.BlockSpec((1, LANES), lambda i: (0, 0))),
        scratch_shapes=[pltpu.VMEM((2, 8, CONV_WIDTH), F32), pltpu.VMEM((1, LANES), F32),
                        pltpu.VMEM((MIX_TM, D_MODEL), F32)],
        compiler_params=pltpu.CompilerParams(
            dimension_semantics=("arbitrary",), vmem_limit_bytes=VMEM_LIMIT),
        name="conv_glu_router",
    )(x, ys, *([w_in] * 4), *([b_in] * 4), *consts)


SC_CORES = 2
SC_SUBCORES = 16
SC_WORKERS = SC_CORES * SC_SUBCORES
SC_ROWS = 64
SC_NBUF = 2


def _sc_mesh():
    return plsc.VectorSubcoreMesh(core_axis_name="c", subcore_axis_name="s")


def _sc_worker_id():
    return lax.axis_index("s") * SC_CORES + lax.axis_index("c")


def _sc_ring(nch, get, puts):
    for b in range(SC_NBUF - 1):
        get(b, b).start()

    @pl.loop(0, nch, step=SC_NBUF)
    def _(j0):
        for b in range(SC_NBUF):
            j = j0 + b
            refill = (b - 1) % SC_NBUF
            get(j, b).wait()

            @pl.when(j + SC_NBUF - 1 < nch)
            def _():
                @pl.when(j >= 1)
                def _():
                    for c in puts(j - 1, refill):
                        c.wait()
                get(j + SC_NBUF - 1, refill).start()

            for c in puts(j, b):
                c.start()

    for b in range(SC_NBUF):
        for c in puts(nch - SC_NBUF + b, b):
            c.wait()


def _sc_dispatch(rows, pos_a, pos_b, n_out):
    T, W = rows.shape
    per_w = T // SC_WORKERS
    nch = per_w // SC_ROWS
    assert per_w * SC_WORKERS == T and nch * SC_ROWS == per_w and nch % SC_NBUF == 0
    idx_a = pos_a.reshape(SC_WORKERS, nch, SC_ROWS)
    idx_b = pos_b.reshape(SC_WORKERS, nch, SC_ROWS)

    @functools.partial(
        pl.kernel, mesh=_sc_mesh(),
        out_type=jax.ShapeDtypeStruct((n_out, W), rows.dtype),
        scratch_types=[
            pltpu.VMEM((nch, SC_ROWS), jnp.int32),
            pltpu.VMEM((nch, SC_ROWS), jnp.int32),
            pltpu.VMEM((SC_NBUF, SC_ROWS, W), rows.dtype),
            pltpu.SemaphoreType.DMA((SC_NBUF,)),
            pltpu.SemaphoreType.DMA((SC_NBUF,)),
            pltpu.SemaphoreType.DMA((SC_NBUF,)),
        ],
        name="moe_dispatch",
    )
    def k(rows_hbm, ia_hbm, ib_hbm, out_hbm, ia_v, ib_v, buf, gsem, asem, bsem):
        wid = _sc_worker_id()
        base = wid * per_w
        pltpu.sync_copy(ia_hbm.at[wid], ia_v)
        pltpu.sync_copy(ib_hbm.at[wid], ib_v)

        def get(j, b):
            return pltpu.make_async_copy(
                rows_hbm.at[pl.ds(base + j * SC_ROWS, SC_ROWS)], buf.at[b], gsem.at[b])

        def put_a(j, b):
            return pltpu.make_async_copy(buf.at[b], out_hbm.at[ia_v.at[j]], asem.at[b])

        def put_b(j, b):
            return pltpu.make_async_copy(buf.at[b], out_hbm.at[ib_v.at[j]], bsem.at[b])

        _sc_ring(nch, get, lambda j, b: (put_a(j, b), put_b(j, b)))

    return k(rows, idx_a, idx_b)


def _sc_gather(table, idx):
    _, W = table.shape
    B = idx.shape[0]
    per_w = B // SC_WORKERS
    nch = per_w // SC_ROWS
    assert per_w * SC_WORKERS == B and nch * SC_ROWS == per_w and nch % SC_NBUF == 0
    idx3 = idx.reshape(SC_WORKERS, nch, SC_ROWS)

    @functools.partial(
        pl.kernel, mesh=_sc_mesh(),
        out_type=jax.ShapeDtypeStruct((B, W), table.dtype),
        scratch_types=[
            pltpu.VMEM((nch, SC_ROWS), jnp.int32),
            pltpu.VMEM((SC_NBUF, SC_ROWS, W), table.dtype),
            pltpu.SemaphoreType.DMA((SC_NBUF,)),
            pltpu.SemaphoreType.DMA((SC_NBUF,)),
        ],
        name="moe_combine_gather",
    )
    def k(table_hbm, idx_hbm, out_hbm, idx_v, buf, gsem, osem):
        wid = _sc_worker_id()
        base = wid * per_w
        pltpu.sync_copy(idx_hbm.at[wid], idx_v)

        def get(j, b):
            return pltpu.make_async_copy(table_hbm.at[idx_v.at[j]], buf.at[b], gsem.at[b])

        def put(j, b):
            return pltpu.make_async_copy(
                buf.at[b], out_hbm.at[pl.ds(base + j * SC_ROWS, SC_ROWS)], osem.at[b])

        _sc_ring(nch, get, lambda j, b: (put(j, b),))

    return k(table, idx3)


MOE_RT = 1024
MOE_SUB = 256
MOE_IN_BUFFERS = 3


def _moe_tile(i, te_ref, nx_ref, seg_ref, ns_ref, x_ref, wg_hbm, wu_hbm, wd_hbm, o_ref,
              wg_scr, wu_scr, wd_scr, wg_buf, wu_buf, wd_buf, sem):
    expert = te_ref[i]
    slot = seg_ref[i] % 2

    def weight_copies(e, b):
        return (pltpu.make_async_copy(wg_hbm.at[e], wg_buf.at[b], sem.at[b, 0]),
                pltpu.make_async_copy(wu_hbm.at[e], wu_buf.at[b], sem.at[b, 1]),
                pltpu.make_async_copy(wd_hbm.at[e], wd_buf.at[b], sem.at[b, 2]))

    @pl.when(i == 0)
    def _():
        for c in weight_copies(expert, 0):
            c.start()

    @pl.when((i == 0) | (expert != te_ref[jnp.maximum(i - 1, 0)]))
    def _():
        for c in weight_copies(expert, slot):
            c.wait()
        wg_scr[...] = wg_buf[slot].astype(BF16)
        wu_scr[...] = wu_buf[slot].astype(BF16)
        wd_scr[...] = wd_buf[slot].astype(BF16)

        @pl.when(nx_ref[i] >= 0)
        def _():
            for c in weight_copies(nx_ref[i], 1 - slot):
                c.start()

    half = D_MODEL // 2

    def up_proj(s):
        lo, hi = _unpack_bf16_pairs(x_ref[pl.ds(s * MOE_SUB, MOE_SUB), :])
        lo = lo.astype(BF16)
        hi = hi.astype(BF16)
        gate = (jnp.dot(lo, wg_scr[0:half, :], preferred_element_type=F32)
                + jnp.dot(hi, wg_scr[half:D_MODEL, :], preferred_element_type=F32))
        up = (jnp.dot(lo, wu_scr[0:half, :], preferred_element_type=F32)
              + jnp.dot(hi, wu_scr[half:D_MODEL, :], preferred_element_type=F32))
        return gate, up

    def down_proj(gate, up):
        hid = (jax.nn.silu(gate) * up).astype(BF16)
        return jnp.dot(hid, wd_scr[...], preferred_element_type=F32)

    def store(s, y):
        o_ref[pl.ds(s * MOE_SUB, MOE_SUB), :] = _pack_bf16_pairs(y.astype(BF16).astype(F32))

    def run(n_sub):
        gu = {0: up_proj(0)}
        ys = {}
        for s in range(n_sub):
            if s + 1 < n_sub:
                gu[s + 1] = up_proj(s + 1)
            ys[s] = down_proj(*gu.pop(s))
            if s >= 1:
                store(s - 1, ys.pop(s - 1))
        store(n_sub - 1, ys.pop(n_sub - 1))

    for n_sub in range(1, x_ref.shape[0] // MOE_SUB + 1):
        pl.when(ns_ref[i] == n_sub)(functools.partial(run, n_sub))


def _moe_call(plan, xs, wg, wu, wd):
    R = xs.shape[0]
    half = D_MODEL // 2

    def outer(te_ref, nt_ref, nx_ref, seg_ref, ns_ref, x_hbm, wg_hbm, wu_hbm, wd_hbm, o_hbm,
              wg_scr, wu_scr, wd_scr, wg_buf, wu_buf, wd_buf, sem, step_ref):
        row_map = lambda i: (jnp.minimum(i, nt_ref[0] - 1), 0)
        step_ref[0] = 0

        def tile(x_ref, o_ref):
            i = step_ref[0]
            step_ref[0] = i + 1
            _moe_tile(i, te_ref, nx_ref, seg_ref, ns_ref, x_ref, wg_hbm, wu_hbm, wd_hbm, o_ref,
                      wg_scr, wu_scr, wd_scr, wg_buf, wu_buf, wd_buf, sem)

        pltpu.emit_pipeline(
            tile, grid=(R // MOE_RT,),
            in_specs=[pl.BlockSpec((MOE_RT, half), row_map,
                                   pipeline_mode=pl.Buffered(MOE_IN_BUFFERS))],
            out_specs=[pl.BlockSpec((MOE_RT, half), row_map)],
        )(x_hbm, o_hbm)

    hbm = pl.BlockSpec(memory_space=pl.ANY)
    smem = pl.BlockSpec(memory_space=pltpu.SMEM)
    return pl.pallas_call(
        outer,
        out_shape=jax.ShapeDtypeStruct((R, half), jnp.int32),
        in_specs=[smem] * 5 + [hbm] * 4,
        out_specs=hbm,
        scratch_shapes=[pltpu.VMEM((D_MODEL, EXPERT_HIDDEN), BF16),
                        pltpu.VMEM((D_MODEL, EXPERT_HIDDEN), BF16),
                        pltpu.VMEM((EXPERT_HIDDEN, D_MODEL), BF16),
                        pltpu.VMEM((2, D_MODEL, EXPERT_HIDDEN), F32),
                        pltpu.VMEM((2, D_MODEL, EXPERT_HIDDEN), F32),
                        pltpu.VMEM((2, EXPERT_HIDDEN, D_MODEL), F32),
                        pltpu.SemaphoreType.DMA((2, 3)),
                        pltpu.SMEM((1,), jnp.int32)],
        compiler_params=pltpu.CompilerParams(vmem_limit_bytes=VMEM_LIMIT),
        name="moe_experts",
    )(*plan, xs, wg, wu, wd)


def _moe_plan(route_t, counts, n_rows):
    cnt = counts[0, :N_EXPERTS].astype(jnp.int32)
    tiles = (cnt + MOE_RT - 1) // MOE_RT
    tile_end = jnp.cumsum(tiles)
    n_tiles = tile_end[-1:]
    row_start = (tile_end - tiles) * MOE_RT
    ids = route_t[0:2].astype(jnp.int32)
    ranks = route_t[4:6].astype(jnp.int32)
    experts = jnp.arange(N_EXPERTS, dtype=jnp.int32)[:, None, None]
    pos = ranks + jnp.sum(jnp.where(ids[None] == experts, row_start[:, None, None], 0), axis=0)
    tile_id = jnp.minimum(jnp.arange(n_rows // MOE_RT, dtype=jnp.int32), n_tiles - 1)
    tile_expert = jnp.sum((tile_id[:, None] >= tile_end[None, :]).astype(jnp.int32), axis=1)
    e_ids = jnp.arange(N_EXPERTS, dtype=jnp.int32)
    later = (e_ids[None, :] > e_ids[:, None]) & (tiles[None, :] > 0)
    nxt = jnp.min(jnp.where(later, e_ids[None, :], N_EXPERTS), axis=1)
    nxt = jnp.where(nxt == N_EXPERTS, -1, nxt)
    seg = jnp.cumsum((tiles > 0).astype(jnp.int32)) - 1
    pick = tile_expert[:, None] == e_ids[None, :]
    tile_next = jnp.sum(jnp.where(pick, nxt[None, :], 0), axis=1)
    tile_seg = jnp.sum(jnp.where(pick, seg[None, :], 0), axis=1)
    first = jnp.sum(jnp.where(pick, (tile_end - tiles)[None, :], 0), axis=1)
    rows_left = (jnp.sum(jnp.where(pick, cnt[None, :], 0), axis=1)
                 - (jnp.arange(n_rows // MOE_RT, dtype=jnp.int32) - first) * MOE_RT)
    tile_subs = (jnp.clip(rows_left, 0, MOE_RT) + MOE_SUB - 1) // MOE_SUB
    return pos, (tile_expert, n_tiles, tile_next, tile_seg, tile_subs)


PLE_TM = 1024
PLE_SUB = 256


def _ple_body(x_ref, ya_ref, yb_ref, route_ref, p_ref, g3_ref, wpg_ref, bpg_ref, wple_ref, gf_ref,
              o_ref):
    n_sub = x_ref.shape[0] // PLE_SUB
    g3, gf, bpg = g3_ref[...], gf_ref[...], bpg_ref[...]

    def head(s):
        rows = pl.ds(s * PLE_SUB, PLE_SUB)
        ya = jnp.concatenate(_unpack_bf16_pairs(ya_ref[rows, :]), axis=1)
        yb = jnp.concatenate(_unpack_bf16_pairs(yb_ref[rows, :]), axis=1)
        route = route_ref[rows, :]
        x2 = x_ref[rows, :] + route[:, 2:3] * ya + route[:, 3:4] * yb
        return x2, _rms(x2, g3).astype(BF16)

    def dots(s, h3):
        rows = pl.ds(s * PLE_SUB, PLE_SUB)
        zg = jnp.dot(h3, wpg_ref[...], preferred_element_type=F32)
        pe = jnp.dot(p_ref[rows, :].astype(BF16), wple_ref[...], preferred_element_type=F32)
        return zg, pe

    def tail(s, x2, zg, pe):
        x3 = x2 + jax.nn.sigmoid(zg + bpg) * pe
        o_ref[pl.ds(s * PLE_SUB, PLE_SUB), :] = _rms(x3, gf)

    x2s, mm = {}, {}
    x2s[0], h3 = head(0)
    for s in range(n_sub):
        mm[s] = dots(s, h3)
        if s + 1 < n_sub:
            x2s[s + 1], h3 = head(s + 1)
        if s >= 1:
            tail(s - 1, x2s.pop(s - 1), *mm.pop(s - 1))
    tail(n_sub - 1, x2s.pop(n_sub - 1), *mm.pop(n_sub - 1))


def _ple_call(x1, y_picks, route, p, g3, wpg, bpg, wple, gf):
    T = x1.shape[0]
    nt = T // PLE_TM
    consts = (g3, wpg, bpg, wple, gf)
    tok = lambda i: (i, 0)
    return pl.pallas_call(
        _ple_body,
        out_shape=jax.ShapeDtypeStruct((T, D_MODEL), F32),
        grid=(nt,),
        in_specs=[pl.BlockSpec((PLE_TM, D_MODEL), tok),
                  pl.BlockSpec((PLE_TM, D_MODEL // 2), tok),
                  pl.BlockSpec((PLE_TM, D_MODEL // 2), lambda i: (i + nt, 0)),
                  pl.BlockSpec((PLE_TM, LANES), tok),
                  pl.BlockSpec((PLE_TM, p.shape[1]), tok)]
                 + [_const_spec(c.shape) for c in consts],
        out_specs=pl.BlockSpec((PLE_TM, D_MODEL), tok),
        compiler_params=pltpu.CompilerParams(
            dimension_semantics=("arbitrary",), vmem_limit_bytes=VMEM_LIMIT),
        name="ple_final",
    )(x1, y_picks, y_picks, route, p, *consts)


def _layer(x, p, norm_mix, w_in, b_in, lam_re, lam_im, log_dt, b_re, b_im, c_re, c_im, d_skip,
           w_glu_a, w_glu_b, conv_w, conv_b, w_conv_out, w_o, norm_ffn, w_rg, b_rg, w_re, b_re_r,
           w_eg, w_eu, w_ed, norm_ple, w_ple, w_pg, b_pg, norm_out):
    nb, seq, d = x.shape
    T = nb * seq
    row = lambda v: v.reshape(1, -1).astype(F32)
    assert w_in.shape[1] == SSM_WIDTH + 3 * CONV_WIDTH + 2 * D_MODEL and SSM_WIDTH == CONV_WIDTH
    w_in16 = w_in.astype(BF16)
    b_in_row = row(b_in)

    kmat, mmat, nmat, aq_re, aq_im = _s5_operators(lam_re, lam_im, log_dt, b_re, b_im, c_re, c_im)
    s5_consts = (row(norm_mix), w_in16, b_in_row, mmat, kmat, nmat, aq_re, aq_im, row(d_skip))

    lane_pad = LANES - N_EXPERTS - N_EXPERT_GROUPS
    w_r = jnp.pad(jnp.concatenate([w_re, w_rg], axis=1).astype(F32), ((0, 0), (0, lane_pad)))
    b_r = jnp.pad(jnp.concatenate([b_re_r, b_rg]).astype(F32), (0, lane_pad)).reshape(1, LANES)
    w_r_hi = w_r.astype(BF16)
    w_r_lo = (w_r - w_r_hi.astype(F32)).astype(BF16)

    mix_consts = (w_in16, b_in_row, row(norm_mix),
                  conv_w.astype(F32), row(conv_b), w_conv_out.astype(BF16),
                  jnp.concatenate([w_glu_a, w_glu_b], axis=1).astype(BF16), w_o.astype(BF16),
                  row(norm_ffn), jnp.concatenate([w_r_hi, w_r_lo], axis=1), b_r)
    ple_consts = (row(norm_ple), w_pg.astype(BF16), row(b_pg), w_ple.astype(BF16), row(norm_out))
    p2d = p.reshape(T, -1)
    n_rows = 2 * T + N_EXPERTS * MOE_RT

    ys = _s5_call(x, *s5_consts)
    x1, h2p, route, route_t, counts = _mix_call(x, ys, *mix_consts)
    pos, plan = _moe_plan(route_t, counts, n_rows)
    xs = _sc_dispatch(h2p, pos[0], pos[1], n_rows)
    ysort = _moe_call(plan, xs, w_eg, w_eu, w_ed)
    y_picks = _sc_gather(ysort, pos.reshape(-1))
    out = _ple_call(x1, y_picks, route, p2d, *ple_consts)
    return out.reshape(nb, seq, d)


def kernel(x, p, norm_mix, w_in, b_in, ssm_lam_re, ssm_lam_im, ssm_log_dt, ssm_b_re, ssm_b_im, ssm_c_re, ssm_c_im, ssm_d, w_glu_a, w_glu_b, conv_w, conv_b, w_conv_out, w_o, norm_ffn, w_router_group, b_router_group, w_router_expert, b_router_expert, w_exp_gate, w_exp_up, w_exp_down, norm_ple, w_ple, w_ple_gate, b_ple_gate, norm_final):
    assert p.shape[0] == 1, "the final RMSNorm is fused into the (single) layer's last kernel"
    i = 0
    return _layer(x, p[i], norm_mix[i], w_in[i], b_in[i], ssm_lam_re[i], ssm_lam_im[i],
                  ssm_log_dt[i], ssm_b_re[i], ssm_b_im[i], ssm_c_re[i], ssm_c_im[i], ssm_d[i],
                  w_glu_a[i], w_glu_b[i], conv_w[i], conv_b[i], w_conv_out[i], w_o[i],
                  norm_ffn[i], w_router_group[i], b_router_group[i], w_router_expert[i],
                  b_router_expert[i], w_exp_gate[i], w_exp_up[i], w_exp_down[i], norm_ple[i],
                  w_ple[i], w_ple_gate[i], b_ple_gate[i], norm_final)
```

```python
import functools
import math

import jax
import jax.numpy as jnp
from jax import lax
from jax.experimental import pallas as pl
from jax.experimental.pallas import tpu as pltpu
from jax.experimental.pallas import tpu_sc as plsc

F32 = jnp.float32
BF16 = jnp.bfloat16

D_MODEL = 1024
SSM_WIDTH = 512
SSM_GROUP = 16
SSM_GROUPS = 32
SSM_STATE = 64
CONV_WIDTH = 512
N_EXPERT_GROUPS = 4
EXPERTS_PER_GROUP = 8
N_EXPERTS = 32
EXPERT_HIDDEN = 256
NORM_EPS = 1e-6

LANES = 128
Q = 8
GROUPS_PER_LANE_TILE = LANES // SSM_GROUP
N_LANE_TILES = SSM_WIDTH // LANES
STATE_LANES = GROUPS_PER_LANE_TILE * SSM_STATE
S5_ROWS = 1024
VMEM_LIMIT = 56 * 1024 * 1024


def _rms(x, g):
    return x * lax.rsqrt(jnp.mean(x * x, axis=-1, keepdims=True) + NORM_EPS) * g


def _pack_bf16_pairs(a):
    w = a.shape[1] // 2
    lo = lax.shift_right_logical(lax.bitcast_convert_type(a[:, :w], jnp.int32), 16)
    hi = lax.bitcast_convert_type(a[:, w:], jnp.int32) & jnp.int32(-65536)
    return lo | hi


def _unpack_bf16_pairs(word):
    lo = lax.bitcast_convert_type(lax.shift_left(word, 16), F32)
    hi = lax.bitcast_convert_type(word & jnp.int32(-65536), F32)
    return lo, hi


def _const_spec(shape):
    n = len(shape)
    return pl.BlockSpec(shape, lambda *_: (0,) * n, pipeline_mode=pl.Buffered(1))


def _s5_operators(lam_re, lam_im, log_dt, b_re, b_im, c_re, c_im):
    G, P, H = SSM_GROUPS, SSM_STATE, SSM_GROUP
    J = N_LANE_TILES
    lr = lam_re.astype(F32)
    li = lam_im.astype(F32)
    dt = jnp.exp(log_dt.astype(F32))[:, None]

    def apow(n):
        n = n.astype(F32)[:, None, None]
        mag = jnp.exp(lr * dt * n)
        ang = li * dt * n
        return mag * jnp.cos(ang), mag * jnp.sin(ang)

    a1_re, a1_im = apow(jnp.ones((1,), F32))
    nr = a1_re[0] - 1.0
    ni = a1_im[0]
    den = lr * lr + li * li
    f_re = (nr * lr + ni * li) / den
    f_im = (ni * lr - nr * li) / den
    br = b_re.astype(F32)
    bi = b_im.astype(F32)
    bbar_re = f_re[:, :, None] * br - f_im[:, :, None] * bi
    bbar_im = f_re[:, :, None] * bi + f_im[:, :, None] * br
    to_rows = lambda v, perm: jnp.transpose(v, perm).reshape(H, G * P)
    bt_re = to_rows(bbar_re, (2, 0, 1))
    bt_im = to_rows(bbar_im, (2, 0, 1))
    ct_re = to_rows(c_re.astype(F32), (1, 0, 2))
    ct_im = to_rows(c_im.astype(F32), (1, 0, 2))
    ap_re, ap_im = apow(jnp.arange(Q + 1))
    ap_re = ap_re.reshape(Q + 1, G * P)
    ap_im = ap_im.reshape(Q + 1, G * P)

    blk = lambda r: pl.BlockSpec((r, STATE_LANES), lambda j: (0, j))
    mat = lambda r, c: pl.BlockSpec((1, r, c), lambda j: (j, 0, 0))
    shape = lambda r, c: jax.ShapeDtypeStruct((J, r, c), BF16)
    qx, st = Q * LANES, 2 * STATE_LANES
    kmat, mmat, nmat = pl.pallas_call(
        _s5_ops_body,
        out_shape=(shape(qx, qx), shape(qx, st), shape(st, qx)),
        grid=(J,),
        in_specs=[blk(Q + 1), blk(Q + 1), blk(H), blk(H), blk(H), blk(H)],
        out_specs=(mat(qx, qx), mat(qx, st), mat(st, qx)),
        compiler_params=pltpu.CompilerParams(
            dimension_semantics=("arbitrary",), vmem_limit_bytes=VMEM_LIMIT),
        name="s5_operators",
    )(ap_re, ap_im, bt_re, bt_im, ct_re, ct_im)
    return kmat, mmat, nmat, ap_re[Q:Q + 1], ap_im[Q:Q + 1]


def _s5_ops_body(apr_ref, api_ref, btr_ref, bti_ref, ctr_ref, cti_ref, k_ref, m_ref, n_ref):
    ri = lax.broadcasted_iota(jnp.int32, (LANES, STATE_LANES), 0)
    li = lax.broadcasted_iota(jnp.int32, (LANES, STATE_LANES), 1)
    same_group = (ri // SSM_GROUP) == (li // SSM_STATE)

    def expand(ref):
        tiled = jnp.concatenate([ref[...]] * GROUPS_PER_LANE_TILE, axis=0)
        return jnp.where(same_group, tiled, 0.0)

    b_re, b_im, c_re, c_im = expand(btr_ref), expand(bti_ref), expand(ctr_ref), expand(cti_ref)

    def cmul(n, x_re, x_im):
        a_re = apr_ref[n:n + 1, :]
        a_im = api_ref[n:n + 1, :]
        return a_re * x_re - a_im * x_im, a_re * x_im + a_im * x_re

    m_blocks = []
    for k in range(Q):
        g_re, g_im = cmul(Q - 1 - k, b_re, b_im)
        m_blocks.append(jnp.concatenate([g_re, g_im], axis=1))
    m = jnp.concatenate(m_blocks, axis=0)

    nt_blocks = []
    for t in range(Q):
        g_re, g_im = cmul(t + 1, c_re, c_im)
        nt_blocks.append(jnp.concatenate([g_re, -g_im], axis=1))
    nt = jnp.concatenate(nt_blocks, axis=0)

    n0t = jnp.concatenate([c_re, -c_im], axis=1)
    p = lax.dot_general(m, n0t, (((1,), (1,)), ((), ())),
                        precision=lax.Precision.HIGHEST, preferred_element_type=F32)
    zeros = jnp.zeros((LANES, LANES), F32)
    cols = []
    for t in range(Q):
        cols.append(jnp.concatenate(
            [p[(Q - 1 - (t - k)) * LANES:(Q - (t - k)) * LANES, :] if t >= k else zeros
             for k in range(Q)], axis=0))
    k_ref[0] = jnp.concatenate(cols, axis=1).astype(BF16)
    m_ref[0] = m.astype(BF16)
    n_ref[0] = nt.T.astype(BF16)


def _s5_body(x_ref, g_ref, wu_ref, bu_ref, m_ref, k_ref, n_ref, aqr_ref, aqi_ref, d_ref,
             o_ref, u_scr, y_scr, z_scr, ss_scr, carry_scr):
    nb, tt = x_ref.shape[0], x_ref.shape[1]
    rows = nb * tt
    nchunk = rows // Q

    @pl.when(pl.program_id(0) == 0)
    def _():
        carry_scr[...] = jnp.zeros_like(carry_scr)

    hb = nb // 2
    for r in range(2):
        x = x_ref[r * hb:(r + 1) * hb].reshape(rows // 2, D_MODEL)
        h = _rms(x, g_ref[...]).astype(BF16)
        u = jnp.dot(h, wu_ref[...], preferred_element_type=F32) + bu_ref[...]
        for j in range(N_LANE_TILES):
            u_scr[j, pl.ds(r * (rows // 2), rows // 2), :] = u[:, j * LANES:(j + 1) * LANES]

    n_st = STATE_LANES // LANES
    cpt = tt // Q

    lane_tiles = range(N_LANE_TILES)


    xs = [jnp.concatenate([u_scr[j, pl.ds(k, nchunk, stride=Q), :] for k in range(Q)],
                          axis=1).astype(BF16) for j in lane_tiles]
    for j in lane_tiles:
        z = jnp.dot(xs[j], m_ref[j], preferred_element_type=F32)
        for i in range(2 * n_st):
            z_scr[j, i] = z[:, i * LANES:(i + 1) * LANES]
    y_intra = [jnp.dot(xs[j], k_ref[j], preferred_element_type=F32) for j in lane_tiles]

    aq = [(jnp.broadcast_to(aqr_ref[:, pl.ds(j * STATE_LANES, STATE_LANES)], (nb, STATE_LANES)),
           jnp.broadcast_to(aqi_ref[:, pl.ds(j * STATE_LANES, STATE_LANES)], (nb, STATE_LANES)))
          for j in lane_tiles]
    st = [(carry_scr[j, :, pl.ds(0, STATE_LANES)], carry_scr[j, :, pl.ds(STATE_LANES, STATE_LANES)])
          for j in lane_tiles]
    for c in range(cpt):
        seq_rows = pl.ds(c, nb, stride=cpt)
        for j in lane_tiles:
            s_re, s_im = st[j]
            aqr, aqi = aq[j]
            for i in range(n_st):
                ss_scr[j, i, seq_rows, :] = s_re[:, i * LANES:(i + 1) * LANES]
                ss_scr[j, n_st + i, seq_rows, :] = s_im[:, i * LANES:(i + 1) * LANES]
            z_re = jnp.concatenate([z_scr[j, i, seq_rows, :] for i in range(n_st)], axis=1)
            z_im = jnp.concatenate([z_scr[j, n_st + i, seq_rows, :] for i in range(n_st)], axis=1)
            st[j] = (aqr * s_re - aqi * s_im + z_re, aqr * s_im + aqi * s_re + z_im)
    for j in lane_tiles:
        carry_scr[j, :, pl.ds(0, STATE_LANES)] = st[j][0]
        carry_scr[j, :, pl.ds(STATE_LANES, STATE_LANES)] = st[j][1]

    def state_to_output(j):
        ss = jnp.concatenate([ss_scr[j, i] for i in range(2 * n_st)], axis=1).astype(BF16)
        return y_intra[j] + jnp.dot(ss, n_ref[j], preferred_element_type=F32)

    def finish(j, yj):
        for k in range(Q):
            y_scr[j, pl.ds(k, nchunk, stride=Q), :] = yj[:, k * LANES:(k + 1) * LANES]
        lanes = pl.ds(j * LANES, LANES)
        y = y_scr[j] + d_ref[:, lanes] * u_scr[j]
        o_ref[:, :, lanes] = jax.nn.gelu(y).astype(BF16).reshape(nb, tt, LANES)

    yj = state_to_output(0)
    for j in lane_tiles:
        y_next = state_to_output(j + 1) if j + 1 < N_LANE_TILES else None
        finish(j, yj)
        yj = y_next


def _col_block(rows, width, block):
    return pl.BlockSpec((rows, width), lambda *_: (0, block), pipeline_mode=pl.Buffered(1))


def _s5_call(x, g, w_in, b_in, mmat, kmat, nmat, aq_re, aq_im, d_skip):
    nb, seq = x.shape[0], x.shape[1]
    tt = S5_ROWS // nb
    rows = S5_ROWS
    nchunk = rows // Q
    return pl.pallas_call(
        _s5_body,
        out_shape=jax.ShapeDtypeStruct((nb, seq, SSM_WIDTH), BF16),
        grid=(seq // tt,),
        in_specs=[
            pl.BlockSpec((nb, tt, D_MODEL), lambda i: (0, i, 0)),
            _const_spec(g.shape), _col_block(D_MODEL, SSM_WIDTH, 0), _col_block(1, SSM_WIDTH, 0),
            _const_spec(mmat.shape), _const_spec(kmat.shape), _const_spec(nmat.shape),
            _const_spec(aq_re.shape), _const_spec(aq_im.shape), _const_spec(d_skip.shape),
        ],
        out_specs=pl.BlockSpec((nb, tt, SSM_WIDTH), lambda i: (0, i, 0)),
        scratch_shapes=[
            pltpu.VMEM((N_LANE_TILES, rows, LANES), F32),
            pltpu.VMEM((N_LANE_TILES, rows, LANES), F32),
            pltpu.VMEM((N_LANE_TILES, 2 * STATE_LANES // LANES, nchunk, LANES), F32),
            pltpu.VMEM((N_LANE_TILES, 2 * STATE_LANES // LANES, nchunk, LANES), F32),
            pltpu.VMEM((N_LANE_TILES, nb, 2 * STATE_LANES), F32),
        ],
        compiler_params=pltpu.CompilerParams(
            dimension_semantics=("arbitrary",), vmem_limit_bytes=VMEM_LIMIT),
        name="s5_mixer",
    )(x, g, w_in, b_in, mmat, kmat, nmat, aq_re, aq_im, d_skip)


MIX_TM = 512


def _mix_body(x_ref, ys_ref, wcb_ref, wcc_ref, wcv_ref, wg_ref, bcb_ref, bcc_ref, bcv_ref, bg_ref,
              g_ref, cw_ref, cb_ref, wco_ref, wab_ref, wo_ref, g2_ref, wr_ref, br_ref,
              x1_ref, h2p_ref, route_ref, route_t_ref, cnt_ref, carry_scr, cnt_scr, x1_scr,
              *, tiles_per_seq, n_tiles):
    i = pl.program_id(0)
    tm = x_ref.shape[1]
    wc_refs = (wcb_ref, wcc_ref, wcv_ref)
    bc_refs = (bcb_ref, bcc_ref, bcv_ref)

    @pl.when(i == 0)
    def _():
        cnt_scr[...] = jnp.zeros_like(cnt_scr)
        x1_scr[...] = jnp.zeros_like(x1_scr)
        carry_scr[...] = jnp.zeros_like(carry_scr)

    def route_logits():
        h2_hi = _rms(x1_scr[...], g2_ref[...]).astype(BF16)
        h2p_ref[...] = _pack_bf16_pairs(h2_hi.astype(F32))
        lg2 = jnp.dot(h2_hi, wr_ref[...], preferred_element_type=F32)
        return lg2[:, 0:LANES] + lg2[:, LANES:2 * LANES] + br_ref[...]

    def route_finish(logits):
        cnt = cnt_scr[...]
        route, new_cnt = _route_tile(logits, cnt)
        cnt = jnp.where(i > 0, new_cnt, cnt)
        cnt_scr[...] = cnt
        cnt_ref[...] = cnt
        route_ref[...] = route
        route_t_ref[...] = route.T[0:8, :]

    @pl.when(i < n_tiles)
    def _():
        yab = jnp.dot(ys_ref[0], wab_ref[...], preferred_element_type=F32)
        logits = route_logits()

        x = x_ref[0]
        h = _rms(x, g_ref[...]).astype(BF16)
        c_b, c_c, c_v = [jnp.dot(h, w[...], preferred_element_type=F32) + b[...]
                         for w, b in zip(wc_refs, bc_refs)]
        zg = jnp.dot(h, wg_ref[...], preferred_element_type=F32) + bg_ref[...]

        route_finish(logits)

        y_a = yab[:, 0:D_MODEL] * jax.nn.sigmoid(yab[:, D_MODEL:2 * D_MODEL])

        cv = c_c * c_v
        row = lax.broadcasted_iota(jnp.int32, (tm, CONV_WIDTH), 0)
        hist = jnp.where(i % tiles_per_seq == 0, 0.0, carry_scr[...])
        last1 = hist[7:8, :]
        last2 = hist[6:7, :]
        p1 = jnp.where(row == 0, last1, pltpu.roll(cv, 1, axis=0))
        p2 = jnp.where(row == 0, last2, jnp.where(row == 1, last1, pltpu.roll(cv, 2, axis=0)))
        carry_scr[...] = cv[tm - 8:tm, :]
        conv = cw_ref[0:1, :] * p2 + cw_ref[1:2, :] * p1 + cw_ref[2:3, :] * cv + cb_ref[...]
        y_b = jnp.dot((c_b * conv).astype(BF16), wco_ref[...], preferred_element_type=F32)

        mix = (jax.nn.sigmoid(zg[:, 0:D_MODEL]) * y_a
               + jax.nn.sigmoid(zg[:, D_MODEL:2 * D_MODEL]) * y_b)
        x1 = x + jnp.dot(mix.astype(BF16), wo_ref[...], preferred_element_type=F32)
        x1_ref[...] = x1
        x1_scr[...] = x1

    @pl.when(i == n_tiles)
    def _():
        route_finish(route_logits())


def _route_tile(logits, cnt):
    tm = logits.shape[0]
    lane = lax.broadcasted_iota(jnp.int32, (tm, LANES), 1)
    neg = jnp.float32(-jnp.inf)
    big = jnp.int32(1 << 20)
    is_g = (lane >= N_EXPERTS) & (lane < N_EXPERTS + N_EXPERT_GROUPS)
    gl = jnp.where(is_g, logits, neg)
    gmax = jnp.max(gl, axis=1, keepdims=True)
    g_w = 1.0 / jnp.sum(jnp.exp(gl - gmax), axis=1, keepdims=True)
    g_idx = jnp.min(jnp.where(gl == gmax, lane - N_EXPERTS, big), axis=1, keepdims=True)
    lo = g_idx * EXPERTS_PER_GROUP
    el = jnp.where((lane >= lo) & (lane < lo + EXPERTS_PER_GROUP), logits, neg)
    m1 = jnp.max(el, axis=1, keepdims=True)
    i1 = jnp.min(jnp.where(el == m1, lane, big), axis=1, keepdims=True)
    el2 = jnp.where(lane == i1, neg, el)
    m2 = jnp.max(el2, axis=1, keepdims=True)
    i2 = jnp.min(jnp.where(el2 == m2, lane, big), axis=1, keepdims=True)
    r = jnp.exp(m2 - m1)
    w1 = g_w / (1.0 + r)
    w2 = g_w * r / (1.0 + r)

    picks = ((lane == i1) | (lane == i2)).astype(BF16)
    r_i = lax.broadcasted_iota(jnp.int32, (tm, tm), 0)
    c_i = lax.broadcasted_iota(jnp.int32, (tm, tm), 1)
    before = (c_i < r_i).astype(BF16)
    excl = jnp.dot(before, picks, preferred_element_type=F32) + cnt
    rank1 = jnp.sum(jnp.where(lane == i1, excl, 0.0), axis=1, keepdims=True)
    rank2 = jnp.sum(jnp.where(lane == i2, excl, 0.0), axis=1, keepdims=True)
    new_cnt = cnt + jnp.sum(picks.astype(F32), axis=0, keepdims=True)
    route = jnp.where(lane == 0, i1.astype(F32), 0.0)
    route = jnp.where(lane == 1, i2.astype(F32), route)
    route = jnp.where(lane == 2, w1, route)
    route = jnp.where(lane == 3, w2, route)
    route = jnp.where(lane == 4, rank1, route)
    route = jnp.where(lane == 5, rank2, route)
    return route, new_cnt


def _mix_call(x, ys, w_in, b_in, g, cw, cb, wco, wab, wo, g2, wr, br):
    nb, seq, _ = ys.shape
    T = nb * seq
    nl = seq // MIX_TM
    n = nb * nl
    consts = (g, cw, cb, wco, wab, wo, g2, wr, br)
    gates = 2 * D_MODEL
    in_proj_specs = ([_col_block(D_MODEL, CONV_WIDTH, b) for b in (1, 2, 3)]
                     + [_col_block(D_MODEL, gates, 1)]
                     + [_col_block(1, CONV_WIDTH, b) for b in (1, 2, 3)]
                     + [_col_block(1, gates, 1)])
    cur = lambda i: jnp.minimum(i, n - 1)
    prev = lambda i: jnp.maximum(i - 1, 0)
    return pl.pallas_call(
        functools.partial(_mix_body, tiles_per_seq=nl, n_tiles=n),
        out_shape=(jax.ShapeDtypeStruct((T, D_MODEL), F32),
                   jax.ShapeDtypeStruct((T, D_MODEL // 2), jnp.int32),
                   jax.ShapeDtypeStruct((T, LANES), F32),
                   jax.ShapeDtypeStruct((8, T), F32),
                   jax.ShapeDtypeStruct((1, LANES), F32)),
        grid=(n + 1,),
        in_specs=[pl.BlockSpec((1, MIX_TM, D_MODEL),
                               lambda i: (cur(i) // nl, cur(i) % nl, 0)),
                  pl.BlockSpec((1, MIX_TM, SSM_WIDTH), lambda i: (cur(i) // nl, cur(i) % nl, 0))]
                 + in_proj_specs + [_const_spec(c.shape) for c in consts],
        out_specs=(pl.BlockSpec((MIX_TM, D_MODEL), lambda i: (cur(i), 0)),
                   pl.BlockSpec((MIX_TM, D_MODEL // 2), lambda i: (prev(i), 0)),
                   pl.BlockSpec((MIX_TM, LANES), lambda i: (prev(i), 0)),
                   pl.BlockSpec((8, MIX_TM), lambda i: (0, prev(i))),
                   pl.BlockSpec((1, LANES), lambda i: (0, 0))),
        scratch_shapes=[pltpu.VMEM((8, CONV_WIDTH), F32), pltpu.VMEM((1, LANES), F32),
                        pltpu.VMEM((MIX_TM, D_MODEL), F32)],
        compiler_params=pltpu.CompilerParams(
            dimension_semantics=("arbitrary",), vmem_limit_bytes=VMEM_LIMIT),
        name="conv_glu_router",
    )(x, ys, *([w_in] * 4), *([b_in] * 4), *consts)


SC_CORES = 2
SC_SUBCORES = 16
SC_WORKERS = SC_CORES * SC_SUBCORES
SC_ROWS = 64
SC_NBUF = 2


def _sc_mesh():
    return plsc.VectorSubcoreMesh(core_axis_name="c", subcore_axis_name="s")


def _sc_worker_id():
    return lax.axis_index("s") * SC_CORES + lax.axis_index("c")


def _sc_ring(nch, get, puts):
    for b in range(SC_NBUF - 1):
        get(b, b).start()

    @pl.loop(0, nch, step=SC_NBUF)
    def _(j0):
        for b in range(SC_NBUF):
            j = j0 + b
            refill = (b - 1) % SC_NBUF
            get(j, b).wait()

            @pl.when(j + SC_NBUF - 1 < nch)
            def _():
                @pl.when(j >= 1)
                def _():
                    for c in puts(j - 1, refill):
                        c.wait()
                get(j + SC_NBUF - 1, refill).start()

            for c in puts(j, b):
                c.start()

    for b in range(SC_NBUF):
        for c in puts(nch - SC_NBUF + b, b):
            c.wait()


def _sc_dispatch(rows, pos_a, pos_b, n_out):
    T, W = rows.shape
    per_w = T // SC_WORKERS
    nch = per_w // SC_ROWS
    assert per_w * SC_WORKERS == T and nch * SC_ROWS == per_w and nch % SC_NBUF == 0
    idx_a = pos_a.reshape(SC_WORKERS, nch, SC_ROWS)
    idx_b = pos_b.reshape(SC_WORKERS, nch, SC_ROWS)

    @functools.partial(
        pl.kernel, mesh=_sc_mesh(),
        out_type=jax.ShapeDtypeStruct((n_out, W), rows.dtype),
        scratch_types=[
            pltpu.VMEM((nch, SC_ROWS), jnp.int32),
            pltpu.VMEM((nch, SC_ROWS), jnp.int32),
            pltpu.VMEM((SC_NBUF, SC_ROWS, W), rows.dtype),
            pltpu.SemaphoreType.DMA((SC_NBUF,)),
            pltpu.SemaphoreType.DMA((SC_NBUF,)),
            pltpu.SemaphoreType.DMA((SC_NBUF,)),
        ],
        name="moe_dispatch",
    )
    def k(rows_hbm, ia_hbm, ib_hbm, out_hbm, ia_v, ib_v, buf, gsem, asem, bsem):
        wid = _sc_worker_id()
        base = wid * per_w
        pltpu.sync_copy(ia_hbm.at[wid], ia_v)
        pltpu.sync_copy(ib_hbm.at[wid], ib_v)

        def get(j, b):
            return pltpu.make_async_copy(
                rows_hbm.at[pl.ds(base + j * SC_ROWS, SC_ROWS)], buf.at[b], gsem.at[b])

        def put_a(j, b):
            return pltpu.make_async_copy(buf.at[b], out_hbm.at[ia_v.at[j]], asem.at[b])

        def put_b(j, b):
            return pltpu.make_async_copy(buf.at[b], out_hbm.at[ib_v.at[j]], bsem.at[b])

        _sc_ring(nch, get, lambda j, b: (put_a(j, b), put_b(j, b)))

    return k(rows, idx_a, idx_b)


def _sc_gather(table, idx):
    _, W = table.shape
    B = idx.shape[0]
    per_w = B // SC_WORKERS
    nch = per_w // SC_ROWS
    assert per_w * SC_WORKERS == B and nch * SC_ROWS == per_w and nch % SC_NBUF == 0
    idx3 = idx.reshape(SC_WORKERS, nch, SC_ROWS)

    @functools.partial(
        pl.kernel, mesh=_sc_mesh(),
        out_type=jax.ShapeDtypeStruct((B, W), table.dtype),
        scratch_types=[
            pltpu.VMEM((nch, SC_ROWS), jnp.int32),
            pltpu.VMEM((SC_NBUF, SC_ROWS, W), table.dtype),
            pltpu.SemaphoreType.DMA((SC_NBUF,)),
            pltpu.SemaphoreType.DMA((SC_NBUF,)),
        ],
        name="moe_combine_gather",
    )
    def k(table_hbm, idx_hbm, out_hbm, idx_v, buf, gsem, osem):
        wid = _sc_worker_id()
        base = wid * per_w
        pltpu.sync_copy(idx_hbm.at[wid], idx_v)

        def get(j, b):
            return pltpu.make_async_copy(table_hbm.at[idx_v.at[j]], buf.at[b], gsem.at[b])

        def put(j, b):
            return pltpu.make_async_copy(
                buf.at[b], out_hbm.at[pl.ds(base + j * SC_ROWS, SC_ROWS)], osem.at[b])

        _sc_ring(nch, get, lambda j, b: (put(j, b),))

    return k(table, idx3)


MOE_RT = 1024
MOE_SUB = 256
MOE_IN_BUFFERS = 3


def _moe_tile(i, te_ref, nx_ref, seg_ref, ns_ref, x_ref, wg_hbm, wu_hbm, wd_hbm, o_ref,
              wg_scr, wu_scr, wd_scr, wg_buf, wu_buf, wd_buf, sem):
    expert = te_ref[i]
    slot = seg_ref[i] % 2

    def weight_copies(e, b):
        return (pltpu.make_async_copy(wg_hbm.at[e], wg_buf.at[b], sem.at[b, 0]),
                pltpu.make_async_copy(wu_hbm.at[e], wu_buf.at[b], sem.at[b, 1]),
                pltpu.make_async_copy(wd_hbm.at[e], wd_buf.at[b], sem.at[b, 2]))

    @pl.when(i == 0)
    def _():
        for c in weight_copies(expert, 0):
            c.start()

    @pl.when((i == 0) | (expert != te_ref[jnp.maximum(i - 1, 0)]))
    def _():
        for c in weight_copies(expert, slot):
            c.wait()
        wg_scr[...] = wg_buf[slot].astype(BF16)
        wu_scr[...] = wu_buf[slot].astype(BF16)
        wd_scr[...] = wd_buf[slot].astype(BF16)

        @pl.when(nx_ref[i] >= 0)
        def _():
            for c in weight_copies(nx_ref[i], 1 - slot):
                c.start()

    half = D_MODEL // 2

    def up_proj(s):
        lo, hi = _unpack_bf16_pairs(x_ref[pl.ds(s * MOE_SUB, MOE_SUB), :])
        lo = lo.astype(BF16)
        hi = hi.astype(BF16)
        gate = (jnp.dot(lo, wg_scr[0:half, :], preferred_element_type=F32)
                + jnp.dot(hi, wg_scr[half:D_MODEL, :], preferred_element_type=F32))
        up = (jnp.dot(lo, wu_scr[0:half, :], preferred_element_type=F32)
              + jnp.dot(hi, wu_scr[half:D_MODEL, :], preferred_element_type=F32))
        return gate, up

    def down_proj(gate, up):
        hid = (jax.nn.silu(gate) * up).astype(BF16)
        return jnp.dot(hid, wd_scr[...], preferred_element_type=F32)

    def store(s, y):
        o_ref[pl.ds(s * MOE_SUB, MOE_SUB), :] = _pack_bf16_pairs(y.astype(BF16).astype(F32))

    def run(n_sub):
        gu = {0: up_proj(0)}
        ys = {}
        for s in range(n_sub):
            if s + 1 < n_sub:
                gu[s + 1] = up_proj(s + 1)
            ys[s] = down_proj(*gu.pop(s))
            if s >= 1:
                store(s - 1, ys.pop(s - 1))
        store(n_sub - 1, ys.pop(n_sub - 1))

    for n_sub in range(1, x_ref.shape[0] // MOE_SUB + 1):
        pl.when(ns_ref[i] == n_sub)(functools.partial(run, n_sub))


def _moe_call(plan, xs, wg, wu, wd):
    R = xs.shape[0]
    half = D_MODEL // 2

    def outer(te_ref, nt_ref, nx_ref, seg_ref, ns_ref, x_hbm, wg_hbm, wu_hbm, wd_hbm, o_hbm,
              wg_scr, wu_scr, wd_scr, wg_buf, wu_buf, wd_buf, sem, step_ref):
        row_map = lambda i: (jnp.minimum(i, nt_ref[0] - 1), 0)
        step_ref[0] = 0

        def tile(x_ref, o_ref):
            i = step_ref[0]
            step_ref[0] = i + 1
            _moe_tile(i, te_ref, nx_ref, seg_ref, ns_ref, x_ref, wg_hbm, wu_hbm, wd_hbm, o_ref,
                      wg_scr, wu_scr, wd_scr, wg_buf, wu_buf, wd_buf, sem)

        pltpu.emit_pipeline(
            tile, grid=(R // MOE_RT,),
            in_specs=[pl.BlockSpec((MOE_RT, half), row_map,
                                   pipeline_mode=pl.Buffered(MOE_IN_BUFFERS))],
            out_specs=[pl.BlockSpec((MOE_RT, half), row_map)],
        )(x_hbm, o_hbm)

    hbm = pl.BlockSpec(memory_space=pl.ANY)
    smem = pl.BlockSpec(memory_space=pltpu.SMEM)
    return pl.pallas_call(
        outer,
        out_shape=jax.ShapeDtypeStruct((R, half), jnp.int32),
        in_specs=[smem] * 5 + [hbm] * 4,
        out_specs=hbm,
        scratch_shapes=[pltpu.VMEM((D_MODEL, EXPERT_HIDDEN), BF16),
                        pltpu.VMEM((D_MODEL, EXPERT_HIDDEN), BF16),
                        pltpu.VMEM((EXPERT_HIDDEN, D_MODEL), BF16),
                        pltpu.VMEM((2, D_MODEL, EXPERT_HIDDEN), F32),
                        pltpu.VMEM((2, D_MODEL, EXPERT_HIDDEN), F32),
                        pltpu.VMEM((2, EXPERT_HIDDEN, D_MODEL), F32),
                        pltpu.SemaphoreType.DMA((2, 3)),
                        pltpu.SMEM((1,), jnp.int32)],
        compiler_params=pltpu.CompilerParams(vmem_limit_bytes=VMEM_LIMIT),
        name="moe_experts",
    )(*plan, xs, wg, wu, wd)


def _moe_plan(route_t, counts, n_rows):
    cnt = counts[0, :N_EXPERTS].astype(jnp.int32)
    tiles = (cnt + MOE_RT - 1) // MOE_RT
    tile_end = jnp.cumsum(tiles)
    n_tiles = tile_end[-1:]
    row_start = (tile_end - tiles) * MOE_RT
    ids = route_t[0:2].astype(jnp.int32)
    ranks = route_t[4:6].astype(jnp.int32)
    experts = jnp.arange(N_EXPERTS, dtype=jnp.int32)[:, None, None]
    pos = ranks + jnp.sum(jnp.where(ids[None] == experts, row_start[:, None, None], 0), axis=0)
    tile_id = jnp.minimum(jnp.arange(n_rows // MOE_RT, dtype=jnp.int32), n_tiles - 1)
    tile_expert = jnp.sum((tile_id[:, None] >= tile_end[None, :]).astype(jnp.int32), axis=1)
    e_ids = jnp.arange(N_EXPERTS, dtype=jnp.int32)
    later = (e_ids[None, :] > e_ids[:, None]) & (tiles[None, :] > 0)
    nxt = jnp.min(jnp.where(later, e_ids[None, :], N_EXPERTS), axis=1)
    nxt = jnp.where(nxt == N_EXPERTS, -1, nxt)
    seg = jnp.cumsum((tiles > 0).astype(jnp.int32)) - 1
    pick = tile_expert[:, None] == e_ids[None, :]
    tile_next = jnp.sum(jnp.where(pick, nxt[None, :], 0), axis=1)
    tile_seg = jnp.sum(jnp.where(pick, seg[None, :], 0), axis=1)
    first = jnp.sum(jnp.where(pick, (tile_end - tiles)[None, :], 0), axis=1)
    rows_left = (jnp.sum(jnp.where(pick, cnt[None, :], 0), axis=1)
                 - (jnp.arange(n_rows // MOE_RT, dtype=jnp.int32) - first) * MOE_RT)
    tile_subs = (jnp.clip(rows_left, 0, MOE_RT) + MOE_SUB - 1) // MOE_SUB
    return pos, (tile_expert, n_tiles, tile_next, tile_seg, tile_subs)


PLE_TM = 1024
PLE_SUB = 256


def _ple_body(x_ref, ya_ref, yb_ref, route_ref, p_ref, g3_ref, wpg_ref, bpg_ref, wple_ref, gf_ref,
              o_ref):
    n_sub = x_ref.shape[0] // PLE_SUB
    g3, gf, bpg = g3_ref[...], gf_ref[...], bpg_ref[...]

    def head(s):
        rows = pl.ds(s * PLE_SUB, PLE_SUB)
        ya = jnp.concatenate(_unpack_bf16_pairs(ya_ref[rows, :]), axis=1)
        yb = jnp.concatenate(_unpack_bf16_pairs(yb_ref[rows, :]), axis=1)
        route = route_ref[rows, :]
        x2 = x_ref[rows, :] + route[:, 2:3] * ya + route[:, 3:4] * yb
        return x2, _rms(x2, g3).astype(BF16)

    def dots(s, h3):
        rows = pl.ds(s * PLE_SUB, PLE_SUB)
        zg = jnp.dot(h3, wpg_ref[...], preferred_element_type=F32)
        pe = jnp.dot(p_ref[rows, :].astype(BF16), wple_ref[...], preferred_element_type=F32)
        return zg, pe

    def tail(s, x2, zg, pe):
        x3 = x2 + jax.nn.sigmoid(zg + bpg) * pe
        o_ref[pl.ds(s * PLE_SUB, PLE_SUB), :] = _rms(x3, gf)

    x2s, mm = {}, {}
    x2s[0], h3 = head(0)
    for s in range(n_sub):
        mm[s] = dots(s, h3)
        if s + 1 < n_sub:
            x2s[s + 1], h3 = head(s + 1)
        if s >= 1:
            tail(s - 1, x2s.pop(s - 1), *mm.pop(s - 1))
    tail(n_sub - 1, x2s.pop(n_sub - 1), *mm.pop(n_sub - 1))


def _ple_call(x1, y_picks, route, p, g3, wpg, bpg, wple, gf):
    T = x1.shape[0]
    nt = T // PLE_TM
    consts = (g3, wpg, bpg, wple, gf)
    tok = lambda i: (i, 0)
    return pl.pallas_call(
        _ple_body,
        out_shape=jax.ShapeDtypeStruct((T, D_MODEL), F32),
        grid=(nt,),
        in_specs=[pl.BlockSpec((PLE_TM, D_MODEL), tok),
                  pl.BlockSpec((PLE_TM, D_MODEL // 2), tok),
                  pl.BlockSpec((PLE_TM, D_MODEL // 2), lambda i: (i + nt, 0)),
                  pl.BlockSpec((PLE_TM, LANES), tok),
                  pl.BlockSpec((PLE_TM, p.shape[1]), tok)]
                 + [_const_spec(c.shape) for c in consts],
        out_specs=pl.BlockSpec((PLE_TM, D_MODEL), tok),
        compiler_params=pltpu.CompilerParams(
            dimension_semantics=("arbitrary",), vmem_limit_bytes=VMEM_LIMIT),
        name="ple_final",
    )(x1, y_picks, y_picks, route, p, *consts)


def _layer(x, p, norm_mix, w_in, b_in, lam_re, lam_im, log_dt, b_re, b_im, c_re, c_im, d_skip,
           w_glu_a, w_glu_b, conv_w, conv_b, w_conv_out, w_o, norm_ffn, w_rg, b_rg, w_re, b_re_r,
           w_eg, w_eu, w_ed, norm_ple, w_ple, w_pg, b_pg, norm_out):
    nb, seq, d = x.shape
    T = nb * seq
    row = lambda v: v.reshape(1, -1).astype(F32)
    assert w_in.shape[1] == SSM_WIDTH + 3 * CONV_WIDTH + 2 * D_MODEL and SSM_WIDTH == CONV_WIDTH
    w_in16 = w_in.astype(BF16)
    b_in_row = row(b_in)

    kmat, mmat, nmat, aq_re, aq_im = _s5_operators(lam_re, lam_im, log_dt, b_re, b_im, c_re, c_im)
    s5_consts = (row(norm_mix), w_in16, b_in_row, mmat, kmat, nmat, aq_re, aq_im, row(d_skip))

    lane_pad = LANES - N_EXPERTS - N_EXPERT_GROUPS
    w_r = jnp.pad(jnp.concatenate([w_re, w_rg], axis=1).astype(F32), ((0, 0), (0, lane_pad)))
    b_r = jnp.pad(jnp.concatenate([b_re_r, b_rg]).astype(F32), (0, lane_pad)).reshape(1, LANES)
    w_r_hi = w_r.astype(BF16)
    w_r_lo = (w_r - w_r_hi.astype(F32)).astype(BF16)

    mix_consts = (w_in16, b_in_row, row(norm_mix),
                  conv_w.astype(F32), row(conv_b), w_conv_out.astype(BF16),
                  jnp.concatenate([w_glu_a, w_glu_b], axis=1).astype(BF16), w_o.astype(BF16),
                  row(norm_ffn), jnp.concatenate([w_r_hi, w_r_lo], axis=1), b_r)
    ple_consts = (row(norm_ple), w_pg.astype(BF16), row(b_pg), w_ple.astype(BF16), row(norm_out))
    p2d = p.reshape(T, -1)
    n_rows = 2 * T + N_EXPERTS * MOE_RT

    ys = _s5_call(x, *s5_consts)
    x1, h2p, route, route_t, counts = _mix_call(x, ys, *mix_consts)
    pos, plan = _moe_plan(route_t, counts, n_rows)
    xs = _sc_dispatch(h2p, pos[0], pos[1], n_rows)
    ysort = _moe_call(plan, xs, w_eg, w_eu, w_ed)
    y_picks = _sc_gather(ysort, pos.reshape(-1))
    out = _ple_call(x1, y_picks, route, p2d, *ple_consts)
    return out.reshape(nb, seq, d)


def kernel(x, p, norm_mix, w_in, b_in, ssm_lam_re, ssm_lam_im, ssm_log_dt, ssm_b_re, ssm_b_im, ssm_c_re, ssm_c_im, ssm_d, w_glu_a, w_glu_b, conv_w, conv_b, w_conv_out, w_o, norm_ffn, w_router_group, b_router_group, w_router_expert, b_router_expert, w_exp_gate, w_exp_up, w_exp_down, norm_ple, w_ple, w_ple_gate, b_ple_gate, norm_final):
    assert p.shape[0] == 1, "the final RMSNorm is fused into the (single) layer's last kernel"
    i = 0
    return _layer(x, p[i], norm_mix[i], w_in[i], b_in[i], ssm_lam_re[i], ssm_lam_im[i],
                  ssm_log_dt[i], ssm_b_re[i], ssm_b_im[i], ssm_c_re[i], ssm_c_im[i], ssm_d[i],
                  w_glu_a[i], w_glu_b[i], conv_w[i], conv_b[i], w_conv_out[i], w_o[i],
                  norm_ffn[i], w_router_group[i], b_router_group[i], w_router_expert[i],
                  b_router_expert[i], w_exp_gate[i], w_exp_up[i], w_exp_down[i], norm_ple[i],
                  w_ple[i], w_ple_gate[i], b_ple_gate[i], norm_final)
```

```python
import functools
import math

import jax
import jax.numpy as jnp
from jax import lax
from jax.experimental import pallas as pl
from jax.experimental.pallas import tpu as pltpu
from jax.experimental.pallas import tpu_sc as plsc

F32 = jnp.float32
BF16 = jnp.bfloat16

D_MODEL = 1024
SSM_WIDTH = 512
SSM_GROUP = 16
SSM_GROUPS = 32
SSM_STATE = 64
CONV_WIDTH = 512
N_EXPERT_GROUPS = 4
EXPERTS_PER_GROUP = 8
N_EXPERTS = 32
EXPERT_HIDDEN = 256
NORM_EPS = 1e-6

LANES = 128
Q = 8
GROUPS_PER_LANE_TILE = LANES // SSM_GROUP
N_LANE_TILES = SSM_WIDTH // LANES
STATE_LANES = GROUPS_PER_LANE_TILE * SSM_STATE
S5_ROWS = 1024
VMEM_LIMIT = 56 * 1024 * 1024


def _rms(x, g):
    return x * lax.rsqrt(jnp.mean(x * x, axis=-1, keepdims=True) + NORM_EPS) * g


def _pack_bf16_pairs(a):
    w = a.shape[1] // 2
    lo = lax.shift_right_logical(lax.bitcast_convert_type(a[:, :w], jnp.int32), 16)
    hi = lax.bitcast_convert_type(a[:, w:], jnp.int32) & jnp.int32(-65536)
    return lo | hi


def _unpack_bf16_pairs(word):
    lo = lax.bitcast_convert_type(lax.shift_left(word, 16), F32)
    hi = lax.bitcast_convert_type(word & jnp.int32(-65536), F32)
    return lo, hi


def _const_spec(shape):
    n = len(shape)
    return pl.BlockSpec(shape, lambda *_: (0,) * n, pipeline_mode=pl.Buffered(1))


def _s5_operators(lam_re, lam_im, log_dt, b_re, b_im, c_re, c_im):
    G, P, H = SSM_GROUPS, SSM_STATE, SSM_GROUP
    lr = lam_re.astype(F32)
    li = lam_im.astype(F32)
    dt = jnp.exp(log_dt.astype(F32))[:, None]

    def apow(n):
        n = n.astype(F32)[:, None, None]
        mag = jnp.exp(lr * dt * n)
        ang = li * dt * n
        return mag * jnp.cos(ang), mag * jnp.sin(ang)

    a1_re, a1_im = apow(jnp.ones((1,), F32))
    nr = a1_re[0] - 1.0
    ni = a1_im[0]
    den = lr * lr + li * li
    f_re = (nr * lr + ni * li) / den
    f_im = (ni * lr - nr * li) / den
    br = b_re.astype(F32)
    bi = b_im.astype(F32)
    bbar_re = f_re[:, :, None] * br - f_im[:, :, None] * bi
    bbar_im = f_re[:, :, None] * bi + f_im[:, :, None] * br
    to_rows = lambda v, perm: jnp.transpose(v, perm).reshape(H, G * P)
    bt_re = to_rows(bbar_re, (2, 0, 1))
    bt_im = to_rows(bbar_im, (2, 0, 1))
    ct_re = to_rows(c_re.astype(F32), (1, 0, 2))
    ct_im = to_rows(c_im.astype(F32), (1, 0, 2))
    ap_re, ap_im = apow(jnp.arange(Q + 1))
    ap_re = ap_re.reshape(Q + 1, G * P)
    ap_im = ap_im.reshape(Q + 1, G * P)

    return ap_re, ap_im, bt_re, bt_im, ct_re, ct_im


def _s5_build_operators(j, apr_ref, api_ref, btr_ref, bti_ref, ctr_ref, cti_ref,
                        k_scr, m_scr, n_scr):
    sl = pl.ds(j * STATE_LANES, STATE_LANES)
    ri = lax.broadcasted_iota(jnp.int32, (LANES, STATE_LANES), 0)
    li = lax.broadcasted_iota(jnp.int32, (LANES, STATE_LANES), 1)
    same_group = (ri // SSM_GROUP) == (li // SSM_STATE)

    def expand(ref):
        tiled = jnp.concatenate([ref[:, sl]] * GROUPS_PER_LANE_TILE, axis=0)
        return jnp.where(same_group, tiled, 0.0)

    b_re, b_im, c_re, c_im = expand(btr_ref), expand(bti_ref), expand(ctr_ref), expand(cti_ref)

    def cmul(n, x_re, x_im):
        a_re = apr_ref[n:n + 1, sl]
        a_im = api_ref[n:n + 1, sl]
        return a_re * x_re - a_im * x_im, a_re * x_im + a_im * x_re

    m_blocks = []
    for k in range(Q):
        g_re, g_im = cmul(Q - 1 - k, b_re, b_im)
        m_blocks.append(jnp.concatenate([g_re, g_im], axis=1))
        m_scr[j, k * LANES:(k + 1) * LANES, :] = m_blocks[k].astype(BF16)
    m = jnp.concatenate(m_blocks, axis=0)

    for t in range(Q):
        g_re, g_im = cmul(t + 1, c_re, c_im)
        nt = jnp.concatenate([g_re, -g_im], axis=1)
        n_scr[j, :, t * LANES:(t + 1) * LANES] = nt.T.astype(BF16)

    n0t = jnp.concatenate([c_re, -c_im], axis=1)
    split = lambda v: (v.astype(BF16), (v - v.astype(BF16).astype(F32)).astype(BF16))
    m_hi, m_lo = split(m)
    n_hi, n_lo = split(n0t)
    dot_t = lambda a, b: lax.dot_general(a, b, (((1,), (1,)), ((), ())),
                                         preferred_element_type=F32)
    p = dot_t(m_hi, n_hi) + (dot_t(m_hi, n_lo) + dot_t(m_lo, n_hi))
    zeros = jnp.zeros((LANES, LANES), BF16)
    for lag in range(Q):
        blk = p[(Q - 1 - lag) * LANES:(Q - lag) * LANES, :].astype(BF16)
        for t in range(lag, Q):
            k = t - lag
            k_scr[j, k * LANES:(k + 1) * LANES, t * LANES:(t + 1) * LANES] = blk
    for t in range(Q):
        for k in range(t + 1, Q):
            k_scr[j, k * LANES:(k + 1) * LANES, t * LANES:(t + 1) * LANES] = zeros


def _s5_body(x_ref, g_ref, wu_ref, bu_ref, apr_ref, api_ref, btr_ref, bti_ref, ctr_ref, cti_ref,
             d_ref, o_ref, u_scr, y_scr, z_scr, ss_scr, carry_scr, k_ref, m_ref, n_ref):
    nb, tt = x_ref.shape[0], x_ref.shape[1]
    rows = nb * tt
    nchunk = rows // Q

    @pl.when(pl.program_id(0) == 0)
    def _():
        carry_scr[...] = jnp.zeros_like(carry_scr)
        for j in range(N_LANE_TILES):
            _s5_build_operators(j, apr_ref, api_ref, btr_ref, bti_ref, ctr_ref, cti_ref,
                                k_ref, m_ref, n_ref)

    hb = nb // 2
    for r in range(2):
        x = x_ref[r * hb:(r + 1) * hb].reshape(rows // 2, D_MODEL)
        h = _rms(x, g_ref[...]).astype(BF16)
        u = jnp.dot(h, wu_ref[...], preferred_element_type=F32) + bu_ref[...]
        for j in range(N_LANE_TILES):
            u_scr[j, pl.ds(r * (rows // 2), rows // 2), :] = u[:, j * LANES:(j + 1) * LANES]

    n_st = STATE_LANES // LANES
    cpt = tt // Q

    lane_tiles = range(N_LANE_TILES)


    xs = [jnp.concatenate([u_scr[j, pl.ds(k, nchunk, stride=Q), :] for k in range(Q)],
                          axis=1).astype(BF16) for j in lane_tiles]
    for j in lane_tiles:
        z = jnp.dot(xs[j], m_ref[j], preferred_element_type=F32)
        for i in range(2 * n_st):
            z_scr[j, i] = z[:, i * LANES:(i + 1) * LANES]
    y_intra = [jnp.dot(xs[j], k_ref[j], preferred_element_type=F32) for j in lane_tiles]

    aq = [(jnp.broadcast_to(apr_ref[Q:Q + 1, pl.ds(j * STATE_LANES, STATE_LANES)], (nb, STATE_LANES)),
           jnp.broadcast_to(api_ref[Q:Q + 1, pl.ds(j * STATE_LANES, STATE_LANES)], (nb, STATE_LANES)))
          for j in lane_tiles]
    st = [(carry_scr[j, :, pl.ds(0, STATE_LANES)], carry_scr[j, :, pl.ds(STATE_LANES, STATE_LANES)])
          for j in lane_tiles]
    for c in range(cpt):
        seq_rows = pl.ds(c, nb, stride=cpt)
        for j in lane_tiles:
            s_re, s_im = st[j]
            aqr, aqi = aq[j]
            for i in range(n_st):
                ss_scr[j, i, seq_rows, :] = s_re[:, i * LANES:(i + 1) * LANES]
                ss_scr[j, n_st + i, seq_rows, :] = s_im[:, i * LANES:(i + 1) * LANES]
            z_re = jnp.concatenate([z_scr[j, i, seq_rows, :] for i in range(n_st)], axis=1)
            z_im = jnp.concatenate([z_scr[j, n_st + i, seq_rows, :] for i in range(n_st)], axis=1)
            st[j] = (aqr * s_re - aqi * s_im + z_re, aqr * s_im + aqi * s_re + z_im)
    for j in lane_tiles:
        carry_scr[j, :, pl.ds(0, STATE_LANES)] = st[j][0]
        carry_scr[j, :, pl.ds(STATE_LANES, STATE_LANES)] = st[j][1]

    def state_to_output(j):
        ss = jnp.concatenate([ss_scr[j, i] for i in range(2 * n_st)], axis=1).astype(BF16)
        return y_intra[j] + jnp.dot(ss, n_ref[j], preferred_element_type=F32)

    def finish(j, yj):
        for k in range(Q):
            y_scr[j, pl.ds(k, nchunk, stride=Q), :] = yj[:, k * LANES:(k + 1) * LANES]
        lanes = pl.ds(j * LANES, LANES)
        y = y_scr[j] + d_ref[:, lanes] * u_scr[j]
        o_ref[:, :, lanes] = jax.nn.gelu(y).astype(BF16).reshape(nb, tt, LANES)

    yj = state_to_output(0)
    for j in lane_tiles:
        y_next = state_to_output(j + 1) if j + 1 < N_LANE_TILES else None
        finish(j, yj)
        yj = y_next


def _col_block(rows, width, block):
    return pl.BlockSpec((rows, width), lambda *_: (0, block), pipeline_mode=pl.Buffered(1))


def _s5_call(x, g, w_in, b_in, ap_re, ap_im, bt_re, bt_im, ct_re, ct_im, d_skip):
    nb, seq = x.shape[0], x.shape[1]
    tt = S5_ROWS // nb
    rows = S5_ROWS
    nchunk = rows // Q
    ops = (ap_re, ap_im, bt_re, bt_im, ct_re, ct_im)
    qx, st = Q * LANES, 2 * STATE_LANES
    return pl.pallas_call(
        _s5_body,
        out_shape=jax.ShapeDtypeStruct((nb, seq, SSM_WIDTH), BF16),
        grid=(seq // tt,),
        in_specs=[
            pl.BlockSpec((nb, tt, D_MODEL), lambda i: (0, i, 0)),
            _const_spec(g.shape), _col_block(D_MODEL, SSM_WIDTH, 0), _col_block(1, SSM_WIDTH, 0),
        ] + [_const_spec(a.shape) for a in ops] + [_const_spec(d_skip.shape)],
        out_specs=pl.BlockSpec((nb, tt, SSM_WIDTH), lambda i: (0, i, 0)),
        scratch_shapes=[
            pltpu.VMEM((N_LANE_TILES, rows, LANES), F32),
            pltpu.VMEM((N_LANE_TILES, rows, LANES), F32),
            pltpu.VMEM((N_LANE_TILES, 2 * STATE_LANES // LANES, nchunk, LANES), F32),
            pltpu.VMEM((N_LANE_TILES, 2 * STATE_LANES // LANES, nchunk, LANES), F32),
            pltpu.VMEM((N_LANE_TILES, nb, 2 * STATE_LANES), F32),
            pltpu.VMEM((N_LANE_TILES, qx, qx), BF16),
            pltpu.VMEM((N_LANE_TILES, qx, st), BF16),
            pltpu.VMEM((N_LANE_TILES, st, qx), BF16),
        ],
        compiler_params=pltpu.CompilerParams(
            dimension_semantics=("arbitrary",), vmem_limit_bytes=VMEM_LIMIT),
        name="s5_mixer",
    )(x, g, w_in, b_in, *ops, d_skip)


MIX_TM = 512


def _mix_body(x_ref, ys_ref, wcb_ref, wcc_ref, wcv_ref, wg_ref, bcb_ref, bcc_ref, bcv_ref, bg_ref,
              g_ref, cw_ref, cb_ref, wco_ref, wab_ref, wo_ref, g2_ref, wr_ref, br_ref,
              x1_ref, h2p_ref, route_ref, route_t_ref, cnt_ref, carry_scr, cnt_scr, x1_scr,
              *, tiles_per_seq, n_tiles):
    i = pl.program_id(0)
    tm = x_ref.shape[1]
    wc_refs = (wcb_ref, wcc_ref, wcv_ref)
    bc_refs = (bcb_ref, bcc_ref, bcv_ref)

    @pl.when(i == 0)
    def _():
        cnt_scr[...] = jnp.zeros_like(cnt_scr)
        x1_scr[...] = jnp.zeros_like(x1_scr)
        carry_scr[...] = jnp.zeros_like(carry_scr)

    def route_logits():
        h2_hi = _rms(x1_scr[...], g2_ref[...]).astype(BF16)
        h2p_ref[...] = _pack_bf16_pairs(h2_hi.astype(F32))
        lg2 = jnp.dot(h2_hi, wr_ref[...], preferred_element_type=F32)
        return lg2[:, 0:LANES] + lg2[:, LANES:2 * LANES] + br_ref[...]

    def route_finish(logits):
        cnt = cnt_scr[...]
        route, new_cnt = _route_tile(logits, cnt)
        cnt = jnp.where(i > 0, new_cnt, cnt)
        cnt_scr[...] = cnt
        cnt_ref[...] = cnt
        route_ref[...] = route
        route_t_ref[...] = route.T[0:8, :]

    @pl.when(i < n_tiles)
    def _():
        yab = jnp.dot(ys_ref[0], wab_ref[...], preferred_element_type=F32)
        logits = route_logits()

        x = x_ref[0]
        h = _rms(x, g_ref[...]).astype(BF16)
        c_b, c_c, c_v = [jnp.dot(h, w[...], preferred_element_type=F32) + b[...]
                         for w, b in zip(wc_refs, bc_refs)]
        zg = jnp.dot(h, wg_ref[...], preferred_element_type=F32) + bg_ref[...]

        route_finish(logits)

        y_a = yab[:, 0:D_MODEL] * jax.nn.sigmoid(yab[:, D_MODEL:2 * D_MODEL])

        cv = c_c * c_v
        row = lax.broadcasted_iota(jnp.int32, (tm, CONV_WIDTH), 0)
        hist = jnp.where(i % tiles_per_seq == 0, 0.0, carry_scr[...])
        last1 = hist[7:8, :]
        last2 = hist[6:7, :]
        p1 = jnp.where(row == 0, last1, pltpu.roll(cv, 1, axis=0))
        p2 = jnp.where(row == 0, last2, jnp.where(row == 1, last1, pltpu.roll(cv, 2, axis=0)))
        carry_scr[...] = cv[tm - 8:tm, :]
        conv = cw_ref[0:1, :] * p2 + cw_ref[1:2, :] * p1 + cw_ref[2:3, :] * cv + cb_ref[...]
        y_b = jnp.dot((c_b * conv).astype(BF16), wco_ref[...], preferred_element_type=F32)

        mix = (jax.nn.sigmoid(zg[:, 0:D_MODEL]) * y_a
               + jax.nn.sigmoid(zg[:, D_MODEL:2 * D_MODEL]) * y_b)
        x1 = x + jnp.dot(mix.astype(BF16), wo_ref[...], preferred_element_type=F32)
        x1_ref[...] = x1
        x1_scr[...] = x1

    @pl.when(i == n_tiles)
    def _():
        route_finish(route_logits())


def _route_tile(logits, cnt):
    tm = logits.shape[0]
    lane = lax.broadcasted_iota(jnp.int32, (tm, LANES), 1)
    neg = jnp.float32(-jnp.inf)
    big = jnp.int32(1 << 20)
    is_g = (lane >= N_EXPERTS) & (lane < N_EXPERTS + N_EXPERT_GROUPS)
    gl = jnp.where(is_g, logits, neg)
    gmax = jnp.max(gl, axis=1, keepdims=True)
    g_w = 1.0 / jnp.sum(jnp.exp(gl - gmax), axis=1, keepdims=True)
    g_idx = jnp.min(jnp.where(gl == gmax, lane - N_EXPERTS, big), axis=1, keepdims=True)
    lo = g_idx * EXPERTS_PER_GROUP
    el = jnp.where((lane >= lo) & (lane < lo + EXPERTS_PER_GROUP), logits, neg)
    m1 = jnp.max(el, axis=1, keepdims=True)
    i1 = jnp.min(jnp.where(el == m1, lane, big), axis=1, keepdims=True)
    el2 = jnp.where(lane == i1, neg, el)
    m2 = jnp.max(el2, axis=1, keepdims=True)
    i2 = jnp.min(jnp.where(el2 == m2, lane, big), axis=1, keepdims=True)
    r = jnp.exp(m2 - m1)
    w1 = g_w / (1.0 + r)
    w2 = g_w * r / (1.0 + r)

    picks = ((lane == i1) | (lane == i2)).astype(BF16)
    r_i = lax.broadcasted_iota(jnp.int32, (tm, tm), 0)
    c_i = lax.broadcasted_iota(jnp.int32, (tm, tm), 1)
    before = (c_i < r_i).astype(BF16)
    excl = jnp.dot(before, picks, preferred_element_type=F32) + cnt
    rank1 = jnp.sum(jnp.where(lane == i1, excl, 0.0), axis=1, keepdims=True)
    rank2 = jnp.sum(jnp.where(lane == i2, excl, 0.0), axis=1, keepdims=True)
    new_cnt = cnt + jnp.sum(picks.astype(F32), axis=0, keepdims=True)
    route = jnp.where(lane == 0, i1.astype(F32), 0.0)
    route = jnp.where(lane == 1, i2.astype(F32), route)
    route = jnp.where(lane == 2, w1, route)
    route = jnp.where(lane == 3, w2, route)
    route = jnp.where(lane == 4, rank1, route)
    route = jnp.where(lane == 5, rank2, route)
    return route, new_cnt


def _mix_call(x, ys, w_in, b_in, g, cw, cb, wco, wab, wo, g2, wr, br):
    nb, seq, _ = ys.shape
    T = nb * seq
    nl = seq // MIX_TM
    n = nb * nl
    consts = (g, cw, cb, wco, wab, wo, g2, wr, br)
    gates = 2 * D_MODEL
    in_proj_specs = ([_col_block(D_MODEL, CONV_WIDTH, b) for b in (1, 2, 3)]
                     + [_col_block(D_MODEL, gates, 1)]
                     + [_col_block(1, CONV_WIDTH, b) for b in (1, 2, 3)]
                     + [_col_block(1, gates, 1)])
    cur = lambda i: jnp.minimum(i, n - 1)
    prev = lambda i: jnp.maximum(i - 1, 0)
    return pl.pallas_call(
        functools.partial(_mix_body, tiles_per_seq=nl, n_tiles=n),
        out_shape=(jax.ShapeDtypeStruct((T, D_MODEL), F32),
                   jax.ShapeDtypeStruct((T, D_MODEL // 2), jnp.int32),
                   jax.ShapeDtypeStruct((T, LANES), F32),
                   jax.ShapeDtypeStruct((8, T), F32),
                   jax.ShapeDtypeStruct((1, LANES), F32)),
        grid=(n + 1,),
        in_specs=[pl.BlockSpec((1, MIX_TM, D_MODEL),
                               lambda i: (cur(i) // nl, cur(i) % nl, 0)),
                  pl.BlockSpec((1, MIX_TM, SSM_WIDTH), lambda i: (cur(i) // nl, cur(i) % nl, 0))]
                 + in_proj_specs + [_const_spec(c.shape) for c in consts],
        out_specs=(pl.BlockSpec((MIX_TM, D_MODEL), lambda i: (cur(i), 0)),
                   pl.BlockSpec((MIX_TM, D_MODEL // 2), lambda i: (prev(i), 0)),
                   pl.BlockSpec((MIX_TM, LANES), lambda i: (prev(i), 0)),
                   pl.BlockSpec((8, MIX_TM), lambda i: (0, prev(i))),
                   pl.BlockSpec((1, LANES), lambda i: (0, 0))),
        scratch_shapes=[pltpu.VMEM((8, CONV_WIDTH), F32), pltpu.VMEM((1, LANES), F32),
                        pltpu.VMEM((MIX_TM, D_MODEL), F32)],
        compiler_params=pltpu.CompilerParams(
            dimension_semantics=("arbitrary",), vmem_limit_bytes=VMEM_LIMIT),
        name="conv_glu_router",
    )(x, ys, *([w_in] * 4), *([b_in] * 4), *consts)


SC_CORES = 2
SC_SUBCORES = 16
SC_WORKERS = SC_CORES * SC_SUBCORES
SC_ROWS = 64
SC_NBUF = 2


def _sc_mesh():
    return plsc.VectorSubcoreMesh(core_axis_name="c", subcore_axis_name="s")


def _sc_worker_id():
    return lax.axis_index("s") * SC_CORES + lax.axis_index("c")


def _sc_ring(nch, get, puts):
    for b in range(SC_NBUF - 1):
        get(b, b).start()

    @pl.loop(0, nch, step=SC_NBUF)
    def _(j0):
        for b in range(SC_NBUF):
            j = j0 + b
            refill = (b - 1) % SC_NBUF
            get(j, b).wait()

            @pl.when(j + SC_NBUF - 1 < nch)
            def _():
                @pl.when(j >= 1)
                def _():
                    for c in puts(j - 1, refill):
                        c.wait()
                get(j + SC_NBUF - 1, refill).start()

            for c in puts(j, b):
                c.start()

    for b in range(SC_NBUF):
        for c in puts(nch - SC_NBUF + b, b):
            c.wait()


def _sc_dispatch(rows, pos_a, pos_b, n_out):
    T, W = rows.shape
    per_w = T // SC_WORKERS
    nch = per_w // SC_ROWS
    assert per_w * SC_WORKERS == T and nch * SC_ROWS == per_w and nch % SC_NBUF == 0
    idx_a = pos_a.reshape(SC_WORKERS, nch, SC_ROWS)
    idx_b = pos_b.reshape(SC_WORKERS, nch, SC_ROWS)

    @functools.partial(
        pl.kernel, mesh=_sc_mesh(),
        out_type=jax.ShapeDtypeStruct((n_out, W), rows.dtype),
        scratch_types=[
            pltpu.VMEM((nch, SC_ROWS), jnp.int32),
            pltpu.VMEM((nch, SC_ROWS), jnp.int32),
            pltpu.VMEM((SC_NBUF, SC_ROWS, W), rows.dtype),
            pltpu.SemaphoreType.DMA((SC_NBUF,)),
            pltpu.SemaphoreType.DMA((SC_NBUF,)),
            pltpu.SemaphoreType.DMA((SC_NBUF,)),
        ],
        name="moe_dispatch",
    )
    def k(rows_hbm, ia_hbm, ib_hbm, out_hbm, ia_v, ib_v, buf, gsem, asem, bsem):
        wid = _sc_worker_id()
        base = wid * per_w
        pltpu.sync_copy(ia_hbm.at[wid], ia_v)
        pltpu.sync_copy(ib_hbm.at[wid], ib_v)

        def get(j, b):
            return pltpu.make_async_copy(
                rows_hbm.at[pl.ds(base + j * SC_ROWS, SC_ROWS)], buf.at[b], gsem.at[b])

        def put_a(j, b):
            return pltpu.make_async_copy(buf.at[b], out_hbm.at[ia_v.at[j]], asem.at[b])

        def put_b(j, b):
            return pltpu.make_async_copy(buf.at[b], out_hbm.at[ib_v.at[j]], bsem.at[b])

        _sc_ring(nch, get, lambda j, b: (put_a(j, b), put_b(j, b)))

    return k(rows, idx_a, idx_b)


def _sc_gather(table, idx):
    _, W = table.shape
    B = idx.shape[0]
    per_w = B // SC_WORKERS
    nch = per_w // SC_ROWS
    assert per_w * SC_WORKERS == B and nch * SC_ROWS == per_w and nch % SC_NBUF == 0
    idx3 = idx.reshape(SC_WORKERS, nch, SC_ROWS)

    @functools.partial(
        pl.kernel, mesh=_sc_mesh(),
        out_type=jax.ShapeDtypeStruct((B, W), table.dtype),
        scratch_types=[
            pltpu.VMEM((nch, SC_ROWS), jnp.int32),
            pltpu.VMEM((SC_NBUF, SC_ROWS, W), table.dtype),
            pltpu.SemaphoreType.DMA((SC_NBUF,)),
            pltpu.SemaphoreType.DMA((SC_NBUF,)),
        ],
        name="moe_combine_gather",
    )
    def k(table_hbm, idx_hbm, out_hbm, idx_v, buf, gsem, osem):
        wid = _sc_worker_id()
        base = wid * per_w
        pltpu.sync_copy(idx_hbm.at[wid], idx_v)

        def get(j, b):
            return pltpu.make_async_copy(table_hbm.at[idx_v.at[j]], buf.at[b], gsem.at[b])

        def put(j, b):
            return pltpu.make_async_copy(
                buf.at[b], out_hbm.at[pl.ds(base + j * SC_ROWS, SC_ROWS)], osem.at[b])

        _sc_ring(nch, get, lambda j, b: (put(j, b),))

    return k(table, idx3)


MOE_RT = 1024
MOE_SUB = 256
MOE_IN_BUFFERS = 3


def _moe_tile(i, te_ref, nx_ref, seg_ref, ns_ref, x_ref, wg_hbm, wu_hbm, wd_hbm, o_ref,
              wg_scr, wu_scr, wd_scr, wg_buf, wu_buf, wd_buf, sem):
    expert = te_ref[i]
    slot = seg_ref[i] % 2

    def weight_copies(e, b):
        return (pltpu.make_async_copy(wg_hbm.at[e], wg_buf.at[b], sem.at[b, 0]),
                pltpu.make_async_copy(wu_hbm.at[e], wu_buf.at[b], sem.at[b, 1]),
                pltpu.make_async_copy(wd_hbm.at[e], wd_buf.at[b], sem.at[b, 2]))

    @pl.when(i == 0)
    def _():
        for c in weight_copies(expert, 0):
            c.start()

    @pl.when((i == 0) | (expert != te_ref[jnp.maximum(i - 1, 0)]))
    def _():
        for c in weight_copies(expert, slot):
            c.wait()
        wg_scr[...] = wg_buf[slot].astype(BF16)
        wu_scr[...] = wu_buf[slot].astype(BF16)
        wd_scr[...] = wd_buf[slot].astype(BF16)

        @pl.when(nx_ref[i] >= 0)
        def _():
            for c in weight_copies(nx_ref[i], 1 - slot):
                c.start()

    half = D_MODEL // 2

    def up_proj(s):
        lo, hi = _unpack_bf16_pairs(x_ref[pl.ds(s * MOE_SUB, MOE_SUB), :])
        lo = lo.astype(BF16)
        hi = hi.astype(BF16)
        gate = (jnp.dot(lo, wg_scr[0:half, :], preferred_element_type=F32)
                + jnp.dot(hi, wg_scr[half:D_MODEL, :], preferred_element_type=F32))
        up = (jnp.dot(lo, wu_scr[0:half, :], preferred_element_type=F32)
              + jnp.dot(hi, wu_scr[half:D_MODEL, :], preferred_element_type=F32))
        return gate, up

    def down_proj(gate, up):
        hid = (jax.nn.silu(gate) * up).astype(BF16)
        return jnp.dot(hid, wd_scr[...], preferred_element_type=F32)

    def store(s, y):
        o_ref[pl.ds(s * MOE_SUB, MOE_SUB), :] = _pack_bf16_pairs(y.astype(BF16).astype(F32))

    def run(n_sub):
        gu = {0: up_proj(0)}
        ys = {}
        for s in range(n_sub):
            if s + 1 < n_sub:
                gu[s + 1] = up_proj(s + 1)
            ys[s] = down_proj(*gu.pop(s))
            if s >= 1:
                store(s - 1, ys.pop(s - 1))
        store(n_sub - 1, ys.pop(n_sub - 1))

    for n_sub in range(1, x_ref.shape[0] // MOE_SUB + 1):
        pl.when(ns_ref[i] == n_sub)(functools.partial(run, n_sub))


def _moe_call(plan, xs, wg, wu, wd):
    R = xs.shape[0]
    half = D_MODEL // 2

    def outer(te_ref, nt_ref, nx_ref, seg_ref, ns_ref, x_hbm, wg_hbm, wu_hbm, wd_hbm, o_hbm,
              wg_scr, wu_scr, wd_scr, wg_buf, wu_buf, wd_buf, sem, step_ref):
        row_map = lambda i: (jnp.minimum(i, nt_ref[0] - 1), 0)
        step_ref[0] = 0

        def tile(x_ref, o_ref):
            i = step_ref[0]
            step_ref[0] = i + 1
            _moe_tile(i, te_ref, nx_ref, seg_ref, ns_ref, x_ref, wg_hbm, wu_hbm, wd_hbm, o_ref,
                      wg_scr, wu_scr, wd_scr, wg_buf, wu_buf, wd_buf, sem)

        pltpu.emit_pipeline(
            tile, grid=(R // MOE_RT,),
            in_specs=[pl.BlockSpec((MOE_RT, half), row_map,
                                   pipeline_mode=pl.Buffered(MOE_IN_BUFFERS))],
            out_specs=[pl.BlockSpec((MOE_RT, half), row_map)],
        )(x_hbm, o_hbm)

    hbm = pl.BlockSpec(memory_space=pl.ANY)
    smem = pl.BlockSpec(memory_space=pltpu.SMEM)
    return pl.pallas_call(
        outer,
        out_shape=jax.ShapeDtypeStruct((R, half), jnp.int32),
        in_specs=[smem] * 5 + [hbm] * 4,
        out_specs=hbm,
        scratch_shapes=[pltpu.VMEM((D_MODEL, EXPERT_HIDDEN), BF16),
                        pltpu.VMEM((D_MODEL, EXPERT_HIDDEN), BF16),
                        pltpu.VMEM((EXPERT_HIDDEN, D_MODEL), BF16),
                        pltpu.VMEM((2, D_MODEL, EXPERT_HIDDEN), F32),
                        pltpu.VMEM((2, D_MODEL, EXPERT_HIDDEN), F32),
                        pltpu.VMEM((2, EXPERT_HIDDEN, D_MODEL), F32),
                        pltpu.SemaphoreType.DMA((2, 3)),
                        pltpu.SMEM((1,), jnp.int32)],
        compiler_params=pltpu.CompilerParams(vmem_limit_bytes=VMEM_LIMIT),
        name="moe_experts",
    )(*plan, xs, wg, wu, wd)


def _moe_plan(route_t, counts, n_rows):
    cnt = counts[0, :N_EXPERTS].astype(jnp.int32)
    tiles = (cnt + MOE_RT - 1) // MOE_RT
    tile_end = jnp.cumsum(tiles)
    n_tiles = tile_end[-1:]
    row_start = (tile_end - tiles) * MOE_RT
    ids = route_t[0:2].astype(jnp.int32)
    ranks = route_t[4:6].astype(jnp.int32)
    experts = jnp.arange(N_EXPERTS, dtype=jnp.int32)[:, None, None]
    pos = ranks + jnp.sum(jnp.where(ids[None] == experts, row_start[:, None, None], 0), axis=0)
    tile_id = jnp.minimum(jnp.arange(n_rows // MOE_RT, dtype=jnp.int32), n_tiles - 1)
    tile_expert = jnp.sum((tile_id[:, None] >= tile_end[None, :]).astype(jnp.int32), axis=1)
    e_ids = jnp.arange(N_EXPERTS, dtype=jnp.int32)
    later = (e_ids[None, :] > e_ids[:, None]) & (tiles[None, :] > 0)
    nxt = jnp.min(jnp.where(later, e_ids[None, :], N_EXPERTS), axis=1)
    nxt = jnp.where(nxt == N_EXPERTS, -1, nxt)
    seg = jnp.cumsum((tiles > 0).astype(jnp.int32)) - 1
    pick = tile_expert[:, None] == e_ids[None, :]
    tile_next = jnp.sum(jnp.where(pick, nxt[None, :], 0), axis=1)
    tile_seg = jnp.sum(jnp.where(pick, seg[None, :], 0), axis=1)
    first = jnp.sum(jnp.where(pick, (tile_end - tiles)[None, :], 0), axis=1)
    rows_left = (jnp.sum(jnp.where(pick, cnt[None, :], 0), axis=1)
                 - (jnp.arange(n_rows // MOE_RT, dtype=jnp.int32) - first) * MOE_RT)
    tile_subs = (jnp.clip(rows_left, 0, MOE_RT) + MOE_SUB - 1) // MOE_SUB
    return pos, (tile_expert, n_tiles, tile_next, tile_seg, tile_subs)


PLE_TM = 1024
PLE_SUB = 256


def _ple_body(x_ref, ya_ref, yb_ref, route_ref, p_ref, g3_ref, wpg_ref, bpg_ref, wple_ref, gf_ref,
              o_ref):
    n_sub = x_ref.shape[0] // PLE_SUB
    g3, gf, bpg = g3_ref[...], gf_ref[...], bpg_ref[...]

    def head(s):
        rows = pl.ds(s * PLE_SUB, PLE_SUB)
        ya = jnp.concatenate(_unpack_bf16_pairs(ya_ref[rows, :]), axis=1)
        yb = jnp.concatenate(_unpack_bf16_pairs(yb_ref[rows, :]), axis=1)
        route = route_ref[rows, :]
        x2 = x_ref[rows, :] + route[:, 2:3] * ya + route[:, 3:4] * yb
        return x2, _rms(x2, g3).astype(BF16)

    def dots(s, h3):
        rows = pl.ds(s * PLE_SUB, PLE_SUB)
        zg = jnp.dot(h3, wpg_ref[...], preferred_element_type=F32)
        pe = jnp.dot(p_ref[rows, :].astype(BF16), wple_ref[...], preferred_element_type=F32)
        return zg, pe

    def tail(s, x2, zg, pe):
        x3 = x2 + jax.nn.sigmoid(zg + bpg) * pe
        o_ref[pl.ds(s * PLE_SUB, PLE_SUB), :] = _rms(x3, gf)

    x2s, mm = {}, {}
    x2s[0], h3 = head(0)
    for s in range(n_sub):
        mm[s] = dots(s, h3)
        if s + 1 < n_sub:
            x2s[s + 1], h3 = head(s + 1)
        if s >= 1:
            tail(s - 1, x2s.pop(s - 1), *mm.pop(s - 1))
    tail(n_sub - 1, x2s.pop(n_sub - 1), *mm.pop(n_sub - 1))


def _ple_call(x1, y_picks, route, p, g3, wpg, bpg, wple, gf):
    T = x1.shape[0]
    nt = T // PLE_TM
    consts = (g3, wpg, bpg, wple, gf)
    tok = lambda i: (i, 0)
    return pl.pallas_call(
        _ple_body,
        out_shape=jax.ShapeDtypeStruct((T, D_MODEL), F32),
        grid=(nt,),
        in_specs=[pl.BlockSpec((PLE_TM, D_MODEL), tok),
                  pl.BlockSpec((PLE_TM, D_MODEL // 2), tok),
                  pl.BlockSpec((PLE_TM, D_MODEL // 2), lambda i: (i + nt, 0)),
                  pl.BlockSpec((PLE_TM, LANES), tok),
                  pl.BlockSpec((PLE_TM, p.shape[1]), tok)]
                 + [_const_spec(c.shape) for c in consts],
        out_specs=pl.BlockSpec((PLE_TM, D_MODEL), tok),
        compiler_params=pltpu.CompilerParams(
            dimension_semantics=("arbitrary",), vmem_limit_bytes=VMEM_LIMIT),
        name="ple_final",
    )(x1, y_picks, y_picks, route, p, *consts)


def _layer(x, p, norm_mix, w_in, b_in, lam_re, lam_im, log_dt, b_re, b_im, c_re, c_im, d_skip,
           w_glu_a, w_glu_b, conv_w, conv_b, w_conv_out, w_o, norm_ffn, w_rg, b_rg, w_re, b_re_r,
           w_eg, w_eu, w_ed, norm_ple, w_ple, w_pg, b_pg, norm_out):
    nb, seq, d = x.shape
    T = nb * seq
    row = lambda v: v.reshape(1, -1).astype(F32)
    assert w_in.shape[1] == SSM_WIDTH + 3 * CONV_WIDTH + 2 * D_MODEL and SSM_WIDTH == CONV_WIDTH
    w_in16 = w_in.astype(BF16)
    b_in_row = row(b_in)

    s5_ops = _s5_operators(lam_re, lam_im, log_dt, b_re, b_im, c_re, c_im)
    s5_consts = (row(norm_mix), w_in16, b_in_row, *s5_ops, row(d_skip))

    lane_pad = LANES - N_EXPERTS - N_EXPERT_GROUPS
    w_r = jnp.pad(jnp.concatenate([w_re, w_rg], axis=1).astype(F32), ((0, 0), (0, lane_pad)))
    b_r = jnp.pad(jnp.concatenate([b_re_r, b_rg]).astype(F32), (0, lane_pad)).reshape(1, LANES)
    w_r_hi = w_r.astype(BF16)
    w_r_lo = (w_r - w_r_hi.astype(F32)).astype(BF16)

    mix_consts = (w_in16, b_in_row, row(norm_mix),
                  conv_w.astype(F32), row(conv_b), w_conv_out.astype(BF16),
                  jnp.concatenate([w_glu_a, w_glu_b], axis=1).astype(BF16), w_o.astype(BF16),
                  row(norm_ffn), jnp.concatenate([w_r_hi, w_r_lo], axis=1), b_r)
    ple_consts = (row(norm_ple), w_pg.astype(BF16), row(b_pg), w_ple.astype(BF16), row(norm_out))
    p2d = p.reshape(T, -1)
    n_rows = 2 * T + N_EXPERTS * MOE_RT

    ys = _s5_call(x, *s5_consts)
    x1, h2p, route, route_t, counts = _mix_call(x, ys, *mix_consts)
    pos, plan = _moe_plan(route_t, counts, n_rows)
    xs = _sc_dispatch(h2p, pos[0], pos[1], n_rows)
    ysort = _moe_call(plan, xs, w_eg, w_eu, w_ed)
    y_picks = _sc_gather(ysort, pos.reshape(-1))
    out = _ple_call(x1, y_picks, route, p2d, *ple_consts)
    return out.reshape(nb, seq, d)


def kernel(x, p, norm_mix, w_in, b_in, ssm_lam_re, ssm_lam_im, ssm_log_dt, ssm_b_re, ssm_b_im, ssm_c_re, ssm_c_im, ssm_d, w_glu_a, w_glu_b, conv_w, conv_b, w_conv_out, w_o, norm_ffn, w_router_group, b_router_group, w_router_expert, b_router_expert, w_exp_gate, w_exp_up, w_exp_down, norm_ple, w_ple, w_ple_gate, b_ple_gate, norm_final):
    assert p.shape[0] == 1, "the final RMSNorm is fused into the (single) layer's last kernel"
    i = 0
    return _layer(x, p[i], norm_mix[i], w_in[i], b_in[i], ssm_lam_re[i], ssm_lam_im[i],
                  ssm_log_dt[i], ssm_b_re[i], ssm_b_im[i], ssm_c_re[i], ssm_c_im[i], ssm_d[i],
                  w_glu_a[i], w_glu_b[i], conv_w[i], conv_b[i], w_conv_out[i], w_o[i],
                  norm_ffn[i], w_router_group[i], b_router_group[i], w_router_expert[i],
                  b_router_expert[i], w_exp_gate[i], w_exp_up[i], w_exp_down[i], norm_ple[i],
                  w_ple[i], w_ple_gate[i], b_ple_gate[i], norm_final)
```

```python
import functools
import math

import jax
import jax.numpy as jnp
from jax import lax
from jax.experimental import pallas as pl
from jax.experimental.pallas import tpu as pltpu
from jax.experimental.pallas import tpu_sc as plsc

F32 = jnp.float32
BF16 = jnp.bfloat16

D_MODEL = 1024
SSM_WIDTH = 512
SSM_GROUP = 16
SSM_GROUPS = 32
SSM_STATE = 64
CONV_WIDTH = 512
N_EXPERT_GROUPS = 4
EXPERTS_PER_GROUP = 8
N_EXPERTS = 32
EXPERT_HIDDEN = 256
NORM_EPS = 1e-6

LANES = 128
Q = 8
GROUPS_PER_LANE_TILE = LANES // SSM_GROUP
N_LANE_TILES = SSM_WIDTH // LANES
STATE_LANES = GROUPS_PER_LANE_TILE * SSM_STATE
S5_ROWS = 1024
VMEM_LIMIT = 56 * 1024 * 1024


def _rms(x, g):
    return x * lax.rsqrt(jnp.mean(x * x, axis=-1, keepdims=True) + NORM_EPS) * g


def _pack_bf16_pairs(a):
    w = a.shape[1] // 2
    lo = lax.shift_right_logical(lax.bitcast_convert_type(a[:, :w], jnp.int32), 16)
    hi = lax.bitcast_convert_type(a[:, w:], jnp.int32) & jnp.int32(-65536)
    return lo | hi


def _unpack_bf16_pairs(word):
    lo = lax.bitcast_convert_type(lax.shift_left(word, 16), F32)
    hi = lax.bitcast_convert_type(word & jnp.int32(-65536), F32)
    return lo, hi


def _const_spec(shape):
    n = len(shape)
    return pl.BlockSpec(shape, lambda *_: (0,) * n, pipeline_mode=pl.Buffered(1))


def _s5_operators(lam_re, lam_im, log_dt, b_re, b_im, c_re, c_im):
    G, P, H = SSM_GROUPS, SSM_STATE, SSM_GROUP
    lr = lam_re.astype(F32)
    li = lam_im.astype(F32)
    dt = jnp.exp(log_dt.astype(F32))[:, None]

    def apow(n):
        n = n.astype(F32)[:, None, None]
        mag = jnp.exp(lr * dt * n)
        ang = li * dt * n
        return mag * jnp.cos(ang), mag * jnp.sin(ang)

    a1_re, a1_im = apow(jnp.ones((1,), F32))
    nr = a1_re[0] - 1.0
    ni = a1_im[0]
    den = lr * lr + li * li
    f_re = (nr * lr + ni * li) / den
    f_im = (ni * lr - nr * li) / den
    br = b_re.astype(F32)
    bi = b_im.astype(F32)
    bbar_re = f_re[:, :, None] * br - f_im[:, :, None] * bi
    bbar_im = f_re[:, :, None] * bi + f_im[:, :, None] * br
    to_rows = lambda v, perm: jnp.transpose(v, perm).reshape(H, G * P)
    bt_re = to_rows(bbar_re, (2, 0, 1))
    bt_im = to_rows(bbar_im, (2, 0, 1))
    ct_re = to_rows(c_re.astype(F32), (1, 0, 2))
    ct_im = to_rows(c_im.astype(F32), (1, 0, 2))
    ap_re, ap_im = apow(jnp.arange(Q + 1))
    ap_re = ap_re.reshape(Q + 1, G * P)
    ap_im = ap_im.reshape(Q + 1, G * P)

    return ap_re, ap_im, bt_re, bt_im, ct_re, ct_im


def _s5_build_operators(j, apr_ref, api_ref, btr_ref, bti_ref, ctr_ref, cti_ref,
                        k_scr, m_scr, n_scr):
    sl = pl.ds(j * STATE_LANES, STATE_LANES)
    ri = lax.broadcasted_iota(jnp.int32, (LANES, STATE_LANES), 0)
    li = lax.broadcasted_iota(jnp.int32, (LANES, STATE_LANES), 1)
    same_group = (ri // SSM_GROUP) == (li // SSM_STATE)

    def expand(ref):
        tiled = jnp.concatenate([ref[:, sl]] * GROUPS_PER_LANE_TILE, axis=0)
        return jnp.where(same_group, tiled, 0.0)

    b_re, b_im, c_re, c_im = expand(btr_ref), expand(bti_ref), expand(ctr_ref), expand(cti_ref)

    def cmul(n, x_re, x_im):
        a_re = apr_ref[n:n + 1, sl]
        a_im = api_ref[n:n + 1, sl]
        return a_re * x_re - a_im * x_im, a_re * x_im + a_im * x_re

    m_blocks = []
    for k in range(Q):
        g_re, g_im = cmul(Q - 1 - k, b_re, b_im)
        m_blocks.append(jnp.concatenate([g_re, g_im], axis=1))
        m_scr[j, k * LANES:(k + 1) * LANES, :] = m_blocks[k].astype(BF16)
    m = jnp.concatenate(m_blocks, axis=0)

    for t in range(Q):
        g_re, g_im = cmul(t + 1, c_re, c_im)
        nt = jnp.concatenate([g_re, -g_im], axis=1)
        n_scr[j, :, t * LANES:(t + 1) * LANES] = nt.T.astype(BF16)

    n0t = jnp.concatenate([c_re, -c_im], axis=1)
    split = lambda v: (v.astype(BF16), (v - v.astype(BF16).astype(F32)).astype(BF16))
    m_hi, m_lo = split(m)
    n_hi, n_lo = split(n0t)
    dot_t = lambda a, b: lax.dot_general(a, b, (((1,), (1,)), ((), ())),
                                         preferred_element_type=F32)
    p = dot_t(m_hi, n_hi) + (dot_t(m_hi, n_lo) + dot_t(m_lo, n_hi))
    zeros = jnp.zeros((LANES, LANES), BF16)
    for lag in range(Q):
        blk = p[(Q - 1 - lag) * LANES:(Q - lag) * LANES, :].astype(BF16)
        for t in range(lag, Q):
            k = t - lag
            k_scr[j, k * LANES:(k + 1) * LANES, t * LANES:(t + 1) * LANES] = blk
    for t in range(Q):
        for k in range(t + 1, Q):
            k_scr[j, k * LANES:(k + 1) * LANES, t * LANES:(t + 1) * LANES] = zeros


def _s5_body(x_ref, g_ref, wu_ref, bu_ref, apr_ref, api_ref, btr_ref, bti_ref, ctr_ref, cti_ref,
             d_ref, o_ref, u_scr, y_scr, z_scr, ss_scr, carry_scr, k_ref, m_ref, n_ref):
    nb, tt = x_ref.shape[0], x_ref.shape[1]
    rows = nb * tt
    nchunk = rows // Q

    @pl.when(pl.program_id(0) == 0)
    def _():
        carry_scr[...] = jnp.zeros_like(carry_scr)
        for j in range(N_LANE_TILES):
            _s5_build_operators(j, apr_ref, api_ref, btr_ref, bti_ref, ctr_ref, cti_ref,
                                k_ref, m_ref, n_ref)

    hb = nb // 2
    for r in range(2):
        x = x_ref[r * hb:(r + 1) * hb].reshape(rows // 2, D_MODEL)
        h = _rms(x, g_ref[...]).astype(BF16)
        u = jnp.dot(h, wu_ref[...], preferred_element_type=F32) + bu_ref[...]
        for j in range(N_LANE_TILES):
            u_scr[j, pl.ds(r * (rows // 2), rows // 2), :] = u[:, j * LANES:(j + 1) * LANES]

    n_st = STATE_LANES // LANES
    cpt = tt // Q

    lane_tiles = range(N_LANE_TILES)


    xs = [jnp.concatenate([u_scr[j, pl.ds(k, nchunk, stride=Q), :] for k in range(Q)],
                          axis=1).astype(BF16) for j in lane_tiles]
    for j in lane_tiles:
        z = jnp.dot(xs[j], m_ref[j], preferred_element_type=F32)
        for i in range(2 * n_st):
            z_scr[j, i] = z[:, i * LANES:(i + 1) * LANES]
    y_intra = [jnp.dot(xs[j], k_ref[j], preferred_element_type=F32) for j in lane_tiles]

    aq = [(jnp.broadcast_to(apr_ref[Q:Q + 1, pl.ds(j * STATE_LANES, STATE_LANES)], (nb, STATE_LANES)),
           jnp.broadcast_to(api_ref[Q:Q + 1, pl.ds(j * STATE_LANES, STATE_LANES)], (nb, STATE_LANES)))
          for j in lane_tiles]
    st = [(carry_scr[j, :, pl.ds(0, STATE_LANES)], carry_scr[j, :, pl.ds(STATE_LANES, STATE_LANES)])
          for j in lane_tiles]
    for c in range(cpt):
        seq_rows = pl.ds(c, nb, stride=cpt)
        for j in lane_tiles:
            s_re, s_im = st[j]
            aqr, aqi = aq[j]
            for i in range(n_st):
                ss_scr[j, i, seq_rows, :] = s_re[:, i * LANES:(i + 1) * LANES]
                ss_scr[j, n_st + i, seq_rows, :] = s_im[:, i * LANES:(i + 1) * LANES]
            z_re = jnp.concatenate([z_scr[j, i, seq_rows, :] for i in range(n_st)], axis=1)
            z_im = jnp.concatenate([z_scr[j, n_st + i, seq_rows, :] for i in range(n_st)], axis=1)
            st[j] = (aqr * s_re - aqi * s_im + z_re, aqr * s_im + aqi * s_re + z_im)
    for j in lane_tiles:
        carry_scr[j, :, pl.ds(0, STATE_LANES)] = st[j][0]
        carry_scr[j, :, pl.ds(STATE_LANES, STATE_LANES)] = st[j][1]

    def state_to_output(j):
        ss = jnp.concatenate([ss_scr[j, i] for i in range(2 * n_st)], axis=1).astype(BF16)
        return y_intra[j] + jnp.dot(ss, n_ref[j], preferred_element_type=F32)

    def finish(j, yj):
        for k in range(Q):
            y_scr[j, pl.ds(k, nchunk, stride=Q), :] = yj[:, k * LANES:(k + 1) * LANES]
        lanes = pl.ds(j * LANES, LANES)
        y = y_scr[j] + d_ref[:, lanes] * u_scr[j]
        o_ref[:, :, lanes] = jax.nn.gelu(y).astype(BF16).reshape(nb, tt, LANES)

    yj = state_to_output(0)
    for j in lane_tiles:
        y_next = state_to_output(j + 1) if j + 1 < N_LANE_TILES else None
        finish(j, yj)
        yj = y_next


def _col_block(rows, width, block):
    return pl.BlockSpec((rows, width), lambda *_: (0, block), pipeline_mode=pl.Buffered(1))


def _s5_call(x, g, w_in, b_in, ap_re, ap_im, bt_re, bt_im, ct_re, ct_im, d_skip):
    nb, seq = x.shape[0], x.shape[1]
    tt = S5_ROWS // nb
    rows = S5_ROWS
    nchunk = rows // Q
    ops = (ap_re, ap_im, bt_re, bt_im, ct_re, ct_im)
    qx, st = Q * LANES, 2 * STATE_LANES
    return pl.pallas_call(
        _s5_body,
        out_shape=jax.ShapeDtypeStruct((nb, seq, SSM_WIDTH), BF16),
        grid=(seq // tt,),
        in_specs=[
            pl.BlockSpec((nb, tt, D_MODEL), lambda i: (0, i, 0)),
            _const_spec(g.shape), _col_block(D_MODEL, SSM_WIDTH, 0), _col_block(1, SSM_WIDTH, 0),
        ] + [_const_spec(a.shape) for a in ops] + [_const_spec(d_skip.shape)],
        out_specs=pl.BlockSpec((nb, tt, SSM_WIDTH), lambda i: (0, i, 0)),
        scratch_shapes=[
            pltpu.VMEM((N_LANE_TILES, rows, LANES), F32),
            pltpu.VMEM((N_LANE_TILES, rows, LANES), F32),
            pltpu.VMEM((N_LANE_TILES, 2 * STATE_LANES // LANES, nchunk, LANES), F32),
            pltpu.VMEM((N_LANE_TILES, 2 * STATE_LANES // LANES, nchunk, LANES), F32),
            pltpu.VMEM((N_LANE_TILES, nb, 2 * STATE_LANES), F32),
            pltpu.VMEM((N_LANE_TILES, qx, qx), BF16),
            pltpu.VMEM((N_LANE_TILES, qx, st), BF16),
            pltpu.VMEM((N_LANE_TILES, st, qx), BF16),
        ],
        compiler_params=pltpu.CompilerParams(
            dimension_semantics=("arbitrary",), vmem_limit_bytes=VMEM_LIMIT),
        name="s5_mixer",
    )(x, g, w_in, b_in, *ops, d_skip)


MIX_TM = 512


def _mix_body(x_ref, ys_ref, wcb_ref, wcc_ref, wcv_ref, wg_ref, bcb_ref, bcc_ref, bcv_ref, bg_ref,
              g_ref, cw_ref, cb_ref, wco_ref, wab_ref, wo_ref, g2_ref, wr_ref, br_ref,
              x1_ref, h2p_ref, route_ref, route_t_ref, cnt_ref, carry_scr, cnt_scr, x1_scr,
              *, tiles_per_seq, n_tiles):
    i = pl.program_id(0)
    tm = x_ref.shape[1]
    wc_refs = (wcb_ref, wcc_ref, wcv_ref)
    bc_refs = (bcb_ref, bcc_ref, bcv_ref)

    @pl.when(i == 0)
    def _():
        cnt_scr[...] = jnp.zeros_like(cnt_scr)
        x1_scr[...] = jnp.zeros_like(x1_scr)
        carry_scr[...] = jnp.zeros_like(carry_scr)

    def route_logits():
        h2_hi = _rms(x1_scr[...], g2_ref[...]).astype(BF16)
        h2p_ref[...] = _pack_bf16_pairs(h2_hi.astype(F32))
        lg2 = jnp.dot(h2_hi, wr_ref[...], preferred_element_type=F32)
        return lg2[:, 0:LANES] + lg2[:, LANES:2 * LANES] + br_ref[...]

    def route_finish(logits):
        cnt = cnt_scr[...]
        route, new_cnt = _route_tile(logits, cnt)
        cnt = jnp.where(i > 0, new_cnt, cnt)
        cnt_scr[...] = cnt
        cnt_ref[...] = cnt
        route_ref[...] = route
        route_t_ref[...] = route.T[0:8, :]

    @pl.when(i < n_tiles)
    def _():
        yab = jnp.dot(ys_ref[0], wab_ref[...], preferred_element_type=F32)
        logits = route_logits()

        x = x_ref[0]
        h = _rms(x, g_ref[...]).astype(BF16)
        c_b, c_c, c_v = [jnp.dot(h, w[...], preferred_element_type=F32) + b[...]
                         for w, b in zip(wc_refs, bc_refs)]
        zg = jnp.dot(h, wg_ref[...], preferred_element_type=F32) + bg_ref[...]

        route_finish(logits)

        y_a = yab[:, 0:D_MODEL] * jax.nn.sigmoid(yab[:, D_MODEL:2 * D_MODEL])

        cv = c_c * c_v
        row = lax.broadcasted_iota(jnp.int32, (tm, CONV_WIDTH), 0)
        hist = jnp.where(i % tiles_per_seq == 0, 0.0, carry_scr[...])
        last1 = hist[7:8, :]
        last2 = hist[6:7, :]
        p1 = jnp.where(row == 0, last1, pltpu.roll(cv, 1, axis=0))
        p2 = jnp.where(row == 0, last2, jnp.where(row == 1, last1, pltpu.roll(cv, 2, axis=0)))
        carry_scr[...] = cv[tm - 8:tm, :]
        conv = cw_ref[0:1, :] * p2 + cw_ref[1:2, :] * p1 + cw_ref[2:3, :] * cv + cb_ref[...]
        y_b = jnp.dot((c_b * conv).astype(BF16), wco_ref[...], preferred_element_type=F32)

        mix = (jax.nn.sigmoid(zg[:, 0:D_MODEL]) * y_a
               + jax.nn.sigmoid(zg[:, D_MODEL:2 * D_MODEL]) * y_b)
        x1 = x + jnp.dot(mix.astype(BF16), wo_ref[...], preferred_element_type=F32)
        x1_ref[...] = x1.astype(BF16)
        x1_scr[...] = x1

    @pl.when(i == n_tiles)
    def _():
        route_finish(route_logits())


def _route_tile(logits, cnt):
    tm = logits.shape[0]
    lane = lax.broadcasted_iota(jnp.int32, (tm, LANES), 1)
    neg = jnp.float32(-jnp.inf)
    big = jnp.int32(1 << 20)
    is_g = (lane >= N_EXPERTS) & (lane < N_EXPERTS + N_EXPERT_GROUPS)
    gl = jnp.where(is_g, logits, neg)
    gmax = jnp.max(gl, axis=1, keepdims=True)
    g_w = 1.0 / jnp.sum(jnp.exp(gl - gmax), axis=1, keepdims=True)
    g_idx = jnp.min(jnp.where(gl == gmax, lane - N_EXPERTS, big), axis=1, keepdims=True)
    lo = g_idx * EXPERTS_PER_GROUP
    el = jnp.where((lane >= lo) & (lane < lo + EXPERTS_PER_GROUP), logits, neg)
    m1 = jnp.max(el, axis=1, keepdims=True)
    i1 = jnp.min(jnp.where(el == m1, lane, big), axis=1, keepdims=True)
    el2 = jnp.where(lane == i1, neg, el)
    m2 = jnp.max(el2, axis=1, keepdims=True)
    i2 = jnp.min(jnp.where(el2 == m2, lane, big), axis=1, keepdims=True)
    r = jnp.exp(m2 - m1)
    w1 = g_w / (1.0 + r)
    w2 = g_w * r / (1.0 + r)

    picks = ((lane == i1) | (lane == i2)).astype(BF16)
    r_i = lax.broadcasted_iota(jnp.int32, (tm, tm), 0)
    c_i = lax.broadcasted_iota(jnp.int32, (tm, tm), 1)
    before = (c_i < r_i).astype(BF16)
    excl = jnp.dot(before, picks, preferred_element_type=F32) + cnt
    rank1 = jnp.sum(jnp.where(lane == i1, excl, 0.0), axis=1, keepdims=True)
    rank2 = jnp.sum(jnp.where(lane == i2, excl, 0.0), axis=1, keepdims=True)
    new_cnt = cnt + jnp.sum(picks.astype(F32), axis=0, keepdims=True)
    route = jnp.where(lane == 0, i1.astype(F32), 0.0)
    route = jnp.where(lane == 1, i2.astype(F32), route)
    route = jnp.where(lane == 2, w1, route)
    route = jnp.where(lane == 3, w2, route)
    route = jnp.where(lane == 4, rank1, route)
    route = jnp.where(lane == 5, rank2, route)
    return route, new_cnt


def _mix_call(x, ys, w_in, b_in, g, cw, cb, wco, wab, wo, g2, wr, br):
    nb, seq, _ = ys.shape
    T = nb * seq
    nl = seq // MIX_TM
    n = nb * nl
    consts = (g, cw, cb, wco, wab, wo, g2, wr, br)
    gates = 2 * D_MODEL
    in_proj_specs = ([_col_block(D_MODEL, CONV_WIDTH, b) for b in (1, 2, 3)]
                     + [_col_block(D_MODEL, gates, 1)]
                     + [_col_block(1, CONV_WIDTH, b) for b in (1, 2, 3)]
                     + [_col_block(1, gates, 1)])
    cur = lambda i: jnp.minimum(i, n - 1)
    prev = lambda i: jnp.maximum(i - 1, 0)
    return pl.pallas_call(
        functools.partial(_mix_body, tiles_per_seq=nl, n_tiles=n),
        out_shape=(jax.ShapeDtypeStruct((T, D_MODEL), BF16),
                   jax.ShapeDtypeStruct((T, D_MODEL // 2), jnp.int32),
                   jax.ShapeDtypeStruct((T, LANES), F32),
                   jax.ShapeDtypeStruct((8, T), F32),
                   jax.ShapeDtypeStruct((1, LANES), F32)),
        grid=(n + 1,),
        in_specs=[pl.BlockSpec((1, MIX_TM, D_MODEL),
                               lambda i: (cur(i) // nl, cur(i) % nl, 0)),
                  pl.BlockSpec((1, MIX_TM, SSM_WIDTH), lambda i: (cur(i) // nl, cur(i) % nl, 0))]
                 + in_proj_specs + [_const_spec(c.shape) for c in consts],
        out_specs=(pl.BlockSpec((MIX_TM, D_MODEL), lambda i: (cur(i), 0)),
                   pl.BlockSpec((MIX_TM, D_MODEL // 2), lambda i: (prev(i), 0)),
                   pl.BlockSpec((MIX_TM, LANES), lambda i: (prev(i), 0)),
                   pl.BlockSpec((8, MIX_TM), lambda i: (0, prev(i))),
                   pl.BlockSpec((1, LANES), lambda i: (0, 0))),
        scratch_shapes=[pltpu.VMEM((8, CONV_WIDTH), F32), pltpu.VMEM((1, LANES), F32),
                        pltpu.VMEM((MIX_TM, D_MODEL), F32)],
        compiler_params=pltpu.CompilerParams(
            dimension_semantics=("arbitrary",), vmem_limit_bytes=VMEM_LIMIT),
        name="conv_glu_router",
    )(x, ys, *([w_in] * 4), *([b_in] * 4), *consts)


SC_CORES = 2
SC_SUBCORES = 16
SC_WORKERS = SC_CORES * SC_SUBCORES
SC_ROWS = 64
SC_NBUF = 2


def _sc_mesh():
    return plsc.VectorSubcoreMesh(core_axis_name="c", subcore_axis_name="s")


def _sc_worker_id():
    return lax.axis_index("s") * SC_CORES + lax.axis_index("c")


def _sc_ring(nch, get, puts):
    for b in range(SC_NBUF - 1):
        get(b, b).start()

    @pl.loop(0, nch, step=SC_NBUF)
    def _(j0):
        for b in range(SC_NBUF):
            j = j0 + b
            refill = (b - 1) % SC_NBUF
            get(j, b).wait()

            @pl.when(j + SC_NBUF - 1 < nch)
            def _():
                @pl.when(j >= 1)
                def _():
                    for c in puts(j - 1, refill):
                        c.wait()
                get(j + SC_NBUF - 1, refill).start()

            for c in puts(j, b):
                c.start()

    for b in range(SC_NBUF):
        for c in puts(nch - SC_NBUF + b, b):
            c.wait()


def _sc_dispatch(rows, pos_a, pos_b, n_out):
    T, W = rows.shape
    per_w = T // SC_WORKERS
    nch = per_w // SC_ROWS
    assert per_w * SC_WORKERS == T and nch * SC_ROWS == per_w and nch % SC_NBUF == 0
    idx_a = pos_a.reshape(SC_WORKERS, nch, SC_ROWS)
    idx_b = pos_b.reshape(SC_WORKERS, nch, SC_ROWS)

    @functools.partial(
        pl.kernel, mesh=_sc_mesh(),
        out_type=jax.ShapeDtypeStruct((n_out, W), rows.dtype),
        scratch_types=[
            pltpu.VMEM((nch, SC_ROWS), jnp.int32),
            pltpu.VMEM((nch, SC_ROWS), jnp.int32),
            pltpu.VMEM((SC_NBUF, SC_ROWS, W), rows.dtype),
            pltpu.SemaphoreType.DMA((SC_NBUF,)),
            pltpu.SemaphoreType.DMA((SC_NBUF,)),
            pltpu.SemaphoreType.DMA((SC_NBUF,)),
        ],
        name="moe_dispatch",
    )
    def k(rows_hbm, ia_hbm, ib_hbm, out_hbm, ia_v, ib_v, buf, gsem, asem, bsem):
        wid = _sc_worker_id()
        base = wid * per_w
        pltpu.sync_copy(ia_hbm.at[wid], ia_v)
        pltpu.sync_copy(ib_hbm.at[wid], ib_v)

        def get(j, b):
            return pltpu.make_async_copy(
                rows_hbm.at[pl.ds(base + j * SC_ROWS, SC_ROWS)], buf.at[b], gsem.at[b])

        def put_a(j, b):
            return pltpu.make_async_copy(buf.at[b], out_hbm.at[ia_v.at[j]], asem.at[b])

        def put_b(j, b):
            return pltpu.make_async_copy(buf.at[b], out_hbm.at[ib_v.at[j]], bsem.at[b])

        _sc_ring(nch, get, lambda j, b: (put_a(j, b), put_b(j, b)))

    return k(rows, idx_a, idx_b)


def _sc_gather(table, idx):
    _, W = table.shape
    B = idx.shape[0]
    per_w = B // SC_WORKERS
    nch = per_w // SC_ROWS
    assert per_w * SC_WORKERS == B and nch * SC_ROWS == per_w and nch % SC_NBUF == 0
    idx3 = idx.reshape(SC_WORKERS, nch, SC_ROWS)

    @functools.partial(
        pl.kernel, mesh=_sc_mesh(),
        out_type=jax.ShapeDtypeStruct((B, W), table.dtype),
        scratch_types=[
            pltpu.VMEM((nch, SC_ROWS), jnp.int32),
            pltpu.VMEM((SC_NBUF, SC_ROWS, W), table.dtype),
            pltpu.SemaphoreType.DMA((SC_NBUF,)),
            pltpu.SemaphoreType.DMA((SC_NBUF,)),
        ],
        name="moe_combine_gather",
    )
    def k(table_hbm, idx_hbm, out_hbm, idx_v, buf, gsem, osem):
        wid = _sc_worker_id()
        base = wid * per_w
        pltpu.sync_copy(idx_hbm.at[wid], idx_v)

        def get(j, b):
            return pltpu.make_async_copy(table_hbm.at[idx_v.at[j]], buf.at[b], gsem.at[b])

        def put(j, b):
            return pltpu.make_async_copy(
                buf.at[b], out_hbm.at[pl.ds(base + j * SC_ROWS, SC_ROWS)], osem.at[b])

        _sc_ring(nch, get, lambda j, b: (put(j, b),))

    return k(table, idx3)


MOE_RT = 1024
MOE_SUB = 256
MOE_IN_BUFFERS = 3


def _moe_tile(i, te_ref, nx_ref, seg_ref, ns_ref, x_ref, wg_hbm, wu_hbm, wd_hbm, o_ref,
              wg_scr, wu_scr, wd_scr, wg_buf, wu_buf, wd_buf, sem):
    expert = te_ref[i]
    slot = seg_ref[i] % 2

    def weight_copies(e, b):
        return (pltpu.make_async_copy(wg_hbm.at[e], wg_buf.at[b], sem.at[b, 0]),
                pltpu.make_async_copy(wu_hbm.at[e], wu_buf.at[b], sem.at[b, 1]),
                pltpu.make_async_copy(wd_hbm.at[e], wd_buf.at[b], sem.at[b, 2]))

    @pl.when(i == 0)
    def _():
        for c in weight_copies(expert, 0):
            c.start()

    @pl.when((i == 0) | (expert != te_ref[jnp.maximum(i - 1, 0)]))
    def _():
        for c in weight_copies(expert, slot):
            c.wait()
        wg_scr[...] = wg_buf[slot].astype(BF16)
        wu_scr[...] = wu_buf[slot].astype(BF16)
        wd_scr[...] = wd_buf[slot].astype(BF16)

        @pl.when(nx_ref[i] >= 0)
        def _():
            for c in weight_copies(nx_ref[i], 1 - slot):
                c.start()

    half = D_MODEL // 2

    def up_proj(s):
        lo, hi = _unpack_bf16_pairs(x_ref[pl.ds(s * MOE_SUB, MOE_SUB), :])
        lo = lo.astype(BF16)
        hi = hi.astype(BF16)
        gate = (jnp.dot(lo, wg_scr[0:half, :], preferred_element_type=F32)
                + jnp.dot(hi, wg_scr[half:D_MODEL, :], preferred_element_type=F32))
        up = (jnp.dot(lo, wu_scr[0:half, :], preferred_element_type=F32)
              + jnp.dot(hi, wu_scr[half:D_MODEL, :], preferred_element_type=F32))
        return gate, up

    def down_proj(gate, up):
        hid = (jax.nn.silu(gate) * up).astype(BF16)
        return jnp.dot(hid, wd_scr[...], preferred_element_type=F32)

    def store(s, y):
        o_ref[pl.ds(s * MOE_SUB, MOE_SUB), :] = _pack_bf16_pairs(y.astype(BF16).astype(F32))

    def run(n_sub):
        gu = {0: up_proj(0)}
        ys = {}
        for s in range(n_sub):
            if s + 1 < n_sub:
                gu[s + 1] = up_proj(s + 1)
            ys[s] = down_proj(*gu.pop(s))
            if s >= 1:
                store(s - 1, ys.pop(s - 1))
        store(n_sub - 1, ys.pop(n_sub - 1))

    for n_sub in range(1, x_ref.shape[0] // MOE_SUB + 1):
        pl.when(ns_ref[i] == n_sub)(functools.partial(run, n_sub))


def _moe_call(plan, xs, wg, wu, wd):
    R = xs.shape[0]
    half = D_MODEL // 2

    def outer(te_ref, nt_ref, nx_ref, seg_ref, ns_ref, x_hbm, wg_hbm, wu_hbm, wd_hbm, o_hbm,
              wg_scr, wu_scr, wd_scr, wg_buf, wu_buf, wd_buf, sem, step_ref):
        row_map = lambda i: (jnp.minimum(i, nt_ref[0] - 1), 0)
        step_ref[0] = 0

        def tile(x_ref, o_ref):
            i = step_ref[0]
            step_ref[0] = i + 1
            _moe_tile(i, te_ref, nx_ref, seg_ref, ns_ref, x_ref, wg_hbm, wu_hbm, wd_hbm, o_ref,
                      wg_scr, wu_scr, wd_scr, wg_buf, wu_buf, wd_buf, sem)

        pltpu.emit_pipeline(
            tile, grid=(R // MOE_RT,),
            in_specs=[pl.BlockSpec((MOE_RT, half), row_map,
                                   pipeline_mode=pl.Buffered(MOE_IN_BUFFERS))],
            out_specs=[pl.BlockSpec((MOE_RT, half), row_map)],
        )(x_hbm, o_hbm)

    hbm = pl.BlockSpec(memory_space=pl.ANY)
    smem = pl.BlockSpec(memory_space=pltpu.SMEM)
    return pl.pallas_call(
        outer,
        out_shape=jax.ShapeDtypeStruct((R, half), jnp.int32),
        in_specs=[smem] * 5 + [hbm] * 4,
        out_specs=hbm,
        scratch_shapes=[pltpu.VMEM((D_MODEL, EXPERT_HIDDEN), BF16),
                        pltpu.VMEM((D_MODEL, EXPERT_HIDDEN), BF16),
                        pltpu.VMEM((EXPERT_HIDDEN, D_MODEL), BF16),
                        pltpu.VMEM((2, D_MODEL, EXPERT_HIDDEN), F32),
                        pltpu.VMEM((2, D_MODEL, EXPERT_HIDDEN), F32),
                        pltpu.VMEM((2, EXPERT_HIDDEN, D_MODEL), F32),
                        pltpu.SemaphoreType.DMA((2, 3)),
                        pltpu.SMEM((1,), jnp.int32)],
        compiler_params=pltpu.CompilerParams(vmem_limit_bytes=VMEM_LIMIT),
        name="moe_experts",
    )(*plan, xs, wg, wu, wd)


def _moe_plan(route_t, counts, n_rows):
    cnt = counts[0, :N_EXPERTS].astype(jnp.int32)
    tiles = (cnt + MOE_RT - 1) // MOE_RT
    tile_end = jnp.cumsum(tiles)
    n_tiles = tile_end[-1:]
    row_start = (tile_end - tiles) * MOE_RT
    ids = route_t[0:2].astype(jnp.int32)
    ranks = route_t[4:6].astype(jnp.int32)
    experts = jnp.arange(N_EXPERTS, dtype=jnp.int32)[:, None, None]
    pos = ranks + jnp.sum(jnp.where(ids[None] == experts, row_start[:, None, None], 0), axis=0)
    tile_id = jnp.minimum(jnp.arange(n_rows // MOE_RT, dtype=jnp.int32), n_tiles - 1)
    tile_expert = jnp.sum((tile_id[:, None] >= tile_end[None, :]).astype(jnp.int32), axis=1)
    e_ids = jnp.arange(N_EXPERTS, dtype=jnp.int32)
    later = (e_ids[None, :] > e_ids[:, None]) & (tiles[None, :] > 0)
    nxt = jnp.min(jnp.where(later, e_ids[None, :], N_EXPERTS), axis=1)
    nxt = jnp.where(nxt == N_EXPERTS, -1, nxt)
    seg = jnp.cumsum((tiles > 0).astype(jnp.int32)) - 1
    pick = tile_expert[:, None] == e_ids[None, :]
    tile_next = jnp.sum(jnp.where(pick, nxt[None, :], 0), axis=1)
    tile_seg = jnp.sum(jnp.where(pick, seg[None, :], 0), axis=1)
    first = jnp.sum(jnp.where(pick, (tile_end - tiles)[None, :], 0), axis=1)
    rows_left = (jnp.sum(jnp.where(pick, cnt[None, :], 0), axis=1)
                 - (jnp.arange(n_rows // MOE_RT, dtype=jnp.int32) - first) * MOE_RT)
    tile_subs = (jnp.clip(rows_left, 0, MOE_RT) + MOE_SUB - 1) // MOE_SUB
    return pos, (tile_expert, n_tiles, tile_next, tile_seg, tile_subs)


PLE_TM = 1024
PLE_SUB = 256


def _ple_body(x_ref, ya_ref, yb_ref, route_ref, p_ref, g3_ref, wpg_ref, bpg_ref, wple_ref, gf_ref,
              o_ref):
    n_sub = x_ref.shape[0] // PLE_SUB
    g3, gf, bpg = g3_ref[...], gf_ref[...], bpg_ref[...]

    def head(s):
        rows = pl.ds(s * PLE_SUB, PLE_SUB)
        ya = jnp.concatenate(_unpack_bf16_pairs(ya_ref[rows, :]), axis=1)
        yb = jnp.concatenate(_unpack_bf16_pairs(yb_ref[rows, :]), axis=1)
        route = route_ref[rows, :]
        x2 = x_ref[rows, :].astype(F32) + route[:, 2:3] * ya + route[:, 3:4] * yb
        return x2, _rms(x2, g3).astype(BF16)

    def dots(s, h3):
        rows = pl.ds(s * PLE_SUB, PLE_SUB)
        zg = jnp.dot(h3, wpg_ref[...], preferred_element_type=F32)
        pe = jnp.dot(p_ref[rows, :].astype(BF16), wple_ref[...], preferred_element_type=F32)
        return zg, pe

    def tail(s, x2, zg, pe):
        x3 = x2 + jax.nn.sigmoid(zg + bpg) * pe
        o_ref[pl.ds(s * PLE_SUB, PLE_SUB), :] = _rms(x3, gf)

    x2s, mm = {}, {}
    x2s[0], h3 = head(0)
    for s in range(n_sub):
        mm[s] = dots(s, h3)
        if s + 1 < n_sub:
            x2s[s + 1], h3 = head(s + 1)
        if s >= 1:
            tail(s - 1, x2s.pop(s - 1), *mm.pop(s - 1))
    tail(n_sub - 1, x2s.pop(n_sub - 1), *mm.pop(n_sub - 1))


def _ple_call(x1, y_picks, route, p, g3, wpg, bpg, wple, gf):
    T = x1.shape[0]
    nt = T // PLE_TM
    consts = (g3, wpg, bpg, wple, gf)
    tok = lambda i: (i, 0)
    return pl.pallas_call(
        _ple_body,
        out_shape=jax.ShapeDtypeStruct((T, D_MODEL), F32),
        grid=(nt,),
        in_specs=[pl.BlockSpec((PLE_TM, D_MODEL), tok),
                  pl.BlockSpec((PLE_TM, D_MODEL // 2), tok),
                  pl.BlockSpec((PLE_TM, D_MODEL // 2), lambda i: (i + nt, 0)),
                  pl.BlockSpec((PLE_TM, LANES), tok),
                  pl.BlockSpec((PLE_TM, p.shape[1]), tok)]
                 + [_const_spec(c.shape) for c in consts],
        out_specs=pl.BlockSpec((PLE_TM, D_MODEL), tok),
        compiler_params=pltpu.CompilerParams(
            dimension_semantics=("arbitrary",), vmem_limit_bytes=VMEM_LIMIT),
        name="ple_final",
    )(x1, y_picks, y_picks, route, p, *consts)


def _layer(x, p, norm_mix, w_in, b_in, lam_re, lam_im, log_dt, b_re, b_im, c_re, c_im, d_skip,
           w_glu_a, w_glu_b, conv_w, conv_b, w_conv_out, w_o, norm_ffn, w_rg, b_rg, w_re, b_re_r,
           w_eg, w_eu, w_ed, norm_ple, w_ple, w_pg, b_pg, norm_out):
    nb, seq, d = x.shape
    T = nb * seq
    row = lambda v: v.reshape(1, -1).astype(F32)
    assert w_in.shape[1] == SSM_WIDTH + 3 * CONV_WIDTH + 2 * D_MODEL and SSM_WIDTH == CONV_WIDTH
    w_in16 = w_in.astype(BF16)
    b_in_row = row(b_in)

    s5_ops = _s5_operators(lam_re, lam_im, log_dt, b_re, b_im, c_re, c_im)
    s5_consts = (row(norm_mix), w_in16, b_in_row, *s5_ops, row(d_skip))

    lane_pad = LANES - N_EXPERTS - N_EXPERT_GROUPS
    w_r = jnp.pad(jnp.concatenate([w_re, w_rg], axis=1).astype(F32), ((0, 0), (0, lane_pad)))
    b_r = jnp.pad(jnp.concatenate([b_re_r, b_rg]).astype(F32), (0, lane_pad)).reshape(1, LANES)
    w_r_hi = w_r.astype(BF16)
    w_r_lo = (w_r - w_r_hi.astype(F32)).astype(BF16)

    mix_consts = (w_in16, b_in_row, row(norm_mix),
                  conv_w.astype(F32), row(conv_b), w_conv_out.astype(BF16),
                  jnp.concatenate([w_glu_a, w_glu_b], axis=1).astype(BF16), w_o.astype(BF16),
                  row(norm_ffn), jnp.concatenate([w_r_hi, w_r_lo], axis=1), b_r)
    ple_consts = (row(norm_ple), w_pg.astype(BF16), row(b_pg), w_ple.astype(BF16), row(norm_out))
    p2d = p.reshape(T, -1)
    n_rows = 2 * T + N_EXPERTS * MOE_RT

    ys = _s5_call(x, *s5_consts)
    x1, h2p, route, route_t, counts = _mix_call(x, ys, *mix_consts)
    pos, plan = _moe_plan(route_t, counts, n_rows)
    xs = _sc_dispatch(h2p, pos[0], pos[1], n_rows)
    ysort = _moe_call(plan, xs, w_eg, w_eu, w_ed)
    y_picks = _sc_gather(ysort, pos.reshape(-1))
    out = _ple_call(x1, y_picks, route, p2d, *ple_consts)
    return out.reshape(nb, seq, d)


def kernel(x, p, norm_mix, w_in, b_in, ssm_lam_re, ssm_lam_im, ssm_log_dt, ssm_b_re, ssm_b_im, ssm_c_re, ssm_c_im, ssm_d, w_glu_a, w_glu_b, conv_w, conv_b, w_conv_out, w_o, norm_ffn, w_router_group, b_router_group, w_router_expert, b_router_expert, w_exp_gate, w_exp_up, w_exp_down, norm_ple, w_ple, w_ple_gate, b_ple_gate, norm_final):
    assert p.shape[0] == 1, "the final RMSNorm is fused into the (single) layer's last kernel"
    i = 0
    return _layer(x, p[i], norm_mix[i], w_in[i], b_in[i], ssm_lam_re[i], ssm_lam_im[i],
                  ssm_log_dt[i], ssm_b_re[i], ssm_b_im[i], ssm_c_re[i], ssm_c_im[i], ssm_d[i],
                  w_glu_a[i], w_glu_b[i], conv_w[i], conv_b[i], w_conv_out[i], w_o[i],
                  norm_ffn[i], w_router_group[i], b_router_group[i], w_router_expert[i],
                  b_router_expert[i], w_exp_gate[i], w_exp_up[i], w_exp_down[i], norm_ple[i],
                  w_ple[i], w_ple_gate[i], b_ple_gate[i], norm_final)
```
